```python
import math
import jax, jax.numpy as jnp
from jax import lax
import numpy as np

D_MODEL = 1024
BATCH = 32
SEQ = 256
DEPTH = 4
DEC_BATCH = 2
DEC_SEQ = 1024
PAST_LEN = 512

GRID_W = 64
N_MIXERS = 4
W_GROUP = D_MODEL // N_MIXERS
H_M = 4
DH_M = W_GROUP // H_M
S5_CH = 16
G_S5 = W_GROUP // S5_CH
P_S5 = 64
H_R = 4
DH_R = W_GROUP // H_R
H_D = 4
DH_D = W_GROUP // (2 * H_D)
N_EXPERTS = 16
D_FF = 2 * D_MODEL
CAPACITY_FACTOR = 2
CHUNK = 64
Q_BLOCK = 128
ROPE_BASE = 10000.0
EPS = 1e-6
IN_SIZES = (W_GROUP,) * 4 + (4 * H_M,) + (W_GROUP,) * 8
N_IN = 12 * W_GROUP + 4 * H_M

kernel_name = 'hybrid_mlstm_s5_retention_diffattn_ec_dit_step'


def rms_norm(x, g):
    xf = x.astype(jnp.float32)
    y = xf * lax.rsqrt(jnp.mean(xf * xf, axis=-1, keepdims=True) + EPS)
    return (y * g.astype(jnp.float32)).astype(x.dtype)


def head_group_norm(x, g):
    xc = x - jnp.mean(x, axis=-1, keepdims=True)
    y = xc * lax.rsqrt(jnp.mean(xc * xc, axis=-1, keepdims=True) + EPS)
    return y.reshape(x.shape[:2] + (-1,)) * g.astype(jnp.float32)


def to_chunks(a):
    B, L = a.shape[:2]
    a = a.reshape((B, L // CHUNK, CHUNK) + a.shape[2:])
    return jnp.moveaxis(jnp.moveaxis(a, 3, 2), 1, 0)


def from_chunks(a):
    a = jnp.moveaxis(jnp.moveaxis(a, 0, 1), 2, 3)
    B, N, T = a.shape[:3]
    return a.reshape((B, N * T) + a.shape[3:])


def mlstm_direction(q, k, v, ig, lf, C0, n0, m0):
    mask = jnp.tril(jnp.ones((CHUNK, CHUNK), dtype=bool))

    def step(carry, inp):
        C, n, m = carry
        qc, kc, vc, ic, fc = inp
        F = jnp.cumsum(fc, axis=-1)
        dlog = jnp.where(mask, F[..., :, None] - F[..., None, :] + ic[..., None, :], -jnp.inf)
        inter = F + m[..., None]
        mt = jnp.maximum(inter, jnp.max(dlog, axis=-1))
        s = jnp.einsum('bhtd,bhsd->bhts', qc, kc) * jnp.exp(dlog - mt[..., None])
        a = jnp.exp(inter - mt)
        num = a[..., None] * jnp.einsum('bhvk,bhtk->bhtv', C, qc) + jnp.einsum('bhts,bhsv->bhtv', s, vc)
        den = a * jnp.einsum('bhk,bhtk->bht', n, qc) + jnp.sum(s, axis=-1)
        h = num / jnp.maximum(jnp.abs(den), jnp.exp(-mt))[..., None]
        FT = F[..., -1]
        wlog = FT[..., None] - F + ic
        m_new = jnp.maximum(FT + m, jnp.max(wlog, axis=-1))
        decay = jnp.exp(FT + m - m_new)
        w = jnp.exp(wlog - m_new[..., None])
        C_new = decay[..., None, None] * C + jnp.einsum('bhs,bhsv,bhsk->bhvk', w, vc, kc)
        n_new = decay[..., None] * n + jnp.einsum('bhs,bhsk->bhk', w, kc)
        return (C_new, n_new, m_new), h

    xs = tuple(to_chunks(a) for a in (q, k, v, ig, lf))
    (C, n, m), h = lax.scan(step, (C0, n0, m0), xs)
    return from_chunks(h), C, n, m


def mlstm_bidir(q, k, v, gates, C0, n0, m0):
    h_sum = 0.0
    Cs, ns, ms = [], [], []
    for d in range(2):
        ig = gates[:, :, 2 * d]
        lf = jax.nn.log_sigmoid(gates[:, :, 2 * d + 1])
        seq = (q, k, v, ig, lf)
        if d == 1:
            seq = tuple(jnp.flip(a, 1) for a in seq)
        h, C, n, m = mlstm_direction(*seq, C0[:, d], n0[:, d], m0[:, d])
        h_sum = h_sum + (h if d == 0 else jnp.flip(h, 1))
        Cs.append(C)
        ns.append(n)
        ms.append(m)
    return h_sum, jnp.stack(Cs, 1), jnp.stack(ns, 1), jnp.stack(ms, 1)


def s5_direction(u, lam_re, lam_im, log_step, b_re, b_im, c_re, c_im, x0_re, x0_im):
    f32 = jnp.float32
    lam_re, lam_im, b_re, b_im, c_re, c_im = (a.astype(f32) for a in (lam_re, lam_im, b_re, b_im, c_re, c_im))
    dt = jnp.exp(log_step.astype(f32))[:, None]
    mag = jnp.exp(lam_re * dt)
    ang = lam_im * dt
    lb_re, lb_im = mag * jnp.cos(ang), mag * jnp.sin(ang)
    nr, ni = lb_re - 1.0, lb_im
    den = lam_re * lam_re + lam_im * lam_im
    f_re = (nr * lam_re + ni * lam_im) / den
    f_im = (ni * lam_re - nr * lam_im) / den
    bb_re = f_re[..., None] * b_re - f_im[..., None] * b_im
    bb_im = f_re[..., None] * b_im + f_im[..., None] * b_re
    bu_re = jnp.einsum('gps,blgs->blgp', bb_re, u)
    bu_im = jnp.einsum('gps,blgs->blgp', bb_im, u)
    bu_re = bu_re.at[:, 0].add(lb_re * x0_re - lb_im * x0_im)
    bu_im = bu_im.at[:, 0].add(lb_re * x0_im + lb_im * x0_re)
    a_re = jnp.broadcast_to(lb_re, bu_re.shape)
    a_im = jnp.broadcast_to(lb_im, bu_im.shape)

    def combine(e1, e2):
        a1r, a1i, b1r, b1i = e1
        a2r, a2i, b2r, b2i = e2
        return (a2r * a1r - a2i * a1i, a2r * a1i + a2i * a1r,
                a2r * b1r - a2i * b1i + b2r, a2r * b1i + a2i * b1r + b2i)

    _, _, xr, xi = lax.associative_scan(combine, (a_re, a_im, bu_re, bu_im), axis=1)
    y = jnp.einsum('gsp,blgp->blgs', c_re, xr) - jnp.einsum('gsp,blgp->blgs', c_im, xi)
    return y, xr[:, -1], xi[:, -1]


def retention_direction(q, k, v, log_gamma, R0):
    pos = jnp.arange(CHUNK, dtype=jnp.float32)
    rel = pos[:, None] - pos[None, :]
    dmat = jnp.where(rel >= 0, jnp.exp(log_gamma[:, None, None] * jnp.maximum(rel, 0.0)), 0.0)
    xi = jnp.exp(log_gamma[:, None] * (pos + 1.0))
    zeta = jnp.exp(log_gamma[:, None] * (CHUNK - 1.0 - pos))
    g_chunk = jnp.exp(log_gamma * CHUNK)

    def step(R, inp):
        qc, kc, vc = inp
        s = jnp.einsum('bhtd,bhsd->bhts', qc, kc) * dmat
        o = jnp.einsum('bhts,bhsv->bhtv', s, vc) + jnp.einsum('bhtk,bhkv->bhtv', qc, R) * xi[:, :, None]
        R_new = g_chunk[:, None, None] * R + jnp.einsum('bhsk,bhsv->bhkv', kc * zeta[:, :, None], vc)
        return R_new, o

    R, o = lax.scan(step, R0, tuple(to_chunks(a) for a in (q, k, v)))
    return from_chunks(o), R


def rope_2d(x):
    f32 = jnp.float32
    L = x.shape[1]
    rows = L // GRID_W
    row = jnp.repeat(jnp.arange(rows), GRID_W).astype(f32)
    col = jnp.tile(jnp.arange(GRID_W), rows).astype(f32)
    half = DH_D // 2
    freqs = ROPE_BASE ** (-jnp.arange(0, half, 2, dtype=f32) / half)

    def rot(xp, p):
        ang = p[:, None] * freqs[None, :]
        cos = jnp.cos(ang)[None, :, None, :]
        sin = jnp.sin(ang)[None, :, None, :]
        x1, x2 = jnp.split(xp, 2, axis=-1)
        return jnp.concatenate([x1 * cos - x2 * sin, x1 * sin + x2 * cos], axis=-1)

    return jnp.concatenate([rot(x[..., :half], row), rot(x[..., half:], col)], axis=-1)


def diff_attention(q, k, v, lam):
    B, L, HH, dh = q.shape
    nb = L // Q_BLOCK
    qb = jnp.moveaxis(q.reshape(B, nb, Q_BLOCK, HH, dh), 1, 0)
    scale = dh ** -0.5

    def one_block(qblk):
        s = jnp.einsum('bqhd,bkhd->bhqk', qblk, k) * scale
        p = jax.nn.softmax(s, axis=-1).reshape(B, H_D, 2, Q_BLOCK, -1)
        a = p[:, :, 0] - lam * p[:, :, 1]
        return jnp.einsum('bhqk,bkhv->bqhv', a, v)

    o = lax.map(one_block, qb)
    return jnp.moveaxis(o, 0, 1).reshape(B, L, H_D, 2 * dh)


def expert_choice(x, router_w, w_gate, w_up, w_down):
    B, L, D = x.shape
    n = B * L
    xt = x.reshape(n, D)
    cap = CAPACITY_FACTOR * n // N_EXPERTS
    aff = jax.nn.softmax((xt @ router_w).astype(jnp.float32), axis=-1)
    g, idx = lax.top_k(aff.T, cap)
    xe = xt[idx]
    hdn = jax.nn.silu(jnp.einsum('ecd,edf->ecf', xe, w_gate)) * jnp.einsum('ecd,edf->ecf', xe, w_up)
    ye = jnp.einsum('ecf,efd->ecd', hdn, w_down) * g[..., None].astype(x.dtype)
    y = jnp.zeros_like(xt).at[idx.reshape(-1)].add(ye.reshape(-1, D))
    return y.reshape(B, L, D)


def adaln_modulation(cvec, w, b):
    m = jax.nn.silu(cvec) @ w + b
    return jnp.split(m[:, None, :], 6, axis=-1)


def token_mixers(h, lp, lam_init, ctx):
    f32 = jnp.float32
    B, L, _ = h.shape
    proj = (h @ lp['w_in']).astype(f32)
    splits = [int(i) for i in np.cumsum(IN_SIZES)[:-1]]
    (mq, mk, mv, mo, mg, su, rq, rk, rv, rg, dq, dk, dv) = jnp.split(proj, splits, axis=-1)
    if ctx is None:
        mC0 = jnp.zeros((B, 2, H_M, DH_M, DH_M), f32)
        mn0 = jnp.zeros((B, 2, H_M, DH_M), f32)
        mm0 = jnp.zeros((B, 2, H_M), f32)
        s5r0 = jnp.zeros((B, 2, G_S5, P_S5), f32)
        s5i0 = jnp.zeros((B, 2, G_S5, P_S5), f32)
        R0 = jnp.zeros((B, 2, H_R, DH_R, DH_R), f32)
    else:
        mC0, mn0, mm0, s5r0, s5i0, R0, k_ctx, v_ctx = (a.astype(f32) for a in ctx)

    q = mq.reshape(B, L, H_M, DH_M)
    k = mk.reshape(B, L, H_M, DH_M) * (DH_M ** -0.5)
    v = mv.reshape(B, L, H_M, DH_M)
    gates = mg.reshape(B, L, 4, H_M) + lp['mlstm_gate_b'].astype(f32)
    hm, mC, mn, mm = mlstm_bidir(q, k, v, gates, mC0, mn0, mm0)
    hm = jax.nn.sigmoid(mo) * rms_norm(hm, lp['mlstm_norm_g'].reshape(H_M, DH_M)).reshape(B, L, W_GROUP)

    u = su.reshape(B, L, G_S5, S5_CH)
    ys = lp['s5_d'].astype(f32) * su
    s5r, s5i = [], []
    for d in range(2):
        ud = u if d == 0 else jnp.flip(u, 1)
        y, xr, xim = s5_direction(ud, lp['s5_lambda_re'][d], lp['s5_lambda_im'][d], lp['s5_log_step'][d],
                                  lp['s5_b_re'], lp['s5_b_im'], lp['s5_c_re'][d], lp['s5_c_im'][d],
                                  s5r0[:, d], s5i0[:, d])
        ys = ys + (y if d == 0 else jnp.flip(y, 1)).reshape(B, L, W_GROUP)
        s5r.append(xr)
        s5i.append(xim)
    ys = jax.nn.gelu(ys)
    ys = ys * jax.nn.sigmoid(ys @ lp['s5_glu_w'].astype(f32) + lp['s5_glu_b'].astype(f32))

    q = rq.reshape(B, L, H_R, DH_R)
    k = rk.reshape(B, L, H_R, DH_R) * (DH_R ** -0.5)
    v = rv.reshape(B, L, H_R, DH_R)
    log_gamma = -jnp.exp(lp['ret_decay'].astype(f32))
    hr = 0.0
    Rs = []
    for d in range(2):
        seq = (q, k, v) if d == 0 else tuple(jnp.flip(a, 1) for a in (q, k, v))
        o, R = retention_direction(*seq, log_gamma[d], R0[:, d])
        hr = hr + (o if d == 0 else jnp.flip(o, 1))
        Rs.append(R)
    hr = head_group_norm(hr, lp['ret_gn_g']) * jax.nn.silu(rg)

    qk_g = lp['diff_qk_norm']
    q = rms_norm(dq.reshape(B, L, 2 * H_D, DH_D), qk_g[0])
    k = rms_norm(dk.reshape(B, L, 2 * H_D, DH_D), qk_g[1])
    v = dv.reshape(B, L, H_D, 2 * DH_D)
    lv = lp['diff_lambda'].astype(f32)
    lam = jnp.exp(jnp.sum(lv[0] * lv[1])) - jnp.exp(jnp.sum(lv[2] * lv[3])) + lam_init
    if ctx is None:
        hd = diff_attention(q, k, v, lam)
    else:
        hd = diff_attention(rope_2d(q), jnp.concatenate([k_ctx, rope_2d(k)], axis=1),
                            jnp.concatenate([v_ctx, v], axis=1), lam)
    hd = rms_norm(hd, lp['diff_subln_g']) * (1.0 - lam_init)

    mixed = jnp.concatenate([hm, ys, hr, hd.reshape(B, L, W_GROUP)], axis=-1).astype(h.dtype)
    out = mixed @ lp['w_out']
    if ctx is None:
        new_ctx = (mC, mn, mm, jnp.stack(s5r, 1), jnp.stack(s5i, 1), jnp.stack(Rs, 1), k, v)
    else:
        new_ctx = None
    return out, new_ctx


def trunk_layer(x, mods, lp, lam_init, ctx):
    sh1, sc1, g1, sh2, sc2, g2 = mods
    h = rms_norm(x, lp['norm1_g']) * (1.0 + sc1) + sh1
    out, new_ctx = token_mixers(h.astype(x.dtype), lp, lam_init, ctx)
    x = x + (g1 * out).astype(x.dtype)
    h = rms_norm(x, lp['norm2_g']) * (1.0 + sc2) + sh2
    ff = expert_choice(h.astype(x.dtype), lp['router_w'], lp['exp_w_gate'], lp['exp_w_up'], lp['exp_w_down'])
    x = x + (g2 * ff).astype(x.dtype)
    return x, new_ctx


def setup_inputs(seed: int = 0) -> dict:
    key = jax.random.key(seed)
    ks = iter(jax.random.split(key, 48))
    f32 = jnp.float32

    def nrm(shape, scale=1.0):
        return scale * jax.random.normal(next(ks), shape, f32)

    def gain(shape):
        return 1.0 + nrm(shape, 0.02)

    D = D_MODEL
    fb = jnp.linspace(3.0, 6.0, H_M, dtype=f32)
    zb = jnp.zeros((H_M,), f32)
    gate_base = jnp.stack([zb, fb, zb, fb])
    ret_base = jnp.log(-jnp.log1p(-jnp.exp2(-5.0 - jnp.arange(H_R, dtype=f32))))
    return {
        'x_prompt': nrm((BATCH, SEQ, D)),
        'x_sample': nrm((DEC_BATCH, DEC_SEQ, D)),
        'state_mlstm_c': nrm((DEC_BATCH, DEPTH, 2, H_M, DH_M, DH_M), 0.3),
        'state_mlstm_n': nrm((DEC_BATCH, DEPTH, 2, H_M, DH_M), 0.3),
        'state_mlstm_m': nrm((DEC_BATCH, DEPTH, 2, H_M), 0.5),
        'state_s5_re': nrm((DEC_BATCH, DEPTH, 2, G_S5, P_S5), 0.3),
        'state_s5_im': nrm((DEC_BATCH, DEPTH, 2, G_S5, P_S5), 0.3),
        'state_ret': nrm((DEC_BATCH, DEPTH, 2, H_R, DH_R, DH_R), 0.3),
        'cache_diff_k': nrm((DEC_BATCH, DEPTH, PAST_LEN, 2 * H_D, DH_D)),
        'cache_diff_v': nrm((DEC_BATCH, DEPTH, PAST_LEN, H_D, 2 * DH_D)),
        'c': nrm((DEC_BATCH, D)),
        'c_ctx': nrm((D,)),
        'norm1_g': gain((DEPTH, D)),
        'norm2_g': gain((DEPTH, D)),
        'ada_w': nrm((DEPTH, D, 6 * D), 0.5 * D ** -0.5),
        'ada_b': nrm((DEPTH, 6 * D), 0.1),
        'w_in': nrm((DEPTH, D, N_IN), D ** -0.5),
        'w_out': nrm((DEPTH, D, D), D ** -0.5),
        'mlstm_gate_b': gate_base + nrm((DEPTH, 4, H_M), 0.1),
        'mlstm_norm_g': gain((DEPTH, W_GROUP)),
        's5_lambda_re': -0.5 + nrm((DEPTH, 2, G_S5, P_S5), 0.01),
        's5_lambda_im': jnp.pi * jnp.arange(P_S5, dtype=f32) + nrm((DEPTH, 2, G_S5, P_S5), 0.01),
        's5_log_step': jax.random.uniform(next(ks), (DEPTH, 2, G_S5), f32, math.log(1e-3), math.log(1e-1)),
        's5_b_re': nrm((DEPTH, G_S5, P_S5, S5_CH), (2 * S5_CH) ** -0.5),
        's5_b_im': nrm((DEPTH, G_S5, P_S5, S5_CH), (2 * S5_CH) ** -0.5),
        's5_c_re': nrm((DEPTH, 2, G_S5, S5_CH, P_S5), (2 * P_S5) ** -0.5),
        's5_c_im': nrm((DEPTH, 2, G_S5, S5_CH, P_S5), (2 * P_S5) ** -0.5),
        's5_d': nrm((DEPTH, W_GROUP)),
        's5_glu_w': nrm((DEPTH, W_GROUP, W_GROUP), W_GROUP ** -0.5),
        's5_glu_b': nrm((DEPTH, W_GROUP), 0.02),
        'ret_decay': ret_base + nrm((DEPTH, 2, H_R), 0.01),
        'ret_gn_g': gain((DEPTH, W_GROUP)),
        'diff_qk_norm': gain((DEPTH, 2, DH_D)),
        'diff_lambda': nrm((DEPTH, 4, DH_D), 0.1),
        'diff_subln_g': gain((DEPTH, 2 * DH_D)),
        'router_w': nrm((DEPTH, D, N_EXPERTS), D ** -0.5),
        'exp_w_gate': nrm((DEPTH, N_EXPERTS, D, D_FF), D ** -0.5),
        'exp_w_up': nrm((DEPTH, N_EXPERTS, D, D_FF), D ** -0.5),
        'exp_w_down': nrm((DEPTH, N_EXPERTS, D_FF, D), D_FF ** -0.5),
    }


def reference(x_prompt, x_sample, state_mlstm_c, state_mlstm_n, state_mlstm_m, state_s5_re, state_s5_im,
              state_ret, cache_diff_k, cache_diff_v, c, c_ctx, norm1_g, norm2_g, ada_w, ada_b, w_in, w_out,
              mlstm_gate_b, mlstm_norm_g, s5_lambda_re, s5_lambda_im, s5_log_step, s5_b_re, s5_b_im,
              s5_c_re, s5_c_im, s5_d, s5_glu_w, s5_glu_b, ret_decay, ret_gn_g, diff_qk_norm, diff_lambda,
              diff_subln_g, router_w, exp_w_gate, exp_w_up, exp_w_down):
    xp = x_prompt
    xs = x_sample
    out_mc, out_mn, out_mm, out_sr, out_si, out_r, out_k, out_v = [], [], [], [], [], [], [], []
    for l in range(DEPTH):
        lp = {
            'norm1_g': norm1_g[l], 'norm2_g': norm2_g[l], 'w_in': w_in[l], 'w_out': w_out[l],
            'mlstm_gate_b': mlstm_gate_b[l], 'mlstm_norm_g': mlstm_norm_g[l],
            's5_lambda_re': s5_lambda_re[l], 's5_lambda_im': s5_lambda_im[l], 's5_log_step': s5_log_step[l],
            's5_b_re': s5_b_re[l], 's5_b_im': s5_b_im[l], 's5_c_re': s5_c_re[l], 's5_c_im': s5_c_im[l],
            's5_d': s5_d[l], 's5_glu_w': s5_glu_w[l], 's5_glu_b': s5_glu_b[l],
            'ret_decay': ret_decay[l], 'ret_gn_g': ret_gn_g[l],
            'diff_qk_norm': diff_qk_norm[l], 'diff_lambda': diff_lambda[l], 'diff_subln_g': diff_subln_g[l],
            'router_w': router_w[l], 'exp_w_gate': exp_w_gate[l], 'exp_w_up': exp_w_up[l],
            'exp_w_down': exp_w_down[l],
        }
        lam_init = 0.8 - 0.6 * math.exp(-0.3 * l)
        mods_ctx = adaln_modulation(c_ctx[None, :], ada_w[l], ada_b[l])
        xp, ctx_t = trunk_layer(xp, mods_ctx, lp, lam_init, None)
        out_mc.append(ctx_t[0])
        out_mn.append(ctx_t[1])
        out_mm.append(ctx_t[2])
        out_sr.append(ctx_t[3])
        out_si.append(ctx_t[4])
        out_r.append(ctx_t[5])
        out_k.append(ctx_t[6])
        out_v.append(ctx_t[7])
        cache_l = (state_mlstm_c[:, l], state_mlstm_n[:, l], state_mlstm_m[:, l], state_s5_re[:, l],
                   state_s5_im[:, l], state_ret[:, l], cache_diff_k[:, l], cache_diff_v[:, l])
        mods_s = adaln_modulation(c, ada_w[l], ada_b[l])
        xs, _ = trunk_layer(xs, mods_s, lp, lam_init, cache_l)
    new_mlstm_c = jnp.stack(out_mc, axis=1)
    new_mlstm_n = jnp.stack(out_mn, axis=1)
    new_mlstm_m = jnp.stack(out_mm, axis=1)
    new_s5_re = jnp.stack(out_sr, axis=1)
    new_s5_im = jnp.stack(out_si, axis=1)
    new_ret = jnp.stack(out_r, axis=1)
    new_diff_k = jnp.stack(out_k, axis=1)
    new_diff_v = jnp.stack(out_v, axis=1)
    return (xp, xs, new_mlstm_c, new_mlstm_n, new_mlstm_m, new_s5_re, new_s5_im, new_ret, new_diff_k, new_diff_v)
```

```python
import functools
import math

import jax
import jax.numpy as jnp
import numpy as np
from jax import lax
from jax.experimental import pallas as pl
from jax.experimental.pallas import tpu as pltpu

D_MODEL = 1024
DEPTH = 4
GRID_W = 64
W_GROUP = 256
H_M = 4
DH_M = 64
S5_CH = 16
G_S5 = 16
P_S5 = 64
S5_STATE = G_S5 * P_S5
H_R = 4
DH_R = 64
H_D = 4
DH_D = 32
N_EXPERTS = 16
CAPACITY_FACTOR = 2
CHUNK = 64
Q_BLOCK = 128
ROPE_BASE = 10000.0
EPS = 1e-6
IN_SIZES = (W_GROUP,) * 4 + (4 * H_M,) + (W_GROUP,) * 8
SUBLANES = 8
VMEM_LIMIT_BYTES = 56 * 1024 * 1024

f32 = jnp.float32


def _gelu_tanh(x):
    return 0.5 * x * (1.0 + jnp.tanh(math.sqrt(2.0 / math.pi) * (x + 0.044715 * (x * x * x))))


def _s5_kernel(su_ref, x0r_ref, x0i_ref, wb_ref, wc_ref, lb_ref, lbl_ref, d_ref, gw_ref, gb_ref,
               y_ref, xr_ref, xi_ref, st_ref, *, seq_len):
    n_steps = seq_len // SUBLANES
    su = su_ref[0]
    y_ref[0] = su * d_ref[...]
    row = lax.broadcasted_iota(jnp.int32, (SUBLANES, S5_STATE), 0)
    zeros = jnp.zeros((SUBLANES, S5_STATE), f32)
    for d in range(2):
        st_ref[...] = jnp.dot(su, wb_ref[d], preferred_element_type=f32)
        lbr = jnp.broadcast_to(lb_ref[d, 0:1, :], (SUBLANES, S5_STATE))
        lbi = jnp.broadcast_to(lb_ref[d, 1:2, :], (SUBLANES, S5_STATE))

        def rows_of(k, d=d):
            kk = k if d == 0 else n_steps - 1 - k
            return pl.ds(pl.multiple_of(kk * SUBLANES, SUBLANES), SUBLANES)

        def scan_step(k, carry, lbr=lbr, lbi=lbi, rows_of=rows_of):
            xr, xi = carry
            r = rows_of(k)
            nxr = lbr * xr - lbi * xi + st_ref[r, 0:S5_STATE]
            nxi = lbr * xi + lbi * xr + st_ref[r, S5_STATE:2 * S5_STATE]
            st_ref[r, 0:S5_STATE] = nxr
            st_ref[r, S5_STATE:2 * S5_STATE] = nxi
            return nxr, nxi

        fr, fi = lax.fori_loop(0, n_steps, scan_step, (zeros, zeros))

        cr = x0r_ref[0, d:d + 1, :]
        ci = x0i_ref[0, d:d + 1, :]
        plr = lbl_ref[d, 0:1, :]
        pli = lbl_ref[d, 1:2, :]
        cmr, cmi = zeros, zeros
        for i in (range(SUBLANES) if d == 0 else reversed(range(SUBLANES))):
            cmr = jnp.where(row == i, cr, cmr)
            cmi = jnp.where(row == i, ci, cmi)
            cr, ci = (plr * cr - pli * ci + fr[i:i + 1], plr * ci + pli * cr + fi[i:i + 1])
        xr_ref[0, d:d + 1, :] = cr
        xi_ref[0, d:d + 1, :] = ci

        def fix_step(k, carry, lbr=lbr, lbi=lbi, cmr=cmr, cmi=cmi, rows_of=rows_of):
            pr, pi = carry
            r = rows_of(k)
            st_ref[r, 0:S5_STATE] = st_ref[r, 0:S5_STATE] + (pr * cmr - pi * cmi)
            st_ref[r, S5_STATE:2 * S5_STATE] = st_ref[r, S5_STATE:2 * S5_STATE] + (pr * cmi + pi * cmr)
            return pr * lbr - pi * lbi, pr * lbi + pi * lbr

        lax.fori_loop(0, n_steps, fix_step, (lbr, lbi))
        y_ref[0] += jnp.dot(st_ref[...], wc_ref[d], preferred_element_type=f32)

    ys = _gelu_tanh(y_ref[0])
    gate = jax.nn.sigmoid(jnp.dot(ys, gw_ref[...], preferred_element_type=f32) + gb_ref[...])
    y_ref[0] = ys * gate


def _s5_prepare(lam_re, lam_im, log_step, b_re, b_im, c_re, c_im, n_steps):
    dt = jnp.exp(log_step)[..., None]
    mag = jnp.exp(lam_re * dt)
    ang = lam_im * dt
    lb_re, lb_im = mag * jnp.cos(ang), mag * jnp.sin(ang)
    nr, ni = lb_re - 1.0, lb_im
    den = lam_re * lam_re + lam_im * lam_im
    f_re = (nr * lam_re + ni * lam_im) / den
    f_im = (ni * lam_re - nr * lam_im) / den
    bb_re = f_re[..., None] * b_re[None] - f_im[..., None] * b_im[None]
    bb_im = f_re[..., None] * b_im[None] + f_im[..., None] * b_re[None]
    eye = jnp.eye(G_S5, dtype=f32)[None, :, None, :, None]

    def block_diag(a):
        return (a[:, :, :, None, :] * eye).reshape(2, G_S5 * a.shape[2], G_S5 * a.shape[3])

    wb = jnp.concatenate([block_diag(jnp.swapaxes(bb_re, 2, 3)), block_diag(jnp.swapaxes(bb_im, 2, 3))],
                         axis=-1)
    wc = jnp.concatenate([block_diag(jnp.swapaxes(c_re, 2, 3)), -block_diag(jnp.swapaxes(c_im, 2, 3))],
                         axis=1)
    lb = jnp.stack([lb_re.reshape(2, S5_STATE), lb_im.reshape(2, S5_STATE)], axis=1)
    pr, pi = lb[:, 0], lb[:, 1]
    for _ in range(int(math.log2(n_steps))):
        pr, pi = pr * pr - pi * pi, 2.0 * pr * pi
    lbl = jnp.stack([pr, pi], axis=1)
    return wb, wc, lb, lbl


def s5_mixer(su, x0r, x0i, lp):
    B, L, _ = su.shape
    n_steps = L // SUBLANES
    assert n_steps * SUBLANES == L and 2 ** int(math.log2(n_steps)) == n_steps
    wb, wc, lb, lbl = _s5_prepare(lp['s5_lambda_re'], lp['s5_lambda_im'], lp['s5_log_step'], lp['s5_b_re'],
                                  lp['s5_b_im'], lp['s5_c_re'], lp['s5_c_im'], n_steps)
    su_p = su.reshape(B, SUBLANES, n_steps, W_GROUP).transpose(0, 2, 1, 3).reshape(B, L, W_GROUP)
    full = lambda *shape: pl.BlockSpec(shape, lambda b: (0,) * len(shape))
    per_b = lambda *shape: pl.BlockSpec((1,) + shape, lambda b: (b,) + (0,) * len(shape))
    y_p, xr, xi = pl.pallas_call(
        functools.partial(_s5_kernel, seq_len=L),
        grid=(B,),
        in_specs=[per_b(L, W_GROUP), per_b(2, S5_STATE), per_b(2, S5_STATE),
                  full(2, W_GROUP, 2 * S5_STATE), full(2, 2 * S5_STATE, W_GROUP),
                  full(2, 2, S5_STATE), full(2, 2, S5_STATE),
                  full(1, W_GROUP), full(W_GROUP, W_GROUP), full(1, W_GROUP)],
        out_specs=[per_b(L, W_GROUP), per_b(2, S5_STATE), per_b(2, S5_STATE)],
        out_shape=[jax.ShapeDtypeStruct((B, L, W_GROUP), f32),
                   jax.ShapeDtypeStruct((B, 2, S5_STATE), f32),
                   jax.ShapeDtypeStruct((B, 2, S5_STATE), f32)],
        scratch_shapes=[pltpu.VMEM((L, 2 * S5_STATE), f32)],
        compiler_params=pltpu.CompilerParams(dimension_semantics=("arbitrary",),
                                             vmem_limit_bytes=VMEM_LIMIT_BYTES),
        name="s5_mixer",
    )(su_p, x0r.reshape(B, 2, S5_STATE), x0i.reshape(B, 2, S5_STATE), wb, wc, lb, lbl,
      lp['s5_d'].reshape(1, W_GROUP), lp['s5_glu_w'], lp['s5_glu_b'].reshape(1, W_GROUP))
    y = y_p.reshape(B, n_steps, SUBLANES, W_GROUP).transpose(0, 2, 1, 3).reshape(B, L, W_GROUP)
    return y, xr.reshape(B, 2, G_S5, P_S5), xi.reshape(B, 2, G_S5, P_S5)


def rms_norm(x, g):
    y = x * lax.rsqrt(jnp.mean(x * x, axis=-1, keepdims=True) + EPS)
    return y * g


def head_group_norm(x, g):
    xc = x - jnp.mean(x, axis=-1, keepdims=True)
    y = xc * lax.rsqrt(jnp.mean(xc * xc, axis=-1, keepdims=True) + EPS)
    return y.reshape(x.shape[:2] + (-1,)) * g


def to_chunks(a):
    B, L = a.shape[:2]
    a = a.reshape((B, L // CHUNK, CHUNK) + a.shape[2:])
    return jnp.moveaxis(jnp.moveaxis(a, 3, 2), 1, 0)


def from_chunks(a):
    a = jnp.moveaxis(jnp.moveaxis(a, 0, 1), 2, 3)
    B, N, T = a.shape[:3]
    return a.reshape((B, N * T) + a.shape[3:])


def mlstm_direction(q, k, v, ig, lf, C0, n0, m0):
    mask = jnp.tril(jnp.ones((CHUNK, CHUNK), dtype=bool))

    def step(carry, inp):
        C, n, m = carry
        qc, kc, vc, ic, fc = inp
        F = jnp.cumsum(fc, axis=-1)
        dlog = jnp.where(mask, F[..., :, None] - F[..., None, :] + ic[..., None, :], -jnp.inf)
        inter = F + m[..., None]
        mt = jnp.maximum(inter, jnp.max(dlog, axis=-1))
        s = jnp.einsum('bhtd,bhsd->bhts', qc, kc) * jnp.exp(dlog - mt[..., None])
        a = jnp.exp(inter - mt)
        num = a[..., None] * jnp.einsum('bhvk,bhtk->bhtv', C, qc) + jnp.einsum('bhts,bhsv->bhtv', s, vc)
        den = a * jnp.einsum('bhk,bhtk->bht', n, qc) + jnp.sum(s, axis=-1)
        h = num / jnp.maximum(jnp.abs(den), jnp.exp(-mt))[..., None]
        FT = F[..., -1]
        wlog = FT[..., None] - F + ic
        m_new = jnp.maximum(FT + m, jnp.max(wlog, axis=-1))
        decay = jnp.exp(FT + m - m_new)
        w = jnp.exp(wlog - m_new[..., None])
        C_new = decay[..., None, None] * C + jnp.einsum('bhs,bhsv,bhsk->bhvk', w, vc, kc)
        n_new = decay[..., None] * n + jnp.einsum('bhs,bhsk->bhk', w, kc)
        return (C_new, n_new, m_new), h

    xs = tuple(to_chunks(a) for a in (q, k, v, ig, lf))
    (C, n, m), h = lax.scan(step, (C0, n0, m0), xs)
    return from_chunks(h), C, n, m


def mlstm_bidir(q, k, v, gates, C0, n0, m0):
    h_sum = 0.0
    Cs, ns, ms = [], [], []
    for d in range(2):
        ig = gates[:, :, 2 * d]
        lf = jax.nn.log_sigmoid(gates[:, :, 2 * d + 1])
        seq = (q, k, v, ig, lf)
        if d == 1:
            seq = tuple(jnp.flip(a, 1) for a in seq)
        h, C, n, m = mlstm_direction(*seq, C0[:, d], n0[:, d], m0[:, d])
        h_sum = h_sum + (h if d == 0 else jnp.flip(h, 1))
        Cs.append(C)
        ns.append(n)
        ms.append(m)
    return h_sum, jnp.stack(Cs, 1), jnp.stack(ns, 1), jnp.stack(ms, 1)


def retention_direction(q, k, v, log_gamma, R0):
    pos = jnp.arange(CHUNK, dtype=f32)
    rel = pos[:, None] - pos[None, :]
    dmat = jnp.where(rel >= 0, jnp.exp(log_gamma[:, None, None] * jnp.maximum(rel, 0.0)), 0.0)
    xi = jnp.exp(log_gamma[:, None] * (pos + 1.0))
    zeta = jnp.exp(log_gamma[:, None] * (CHUNK - 1.0 - pos))
    g_chunk = jnp.exp(log_gamma * CHUNK)

    def step(R, inp):
        qc, kc, vc = inp
        s = jnp.einsum('bhtd,bhsd->bhts', qc, kc) * dmat
        o = jnp.einsum('bhts,bhsv->bhtv', s, vc) + jnp.einsum('bhtk,bhkv->bhtv', qc, R) * xi[:, :, None]
        R_new = g_chunk[:, None, None] * R + jnp.einsum('bhsk,bhsv->bhkv', kc * zeta[:, :, None], vc)
        return R_new, o

    R, o = lax.scan(step, R0, tuple(to_chunks(a) for a in (q, k, v)))
    return from_chunks(o), R


def rope_2d(x):
    L = x.shape[1]
    rows = L // GRID_W
    row = jnp.repeat(jnp.arange(rows), GRID_W).astype(f32)
    col = jnp.tile(jnp.arange(GRID_W), rows).astype(f32)
    half = DH_D // 2
    freqs = ROPE_BASE ** (-jnp.arange(0, half, 2, dtype=f32) / half)

    def rot(xp, p):
        ang = p[:, None] * freqs[None, :]
        cos = jnp.cos(ang)[None, :, None, :]
        sin = jnp.sin(ang)[None, :, None, :]
        x1, x2 = jnp.split(xp, 2, axis=-1)
        return jnp.concatenate([x1 * cos - x2 * sin, x1 * sin + x2 * cos], axis=-1)

    return jnp.concatenate([rot(x[..., :half], row), rot(x[..., half:], col)], axis=-1)


def diff_attention(q, k, v, lam):
    B, L, HH, dh = q.shape
    nb = L // Q_BLOCK
    qb = jnp.moveaxis(q.reshape(B, nb, Q_BLOCK, HH, dh), 1, 0)
    scale = dh ** -0.5

    def one_block(qblk):
        s = jnp.einsum('bqhd,bkhd->bhqk', qblk, k) * scale
        p = jax.nn.softmax(s, axis=-1).reshape(B, H_D, 2, Q_BLOCK, -1)
        a = p[:, :, 0] - lam * p[:, :, 1]
        return jnp.einsum('bhqk,bkhv->bqhv', a, v)

    o = lax.map(one_block, qb)
    return jnp.moveaxis(o, 0, 1).reshape(B, L, H_D, 2 * dh)


def expert_choice(x, router_w, w_gate, w_up, w_down):
    B, L, D = x.shape
    n = B * L
    xt = x.reshape(n, D)
    cap = CAPACITY_FACTOR * n // N_EXPERTS
    aff = jax.nn.softmax(xt @ router_w, axis=-1)
    g, idx = lax.top_k(aff.T, cap)
    xe = xt[idx]
    hdn = jax.nn.silu(jnp.einsum('ecd,edf->ecf', xe, w_gate)) * jnp.einsum('ecd,edf->ecf', xe, w_up)
    ye = jnp.einsum('ecf,efd->ecd', hdn, w_down) * g[..., None]
    y = jnp.zeros_like(xt).at[idx.reshape(-1)].add(ye.reshape(-1, D))
    return y.reshape(B, L, D)


def adaln_modulation(cvec, w, b):
    m = jax.nn.silu(cvec) @ w + b
    return jnp.split(m[:, None, :], 6, axis=-1)


def token_mixers(h, lp, lam_init, ctx):
    B, L, _ = h.shape
    proj = h @ lp['w_in']
    splits = [int(i) for i in np.cumsum(IN_SIZES)[:-1]]
    (mq, mk, mv, mo, mg, su, rq, rk, rv, rg, dq, dk, dv) = jnp.split(proj, splits, axis=-1)
    if ctx is None:
        mC0 = jnp.zeros((B, 2, H_M, DH_M, DH_M), f32)
        mn0 = jnp.zeros((B, 2, H_M, DH_M), f32)
        mm0 = jnp.zeros((B, 2, H_M), f32)
        s5r0 = jnp.zeros((B, 2, G_S5, P_S5), f32)
        s5i0 = jnp.zeros((B, 2, G_S5, P_S5), f32)
        R0 = jnp.zeros((B, 2, H_R, DH_R, DH_R), f32)
    else:
        mC0, mn0, mm0, s5r0, s5i0, R0, k_ctx, v_ctx = ctx

    q = mq.reshape(B, L, H_M, DH_M)
    k = mk.reshape(B, L, H_M, DH_M) * (DH_M ** -0.5)
    v = mv.reshape(B, L, H_M, DH_M)
    gates = mg.reshape(B, L, 4, H_M) + lp['mlstm_gate_b']
    hm, mC, mn, mm = mlstm_bidir(q, k, v, gates, mC0, mn0, mm0)
    hm = jax.nn.sigmoid(mo) * rms_norm(hm, lp['mlstm_norm_g'].reshape(H_M, DH_M)).reshape(B, L, W_GROUP)

    ys, s5r, s5i = s5_mixer(su, s5r0, s5i0, lp)

    q = rq.reshape(B, L, H_R, DH_R)
    k = rk.reshape(B, L, H_R, DH_R) * (DH_R ** -0.5)
    v = rv.reshape(B, L, H_R, DH_R)
    log_gamma = -jnp.exp(lp['ret_decay'])
    hr = 0.0
    Rs = []
    for d in range(2):
        seq = (q, k, v) if d == 0 else tuple(jnp.flip(a, 1) for a in (q, k, v))
        o, R = retention_direction(*seq, log_gamma[d], R0[:, d])
        hr = hr + (o if d == 0 else jnp.flip(o, 1))
        Rs.append(R)
    hr = head_group_norm(hr, lp['ret_gn_g']) * jax.nn.silu(rg)

    qk_g = lp['diff_qk_norm']
    q = rms_norm(dq.reshape(B, L, 2 * H_D, DH_D), qk_g[0])
    k = rms_norm(dk.reshape(B, L, 2 * H_D, DH_D), qk_g[1])
    v = dv.reshape(B, L, H_D, 2 * DH_D)
    lv = lp['diff_lambda']
    lam = jnp.exp(jnp.sum(lv[0] * lv[1])) - jnp.exp(jnp.sum(lv[2] * lv[3])) + lam_init
    if ctx is None:
        hd = diff_attention(q, k, v, lam)
    else:
        hd = diff_attention(rope_2d(q), jnp.concatenate([k_ctx, rope_2d(k)], axis=1),
                            jnp.concatenate([v_ctx, v], axis=1), lam)
    hd = rms_norm(hd, lp['diff_subln_g']) * (1.0 - lam_init)

    mixed = jnp.concatenate([hm, ys, hr, hd.reshape(B, L, W_GROUP)], axis=-1)
    out = mixed @ lp['w_out']
    new_ctx = (mC, mn, mm, s5r, s5i, jnp.stack(Rs, 1), k, v) if ctx is None else None
    return out, new_ctx


def trunk_layer(x, mods, lp, lam_init, ctx):
    sh1, sc1, g1, sh2, sc2, g2 = mods
    h = rms_norm(x, lp['norm1_g']) * (1.0 + sc1) + sh1
    out, new_ctx = token_mixers(h, lp, lam_init, ctx)
    x = x + g1 * out
    h = rms_norm(x, lp['norm2_g']) * (1.0 + sc2) + sh2
    ff = expert_choice(h, lp['router_w'], lp['exp_w_gate'], lp['exp_w_up'], lp['exp_w_down'])
    x = x + g2 * ff
    return x, new_ctx


PER_LAYER = ('norm1_g', 'norm2_g', 'w_in', 'w_out', 'mlstm_gate_b', 'mlstm_norm_g', 's5_lambda_re', 's5_lambda_im',
             's5_log_step', 's5_b_re', 's5_b_im', 's5_c_re', 's5_c_im', 's5_d', 's5_glu_w', 's5_glu_b', 'ret_decay',
             'ret_gn_g', 'diff_qk_norm', 'diff_lambda', 'diff_subln_g', 'router_w', 'exp_w_gate', 'exp_w_up',
             'exp_w_down')


def kernel(x_prompt, x_sample, state_mlstm_c, state_mlstm_n, state_mlstm_m, state_s5_re, state_s5_im, state_ret, cache_diff_k, cache_diff_v, c, c_ctx, norm1_g, norm2_g, ada_w, ada_b, w_in, w_out, mlstm_gate_b, mlstm_norm_g, s5_lambda_re, s5_lambda_im, s5_log_step, s5_b_re, s5_b_im, s5_c_re, s5_c_im, s5_d, s5_glu_w, s5_glu_b, ret_decay, ret_gn_g, diff_qk_norm, diff_lambda, diff_subln_g, router_w, exp_w_gate, exp_w_up, exp_w_down):
    weights = dict(norm1_g=norm1_g, norm2_g=norm2_g, w_in=w_in, w_out=w_out, mlstm_gate_b=mlstm_gate_b,
                   mlstm_norm_g=mlstm_norm_g, s5_lambda_re=s5_lambda_re, s5_lambda_im=s5_lambda_im,
                   s5_log_step=s5_log_step, s5_b_re=s5_b_re, s5_b_im=s5_b_im, s5_c_re=s5_c_re, s5_c_im=s5_c_im,
                   s5_d=s5_d, s5_glu_w=s5_glu_w, s5_glu_b=s5_glu_b, ret_decay=ret_decay, ret_gn_g=ret_gn_g,
                   diff_qk_norm=diff_qk_norm, diff_lambda=diff_lambda, diff_subln_g=diff_subln_g,
                   router_w=router_w, exp_w_gate=exp_w_gate, exp_w_up=exp_w_up, exp_w_down=exp_w_down)
    xp, xs = x_prompt, x_sample
    outs = [[] for _ in range(8)]
    for l in range(DEPTH):
        lp = {name: weights[name][l] for name in PER_LAYER}
        lam_init = 0.8 - 0.6 * math.exp(-0.3 * l)
        mods_ctx = adaln_modulation(c_ctx[None, :], ada_w[l], ada_b[l])
        xp, ctx_t = trunk_layer(xp, mods_ctx, lp, lam_init, None)
        for acc, t in zip(outs, ctx_t):
            acc.append(t)
        cache_l = (state_mlstm_c[:, l], state_mlstm_n[:, l], state_mlstm_m[:, l], state_s5_re[:, l],
                   state_s5_im[:, l], state_ret[:, l], cache_diff_k[:, l], cache_diff_v[:, l])
        mods_s = adaln_modulation(c, ada_w[l], ada_b[l])
        xs, _ = trunk_layer(xs, mods_s, lp, lam_init, cache_l)
    return (xp, xs) + tuple(jnp.stack(o, axis=1) for o in outs)
```

```python
import functools
import math

import jax
import jax.numpy as jnp
import numpy as np
from jax import lax
from jax.experimental import pallas as pl
from jax.experimental.pallas import tpu as pltpu

D_MODEL = 1024
DEPTH = 4
GRID_W = 64
W_GROUP = 256
H_M = 4
DH_M = 64
S5_CH = 16
G_S5 = 16
P_S5 = 64
S5_STATE = G_S5 * P_S5
H_R = 4
DH_R = 64
H_D = 4
DH_D = 32
N_EXPERTS = 16
CAPACITY_FACTOR = 2
ROPE_BASE = 10000.0
EPS = 1e-6
SUBLANES = 8
VMEM_LIMIT_BYTES = 56 * 1024 * 1024

f32 = jnp.float32
bf16 = jnp.bfloat16
HIGHEST = lax.Precision.HIGHEST
NEG_INF = float("-inf")


def _gelu_tanh(x):
    return 0.5 * x * (1.0 + jnp.tanh(math.sqrt(2.0 / math.pi) * (x + 0.044715 * (x * x * x))))


def _s5_kernel(su_ref, x0r_ref, x0i_ref, wb_ref, wc_ref, lb_ref, lbl_ref, d_ref, gw_ref, gb_ref,
               y_ref, xr_ref, xi_ref, st_ref, *, seq_len):
    n_steps = seq_len // SUBLANES
    su = su_ref[0]
    y_ref[0] = su * d_ref[...]
    row = lax.broadcasted_iota(jnp.int32, (SUBLANES, S5_STATE), 0)
    zeros = jnp.zeros((SUBLANES, S5_STATE), f32)
    for d in range(2):
        st_ref[...] = jnp.dot(su, wb_ref[d], preferred_element_type=f32)
        lbr = jnp.broadcast_to(lb_ref[d, 0:1, :], (SUBLANES, S5_STATE))
        lbi = jnp.broadcast_to(lb_ref[d, 1:2, :], (SUBLANES, S5_STATE))

        def rows_of(k, d=d):
            kk = k if d == 0 else n_steps - 1 - k
            return pl.ds(pl.multiple_of(kk * SUBLANES, SUBLANES), SUBLANES)

        def scan_step(k, carry, lbr=lbr, lbi=lbi, rows_of=rows_of):
            xr, xi = carry
            r = rows_of(k)
            nxr = lbr * xr - lbi * xi + st_ref[r, 0:S5_STATE]
            nxi = lbr * xi + lbi * xr + st_ref[r, S5_STATE:2 * S5_STATE]
            st_ref[r, 0:S5_STATE] = nxr
            st_ref[r, S5_STATE:2 * S5_STATE] = nxi
            return nxr, nxi

        fr, fi = lax.fori_loop(0, n_steps, scan_step, (zeros, zeros))

        cr = x0r_ref[0, d:d + 1, :]
        ci = x0i_ref[0, d:d + 1, :]
        plr = lbl_ref[d, 0:1, :]
        pli = lbl_ref[d, 1:2, :]
        cmr, cmi = zeros, zeros
        for i in (range(SUBLANES) if d == 0 else reversed(range(SUBLANES))):
            cmr = jnp.where(row == i, cr, cmr)
            cmi = jnp.where(row == i, ci, cmi)
            cr, ci = (plr * cr - pli * ci + fr[i:i + 1], plr * ci + pli * cr + fi[i:i + 1])
        xr_ref[0, d:d + 1, :] = cr
        xi_ref[0, d:d + 1, :] = ci

        def fix_step(k, carry, lbr=lbr, lbi=lbi, cmr=cmr, cmi=cmi, rows_of=rows_of):
            pr, pi = carry
            r = rows_of(k)
            st_ref[r, 0:S5_STATE] = st_ref[r, 0:S5_STATE] + (pr * cmr - pi * cmi)
            st_ref[r, S5_STATE:2 * S5_STATE] = st_ref[r, S5_STATE:2 * S5_STATE] + (pr * cmi + pi * cmr)
            return pr * lbr - pi * lbi, pr * lbi + pi * lbr

        lax.fori_loop(0, n_steps, fix_step, (lbr, lbi))
        y_ref[0] += jnp.dot(st_ref[...], wc_ref[d], preferred_element_type=f32)

    ys = _gelu_tanh(y_ref[0])
    gate = jax.nn.sigmoid(jnp.dot(ys, gw_ref[...], preferred_element_type=f32) + gb_ref[...])
    y_ref[0] = ys * gate


def _s5_prepare(lam_re, lam_im, log_step, b_re, b_im, c_re, c_im, n_steps):
    dt = jnp.exp(log_step)[..., None]
    mag = jnp.exp(lam_re * dt)
    ang = lam_im * dt
    lb_re, lb_im = mag * jnp.cos(ang), mag * jnp.sin(ang)
    nr, ni = lb_re - 1.0, lb_im
    den = lam_re * lam_re + lam_im * lam_im
    f_re = (nr * lam_re + ni * lam_im) / den
    f_im = (ni * lam_re - nr * lam_im) / den
    bb_re = f_re[..., None] * b_re[None] - f_im[..., None] * b_im[None]
    bb_im = f_re[..., None] * b_im[None] + f_im[..., None] * b_re[None]
    eye = jnp.eye(G_S5, dtype=f32)[None, :, None, :, None]

    def block_diag(a):
        return (a[:, :, :, None, :] * eye).reshape(2, G_S5 * a.shape[2], G_S5 * a.shape[3])

    wb = jnp.concatenate([block_diag(jnp.swapaxes(bb_re, 2, 3)), block_diag(jnp.swapaxes(bb_im, 2, 3))],
                         axis=-1)
    wc = jnp.concatenate([block_diag(jnp.swapaxes(c_re, 2, 3)), -block_diag(jnp.swapaxes(c_im, 2, 3))],
                         axis=1)
    lb = jnp.stack([lb_re.reshape(2, S5_STATE), lb_im.reshape(2, S5_STATE)], axis=1)
    pr, pi = lb[:, 0], lb[:, 1]
    for _ in range(int(math.log2(n_steps))):
        pr, pi = pr * pr - pi * pi, 2.0 * pr * pi
    lbl = jnp.stack([pr, pi], axis=1)
    return wb, wc, lb, lbl


def s5_mixer(su, x0r, x0i, lp):
    B, L, _ = su.shape
    n_steps = L // SUBLANES
    assert n_steps * SUBLANES == L and 2 ** int(math.log2(n_steps)) == n_steps
    wb, wc, lb, lbl = _s5_prepare(lp['s5_lambda_re'], lp['s5_lambda_im'], lp['s5_log_step'], lp['s5_b_re'],
                                  lp['s5_b_im'], lp['s5_c_re'], lp['s5_c_im'], n_steps)
    su_p = su.reshape(B, SUBLANES, n_steps, W_GROUP).transpose(0, 2, 1, 3).reshape(B, L, W_GROUP)
    full = lambda *shape: pl.BlockSpec(shape, lambda b: (0,) * len(shape))
    per_b = lambda *shape: pl.BlockSpec((1,) + shape, lambda b: (b,) + (0,) * len(shape))
    y_p, xr, xi = pl.pallas_call(
        functools.partial(_s5_kernel, seq_len=L),
        grid=(B,),
        in_specs=[per_b(L, W_GROUP), per_b(2, S5_STATE), per_b(2, S5_STATE),
                  full(2, W_GROUP, 2 * S5_STATE), full(2, 2 * S5_STATE, W_GROUP),
                  full(2, 2, S5_STATE), full(2, 2, S5_STATE),
                  full(1, W_GROUP), full(W_GROUP, W_GROUP), full(1, W_GROUP)],
        out_specs=[per_b(L, W_GROUP), per_b(2, S5_STATE), per_b(2, S5_STATE)],
        out_shape=[jax.ShapeDtypeStruct((B, L, W_GROUP), f32),
                   jax.ShapeDtypeStruct((B, 2, S5_STATE), f32),
                   jax.ShapeDtypeStruct((B, 2, S5_STATE), f32)],
        scratch_shapes=[pltpu.VMEM((L, 2 * S5_STATE), f32)],
        compiler_params=pltpu.CompilerParams(dimension_semantics=("arbitrary",),
                                             vmem_limit_bytes=VMEM_LIMIT_BYTES),
        name="s5_mixer",
    )(su_p, x0r.reshape(B, 2, S5_STATE), x0i.reshape(B, 2, S5_STATE), wb, wc, lb, lbl,
      lp['s5_d'].reshape(1, W_GROUP), lp['s5_glu_w'], lp['s5_glu_b'].reshape(1, W_GROUP))
    y = y_p.reshape(B, n_steps, SUBLANES, W_GROUP).transpose(0, 2, 1, 3).reshape(B, L, W_GROUP)
    return y, xr.reshape(B, 2, G_S5, P_S5), xi.reshape(B, 2, G_S5, P_S5)


Q_TILE = 256
PROJ_BLOCKS = dict(mq=0, mk=1, mv=2, mo=3, su=4, rq=5, rk=6, rv=7, rg=8, dq=9, dk=10, dv=11)
GATE_LANES = 128
N_PROJ = 12 * W_GROUP + GATE_LANES


def _log_sigmoid(x):
    return jnp.minimum(x, 0.0) - jnp.log1p(jnp.exp(-jnp.abs(x)))


def _group_ones(width, group):
    shift = int(math.log2(group))
    r = lax.broadcasted_iota(jnp.int32, (width, width), 0) >> shift
    c = lax.broadcasted_iota(jnp.int32, (width, width), 1) >> shift
    return (r == c).astype(f32)


def _group_mean(x, ones, group):
    return jnp.dot(x, ones, precision=HIGHEST, preferred_element_type=f32) * (1.0 / group)


def _dot_nt(a, b):
    return lax.dot_general(a.astype(bf16), b.astype(bf16), (((1,), (1,)), ((), ())), preferred_element_type=f32)


def _dot(a, b):
    return jnp.dot(a.astype(bf16), b.astype(bf16), preferred_element_type=f32)


def _proj_block(name, rows):
    j = PROJ_BLOCKS[name]
    return pl.BlockSpec((1, rows, W_GROUP), lambda b, qi, j=j, rows=rows: (b, qi if rows == Q_TILE else 0, j))


def _mlstm_kernel(q_ref, k_ref, v_ref, o_ref, g_ref, gt_ref, gb_ref, gbt_ref, ng_ref, c0_ref, n0_ref, m0_ref,
                  h_ref, c_ref, n_ref, m_ref, glr_ref, gur_ref, *, seq_len, q_tile):
    L, TQ = seq_len, q_tile
    qi = pl.program_id(1)
    grow = gt_ref[0] + gbt_ref[...]

    @pl.when(qi == 0)
    def _():
        lfrow = _log_sigmoid(grow)
        ss = lax.broadcasted_iota(jnp.int32, (L, L), 0)
        tt = lax.broadcasted_iota(jnp.int32, (L, L), 1)
        glr_ref[...] = jnp.dot(lfrow, (ss <= tt).astype(f32), precision=HIGHEST, preferred_element_type=f32)
        gur_ref[...] = jnp.dot(lfrow, (ss >= tt).astype(f32), precision=HIGHEST, preferred_element_type=f32)

    gl_row = glr_ref[...]
    gu_row = gur_ref[...]
    t_idx = qi * TQ + lax.broadcasted_iota(jnp.int32, (TQ, L), 0)
    s_idx = lax.broadcasted_iota(jnp.int32, (TQ, L), 1)
    low = s_idx <= t_idx
    upp = s_idx >= t_idx
    lfcol = _log_sigmoid(g_ref[0] + gb_ref[...])
    gl_col = jnp.dot(low.astype(f32), lfcol, precision=HIGHEST, preferred_element_type=f32)
    gu_col = jnp.dot(upp.astype(f32), lfcol, precision=HIGHEST, preferred_element_type=f32)

    q = q_ref[0]
    k = k_ref[0] * (DH_M ** -0.5)
    v = v_ref[0]
    for h in range(H_M):
        hs = slice(h * DH_M, (h + 1) * DH_M)
        qh, kh, vh = q[:, hs], k[:, hs], v[:, hs]
        s0 = _dot_nt(qh, kh)
        h_sum = None
        for d in range(2):
            ii, fi = 8 * d + h, 8 * d + 4 + h
            g_t = (gl_col if d == 0 else gu_col)[:, fi:fi + 1]
            g_s = (gl_row if d == 0 else gu_row)[fi:fi + 1, :]
            dlog = jnp.where(low if d == 0 else upp, g_t - g_s + grow[ii:ii + 1, :], NEG_INF)
            inter = g_t + m0_ref[0, d:d + 1, h:h + 1]
            m_t = jnp.maximum(inter, jnp.max(dlog, axis=1, keepdims=True))
            p = s0 * jnp.exp(dlog - m_t)
            a = jnp.exp(inter - m_t)
            num = _dot(p, vh) + a * _dot_nt(qh, c0_ref[0, d, h])
            den = jnp.sum(p, axis=1, keepdims=True) + a * jnp.sum(qh * n0_ref[0, d, h:h + 1, :], axis=1, keepdims=True)
            hd = num / jnp.maximum(jnp.abs(den), jnp.exp(-m_t))
            h_sum = hd if h_sum is None else h_sum + hd
        y = h_sum * lax.rsqrt(jnp.mean(h_sum * h_sum, axis=1, keepdims=True) + EPS) * ng_ref[:, hs]
        h_ref[0, :, hs] = jax.nn.sigmoid(o_ref[0][:, hs]) * y

    @pl.when(qi == 0)
    def _():
        vt = v.T
        for d in range(2):
            for h in range(H_M):
                hs = slice(h * DH_M, (h + 1) * DH_M)
                ii, fi = 8 * d + h, 8 * d + 4 + h
                g_row = (gl_row if d == 0 else gu_row)[fi:fi + 1, :]
                g_tot = g_row[:, L - 1:L] if d == 0 else g_row[:, 0:1]
                wlog = g_tot - g_row + grow[ii:ii + 1, :]
                m0 = m0_ref[0, d:d + 1, h:h + 1]
                m_new = jnp.maximum(g_tot + m0, jnp.max(wlog, axis=1, keepdims=True))
                decay = jnp.exp(g_tot + m0 - m_new)
                w = jnp.exp(wlog - m_new)
                kh = k[:, hs]
                c_ref[0, d, h] = decay * c0_ref[0, d, h] + _dot(vt[hs, :] * w, kh)
                n_upd = jnp.dot(jnp.broadcast_to(w, (SUBLANES, L)), kh, precision=HIGHEST,
                                preferred_element_type=f32)[0:1, :]
                n_ref[0, d, h:h + 1, :] = decay * n0_ref[0, d, h:h + 1, :] + n_upd
                m_ref[0, d:d + 1, h:h + 1] = m_new


def mlstm_mixer(proj, gates_t, c0, n0, m0, lp):
    B, L, _ = proj.shape
    TQ = min(L, Q_TILE)
    gb = lp['mlstm_gate_b'].reshape(1, 4 * H_M)
    const = lambda *shape: pl.BlockSpec(shape, lambda b, qi: (0,) * len(shape))
    per_b = lambda *shape: pl.BlockSpec((1,) + shape, lambda b, qi: (b,) + (0,) * len(shape))
    return pl.pallas_call(
        functools.partial(_mlstm_kernel, seq_len=L, q_tile=TQ),
        grid=(B, L // TQ),
        in_specs=[_proj_block('mq', TQ), _proj_block('mk', L), _proj_block('mv', L), _proj_block('mo', TQ),
                  pl.BlockSpec((1, L, GATE_LANES), lambda b, qi: (b, 0, 12 * W_GROUP // GATE_LANES)),
                  per_b(4 * H_M, L), const(1, GATE_LANES), const(4 * H_M, 1), const(1, W_GROUP),
                  per_b(2, H_M, DH_M, DH_M), per_b(2, H_M, DH_M), per_b(2, H_M)],
        out_specs=[pl.BlockSpec((1, TQ, W_GROUP), lambda b, qi: (b, qi, 0)),
                   per_b(2, H_M, DH_M, DH_M), per_b(2, H_M, DH_M), per_b(2, H_M)],
        out_shape=[jax.ShapeDtypeStruct((B, L, W_GROUP), f32),
                   jax.ShapeDtypeStruct((B, 2, H_M, DH_M, DH_M), f32),
                   jax.ShapeDtypeStruct((B, 2, H_M, DH_M), f32),
                   jax.ShapeDtypeStruct((B, 2, H_M), f32)],
        scratch_shapes=[pltpu.VMEM((4 * H_M, L), f32), pltpu.VMEM((4 * H_M, L), f32)],
        compiler_params=pltpu.CompilerParams(dimension_semantics=("arbitrary", "arbitrary"),
                                             vmem_limit_bytes=VMEM_LIMIT_BYTES),
        name="mlstm_mixer",
    )(proj, proj, proj, proj, proj, gates_t, jnp.pad(gb, ((0, 0), (0, GATE_LANES - 4 * H_M))),
      gb.reshape(4 * H_M, 1), lp['mlstm_norm_g'].reshape(1, W_GROUP), c0, n0, m0)


def _retention_kernel(lg_ref, q_ref, k_ref, v_ref, g_ref, gn_ref, r0_ref, h_ref, r_ref, *, seq_len, q_tile):
    L, TQ = seq_len, q_tile
    qi = pl.program_id(1)
    t_col = qi * TQ + lax.broadcasted_iota(jnp.int32, (TQ, 1), 0)
    rel = (qi * TQ + lax.broadcasted_iota(jnp.int32, (TQ, L), 0)
           - lax.broadcasted_iota(jnp.int32, (TQ, L), 1)).astype(f32)
    q = q_ref[0]
    k = k_ref[0] * (DH_R ** -0.5)
    v = v_ref[0]
    ones = _group_ones(W_GROUP, DH_R)
    for h in range(H_R):
        hs = slice(h * DH_R, (h + 1) * DH_R)
        lgf, lgb = lg_ref[0, h], lg_ref[1, h]
        qh, kh, vh = q[:, hs], k[:, hs], v[:, hs]
        decay = jnp.where(rel > 0.0, jnp.exp(lgf * jnp.maximum(rel, 0.0)),
                          jnp.where(rel < 0.0, jnp.exp(lgb * jnp.maximum(-rel, 0.0)), 2.0))
        o = _dot(_dot_nt(qh, kh) * decay, vh)
        xi_f = jnp.exp(lgf * (t_col + 1).astype(f32))
        xi_b = jnp.exp(lgb * (L - t_col).astype(f32))
        o = o + xi_f * _dot(qh, r0_ref[0, 0, h]) + xi_b * _dot(qh, r0_ref[0, 1, h])
        h_ref[0, :, hs] = o
    o = h_ref[0]
    oc = o - _group_mean(o, ones, DH_R)
    y = oc * lax.rsqrt(_group_mean(oc * oc, ones, DH_R) + EPS) * gn_ref[...]
    h_ref[0] = y * jax.nn.silu(g_ref[0])

    @pl.when(qi == 0)
    def _():
        kt = k.T
        s_row = lax.broadcasted_iota(jnp.int32, (1, L), 1).astype(f32)
        for d in range(2):
            for h in range(H_R):
                hs = slice(h * DH_R, (h + 1) * DH_R)
                lg = lg_ref[d, h]
                zeta = jnp.exp(lg * ((L - 1.0) - s_row)) if d == 0 else jnp.exp(lg * s_row)
                r_ref[0, d, h] = jnp.exp(lg * float(L)) * r0_ref[0, d, h] + _dot(kt[hs, :] * zeta, v[:, hs])


def retention_mixer(proj, r0, lp):
    B, L, _ = proj.shape
    TQ = min(L, Q_TILE)
    log_gamma = -jnp.exp(lp['ret_decay'])
    per_b = lambda *shape: pl.BlockSpec((1,) + shape, lambda b, qi: (b,) + (0,) * len(shape))
    return pl.pallas_call(
        functools.partial(_retention_kernel, seq_len=L, q_tile=TQ),
        grid=(B, L // TQ),
        in_specs=[pl.BlockSpec(memory_space=pltpu.SMEM),
                  _proj_block('rq', TQ), _proj_block('rk', L), _proj_block('rv', L), _proj_block('rg', TQ),
                  pl.BlockSpec((1, W_GROUP), lambda b, qi: (0, 0)), per_b(2, H_R, DH_R, DH_R)],
        out_specs=[pl.BlockSpec((1, TQ, W_GROUP), lambda b, qi: (b, qi, 0)), per_b(2, H_R, DH_R, DH_R)],
        out_shape=[jax.ShapeDtypeStruct((B, L, W_GROUP), f32),
                   jax.ShapeDtypeStruct((B, 2, H_R, DH_R, DH_R), f32)],
        compiler_params=pltpu.CompilerParams(dimension_semantics=("arbitrary", "arbitrary"),
                                             vmem_limit_bytes=VMEM_LIMIT_BYTES),
        name="retention_mixer",
    )(log_gamma, proj, proj, proj, proj, lp['ret_gn_g'].reshape(1, W_GROUP), r0)


def _rope_tables(L):
    half = DH_D // 2
    freqs = ROPE_BASE ** (-np.arange(0, half, 2, dtype=np.float64) / half)
    pos = np.arange(L)
    row, col = (pos // GRID_W).astype(np.float64), (pos % GRID_W).astype(np.float64)
    ang = np.concatenate([np.tile(row[:, None] * freqs, (1, 2)), np.tile(col[:, None] * freqs, (1, 2))], axis=1)
    sign = np.tile(np.concatenate([-np.ones(half // 2), np.ones(half // 2)]), 2)
    cos = np.tile(np.cos(ang), (1, 2 * H_D))
    sin = np.tile(np.sin(ang) * sign, (1, 2 * H_D))
    return jnp.asarray(cos, f32), jnp.asarray(sin, f32)


def _swap_pairs(x):
    parts = []
    for j in range(x.shape[1] // 128):
        xs = x[:, j * 128:(j + 1) * 128]
        lane = lax.broadcasted_iota(jnp.int32, xs.shape, 1)
        parts.append(jnp.where((lane & 15) < 8, pltpu.roll(xs, 120, 1), pltpu.roll(xs, 8, 1)))
    return jnp.concatenate(parts, axis=1)


def _qk_norm(x, gain, ones):
    return x * lax.rsqrt(_group_mean(x * x, ones, DH_D) + EPS) * gain


def _diff_attn_kernel(lam_ref, q_ref, k_ref, v_ref, qkg_ref, sg_ref, *rest, seq_len, q_tile, past_len, out_scale):
    L, TQ, P = seq_len, q_tile, past_len
    if P:
        kc_ref, vc_ref, cos_ref, sin_ref, h_ref, ka_ref, va_ref = rest
    else:
        h_ref, kn_ref, ka_ref, va_ref = rest
    qi = pl.program_id(1)
    ones = _group_ones(W_GROUP, DH_D)

    @pl.when(qi == 0)
    def _():
        kn = _qk_norm(k_ref[0], qkg_ref[1:2, :], ones)
        if P:
            kn = kn * cos_ref[...] + _swap_pairs(kn) * sin_ref[...]
            ka_ref[0:P, :] = kc_ref[0, 0].astype(bf16)
            va_ref[0:P, :] = vc_ref[0, 0].astype(bf16)
        else:
            kn_ref[0] = kn
        ka_ref[P:P + L, :] = kn.astype(bf16)
        va_ref[P:P + L, :] = v_ref[0].astype(bf16)

    qn = _qk_norm(q_ref[0], qkg_ref[0:1, :], ones)
    if P:
        rows = pl.ds(pl.multiple_of(qi * TQ, TQ), TQ)
        qn = qn * cos_ref[rows, :] + _swap_pairs(qn) * sin_ref[rows, :]
    qn = qn * (DH_D ** -0.5)
    lam = lam_ref[0, 0]
    ka = ka_ref[...]
    va = va_ref[...]
    for h in range(H_D):
        probs = []
        for j in range(2):
            cs = slice((2 * h + j) * DH_D, (2 * h + j + 1) * DH_D)
            s = _dot_nt(qn[:, cs], ka[:, cs])
            e = jnp.exp(s - jnp.max(s, axis=1, keepdims=True))
            probs.append(e * (1.0 / jnp.sum(e, axis=1, keepdims=True)))
        vs = slice(h * 2 * DH_D, (h + 1) * 2 * DH_D)
        h_ref[0, :, vs] = _dot(probs[0] - lam * probs[1], va[:, vs])
    o = h_ref[0]
    ones_v = _group_ones(W_GROUP, 2 * DH_D)
    h_ref[0] = o * lax.rsqrt(_group_mean(o * o, ones_v, 2 * DH_D) + EPS) * (sg_ref[...] * out_scale)


def diff_attn_mixer(proj, cache, lp, lam_init, layer):
    B, L, _ = proj.shape
    TQ = min(L, Q_TILE)
    lv = lp['diff_lambda']
    lam = (jnp.exp(jnp.sum(lv[0] * lv[1])) - jnp.exp(jnp.sum(lv[2] * lv[3])) + lam_init).reshape(1, 1)
    qkg = jnp.tile(lp['diff_qk_norm'], (1, 2 * H_D))
    sg = jnp.tile(lp['diff_subln_g'], (H_D,)).reshape(1, W_GROUP)
    const = lambda *shape: pl.BlockSpec(shape, lambda b, qi: (0,) * len(shape))
    in_specs = [pl.BlockSpec(memory_space=pltpu.SMEM),
                _proj_block('dq', TQ), _proj_block('dk', L), _proj_block('dv', L), const(2, W_GROUP), const(1, W_GROUP)]
    args = [lam, proj, proj, proj, qkg, sg]
    out_specs = [pl.BlockSpec((1, TQ, W_GROUP), lambda b, qi: (b, qi, 0))]
    out_shape = [jax.ShapeDtypeStruct((B, L, W_GROUP), f32)]
    P = 0
    if cache is not None:
        ck, cv = cache
        P = ck.shape[2]
        cspec = pl.BlockSpec((1, 1, P, W_GROUP), lambda b, qi, layer=layer: (b, layer, 0, 0))
        cos, sin = _rope_tables(L)
        in_specs += [cspec, cspec, const(L, W_GROUP), const(L, W_GROUP)]
        args += [ck, cv, cos, sin]
    else:
        out_specs.append(pl.BlockSpec((1, L, W_GROUP), lambda b, qi: (b, 0, 0)))
        out_shape.append(jax.ShapeDtypeStruct((B, L, W_GROUP), f32))
    return pl.pallas_call(
        functools.partial(_diff_attn_kernel, seq_len=L, q_tile=TQ, past_len=P, out_scale=1.0 - lam_init),
        grid=(B, L // TQ),
        in_specs=in_specs, out_specs=out_specs, out_shape=out_shape,
        scratch_shapes=[pltpu.VMEM((P + L, W_GROUP), bf16), pltpu.VMEM((P + L, W_GROUP), bf16)],
        compiler_params=pltpu.CompilerParams(dimension_semantics=("arbitrary", "arbitrary"),
                                             vmem_limit_bytes=VMEM_LIMIT_BYTES),
        name="diff_attention",
    )(*args)


def rms_norm(x, g):
    y = x * lax.rsqrt(jnp.mean(x * x, axis=-1, keepdims=True) + EPS)
    return y * g


def expert_choice(x, router_w, w_gate, w_up, w_down):
    B, L, D = x.shape
    n = B * L
    xt = x.reshape(n, D)
    cap = CAPACITY_FACTOR * n // N_EXPERTS
    aff = jax.nn.softmax(xt @ router_w, axis=-1)
    g, idx = lax.top_k(aff.T, cap)
    xe = xt[idx]
    hdn = jax.nn.silu(jnp.einsum('ecd,edf->ecf', xe, w_gate)) * jnp.einsum('ecd,edf->ecf', xe, w_up)
    ye = jnp.einsum('ecf,efd->ecd', hdn, w_down) * g[..., None]
    y = jnp.zeros_like(xt).at[idx.reshape(-1)].add(ye.reshape(-1, D))
    return y.reshape(B, L, D)


def adaln_modulation(cvec, w, b):
    m = jax.nn.silu(cvec) @ w + b
    return jnp.split(m[:, None, :], 6, axis=-1)


def _permute_w_in(w):
    gate0 = 4 * W_GROUP
    return jnp.concatenate([w[:, :gate0], w[:, gate0 + 4 * H_M:], w[:, gate0:gate0 + 4 * H_M],
                            jnp.zeros((w.shape[0], GATE_LANES - 4 * H_M), w.dtype)], axis=1)


def token_mixers(h, lp, lam_init, ctx, layer):
    B, L, _ = h.shape
    proj = h @ _permute_w_in(lp['w_in'])
    gates_t = jnp.swapaxes(proj[:, :, 12 * W_GROUP:12 * W_GROUP + 4 * H_M], 1, 2)
    if ctx is None:
        mC0 = jnp.zeros((B, 2, H_M, DH_M, DH_M), f32)
        mn0 = jnp.zeros((B, 2, H_M, DH_M), f32)
        mm0 = jnp.zeros((B, 2, H_M), f32)
        s5r0 = jnp.zeros((B, 2, G_S5, P_S5), f32)
        s5i0 = jnp.zeros((B, 2, G_S5, P_S5), f32)
        R0 = jnp.zeros((B, 2, H_R, DH_R, DH_R), f32)
        cache = None
    else:
        mC0, mn0, mm0, s5r0, s5i0, R0, cache_k, cache_v = ctx
        cache = (cache_k.reshape(cache_k.shape[:3] + (W_GROUP,)), cache_v.reshape(cache_v.shape[:3] + (W_GROUP,)))

    hm, mC, mn, mm = mlstm_mixer(proj, gates_t, mC0, mn0, mm0, lp)
    su = proj[:, :, PROJ_BLOCKS['su'] * W_GROUP:(PROJ_BLOCKS['su'] + 1) * W_GROUP]
    ys, s5r, s5i = s5_mixer(su, s5r0, s5i0, lp)
    hr, R = retention_mixer(proj, R0, lp)
    attn = diff_attn_mixer(proj, cache, lp, lam_init, layer)
    mixed = jnp.concatenate([hm, ys, hr, attn[0]], axis=-1)
    out = mixed @ lp['w_out']
    if ctx is None:
        v = proj[:, :, PROJ_BLOCKS['dv'] * W_GROUP:(PROJ_BLOCKS['dv'] + 1) * W_GROUP]
        new_ctx = (mC, mn, mm, s5r, s5i, R, attn[1].reshape(B, L, 2 * H_D, DH_D), v.reshape(B, L, H_D, 2 * DH_D))
    else:
        new_ctx = None
    return out, new_ctx


def trunk_layer(x, mods, lp, lam_init, ctx, layer):
    sh1, sc1, g1, sh2, sc2, g2 = mods
    h = rms_norm(x, lp['norm1_g']) * (1.0 + sc1) + sh1
    out, new_ctx = token_mixers(h, lp, lam_init, ctx, layer)
    x = x + g1 * out
    h = rms_norm(x, lp['norm2_g']) * (1.0 + sc2) + sh2
    ff = expert_choice(h, lp['router_w'], lp['exp_w_gate'], lp['exp_w_up'], lp['exp_w_down'])
    x = x + g2 * ff
    return x, new_ctx


PER_LAYER = ('norm1_g', 'norm2_g', 'w_in', 'w_out', 'mlstm_gate_b', 'mlstm_norm_g', 's5_lambda_re', 's5_lambda_im',
             's5_log_step', 's5_b_re', 's5_b_im', 's5_c_re', 's5_c_im', 's5_d', 's5_glu_w', 's5_glu_b', 'ret_decay',
             'ret_gn_g', 'diff_qk_norm', 'diff_lambda', 'diff_subln_g', 'router_w', 'exp_w_gate', 'exp_w_up',
             'exp_w_down')


def kernel(x_prompt, x_sample, state_mlstm_c, state_mlstm_n, state_mlstm_m, state_s5_re, state_s5_im, state_ret, cache_diff_k, cache_diff_v, c, c_ctx, norm1_g, norm2_g, ada_w, ada_b, w_in, w_out, mlstm_gate_b, mlstm_norm_g, s5_lambda_re, s5_lambda_im, s5_log_step, s5_b_re, s5_b_im, s5_c_re, s5_c_im, s5_d, s5_glu_w, s5_glu_b, ret_decay, ret_gn_g, diff_qk_norm, diff_lambda, diff_subln_g, router_w, exp_w_gate, exp_w_up, exp_w_down):
    weights = dict(norm1_g=norm1_g, norm2_g=norm2_g, w_in=w_in, w_out=w_out, mlstm_gate_b=mlstm_gate_b,
                   mlstm_norm_g=mlstm_norm_g, s5_lambda_re=s5_lambda_re, s5_lambda_im=s5_lambda_im,
                   s5_log_step=s5_log_step, s5_b_re=s5_b_re, s5_b_im=s5_b_im, s5_c_re=s5_c_re, s5_c_im=s5_c_im,
                   s5_d=s5_d, s5_glu_w=s5_glu_w, s5_glu_b=s5_glu_b, ret_decay=ret_decay, ret_gn_g=ret_gn_g,
                   diff_qk_norm=diff_qk_norm, diff_lambda=diff_lambda, diff_subln_g=diff_subln_g,
                   router_w=router_w, exp_w_gate=exp_w_gate, exp_w_up=exp_w_up, exp_w_down=exp_w_down)
    xp, xs = x_prompt, x_sample
    outs = [[] for _ in range(8)]
    for l in range(DEPTH):
        lp = {name: weights[name][l] for name in PER_LAYER}
        lam_init = 0.8 - 0.6 * math.exp(-0.3 * l)
        mods_ctx = adaln_modulation(c_ctx[None, :], ada_w[l], ada_b[l])
        xp, ctx_t = trunk_layer(xp, mods_ctx, lp, lam_init, None, l)
        for acc, t in zip(outs, ctx_t):
            acc.append(t)
        cache_l = (state_mlstm_c[:, l], state_mlstm_n[:, l], state_mlstm_m[:, l], state_s5_re[:, l],
                   state_s5_im[:, l], state_ret[:, l], cache_diff_k, cache_diff_v)
        mods_s = adaln_modulation(c, ada_w[l], ada_b[l])
        xs, _ = trunk_layer(xs, mods_s, lp, lam_init, cache_l, l)
    return (xp, xs) + tuple(jnp.stack(o, axis=1) for o in outs)
```

```python
import functools
import math

import jax
import jax.numpy as jnp
import numpy as np
from jax import lax
from jax.experimental import pallas as pl
from jax.experimental.pallas import tpu as pltpu

D_MODEL = 1024
DEPTH = 4
GRID_W = 64
W_GROUP = 256
H_M = 4
DH_M = 64
S5_CH = 16
G_S5 = 16
P_S5 = 64
S5_STATE = G_S5 * P_S5
H_R = 4
DH_R = 64
H_D = 4
DH_D = 32
N_EXPERTS = 16
CAPACITY_FACTOR = 2
ROPE_BASE = 10000.0
EPS = 1e-6
SUBLANES = 8
VMEM_LIMIT_BYTES = 56 * 1024 * 1024

f32 = jnp.float32
bf16 = jnp.bfloat16
HIGHEST = lax.Precision.HIGHEST
NEG_INF = float("-inf")


def _gelu_tanh(x):
    return 0.5 * x * (1.0 + jnp.tanh(math.sqrt(2.0 / math.pi) * (x + 0.044715 * (x * x * x))))


def _s5_kernel(su_ref, x0r_ref, x0i_ref, wb_ref, wc_ref, lb_ref, lbl_ref, d_ref, gw_ref, gb_ref,
               y_ref, xr_ref, xi_ref, st_ref, *, seq_len):
    n_steps = seq_len // SUBLANES
    su = su_ref[0]
    y_ref[0] = su * d_ref[...]
    row = lax.broadcasted_iota(jnp.int32, (SUBLANES, S5_STATE), 0)
    zeros = jnp.zeros((SUBLANES, S5_STATE), f32)
    for d in range(2):
        st_ref[...] = jnp.dot(su, wb_ref[d], preferred_element_type=f32)
        lbr = jnp.broadcast_to(lb_ref[d, 0:1, :], (SUBLANES, S5_STATE))
        lbi = jnp.broadcast_to(lb_ref[d, 1:2, :], (SUBLANES, S5_STATE))

        def rows_of(k, d=d):
            kk = k if d == 0 else n_steps - 1 - k
            return pl.ds(pl.multiple_of(kk * SUBLANES, SUBLANES), SUBLANES)

        def scan_step(k, carry, lbr=lbr, lbi=lbi, rows_of=rows_of):
            xr, xi = carry
            r = rows_of(k)
            nxr = lbr * xr - lbi * xi + st_ref[r, 0:S5_STATE]
            nxi = lbr * xi + lbi * xr + st_ref[r, S5_STATE:2 * S5_STATE]
            st_ref[r, 0:S5_STATE] = nxr
            st_ref[r, S5_STATE:2 * S5_STATE] = nxi
            return nxr, nxi

        fr, fi = lax.fori_loop(0, n_steps, scan_step, (zeros, zeros))

        cr = x0r_ref[0, d:d + 1, :]
        ci = x0i_ref[0, d:d + 1, :]
        plr = lbl_ref[d, 0:1, :]
        pli = lbl_ref[d, 1:2, :]
        cmr, cmi = zeros, zeros
        for i in (range(SUBLANES) if d == 0 else reversed(range(SUBLANES))):
            cmr = jnp.where(row == i, cr, cmr)
            cmi = jnp.where(row == i, ci, cmi)
            cr, ci = (plr * cr - pli * ci + fr[i:i + 1], plr * ci + pli * cr + fi[i:i + 1])
        xr_ref[0, d:d + 1, :] = cr
        xi_ref[0, d:d + 1, :] = ci

        def fix_step(k, carry, lbr=lbr, lbi=lbi, cmr=cmr, cmi=cmi, rows_of=rows_of):
            pr, pi = carry
            r = rows_of(k)
            st_ref[r, 0:S5_STATE] = st_ref[r, 0:S5_STATE] + (pr * cmr - pi * cmi)
            st_ref[r, S5_STATE:2 * S5_STATE] = st_ref[r, S5_STATE:2 * S5_STATE] + (pr * cmi + pi * cmr)
            return pr * lbr - pi * lbi, pr * lbi + pi * lbr

        lax.fori_loop(0, n_steps, fix_step, (lbr, lbi))
        y_ref[0] += jnp.dot(st_ref[...], wc_ref[d], preferred_element_type=f32)

    ys = _gelu_tanh(y_ref[0])
    gate = jax.nn.sigmoid(jnp.dot(ys, gw_ref[...], preferred_element_type=f32) + gb_ref[...])
    y_ref[0] = ys * gate


def _s5_prepare(lam_re, lam_im, log_step, b_re, b_im, c_re, c_im, n_steps):
    dt = jnp.exp(log_step)[..., None]
    mag = jnp.exp(lam_re * dt)
    ang = lam_im * dt
    lb_re, lb_im = mag * jnp.cos(ang), mag * jnp.sin(ang)
    nr, ni = lb_re - 1.0, lb_im
    den = lam_re * lam_re + lam_im * lam_im
    f_re = (nr * lam_re + ni * lam_im) / den
    f_im = (ni * lam_re - nr * lam_im) / den
    bb_re = f_re[..., None] * b_re[None] - f_im[..., None] * b_im[None]
    bb_im = f_re[..., None] * b_im[None] + f_im[..., None] * b_re[None]
    eye = jnp.eye(G_S5, dtype=f32)[None, :, None, :, None]

    def block_diag(a):
        return (a[:, :, :, None, :] * eye).reshape(2, G_S5 * a.shape[2], G_S5 * a.shape[3])

    wb = jnp.concatenate([block_diag(jnp.swapaxes(bb_re, 2, 3)), block_diag(jnp.swapaxes(bb_im, 2, 3))],
                         axis=-1)
    wc = jnp.concatenate([block_diag(jnp.swapaxes(c_re, 2, 3)), -block_diag(jnp.swapaxes(c_im, 2, 3))],
                         axis=1)
    lb = jnp.stack([lb_re.reshape(2, S5_STATE), lb_im.reshape(2, S5_STATE)], axis=1)
    pr, pi = lb[:, 0], lb[:, 1]
    for _ in range(int(math.log2(n_steps))):
        pr, pi = pr * pr - pi * pi, 2.0 * pr * pi
    lbl = jnp.stack([pr, pi], axis=1)
    return wb, wc, lb, lbl


def s5_mixer(su, x0r, x0i, lp):
    B, L, _ = su.shape
    n_steps = L // SUBLANES
    assert n_steps * SUBLANES == L and 2 ** int(math.log2(n_steps)) == n_steps
    wb, wc, lb, lbl = _s5_prepare(lp['s5_lambda_re'], lp['s5_lambda_im'], lp['s5_log_step'], lp['s5_b_re'],
                                  lp['s5_b_im'], lp['s5_c_re'], lp['s5_c_im'], n_steps)
    su_p = su.reshape(B, SUBLANES, n_steps, W_GROUP).transpose(0, 2, 1, 3).reshape(B, L, W_GROUP)
    full = lambda *shape: pl.BlockSpec(shape, lambda b: (0,) * len(shape))
    per_b = lambda *shape: pl.BlockSpec((1,) + shape, lambda b: (b,) + (0,) * len(shape))
    y_p, xr, xi = pl.pallas_call(
        functools.partial(_s5_kernel, seq_len=L),
        grid=(B,),
        in_specs=[per_b(L, W_GROUP), per_b(2, S5_STATE), per_b(2, S5_STATE),
                  full(2, W_GROUP, 2 * S5_STATE), full(2, 2 * S5_STATE, W_GROUP),
                  full(2, 2, S5_STATE), full(2, 2, S5_STATE),
                  full(1, W_GROUP), full(W_GROUP, W_GROUP), full(1, W_GROUP)],
        out_specs=[per_b(L, W_GROUP), per_b(2, S5_STATE), per_b(2, S5_STATE)],
        out_shape=[jax.ShapeDtypeStruct((B, L, W_GROUP), f32),
                   jax.ShapeDtypeStruct((B, 2, S5_STATE), f32),
                   jax.ShapeDtypeStruct((B, 2, S5_STATE), f32)],
        scratch_shapes=[pltpu.VMEM((L, 2 * S5_STATE), f32)],
        compiler_params=pltpu.CompilerParams(dimension_semantics=("arbitrary",),
                                             vmem_limit_bytes=VMEM_LIMIT_BYTES),
        name="s5_mixer",
    )(su_p, x0r.reshape(B, 2, S5_STATE), x0i.reshape(B, 2, S5_STATE), wb, wc, lb, lbl,
      lp['s5_d'].reshape(1, W_GROUP), lp['s5_glu_w'], lp['s5_glu_b'].reshape(1, W_GROUP))
    y = y_p.reshape(B, n_steps, SUBLANES, W_GROUP).transpose(0, 2, 1, 3).reshape(B, L, W_GROUP)
    return y, xr.reshape(B, 2, G_S5, P_S5), xi.reshape(B, 2, G_S5, P_S5)


Q_TILE = 256
PROJ_BLOCKS = dict(mq=0, mk=1, mv=2, mo=3, su=4, rq=5, rk=6, rv=7, rg=8, dq=9, dk=10, dv=11)
GATE_LANES = 128
N_PROJ = 12 * W_GROUP + GATE_LANES


def _log_sigmoid(x):
    return jnp.minimum(x, 0.0) - jnp.log1p(jnp.exp(-jnp.abs(x)))


def _group_ones(width, group):
    shift = int(math.log2(group))
    r = lax.broadcasted_iota(jnp.int32, (width, width), 0) >> shift
    c = lax.broadcasted_iota(jnp.int32, (width, width), 1) >> shift
    return (r == c).astype(f32)


def _group_mean(x, ones, group):
    return jnp.dot(x, ones, precision=HIGHEST, preferred_element_type=f32) * (1.0 / group)


def _dot_nt(a, b):
    return lax.dot_general(a.astype(bf16), b.astype(bf16), (((1,), (1,)), ((), ())), preferred_element_type=f32)


def _dot(a, b):
    return jnp.dot(a.astype(bf16), b.astype(bf16), preferred_element_type=f32)


def _proj_block(name, rows):
    j = PROJ_BLOCKS[name]
    return pl.BlockSpec((1, rows, W_GROUP), lambda b, qi, j=j, rows=rows: (b, qi if rows == Q_TILE else 0, j))


def _mlstm_kernel(q_ref, k_ref, v_ref, o_ref, g_ref, gt_ref, gb_ref, gbt_ref, ng_ref, c0_ref, n0_ref, m0_ref,
                  h_ref, c_ref, n_ref, m_ref, glr_ref, gur_ref, *, seq_len, q_tile):
    L, TQ = seq_len, q_tile
    qi = pl.program_id(1)
    grow = gt_ref[0] + gbt_ref[...]

    @pl.when(qi == 0)
    def _():
        lfrow = _log_sigmoid(grow)
        ss = lax.broadcasted_iota(jnp.int32, (L, L), 0)
        tt = lax.broadcasted_iota(jnp.int32, (L, L), 1)
        glr_ref[...] = jnp.dot(lfrow, (ss <= tt).astype(f32), precision=HIGHEST, preferred_element_type=f32)
        gur_ref[...] = jnp.dot(lfrow, (ss >= tt).astype(f32), precision=HIGHEST, preferred_element_type=f32)

    gl_row = glr_ref[...]
    gu_row = gur_ref[...]
    t_idx = qi * TQ + lax.broadcasted_iota(jnp.int32, (TQ, L), 0)
    s_idx = lax.broadcasted_iota(jnp.int32, (TQ, L), 1)
    low = s_idx <= t_idx
    upp = s_idx >= t_idx
    lfcol = _log_sigmoid(g_ref[0] + gb_ref[...])
    gl_col = jnp.dot(low.astype(f32), lfcol, precision=HIGHEST, preferred_element_type=f32)
    gu_col = jnp.dot(upp.astype(f32), lfcol, precision=HIGHEST, preferred_element_type=f32)

    q = q_ref[0]
    k = k_ref[0] * (DH_M ** -0.5)
    v = v_ref[0]
    for h in range(H_M):
        hs = slice(h * DH_M, (h + 1) * DH_M)
        qh, kh, vh = q[:, hs], k[:, hs], v[:, hs]
        s0 = _dot_nt(qh, kh)
        h_sum = None
        for d in range(2):
            ii, fi = 8 * d + h, 8 * d + 4 + h
            g_t = (gl_col if d == 0 else gu_col)[:, fi:fi + 1]
            g_s = (gl_row if d == 0 else gu_row)[fi:fi + 1, :]
            dlog = jnp.where(low if d == 0 else upp, g_t - g_s + grow[ii:ii + 1, :], NEG_INF)
            inter = g_t + m0_ref[0, d:d + 1, h:h + 1]
            m_t = jnp.maximum(inter, jnp.max(dlog, axis=1, keepdims=True))
            p = s0 * jnp.exp(dlog - m_t)
            a = jnp.exp(inter - m_t)
            num = _dot(p, vh) + a * _dot_nt(qh, c0_ref[0, d, h])
            den = jnp.sum(p, axis=1, keepdims=True) + a * jnp.sum(qh * n0_ref[0, d, h:h + 1, :], axis=1, keepdims=True)
            hd = num / jnp.maximum(jnp.abs(den), jnp.exp(-m_t))
            h_sum = hd if h_sum is None else h_sum + hd
        y = h_sum * lax.rsqrt(jnp.mean(h_sum * h_sum, axis=1, keepdims=True) + EPS) * ng_ref[:, hs]
        h_ref[0, :, hs] = jax.nn.sigmoid(o_ref[0][:, hs]) * y

    @pl.when(qi == 0)
    def _():
        vt = v.T
        for d in range(2):
            for h in range(H_M):
                hs = slice(h * DH_M, (h + 1) * DH_M)
                ii, fi = 8 * d + h, 8 * d + 4 + h
                g_row = (gl_row if d == 0 else gu_row)[fi:fi + 1, :]
                g_tot = g_row[:, L - 1:L] if d == 0 else g_row[:, 0:1]
                wlog = g_tot - g_row + grow[ii:ii + 1, :]
                m0 = m0_ref[0, d:d + 1, h:h + 1]
                m_new = jnp.maximum(g_tot + m0, jnp.max(wlog, axis=1, keepdims=True))
                decay = jnp.exp(g_tot + m0 - m_new)
                w = jnp.exp(wlog - m_new)
                kh = k[:, hs]
                c_ref[0, d, h] = decay * c0_ref[0, d, h] + _dot(vt[hs, :] * w, kh)
                n_upd = jnp.dot(jnp.broadcast_to(w, (SUBLANES, L)), kh, precision=HIGHEST,
                                preferred_element_type=f32)[0:1, :]
                n_ref[0, d, h:h + 1, :] = decay * n0_ref[0, d, h:h + 1, :] + n_upd
                m_ref[0, d:d + 1, h:h + 1] = m_new


def mlstm_mixer(proj, gates_t, c0, n0, m0, lp):
    B, L, _ = proj.shape
    TQ = min(L, Q_TILE)
    gb = lp['mlstm_gate_b'].reshape(1, 4 * H_M)
    const = lambda *shape: pl.BlockSpec(shape, lambda b, qi: (0,) * len(shape))
    per_b = lambda *shape: pl.BlockSpec((1,) + shape, lambda b, qi: (b,) + (0,) * len(shape))
    return pl.pallas_call(
        functools.partial(_mlstm_kernel, seq_len=L, q_tile=TQ),
        grid=(B, L // TQ),
        in_specs=[_proj_block('mq', TQ), _proj_block('mk', L), _proj_block('mv', L), _proj_block('mo', TQ),
                  pl.BlockSpec((1, L, GATE_LANES), lambda b, qi: (b, 0, 12 * W_GROUP // GATE_LANES)),
                  per_b(4 * H_M, L), const(1, GATE_LANES), const(4 * H_M, 1), const(1, W_GROUP),
                  per_b(2, H_M, DH_M, DH_M), per_b(2, H_M, DH_M), per_b(2, H_M)],
        out_specs=[pl.BlockSpec((1, TQ, W_GROUP), lambda b, qi: (b, qi, 0)),
                   per_b(2, H_M, DH_M, DH_M), per_b(2, H_M, DH_M), per_b(2, H_M)],
        out_shape=[jax.ShapeDtypeStruct((B, L, W_GROUP), f32),
                   jax.ShapeDtypeStruct((B, 2, H_M, DH_M, DH_M), f32),
                   jax.ShapeDtypeStruct((B, 2, H_M, DH_M), f32),
                   jax.ShapeDtypeStruct((B, 2, H_M), f32)],
        scratch_shapes=[pltpu.VMEM((4 * H_M, L), f32), pltpu.VMEM((4 * H_M, L), f32)],
        compiler_params=pltpu.CompilerParams(dimension_semantics=("arbitrary", "arbitrary"),
                                             vmem_limit_bytes=VMEM_LIMIT_BYTES),
        name="mlstm_mixer",
    )(proj, proj, proj, proj, proj, gates_t, jnp.pad(gb, ((0, 0), (0, GATE_LANES - 4 * H_M))),
      gb.reshape(4 * H_M, 1), lp['mlstm_norm_g'].reshape(1, W_GROUP), c0, n0, m0)


def _retention_kernel(lg_ref, q_ref, k_ref, v_ref, g_ref, gn_ref, r0_ref, h_ref, r_ref, *, seq_len, q_tile):
    L, TQ = seq_len, q_tile
    qi = pl.program_id(1)
    t_col = qi * TQ + lax.broadcasted_iota(jnp.int32, (TQ, 1), 0)
    rel = (qi * TQ + lax.broadcasted_iota(jnp.int32, (TQ, L), 0)
           - lax.broadcasted_iota(jnp.int32, (TQ, L), 1)).astype(f32)
    q = q_ref[0]
    k = k_ref[0] * (DH_R ** -0.5)
    v = v_ref[0]
    ones = _group_ones(W_GROUP, DH_R)
    for h in range(H_R):
        hs = slice(h * DH_R, (h + 1) * DH_R)
        lgf, lgb = lg_ref[0, h], lg_ref[1, h]
        qh, kh, vh = q[:, hs], k[:, hs], v[:, hs]
        decay = jnp.where(rel > 0.0, jnp.exp(lgf * jnp.maximum(rel, 0.0)),
                          jnp.where(rel < 0.0, jnp.exp(lgb * jnp.maximum(-rel, 0.0)), 2.0))
        o = _dot(_dot_nt(qh, kh) * decay, vh)
        xi_f = jnp.exp(lgf * (t_col + 1).astype(f32))
        xi_b = jnp.exp(lgb * (L - t_col).astype(f32))
        o = o + xi_f * _dot(qh, r0_ref[0, 0, h]) + xi_b * _dot(qh, r0_ref[0, 1, h])
        h_ref[0, :, hs] = o
    o = h_ref[0]
    oc = o - _group_mean(o, ones, DH_R)
    y = oc * lax.rsqrt(_group_mean(oc * oc, ones, DH_R) + EPS) * gn_ref[...]
    h_ref[0] = y * jax.nn.silu(g_ref[0])

    @pl.when(qi == 0)
    def _():
        kt = k.T
        s_row = lax.broadcasted_iota(jnp.int32, (1, L), 1).astype(f32)
        for d in range(2):
            for h in range(H_R):
                hs = slice(h * DH_R, (h + 1) * DH_R)
                lg = lg_ref[d, h]
                zeta = jnp.exp(lg * ((L - 1.0) - s_row)) if d == 0 else jnp.exp(lg * s_row)
                r_ref[0, d, h] = jnp.exp(lg * float(L)) * r0_ref[0, d, h] + _dot(kt[hs, :] * zeta, v[:, hs])


def retention_mixer(proj, r0, lp):
    B, L, _ = proj.shape
    TQ = min(L, Q_TILE)
    log_gamma = -jnp.exp(lp['ret_decay'])
    per_b = lambda *shape: pl.BlockSpec((1,) + shape, lambda b, qi: (b,) + (0,) * len(shape))
    return pl.pallas_call(
        functools.partial(_retention_kernel, seq_len=L, q_tile=TQ),
        grid=(B, L // TQ),
        in_specs=[pl.BlockSpec(memory_space=pltpu.SMEM),
                  _proj_block('rq', TQ), _proj_block('rk', L), _proj_block('rv', L), _proj_block('rg', TQ),
                  pl.BlockSpec((1, W_GROUP), lambda b, qi: (0, 0)), per_b(2, H_R, DH_R, DH_R)],
        out_specs=[pl.BlockSpec((1, TQ, W_GROUP), lambda b, qi: (b, qi, 0)), per_b(2, H_R, DH_R, DH_R)],
        out_shape=[jax.ShapeDtypeStruct((B, L, W_GROUP), f32),
                   jax.ShapeDtypeStruct((B, 2, H_R, DH_R, DH_R), f32)],
        compiler_params=pltpu.CompilerParams(dimension_semantics=("arbitrary", "arbitrary"),
                                             vmem_limit_bytes=VMEM_LIMIT_BYTES),
        name="retention_mixer",
    )(log_gamma, proj, proj, proj, proj, lp['ret_gn_g'].reshape(1, W_GROUP), r0)


def _rope_tables(L):
    half = DH_D // 2
    freqs = ROPE_BASE ** (-np.arange(0, half, 2, dtype=np.float64) / half)
    pos = np.arange(L)
    row, col = (pos // GRID_W).astype(np.float64), (pos % GRID_W).astype(np.float64)
    ang = np.concatenate([np.tile(row[:, None] * freqs, (1, 2)), np.tile(col[:, None] * freqs, (1, 2))], axis=1)
    sign = np.tile(np.concatenate([-np.ones(half // 2), np.ones(half // 2)]), 2)
    cos = np.tile(np.cos(ang), (1, 2 * H_D))
    sin = np.tile(np.sin(ang) * sign, (1, 2 * H_D))
    return jnp.asarray(cos, f32), jnp.asarray(sin, f32)


def _swap_pairs(x):
    parts = []
    for j in range(x.shape[1] // 128):
        xs = x[:, j * 128:(j + 1) * 128]
        lane = lax.broadcasted_iota(jnp.int32, xs.shape, 1)
        parts.append(jnp.where((lane & 15) < 8, pltpu.roll(xs, 120, 1), pltpu.roll(xs, 8, 1)))
    return jnp.concatenate(parts, axis=1)


def _qk_norm(x, gain, ones):
    return x * lax.rsqrt(_group_mean(x * x, ones, DH_D) + EPS) * gain


def _diff_attn_kernel(lam_ref, q_ref, k_ref, v_ref, qkg_ref, sg_ref, *rest, seq_len, q_tile, past_len, out_scale):
    L, TQ, P = seq_len, q_tile, past_len
    if P:
        kc_ref, vc_ref, cos_ref, sin_ref, h_ref, ka_ref, va_ref = rest
    else:
        h_ref, kn_ref, ka_ref, va_ref = rest
    qi = pl.program_id(1)
    ones = _group_ones(W_GROUP, DH_D)

    @pl.when(qi == 0)
    def _():
        kn = _qk_norm(k_ref[0], qkg_ref[1:2, :], ones)
        if P:
            kn = kn * cos_ref[...] + _swap_pairs(kn) * sin_ref[...]
            ka_ref[0:P, :] = kc_ref[0, 0].astype(bf16)
            va_ref[0:P, :] = vc_ref[0, 0].astype(bf16)
        else:
            kn_ref[0] = kn
        ka_ref[P:P + L, :] = kn.astype(bf16)
        va_ref[P:P + L, :] = v_ref[0].astype(bf16)

    qn = _qk_norm(q_ref[0], qkg_ref[0:1, :], ones)
    if P:
        rows = pl.ds(pl.multiple_of(qi * TQ, TQ), TQ)
        qn = qn * cos_ref[rows, :] + _swap_pairs(qn) * sin_ref[rows, :]
    qn = qn * (DH_D ** -0.5)
    lam = lam_ref[0, 0]
    ka = ka_ref[...]
    va = va_ref[...]
    for h in range(H_D):
        probs = []
        for j in range(2):
            cs = slice((2 * h + j) * DH_D, (2 * h + j + 1) * DH_D)
            s = _dot_nt(qn[:, cs], ka[:, cs])
            e = jnp.exp(s - jnp.max(s, axis=1, keepdims=True))
            probs.append(e * (1.0 / jnp.sum(e, axis=1, keepdims=True)))
        vs = slice(h * 2 * DH_D, (h + 1) * 2 * DH_D)
        h_ref[0, :, vs] = _dot(probs[0] - lam * probs[1], va[:, vs])
    o = h_ref[0]
    ones_v = _group_ones(W_GROUP, 2 * DH_D)
    h_ref[0] = o * lax.rsqrt(_group_mean(o * o, ones_v, 2 * DH_D) + EPS) * (sg_ref[...] * out_scale)


def diff_attn_mixer(proj, cache, lp, lam_init, layer):
    B, L, _ = proj.shape
    TQ = min(L, Q_TILE)
    lv = lp['diff_lambda']
    lam = (jnp.exp(jnp.sum(lv[0] * lv[1])) - jnp.exp(jnp.sum(lv[2] * lv[3])) + lam_init).reshape(1, 1)
    qkg = jnp.tile(lp['diff_qk_norm'], (1, 2 * H_D))
    sg = jnp.tile(lp['diff_subln_g'], (H_D,)).reshape(1, W_GROUP)
    const = lambda *shape: pl.BlockSpec(shape, lambda b, qi: (0,) * len(shape))
    in_specs = [pl.BlockSpec(memory_space=pltpu.SMEM),
                _proj_block('dq', TQ), _proj_block('dk', L), _proj_block('dv', L), const(2, W_GROUP), const(1, W_GROUP)]
    args = [lam, proj, proj, proj, qkg, sg]
    out_specs = [pl.BlockSpec((1, TQ, W_GROUP), lambda b, qi: (b, qi, 0))]
    out_shape = [jax.ShapeDtypeStruct((B, L, W_GROUP), f32)]
    P = 0
    if cache is not None:
        ck, cv = cache
        P = ck.shape[2]
        cspec = pl.BlockSpec((1, 1, P, W_GROUP), lambda b, qi, layer=layer: (b, layer, 0, 0))
        cos, sin = _rope_tables(L)
        in_specs += [cspec, cspec, const(L, W_GROUP), const(L, W_GROUP)]
        args += [ck, cv, cos, sin]
    else:
        out_specs.append(pl.BlockSpec((1, L, W_GROUP), lambda b, qi: (b, 0, 0)))
        out_shape.append(jax.ShapeDtypeStruct((B, L, W_GROUP), f32))
    return pl.pallas_call(
        functools.partial(_diff_attn_kernel, seq_len=L, q_tile=TQ, past_len=P, out_scale=1.0 - lam_init),
        grid=(B, L // TQ),
        in_specs=in_specs, out_specs=out_specs, out_shape=out_shape,
        scratch_shapes=[pltpu.VMEM((P + L, W_GROUP), bf16), pltpu.VMEM((P + L, W_GROUP), bf16)],
        compiler_params=pltpu.CompilerParams(dimension_semantics=("arbitrary", "arbitrary"),
                                             vmem_limit_bytes=VMEM_LIMIT_BYTES),
        name="diff_attention",
    )(*args)


PROJ_ROW_TILE = 1024
PROJ_COL_TILE = 640
OUT_ROW_TILE = 512
FF_TILE = 512
D_FF = 2 * D_MODEL


def _in_proj_kernel(x_ref, g_ref, sc_ref, sh_ref, w_ref, o_ref, h_ref):
    @pl.when(pl.program_id(1) == 0)
    def _():
        x = x_ref[...]
        y = x * lax.rsqrt(jnp.mean(x * x, axis=1, keepdims=True) + EPS) * g_ref[...]
        h_ref[...] = (y * (1.0 + sc_ref[0]) + sh_ref[0]).astype(bf16)

    o_ref[...] = jnp.dot(h_ref[...], w_ref[...].astype(bf16), preferred_element_type=f32)


def in_projection(x, gain, scale, shift, w_p, rows_per_mod):
    n, D = x.shape
    TM = min(n, PROJ_ROW_TILE)
    mod = pl.BlockSpec((1, 1, D), lambda i, j: (i * TM // rows_per_mod, 0, 0))
    return pl.pallas_call(
        _in_proj_kernel,
        grid=(n // TM, N_PROJ // PROJ_COL_TILE),
        in_specs=[pl.BlockSpec((TM, D), lambda i, j: (i, 0)), pl.BlockSpec((1, D), lambda i, j: (0, 0)), mod, mod,
                  pl.BlockSpec((D, PROJ_COL_TILE), lambda i, j: (0, j))],
        out_specs=pl.BlockSpec((TM, PROJ_COL_TILE), lambda i, j: (i, j)),
        out_shape=jax.ShapeDtypeStruct((n, N_PROJ), f32),
        scratch_shapes=[pltpu.VMEM((TM, D), bf16)],
        compiler_params=pltpu.CompilerParams(dimension_semantics=("arbitrary", "arbitrary"),
                                             vmem_limit_bytes=VMEM_LIMIT_BYTES),
        name="in_projection",
    )(x, gain.reshape(1, D), scale, shift, w_p)


def _out_proj_kernel(x_ref, m0_ref, m1_ref, m2_ref, m3_ref, w_ref, g1_ref, ng_ref, sc_ref, sh_ref, rw_ref,
                     xo_ref, h_ref, aff_ref, wb_ref):
    @pl.when(pl.program_id(0) == 0)
    def _():
        wb_ref[...] = w_ref[...].astype(bf16)

    out = None
    for j, m_ref in enumerate((m0_ref, m1_ref, m2_ref, m3_ref)):
        part = jnp.dot(m_ref[...].astype(bf16), wb_ref[j * W_GROUP:(j + 1) * W_GROUP, :], preferred_element_type=f32)
        out = part if out is None else out + part
    x = x_ref[...] + g1_ref[0] * out
    xo_ref[...] = x
    h = x * lax.rsqrt(jnp.mean(x * x, axis=1, keepdims=True) + EPS) * ng_ref[...]
    h = h * (1.0 + sc_ref[0]) + sh_ref[0]
    h_ref[...] = h.astype(bf16)
    logits = jnp.dot(h, rw_ref[...], precision=HIGHEST, preferred_element_type=f32)
    e = jnp.exp(logits - jnp.max(logits, axis=1, keepdims=True))
    aff_ref[...] = e / jnp.sum(e, axis=1, keepdims=True)


def out_projection(x, mixed, w_out, gate1, gain2, scale2, shift2, router_w, rows_per_mod):
    n, D = x.shape
    TM = OUT_ROW_TILE
    row = lambda width: pl.BlockSpec((TM, width), lambda i: (i, 0))
    const = lambda *shape: pl.BlockSpec(shape, lambda i: (0,) * len(shape))
    mod = pl.BlockSpec((1, 1, D), lambda i: (i * TM // rows_per_mod, 0, 0))
    return pl.pallas_call(
        _out_proj_kernel,
        grid=(n // TM,),
        in_specs=[row(D), row(W_GROUP), row(W_GROUP), row(W_GROUP), row(W_GROUP), const(D, D), mod, const(1, D),
                  mod, mod, const(D, N_EXPERTS)],
        out_specs=[row(D), row(D), row(N_EXPERTS)],
        out_shape=[jax.ShapeDtypeStruct((n, D), f32), jax.ShapeDtypeStruct((n, D), bf16),
                   jax.ShapeDtypeStruct((n, N_EXPERTS), f32)],
        scratch_shapes=[pltpu.VMEM((D, D), bf16)],
        compiler_params=pltpu.CompilerParams(dimension_semantics=("arbitrary",), vmem_limit_bytes=VMEM_LIMIT_BYTES),
        name="out_projection",
    )(x, *mixed, w_out, gate1, gain2.reshape(1, D), scale2, shift2, router_w)


def _experts_kernel(xc_ref, xl_ref, gc_ref, gl_ref, wg_ref, wu_ref, wd_ref, yc_ref, yl_ref, ac_ref, al_ref):
    f = pl.program_id(1)
    wg = wg_ref[0].astype(bf16)
    wu = wu_ref[0].astype(bf16)
    wd = wd_ref[0].astype(bf16)
    for x_ref, acc_ref in ((xc_ref, ac_ref), (xl_ref, al_ref)):
        x = x_ref[0]
        hidden = jax.nn.silu(jnp.dot(x, wg, preferred_element_type=f32)) * jnp.dot(x, wu, preferred_element_type=f32)
        part = jnp.dot(hidden.astype(bf16), wd, preferred_element_type=f32)

        @pl.when(f == 0)
        def _(acc_ref=acc_ref, part=part):
            acc_ref[...] = part

        @pl.when(f > 0)
        def _(acc_ref=acc_ref, part=part):
            acc_ref[...] += part

    @pl.when(f == pl.num_programs(1) - 1)
    def _():
        yc_ref[0] = ac_ref[...] * gc_ref[0]
        yl_ref[0] = al_ref[...] * gl_ref[0]


def expert_ffn(xe_c, xe_l, g_c, g_l, w_gate, w_up, w_down):
    E, Cc, D = xe_c.shape
    Cl = xe_l.shape[1]
    tok = lambda C, width: pl.BlockSpec((1, C, width), lambda e, f: (e, 0, 0))
    return pl.pallas_call(
        _experts_kernel,
        grid=(E, D_FF // FF_TILE),
        in_specs=[tok(Cc, D), tok(Cl, D), tok(Cc, 1), tok(Cl, 1),
                  pl.BlockSpec((1, D, FF_TILE), lambda e, f: (e, 0, f)),
                  pl.BlockSpec((1, D, FF_TILE), lambda e, f: (e, 0, f)),
                  pl.BlockSpec((1, FF_TILE, D), lambda e, f: (e, f, 0))],
        out_specs=[tok(Cc, D), tok(Cl, D)],
        out_shape=[jax.ShapeDtypeStruct((E, Cc, D), f32), jax.ShapeDtypeStruct((E, Cl, D), f32)],
        scratch_shapes=[pltpu.VMEM((Cc, D), f32), pltpu.VMEM((Cl, D), f32)],
        compiler_params=pltpu.CompilerParams(dimension_semantics=("arbitrary", "arbitrary"),
                                             vmem_limit_bytes=VMEM_LIMIT_BYTES),
        name="expert_ffn",
    )(xe_c, xe_l, g_c, g_l, w_gate, w_up, w_down)


def _permute_w_in(w):
    gate0 = 4 * W_GROUP
    return jnp.concatenate([w[:, :gate0], w[:, gate0 + 4 * H_M:], w[:, gate0:gate0 + 4 * H_M],
                            jnp.zeros((w.shape[0], GATE_LANES - 4 * H_M), w.dtype)], axis=1)


def token_mixers(proj, lp, lam_init, states, cache, layer):
    B, L, _ = proj.shape
    mC0, mn0, mm0, s5r0, s5i0, R0 = states
    gates_t = jnp.swapaxes(proj[:, :, 12 * W_GROUP:12 * W_GROUP + 4 * H_M], 1, 2)
    hm, mC, mn, mm = mlstm_mixer(proj, gates_t, mC0, mn0, mm0, lp)
    su = proj[:, :, PROJ_BLOCKS['su'] * W_GROUP:(PROJ_BLOCKS['su'] + 1) * W_GROUP]
    ys, s5r, s5i = s5_mixer(su, s5r0, s5i0, lp)
    hr, R = retention_mixer(proj, R0, lp)
    attn = diff_attn_mixer(proj, cache, lp, lam_init, layer)
    mixed = [a.reshape(B * L, W_GROUP) for a in (hm, ys, hr, attn[0])]
    new_ctx = None
    if cache is None:
        v = proj[:, :, PROJ_BLOCKS['dv'] * W_GROUP:(PROJ_BLOCKS['dv'] + 1) * W_GROUP]
        new_ctx = (mC, mn, mm, s5r, s5i, R, attn[1].reshape(B, L, 2 * H_D, DH_D), v.reshape(B, L, H_D, 2 * DH_D))
    return mixed, new_ctx


def _route(aff, h2):
    n = aff.shape[0]
    gates, idx = lax.top_k(aff.T, CAPACITY_FACTOR * n // N_EXPERTS)
    return gates[..., None], idx, h2[idx]


def _combine(x, ye, idx, gate2, B):
    n, D = x.shape
    y = jnp.zeros_like(x).at[idx.reshape(-1)].add(ye.reshape(-1, D))
    return (x.reshape(B, n // B, D) + gate2 * y.reshape(B, n // B, D)).reshape(n, D)


PER_LAYER = ('norm1_g', 'norm2_g', 'w_in', 'w_out', 'mlstm_gate_b', 'mlstm_norm_g', 's5_lambda_re', 's5_lambda_im',
             's5_log_step', 's5_b_re', 's5_b_im', 's5_c_re', 's5_c_im', 's5_d', 's5_glu_w', 's5_glu_b', 'ret_decay',
             'ret_gn_g', 'diff_qk_norm', 'diff_lambda', 'diff_subln_g', 'router_w', 'exp_w_gate', 'exp_w_up',
             'exp_w_down')


def kernel(x_prompt, x_sample, state_mlstm_c, state_mlstm_n, state_mlstm_m, state_s5_re, state_s5_im, state_ret, cache_diff_k, cache_diff_v, c, c_ctx, norm1_g, norm2_g, ada_w, ada_b, w_in, w_out, mlstm_gate_b, mlstm_norm_g, s5_lambda_re, s5_lambda_im, s5_log_step, s5_b_re, s5_b_im, s5_c_re, s5_c_im, s5_d, s5_glu_w, s5_glu_b, ret_decay, ret_gn_g, diff_qk_norm, diff_lambda, diff_subln_g, router_w, exp_w_gate, exp_w_up, exp_w_down):
    weights = dict(norm1_g=norm1_g, norm2_g=norm2_g, w_in=w_in, w_out=w_out, mlstm_gate_b=mlstm_gate_b,
                   mlstm_norm_g=mlstm_norm_g, s5_lambda_re=s5_lambda_re, s5_lambda_im=s5_lambda_im,
                   s5_log_step=s5_log_step, s5_b_re=s5_b_re, s5_b_im=s5_b_im, s5_c_re=s5_c_re, s5_c_im=s5_c_im,
                   s5_d=s5_d, s5_glu_w=s5_glu_w, s5_glu_b=s5_glu_b, ret_decay=ret_decay, ret_gn_g=ret_gn_g,
                   diff_qk_norm=diff_qk_norm, diff_lambda=diff_lambda, diff_subln_g=diff_subln_g,
                   router_w=router_w, exp_w_gate=exp_w_gate, exp_w_up=exp_w_up, exp_w_down=exp_w_down)
    Bc, Lc, D = x_prompt.shape
    Bl, Ll, _ = x_sample.shape
    xc = x_prompt.reshape(Bc * Lc, D)
    xl = x_sample.reshape(Bl * Ll, D)
    zero_states = (jnp.zeros((Bc, 2, H_M, DH_M, DH_M), f32), jnp.zeros((Bc, 2, H_M, DH_M), f32),
                   jnp.zeros((Bc, 2, H_M), f32), jnp.zeros((Bc, 2, G_S5, P_S5), f32),
                   jnp.zeros((Bc, 2, G_S5, P_S5), f32), jnp.zeros((Bc, 2, H_R, DH_R, DH_R), f32))
    cache = (cache_diff_k.reshape(cache_diff_k.shape[:3] + (W_GROUP,)),
             cache_diff_v.reshape(cache_diff_v.shape[:3] + (W_GROUP,)))
    cvec = jnp.concatenate([c_ctx[None, :], c], axis=0)
    outs = [[] for _ in range(8)]
    for l in range(DEPTH):
        lp = {name: weights[name][l] for name in PER_LAYER}
        lam_init = 0.8 - 0.6 * math.exp(-0.3 * l)
        mods = jnp.split((jax.nn.silu(cvec) @ ada_w[l] + ada_b[l])[:, None, :], 6, axis=-1)
        w_in_p = _permute_w_in(lp['w_in'])
        lat_states = (state_mlstm_c[:, l], state_mlstm_n[:, l], state_mlstm_m[:, l], state_s5_re[:, l],
                      state_s5_im[:, l], state_ret[:, l])
        routed = []
        for x, B, L, sel, states, kv in ((xc, Bc, Lc, slice(0, 1), zero_states, None),
                                        (xl, Bl, Ll, slice(1, 1 + Bl), lat_states, cache)):
            sh1, sc1, g1, sh2, sc2, g2 = (m[sel] for m in mods)
            rows_per_mod = x.shape[0] // sh1.shape[0]
            proj = in_projection(x, lp['norm1_g'], sc1, sh1, w_in_p, rows_per_mod).reshape(B, L, N_PROJ)
            mixed, new_ctx = token_mixers(proj, lp, lam_init, states, kv, l)
            if new_ctx is not None:
                for acc, t in zip(outs, new_ctx):
                    acc.append(t)
            x1, h2, aff = out_projection(x, mixed, lp['w_out'], g1, lp['norm2_g'], sc2, sh2, lp['router_w'],
                                         rows_per_mod)
            routed.append((x1, g2, sh1.shape[0]) + _route(aff, h2))
        (x1c, g2c, nbc, gc, idxc, xec), (x1l, g2l, nbl, gl, idxl, xel) = routed
        yec, yel = expert_ffn(xec, xel, gc, gl, lp['exp_w_gate'], lp['exp_w_up'], lp['exp_w_down'])
        xc = _combine(x1c, yec, idxc, g2c, nbc)
        xl = _combine(x1l, yel, idxl, g2l, nbl)
    return (xc.reshape(Bc, Lc, D), xl.reshape(Bl, Ll, D)) + tuple(jnp.stack(o, axis=1) for o in outs)
```

```python
import functools
import math

import jax
import jax.numpy as jnp
import numpy as np
from jax import lax
from jax.experimental import pallas as pl
from jax.experimental.pallas import tpu as pltpu

D_MODEL = 1024
DEPTH = 4
GRID_W = 64
W_GROUP = 256
H_M = 4
DH_M = 64
S5_CH = 16
G_S5 = 16
P_S5 = 64
S5_STATE = G_S5 * P_S5
H_R = 4
DH_R = 64
H_D = 4
DH_D = 32
N_EXPERTS = 16
CAPACITY_FACTOR = 2
ROPE_BASE = 10000.0
EPS = 1e-6
SUBLANES = 8
VMEM_LIMIT_BYTES = 56 * 1024 * 1024

f32 = jnp.float32
bf16 = jnp.bfloat16
HIGHEST = lax.Precision.HIGHEST
NEG_INF = float("-inf")


def _gelu_tanh(x):
    return 0.5 * x * (1.0 + jnp.tanh(math.sqrt(2.0 / math.pi) * (x + 0.044715 * (x * x * x))))


def _s5_kernel(su_ref, x0r_ref, x0i_ref, wb_ref, wc_ref, lb_ref, lbl_ref, d_ref, gw_ref, gb_ref,
               y_ref, xr_ref, xi_ref, st_ref, *, seq_len):
    n_steps = seq_len // SUBLANES
    su = su_ref[0]
    y_ref[0] = su * d_ref[...]
    row = lax.broadcasted_iota(jnp.int32, (SUBLANES, S5_STATE), 0)
    zeros = jnp.zeros((SUBLANES, S5_STATE), f32)
    for d in range(2):
        st_ref[...] = jnp.dot(su, wb_ref[d], preferred_element_type=f32)
        lbr = jnp.broadcast_to(lb_ref[d, 0:1, :], (SUBLANES, S5_STATE))
        lbi = jnp.broadcast_to(lb_ref[d, 1:2, :], (SUBLANES, S5_STATE))

        def rows_of(k, d=d):
            kk = k if d == 0 else n_steps - 1 - k
            return pl.ds(pl.multiple_of(kk * SUBLANES, SUBLANES), SUBLANES)

        def scan_step(k, carry, lbr=lbr, lbi=lbi, rows_of=rows_of):
            xr, xi = carry
            r = rows_of(k)
            nxr = lbr * xr - lbi * xi + st_ref[r, 0:S5_STATE]
            nxi = lbr * xi + lbi * xr + st_ref[r, S5_STATE:2 * S5_STATE]
            st_ref[r, 0:S5_STATE] = nxr
            st_ref[r, S5_STATE:2 * S5_STATE] = nxi
            return nxr, nxi

        fr, fi = lax.fori_loop(0, n_steps, scan_step, (zeros, zeros))

        cr = x0r_ref[0, d:d + 1, :]
        ci = x0i_ref[0, d:d + 1, :]
        plr = lbl_ref[d, 0:1, :]
        pli = lbl_ref[d, 1:2, :]
        cmr, cmi = zeros, zeros
        for i in (range(SUBLANES) if d == 0 else reversed(range(SUBLANES))):
            cmr = jnp.where(row == i, cr, cmr)
            cmi = jnp.where(row == i, ci, cmi)
            cr, ci = (plr * cr - pli * ci + fr[i:i + 1], plr * ci + pli * cr + fi[i:i + 1])
        xr_ref[0, d:d + 1, :] = cr
        xi_ref[0, d:d + 1, :] = ci

        def fix_step(k, carry, lbr=lbr, lbi=lbi, cmr=cmr, cmi=cmi, rows_of=rows_of):
            pr, pi = carry
            r = rows_of(k)
            st_ref[r, 0:S5_STATE] = st_ref[r, 0:S5_STATE] + (pr * cmr - pi * cmi)
            st_ref[r, S5_STATE:2 * S5_STATE] = st_ref[r, S5_STATE:2 * S5_STATE] + (pr * cmi + pi * cmr)
            return pr * lbr - pi * lbi, pr * lbi + pi * lbr

        lax.fori_loop(0, n_steps, fix_step, (lbr, lbi))
        y_ref[0] += jnp.dot(st_ref[...], wc_ref[d], preferred_element_type=f32)

    ys = _gelu_tanh(y_ref[0])
    gate = jax.nn.sigmoid(jnp.dot(ys, gw_ref[...], preferred_element_type=f32) + gb_ref[...])
    y_ref[0] = ys * gate


def _s5_prepare(lam_re, lam_im, log_step, b_re, b_im, c_re, c_im, n_steps):
    dt = jnp.exp(log_step)[..., None]
    mag = jnp.exp(lam_re * dt)
    ang = lam_im * dt
    lb_re, lb_im = mag * jnp.cos(ang), mag * jnp.sin(ang)
    nr, ni = lb_re - 1.0, lb_im
    den = lam_re * lam_re + lam_im * lam_im
    f_re = (nr * lam_re + ni * lam_im) / den
    f_im = (ni * lam_re - nr * lam_im) / den
    bb_re = f_re[..., None] * b_re[None] - f_im[..., None] * b_im[None]
    bb_im = f_re[..., None] * b_im[None] + f_im[..., None] * b_re[None]
    eye = jnp.eye(G_S5, dtype=f32)[None, :, None, :, None]

    def block_diag(a):
        return (a[:, :, :, None, :] * eye).reshape(2, G_S5 * a.shape[2], G_S5 * a.shape[3])

    wb = jnp.concatenate([block_diag(jnp.swapaxes(bb_re, 2, 3)), block_diag(jnp.swapaxes(bb_im, 2, 3))],
                         axis=-1)
    wc = jnp.concatenate([block_diag(jnp.swapaxes(c_re, 2, 3)), -block_diag(jnp.swapaxes(c_im, 2, 3))],
                         axis=1)
    lb = jnp.stack([lb_re.reshape(2, S5_STATE), lb_im.reshape(2, S5_STATE)], axis=1)
    pr, pi = lb[:, 0], lb[:, 1]
    for _ in range(int(math.log2(n_steps))):
        pr, pi = pr * pr - pi * pi, 2.0 * pr * pi
    lbl = jnp.stack([pr, pi], axis=1)
    return wb, wc, lb, lbl


def s5_mixer(su, x0r, x0i, lp):
    B, L, _ = su.shape
    n_steps = L // SUBLANES
    assert n_steps * SUBLANES == L and 2 ** int(math.log2(n_steps)) == n_steps
    wb, wc, lb, lbl = _s5_prepare(lp['s5_lambda_re'], lp['s5_lambda_im'], lp['s5_log_step'], lp['s5_b_re'],
                                  lp['s5_b_im'], lp['s5_c_re'], lp['s5_c_im'], n_steps)
    su_p = su.reshape(B, SUBLANES, n_steps, W_GROUP).transpose(0, 2, 1, 3).reshape(B, L, W_GROUP)
    full = lambda *shape: pl.BlockSpec(shape, lambda b: (0,) * len(shape))
    per_b = lambda *shape: pl.BlockSpec((1,) + shape, lambda b: (b,) + (0,) * len(shape))
    y_p, xr, xi = pl.pallas_call(
        functools.partial(_s5_kernel, seq_len=L),
        grid=(B,),
        in_specs=[per_b(L, W_GROUP), per_b(2, S5_STATE), per_b(2, S5_STATE),
                  full(2, W_GROUP, 2 * S5_STATE), full(2, 2 * S5_STATE, W_GROUP),
                  full(2, 2, S5_STATE), full(2, 2, S5_STATE),
                  full(1, W_GROUP), full(W_GROUP, W_GROUP), full(1, W_GROUP)],
        out_specs=[per_b(L, W_GROUP), per_b(2, S5_STATE), per_b(2, S5_STATE)],
        out_shape=[jax.ShapeDtypeStruct((B, L, W_GROUP), f32),
                   jax.ShapeDtypeStruct((B, 2, S5_STATE), f32),
                   jax.ShapeDtypeStruct((B, 2, S5_STATE), f32)],
        scratch_shapes=[pltpu.VMEM((L, 2 * S5_STATE), f32)],
        compiler_params=pltpu.CompilerParams(dimension_semantics=("arbitrary",),
                                             vmem_limit_bytes=VMEM_LIMIT_BYTES),
        name="s5_mixer",
    )(su_p, x0r.reshape(B, 2, S5_STATE), x0i.reshape(B, 2, S5_STATE), wb, wc, lb, lbl,
      lp['s5_d'].reshape(1, W_GROUP), lp['s5_glu_w'], lp['s5_glu_b'].reshape(1, W_GROUP))
    y = y_p.reshape(B, n_steps, SUBLANES, W_GROUP).transpose(0, 2, 1, 3).reshape(B, L, W_GROUP)
    return y, xr.reshape(B, 2, G_S5, P_S5), xi.reshape(B, 2, G_S5, P_S5)


Q_TILE = 256
PROJ_BLOCKS = dict(mq=0, mk=1, mv=2, mo=3, su=4, rq=5, rk=6, rv=7, rg=8, dq=9, dk=10, dv=11)
GATE_LANES = 128
N_PROJ = 12 * W_GROUP + GATE_LANES


def _log_sigmoid(x):
    return jnp.minimum(x, 0.0) - jnp.log1p(jnp.exp(-jnp.abs(x)))


def _group_ones(width, group):
    shift = int(math.log2(group))
    r = lax.broadcasted_iota(jnp.int32, (width, width), 0) >> shift
    c = lax.broadcasted_iota(jnp.int32, (width, width), 1) >> shift
    return (r == c).astype(f32)


def _group_mean(x, ones, group):
    return jnp.dot(x, ones, precision=HIGHEST, preferred_element_type=f32) * (1.0 / group)


def _dot_nt(a, b):
    return lax.dot_general(a.astype(bf16), b.astype(bf16), (((1,), (1,)), ((), ())), preferred_element_type=f32)


def _dot(a, b):
    return jnp.dot(a.astype(bf16), b.astype(bf16), preferred_element_type=f32)


def _proj_block(name, rows):
    j = PROJ_BLOCKS[name]
    return pl.BlockSpec((1, rows, W_GROUP), lambda b, qi, j=j, rows=rows: (b, qi if rows == Q_TILE else 0, j))


def _mlstm_kernel(q_ref, k_ref, v_ref, o_ref, g_ref, gt_ref, gb_ref, gbt_ref, ng_ref, c0_ref, n0_ref, m0_ref,
                  h_ref, c_ref, n_ref, m_ref, glr_ref, gur_ref, *, seq_len, q_tile):
    L, TQ = seq_len, q_tile
    qi = pl.program_id(1)
    grow = gt_ref[0] + gbt_ref[...]

    @pl.when(qi == 0)
    def _():
        lfrow = _log_sigmoid(grow)
        ss = lax.broadcasted_iota(jnp.int32, (L, L), 0)
        tt = lax.broadcasted_iota(jnp.int32, (L, L), 1)
        glr_ref[...] = jnp.dot(lfrow, (ss <= tt).astype(f32), precision=HIGHEST, preferred_element_type=f32)
        gur_ref[...] = jnp.dot(lfrow, (ss >= tt).astype(f32), precision=HIGHEST, preferred_element_type=f32)

    gl_row = glr_ref[...]
    gu_row = gur_ref[...]
    t_idx = qi * TQ + lax.broadcasted_iota(jnp.int32, (TQ, L), 0)
    s_idx = lax.broadcasted_iota(jnp.int32, (TQ, L), 1)
    low = s_idx <= t_idx
    upp = s_idx >= t_idx
    lfcol = _log_sigmoid(g_ref[0] + gb_ref[...])
    gl_col = jnp.dot(low.astype(f32), lfcol, precision=HIGHEST, preferred_element_type=f32)
    gu_col = jnp.dot(upp.astype(f32), lfcol, precision=HIGHEST, preferred_element_type=f32)

    q = q_ref[0]
    k = k_ref[0] * (DH_M ** -0.5)
    v = v_ref[0]
    for h in range(H_M):
        hs = slice(h * DH_M, (h + 1) * DH_M)
        qh, kh, vh = q[:, hs], k[:, hs], v[:, hs]
        s0 = _dot_nt(qh, kh)
        h_sum = None
        for d in range(2):
            ii, fi = 8 * d + h, 8 * d + 4 + h
            g_t = (gl_col if d == 0 else gu_col)[:, fi:fi + 1]
            g_s = (gl_row if d == 0 else gu_row)[fi:fi + 1, :]
            dlog = jnp.where(low if d == 0 else upp, g_t - g_s + grow[ii:ii + 1, :], NEG_INF)
            inter = g_t + m0_ref[0, d:d + 1, h:h + 1]
            m_t = jnp.maximum(inter, jnp.max(dlog, axis=1, keepdims=True))
            p = s0 * jnp.exp(dlog - m_t)
            a = jnp.exp(inter - m_t)
            num = _dot(p, vh) + a * _dot_nt(qh, c0_ref[0, d, h])
            den = jnp.sum(p, axis=1, keepdims=True) + a * jnp.sum(qh * n0_ref[0, d, h:h + 1, :], axis=1, keepdims=True)
            hd = num / jnp.maximum(jnp.abs(den), jnp.exp(-m_t))
            h_sum = hd if h_sum is None else h_sum + hd
        y = h_sum * lax.rsqrt(jnp.mean(h_sum * h_sum, axis=1, keepdims=True) + EPS) * ng_ref[:, hs]
        h_ref[0, :, hs] = jax.nn.sigmoid(o_ref[0][:, hs]) * y

    @pl.when(qi == 0)
    def _():
        vt = v.T
        for d in range(2):
            for h in range(H_M):
                hs = slice(h * DH_M, (h + 1) * DH_M)
                ii, fi = 8 * d + h, 8 * d + 4 + h
                g_row = (gl_row if d == 0 else gu_row)[fi:fi + 1, :]
                g_tot = g_row[:, L - 1:L] if d == 0 else g_row[:, 0:1]
                wlog = g_tot - g_row + grow[ii:ii + 1, :]
                m0 = m0_ref[0, d:d + 1, h:h + 1]
                m_new = jnp.maximum(g_tot + m0, jnp.max(wlog, axis=1, keepdims=True))
                decay = jnp.exp(g_tot + m0 - m_new)
                w = jnp.exp(wlog - m_new)
                kh = k[:, hs]
                c_ref[0, d, h] = decay * c0_ref[0, d, h] + _dot(vt[hs, :] * w, kh)
                n_upd = jnp.dot(jnp.broadcast_to(w, (SUBLANES, L)), kh, precision=HIGHEST,
                                preferred_element_type=f32)[0:1, :]
                n_ref[0, d, h:h + 1, :] = decay * n0_ref[0, d, h:h + 1, :] + n_upd
                m_ref[0, d:d + 1, h:h + 1] = m_new


def mlstm_mixer(proj, gates_t, c0, n0, m0, lp):
    B, L, _ = proj.shape
    TQ = min(L, Q_TILE)
    gb = lp['mlstm_gate_b'].reshape(1, 4 * H_M)
    const = lambda *shape: pl.BlockSpec(shape, lambda b, qi: (0,) * len(shape))
    per_b = lambda *shape: pl.BlockSpec((1,) + shape, lambda b, qi: (b,) + (0,) * len(shape))
    return pl.pallas_call(
        functools.partial(_mlstm_kernel, seq_len=L, q_tile=TQ),
        grid=(B, L // TQ),
        in_specs=[_proj_block('mq', TQ), _proj_block('mk', L), _proj_block('mv', L), _proj_block('mo', TQ),
                  pl.BlockSpec((1, L, GATE_LANES), lambda b, qi: (b, 0, 12 * W_GROUP // GATE_LANES)),
                  per_b(4 * H_M, L), const(1, GATE_LANES), const(4 * H_M, 1), const(1, W_GROUP),
                  per_b(2, H_M, DH_M, DH_M), per_b(2, H_M, DH_M), per_b(2, H_M)],
        out_specs=[pl.BlockSpec((1, TQ, W_GROUP), lambda b, qi: (b, qi, 0)),
                   per_b(2, H_M, DH_M, DH_M), per_b(2, H_M, DH_M), per_b(2, H_M)],
        out_shape=[jax.ShapeDtypeStruct((B, L, W_GROUP), f32),
                   jax.ShapeDtypeStruct((B, 2, H_M, DH_M, DH_M), f32),
                   jax.ShapeDtypeStruct((B, 2, H_M, DH_M), f32),
                   jax.ShapeDtypeStruct((B, 2, H_M), f32)],
        scratch_shapes=[pltpu.VMEM((4 * H_M, L), f32), pltpu.VMEM((4 * H_M, L), f32)],
        compiler_params=pltpu.CompilerParams(dimension_semantics=("arbitrary", "arbitrary"),
                                             vmem_limit_bytes=VMEM_LIMIT_BYTES),
        name="mlstm_mixer",
    )(proj, proj, proj, proj, proj, gates_t, jnp.pad(gb, ((0, 0), (0, GATE_LANES - 4 * H_M))),
      gb.reshape(4 * H_M, 1), lp['mlstm_norm_g'].reshape(1, W_GROUP), c0, n0, m0)


def _retention_kernel(lg_ref, q_ref, k_ref, v_ref, g_ref, gn_ref, r0_ref, h_ref, r_ref, *, seq_len, q_tile):
    L, TQ = seq_len, q_tile
    qi = pl.program_id(1)
    t_col = qi * TQ + lax.broadcasted_iota(jnp.int32, (TQ, 1), 0)
    rel = (qi * TQ + lax.broadcasted_iota(jnp.int32, (TQ, L), 0)
           - lax.broadcasted_iota(jnp.int32, (TQ, L), 1)).astype(f32)
    q = q_ref[0]
    k = k_ref[0] * (DH_R ** -0.5)
    v = v_ref[0]
    ones = _group_ones(W_GROUP, DH_R)
    for h in range(H_R):
        hs = slice(h * DH_R, (h + 1) * DH_R)
        lgf, lgb = lg_ref[0, h], lg_ref[1, h]
        qh, kh, vh = q[:, hs], k[:, hs], v[:, hs]
        decay = jnp.where(rel > 0.0, jnp.exp(lgf * jnp.maximum(rel, 0.0)),
                          jnp.where(rel < 0.0, jnp.exp(lgb * jnp.maximum(-rel, 0.0)), 2.0))
        o = _dot(_dot_nt(qh, kh) * decay, vh)
        xi_f = jnp.exp(lgf * (t_col + 1).astype(f32))
        xi_b = jnp.exp(lgb * (L - t_col).astype(f32))
        o = o + xi_f * _dot(qh, r0_ref[0, 0, h]) + xi_b * _dot(qh, r0_ref[0, 1, h])
        h_ref[0, :, hs] = o
    o = h_ref[0]
    oc = o - _group_mean(o, ones, DH_R)
    y = oc * lax.rsqrt(_group_mean(oc * oc, ones, DH_R) + EPS) * gn_ref[...]
    h_ref[0] = y * jax.nn.silu(g_ref[0])

    @pl.when(qi == 0)
    def _():
        kt = k.T
        s_row = lax.broadcasted_iota(jnp.int32, (1, L), 1).astype(f32)
        for d in range(2):
            for h in range(H_R):
                hs = slice(h * DH_R, (h + 1) * DH_R)
                lg = lg_ref[d, h]
                zeta = jnp.exp(lg * ((L - 1.0) - s_row)) if d == 0 else jnp.exp(lg * s_row)
                r_ref[0, d, h] = jnp.exp(lg * float(L)) * r0_ref[0, d, h] + _dot(kt[hs, :] * zeta, v[:, hs])


def retention_mixer(proj, r0, lp):
    B, L, _ = proj.shape
    TQ = min(L, Q_TILE)
    log_gamma = -jnp.exp(lp['ret_decay'])
    per_b = lambda *shape: pl.BlockSpec((1,) + shape, lambda b, qi: (b,) + (0,) * len(shape))
    return pl.pallas_call(
        functools.partial(_retention_kernel, seq_len=L, q_tile=TQ),
        grid=(B, L // TQ),
        in_specs=[pl.BlockSpec(memory_space=pltpu.SMEM),
                  _proj_block('rq', TQ), _proj_block('rk', L), _proj_block('rv', L), _proj_block('rg', TQ),
                  pl.BlockSpec((1, W_GROUP), lambda b, qi: (0, 0)), per_b(2, H_R, DH_R, DH_R)],
        out_specs=[pl.BlockSpec((1, TQ, W_GROUP), lambda b, qi: (b, qi, 0)), per_b(2, H_R, DH_R, DH_R)],
        out_shape=[jax.ShapeDtypeStruct((B, L, W_GROUP), f32),
                   jax.ShapeDtypeStruct((B, 2, H_R, DH_R, DH_R), f32)],
        compiler_params=pltpu.CompilerParams(dimension_semantics=("arbitrary", "arbitrary"),
                                             vmem_limit_bytes=VMEM_LIMIT_BYTES),
        name="retention_mixer",
    )(log_gamma, proj, proj, proj, proj, lp['ret_gn_g'].reshape(1, W_GROUP), r0)


def _rope_tables(L):
    half = DH_D // 2
    freqs = ROPE_BASE ** (-np.arange(0, half, 2, dtype=np.float64) / half)
    pos = np.arange(L)
    row, col = (pos // GRID_W).astype(np.float64), (pos % GRID_W).astype(np.float64)
    ang = np.concatenate([np.tile(row[:, None] * freqs, (1, 2)), np.tile(col[:, None] * freqs, (1, 2))], axis=1)
    sign = np.tile(np.concatenate([-np.ones(half // 2), np.ones(half // 2)]), 2)
    cos = np.tile(np.cos(ang), (1, 2 * H_D))
    sin = np.tile(np.sin(ang) * sign, (1, 2 * H_D))
    return jnp.asarray(cos, f32), jnp.asarray(sin, f32)


def _swap_pairs(x):
    parts = []
    for j in range(x.shape[1] // 128):
        xs = x[:, j * 128:(j + 1) * 128]
        lane = lax.broadcasted_iota(jnp.int32, xs.shape, 1)
        parts.append(jnp.where((lane & 15) < 8, pltpu.roll(xs, 120, 1), pltpu.roll(xs, 8, 1)))
    return jnp.concatenate(parts, axis=1)


def _qk_norm(x, gain, ones):
    return x * lax.rsqrt(_group_mean(x * x, ones, DH_D) + EPS) * gain


def _diff_attn_kernel(lam_ref, q_ref, k_ref, v_ref, qkg_ref, sg_ref, *rest, seq_len, q_tile, past_len, out_scale):
    L, TQ, P = seq_len, q_tile, past_len
    if P:
        kc_ref, vc_ref, cos_ref, sin_ref, h_ref, ka_ref, va_ref = rest
    else:
        h_ref, kn_ref, ka_ref, va_ref = rest
    qi = pl.program_id(1)
    ones = _group_ones(W_GROUP, DH_D)

    @pl.when(qi == 0)
    def _():
        kn = _qk_norm(k_ref[0], qkg_ref[1:2, :], ones)
        if P:
            kn = kn * cos_ref[...] + _swap_pairs(kn) * sin_ref[...]
            ka_ref[0:P, :] = kc_ref[0, 0].astype(bf16)
            va_ref[0:P, :] = vc_ref[0, 0].astype(bf16)
        else:
            kn_ref[0] = kn
        ka_ref[P:P + L, :] = kn.astype(bf16)
        va_ref[P:P + L, :] = v_ref[0].astype(bf16)

    qn = _qk_norm(q_ref[0], qkg_ref[0:1, :], ones)
    if P:
        rows = pl.ds(pl.multiple_of(qi * TQ, TQ), TQ)
        qn = qn * cos_ref[rows, :] + _swap_pairs(qn) * sin_ref[rows, :]
    qn = qn * (DH_D ** -0.5)
    lam = lam_ref[0, 0]
    ka = ka_ref[...]
    va = va_ref[...]
    for h in range(H_D):
        probs = []
        for j in range(2):
            cs = slice((2 * h + j) * DH_D, (2 * h + j + 1) * DH_D)
            s = _dot_nt(qn[:, cs], ka[:, cs])
            e = jnp.exp(s - jnp.max(s, axis=1, keepdims=True))
            probs.append(e * (1.0 / jnp.sum(e, axis=1, keepdims=True)))
        vs = slice(h * 2 * DH_D, (h + 1) * 2 * DH_D)
        h_ref[0, :, vs] = _dot(probs[0] - lam * probs[1], va[:, vs])
    o = h_ref[0]
    ones_v = _group_ones(W_GROUP, 2 * DH_D)
    h_ref[0] = o * lax.rsqrt(_group_mean(o * o, ones_v, 2 * DH_D) + EPS) * (sg_ref[...] * out_scale)


def diff_attn_mixer(proj, cache, lp, lam_init, layer):
    B, L, _ = proj.shape
    TQ = min(L, Q_TILE)
    lv = lp['diff_lambda']
    lam = (jnp.exp(jnp.sum(lv[0] * lv[1])) - jnp.exp(jnp.sum(lv[2] * lv[3])) + lam_init).reshape(1, 1)
    qkg = jnp.tile(lp['diff_qk_norm'], (1, 2 * H_D))
    sg = jnp.tile(lp['diff_subln_g'], (H_D,)).reshape(1, W_GROUP)
    const = lambda *shape: pl.BlockSpec(shape, lambda b, qi: (0,) * len(shape))
    in_specs = [pl.BlockSpec(memory_space=pltpu.SMEM),
                _proj_block('dq', TQ), _proj_block('dk', L), _proj_block('dv', L), const(2, W_GROUP), const(1, W_GROUP)]
    args = [lam, proj, proj, proj, qkg, sg]
    out_specs = [pl.BlockSpec((1, TQ, W_GROUP), lambda b, qi: (b, qi, 0))]
    out_shape = [jax.ShapeDtypeStruct((B, L, W_GROUP), f32)]
    P = 0
    if cache is not None:
        ck, cv = cache
        P = ck.shape[2]
        cspec = pl.BlockSpec((1, 1, P, W_GROUP), lambda b, qi, layer=layer: (b, layer, 0, 0))
        cos, sin = _rope_tables(L)
        in_specs += [cspec, cspec, const(L, W_GROUP), const(L, W_GROUP)]
        args += [ck, cv, cos, sin]
    else:
        out_specs.append(pl.BlockSpec((1, L, W_GROUP), lambda b, qi: (b, 0, 0)))
        out_shape.append(jax.ShapeDtypeStruct((B, L, W_GROUP), f32))
    return pl.pallas_call(
        functools.partial(_diff_attn_kernel, seq_len=L, q_tile=TQ, past_len=P, out_scale=1.0 - lam_init),
        grid=(B, L // TQ),
        in_specs=in_specs, out_specs=out_specs, out_shape=out_shape,
        scratch_shapes=[pltpu.VMEM((P + L, W_GROUP), bf16), pltpu.VMEM((P + L, W_GROUP), bf16)],
        compiler_params=pltpu.CompilerParams(dimension_semantics=("arbitrary", "arbitrary"),
                                             vmem_limit_bytes=VMEM_LIMIT_BYTES),
        name="diff_attention",
    )(*args)


PROJ_ROW_TILE = 1024
PROJ_COL_TILE = 640
OUT_ROW_TILE = 512
FF_TILE = 512
D_FF = 2 * D_MODEL


def _in_proj_kernel(x_ref, g_ref, sc_ref, sh_ref, w_ref, o_ref, h_ref):
    @pl.when(pl.program_id(1) == 0)
    def _():
        x = x_ref[...]
        y = x * lax.rsqrt(jnp.mean(x * x, axis=1, keepdims=True) + EPS) * g_ref[...]
        h_ref[...] = (y * (1.0 + sc_ref[0]) + sh_ref[0]).astype(bf16)

    o_ref[...] = jnp.dot(h_ref[...], w_ref[0], preferred_element_type=f32)


def in_projection(x, gain, scale, shift, w_p, layer, rows_per_mod):
    n, D = x.shape
    TM = min(n, PROJ_ROW_TILE)
    mod = pl.BlockSpec((1, 1, D), lambda i, j: (i * TM // rows_per_mod, 0, 0))
    return pl.pallas_call(
        _in_proj_kernel,
        grid=(n // TM, N_PROJ // PROJ_COL_TILE),
        in_specs=[pl.BlockSpec((TM, D), lambda i, j: (i, 0)), pl.BlockSpec((1, D), lambda i, j: (0, 0)), mod, mod,
                  pl.BlockSpec((1, D, PROJ_COL_TILE), lambda i, j: (layer, 0, j))],
        out_specs=pl.BlockSpec((TM, PROJ_COL_TILE), lambda i, j: (i, j)),
        out_shape=jax.ShapeDtypeStruct((n, N_PROJ), f32),
        scratch_shapes=[pltpu.VMEM((TM, D), bf16)],
        compiler_params=pltpu.CompilerParams(dimension_semantics=("arbitrary", "arbitrary"),
                                             vmem_limit_bytes=VMEM_LIMIT_BYTES),
        name="in_projection",
    )(x, gain.reshape(1, D), scale, shift, w_p)


def _out_proj_kernel(x_ref, m0_ref, m1_ref, m2_ref, m3_ref, w_ref, g1_ref, ng_ref, sc_ref, sh_ref, rw_ref,
                     xo_ref, h_ref, aff_ref, wb_ref):
    @pl.when(pl.program_id(0) == 0)
    def _():
        wb_ref[...] = w_ref[0].astype(bf16)

    out = None
    for j, m_ref in enumerate((m0_ref, m1_ref, m2_ref, m3_ref)):
        part = jnp.dot(m_ref[...].astype(bf16), wb_ref[j * W_GROUP:(j + 1) * W_GROUP, :], preferred_element_type=f32)
        out = part if out is None else out + part
    x = x_ref[...] + g1_ref[0] * out
    xo_ref[...] = x
    h = x * lax.rsqrt(jnp.mean(x * x, axis=1, keepdims=True) + EPS) * ng_ref[...]
    h = h * (1.0 + sc_ref[0]) + sh_ref[0]
    h_ref[...] = h
    logits = jnp.dot(h, rw_ref[0], precision=HIGHEST, preferred_element_type=f32)
    e = jnp.exp(logits - jnp.max(logits, axis=1, keepdims=True))
    aff_ref[...] = e / jnp.sum(e, axis=1, keepdims=True)


def out_projection(x, mixed, w_out, gate1, gain2, scale2, shift2, router_w, layer, rows_per_mod):
    n, D = x.shape
    TM = OUT_ROW_TILE
    row = lambda width: pl.BlockSpec((TM, width), lambda i: (i, 0))
    const = lambda *shape: pl.BlockSpec(shape, lambda i: (0,) * len(shape))
    mod = pl.BlockSpec((1, 1, D), lambda i: (i * TM // rows_per_mod, 0, 0))
    return pl.pallas_call(
        _out_proj_kernel,
        grid=(n // TM,),
        in_specs=[row(D), row(W_GROUP), row(W_GROUP), row(W_GROUP), row(W_GROUP),
                  pl.BlockSpec((1, D, D), lambda i: (layer, 0, 0)), mod, const(1, D), mod, mod,
                  pl.BlockSpec((1, D, N_EXPERTS), lambda i: (layer, 0, 0))],
        out_specs=[row(D), row(D), row(N_EXPERTS)],
        out_shape=[jax.ShapeDtypeStruct((n, D), f32), jax.ShapeDtypeStruct((n, D), f32),
                   jax.ShapeDtypeStruct((n, N_EXPERTS), f32)],
        scratch_shapes=[pltpu.VMEM((D, D), bf16)],
        compiler_params=pltpu.CompilerParams(dimension_semantics=("arbitrary",), vmem_limit_bytes=VMEM_LIMIT_BYTES),
        name="out_projection",
    )(x, *mixed, w_out, gate1, gain2.reshape(1, D), scale2, shift2, router_w)


def _experts_kernel(xc_ref, xl_ref, gc_ref, gl_ref, wg_ref, wu_ref, wd_ref, yc_ref, yl_ref, ac_ref, al_ref):
    f = pl.program_id(1)
    wg = wg_ref[0, 0].astype(bf16)
    wu = wu_ref[0, 0].astype(bf16)
    wd = wd_ref[0, 0].astype(bf16)
    for x_ref, acc_ref in ((xc_ref, ac_ref), (xl_ref, al_ref)):
        x = x_ref[0].astype(bf16)
        hidden = jax.nn.silu(jnp.dot(x, wg, preferred_element_type=f32)) * jnp.dot(x, wu, preferred_element_type=f32)
        part = jnp.dot(hidden.astype(bf16), wd, preferred_element_type=f32)

        @pl.when(f == 0)
        def _(acc_ref=acc_ref, part=part):
            acc_ref[...] = part

        @pl.when(f > 0)
        def _(acc_ref=acc_ref, part=part):
            acc_ref[...] += part

    @pl.when(f == pl.num_programs(1) - 1)
    def _():
        yc_ref[0] = ac_ref[...] * gc_ref[0]
        yl_ref[0] = al_ref[...] * gl_ref[0]


def expert_ffn(xe_c, xe_l, g_c, g_l, w_gate, w_up, w_down, layer):
    E, Cc, D = xe_c.shape
    Cl = xe_l.shape[1]
    tok = lambda C, width: pl.BlockSpec((1, C, width), lambda e, f: (e, 0, 0))
    return pl.pallas_call(
        _experts_kernel,
        grid=(E, D_FF // FF_TILE),
        in_specs=[tok(Cc, D), tok(Cl, D), tok(Cc, 1), tok(Cl, 1),
                  pl.BlockSpec((1, 1, D, FF_TILE), lambda e, f: (layer, e, 0, f)),
                  pl.BlockSpec((1, 1, D, FF_TILE), lambda e, f: (layer, e, 0, f)),
                  pl.BlockSpec((1, 1, FF_TILE, D), lambda e, f: (layer, e, f, 0))],
        out_specs=[tok(Cc, D), tok(Cl, D)],
        out_shape=[jax.ShapeDtypeStruct((E, Cc, D), f32), jax.ShapeDtypeStruct((E, Cl, D), f32)],
        scratch_shapes=[pltpu.VMEM((Cc, D), f32), pltpu.VMEM((Cl, D), f32)],
        compiler_params=pltpu.CompilerParams(dimension_semantics=("arbitrary", "arbitrary"),
                                             vmem_limit_bytes=VMEM_LIMIT_BYTES),
        name="expert_ffn",
    )(xe_c, xe_l, g_c, g_l, w_gate, w_up, w_down)


def _permute_w_in(w):
    gate0 = 4 * W_GROUP
    pad = jnp.zeros(w.shape[:-1] + (GATE_LANES - 4 * H_M,), w.dtype)
    return jnp.concatenate([w[..., :gate0], w[..., gate0 + 4 * H_M:], w[..., gate0:gate0 + 4 * H_M], pad],
                           axis=-1).astype(bf16)


def token_mixers(proj, lp, lam_init, states, cache, layer):
    B, L, _ = proj.shape
    mC0, mn0, mm0, s5r0, s5i0, R0 = states
    gates_t = jnp.swapaxes(proj[:, :, 12 * W_GROUP:12 * W_GROUP + 4 * H_M], 1, 2)
    hm, mC, mn, mm = mlstm_mixer(proj, gates_t, mC0, mn0, mm0, lp)
    su = proj[:, :, PROJ_BLOCKS['su'] * W_GROUP:(PROJ_BLOCKS['su'] + 1) * W_GROUP]
    ys, s5r, s5i = s5_mixer(su, s5r0, s5i0, lp)
    hr, R = retention_mixer(proj, R0, lp)
    attn = diff_attn_mixer(proj, cache, lp, lam_init, layer)
    mixed = [a.reshape(B * L, W_GROUP) for a in (hm, ys, hr, attn[0])]
    new_ctx = None
    if cache is None:
        v = proj[:, :, PROJ_BLOCKS['dv'] * W_GROUP:(PROJ_BLOCKS['dv'] + 1) * W_GROUP]
        new_ctx = (mC, mn, mm, s5r, s5i, R, attn[1].reshape(B, L, 2 * H_D, DH_D), v.reshape(B, L, H_D, 2 * DH_D))
    return mixed, new_ctx


def _route(aff, h2):
    n = aff.shape[0]
    gates, idx = lax.top_k(aff.T, CAPACITY_FACTOR * n // N_EXPERTS)
    return gates[..., None], idx, h2[idx]


def _combine(x, ye, idx, gate2, B):
    n, D = x.shape
    y = jnp.zeros_like(x).at[idx.reshape(-1)].add(ye.reshape(-1, D))
    return (x.reshape(B, n // B, D) + gate2 * y.reshape(B, n // B, D)).reshape(n, D)


PER_LAYER = ('norm1_g', 'norm2_g', 'mlstm_gate_b', 'mlstm_norm_g', 's5_lambda_re', 's5_lambda_im',
             's5_log_step', 's5_b_re', 's5_b_im', 's5_c_re', 's5_c_im', 's5_d', 's5_glu_w', 's5_glu_b', 'ret_decay',
             'ret_gn_g', 'diff_qk_norm', 'diff_lambda', 'diff_subln_g')


def kernel(x_prompt, x_sample, state_mlstm_c, state_mlstm_n, state_mlstm_m, state_s5_re, state_s5_im, state_ret, cache_diff_k, cache_diff_v, c, c_ctx, norm1_g, norm2_g, ada_w, ada_b, w_in, w_out, mlstm_gate_b, mlstm_norm_g, s5_lambda_re, s5_lambda_im, s5_log_step, s5_b_re, s5_b_im, s5_c_re, s5_c_im, s5_d, s5_glu_w, s5_glu_b, ret_decay, ret_gn_g, diff_qk_norm, diff_lambda, diff_subln_g, router_w, exp_w_gate, exp_w_up, exp_w_down):
    weights = dict(norm1_g=norm1_g, norm2_g=norm2_g, mlstm_gate_b=mlstm_gate_b,
                   mlstm_norm_g=mlstm_norm_g, s5_lambda_re=s5_lambda_re, s5_lambda_im=s5_lambda_im,
                   s5_log_step=s5_log_step, s5_b_re=s5_b_re, s5_b_im=s5_b_im, s5_c_re=s5_c_re, s5_c_im=s5_c_im,
                   s5_d=s5_d, s5_glu_w=s5_glu_w, s5_glu_b=s5_glu_b, ret_decay=ret_decay, ret_gn_g=ret_gn_g,
                   diff_qk_norm=diff_qk_norm, diff_lambda=diff_lambda, diff_subln_g=diff_subln_g)
    w_in_p = _permute_w_in(w_in)
    Bc, Lc, D = x_prompt.shape
    Bl, Ll, _ = x_sample.shape
    xc = x_prompt.reshape(Bc * Lc, D)
    xl = x_sample.reshape(Bl * Ll, D)
    zero_states = (jnp.zeros((Bc, 2, H_M, DH_M, DH_M), f32), jnp.zeros((Bc, 2, H_M, DH_M), f32),
                   jnp.zeros((Bc, 2, H_M), f32), jnp.zeros((Bc, 2, G_S5, P_S5), f32),
                   jnp.zeros((Bc, 2, G_S5, P_S5), f32), jnp.zeros((Bc, 2, H_R, DH_R, DH_R), f32))
    cache = (cache_diff_k.reshape(cache_diff_k.shape[:3] + (W_GROUP,)),
             cache_diff_v.reshape(cache_diff_v.shape[:3] + (W_GROUP,)))
    cvec = jnp.concatenate([c_ctx[None, :], c], axis=0)
    outs = [[] for _ in range(8)]
    for l in range(DEPTH):
        lp = {name: weights[name][l] for name in PER_LAYER}
        lam_init = 0.8 - 0.6 * math.exp(-0.3 * l)
        mods = jnp.split((jax.nn.silu(cvec) @ ada_w[l] + ada_b[l])[:, None, :], 6, axis=-1)
        lat_states = (state_mlstm_c[:, l], state_mlstm_n[:, l], state_mlstm_m[:, l], state_s5_re[:, l],
                      state_s5_im[:, l], state_ret[:, l])
        routed = []
        for x, B, L, sel, states, kv in ((xc, Bc, Lc, slice(0, 1), zero_states, None),
                                        (xl, Bl, Ll, slice(1, 1 + Bl), lat_states, cache)):
            sh1, sc1, g1, sh2, sc2, g2 = (m[sel] for m in mods)
            rows_per_mod = x.shape[0] // sh1.shape[0]
            proj = in_projection(x, lp['norm1_g'], sc1, sh1, w_in_p, l, rows_per_mod).reshape(B, L, N_PROJ)
            mixed, new_ctx = token_mixers(proj, lp, lam_init, states, kv, l)
            if new_ctx is not None:
                for acc, t in zip(outs, new_ctx):
                    acc.append(t)
            x1, h2, aff = out_projection(x, mixed, w_out, g1, lp['norm2_g'], sc2, sh2, router_w, l, rows_per_mod)
            routed.append((x1, g2, sh1.shape[0]) + _route(aff, h2))
        (x1c, g2c, nbc, gc, idxc, xec), (x1l, g2l, nbl, gl, idxl, xel) = routed
        yec, yel = expert_ffn(xec, xel, gc, gl, exp_w_gate, exp_w_up, exp_w_down, l)
        xc = _combine(x1c, yec, idxc, g2c, nbc)
        xl = _combine(x1l, yel, idxl, g2l, nbl)
    return (xc.reshape(Bc, Lc, D), xl.reshape(Bl, Ll, D)) + tuple(jnp.stack(o, axis=1) for o in outs)
```

```python
import functools
import math

import jax
import jax.numpy as jnp
import numpy as np
from jax import lax
from jax.experimental import pallas as pl
from jax.experimental.pallas import tpu as pltpu

D_MODEL = 1024
DEPTH = 4
GRID_W = 64
W_GROUP = 256
H_M = 4
DH_M = 64
S5_CH = 16
G_S5 = 16
P_S5 = 64
S5_STATE = G_S5 * P_S5
H_R = 4
DH_R = 64
H_D = 4
DH_D = 32
N_EXPERTS = 16
CAPACITY_FACTOR = 2
ROPE_BASE = 10000.0
EPS = 1e-6
SUBLANES = 8
VMEM_LIMIT_BYTES = 56 * 1024 * 1024

f32 = jnp.float32
bf16 = jnp.bfloat16
HIGHEST = lax.Precision.HIGHEST
NEG_INF = float("-inf")


def _gelu_tanh(x):
    return 0.5 * x * (1.0 + jnp.tanh(math.sqrt(2.0 / math.pi) * (x + 0.044715 * (x * x * x))))


def _s5_kernel(su_ref, x0r_ref, x0i_ref, wb_ref, wc_ref, lb_ref, pw_ref, d_ref, gw_ref, gb_ref,
               y_ref, xr_ref, xi_ref, st_ref, *, seq_len):
    n_steps = seq_len // SUBLANES
    su = su_ref[0]
    y_ref[0] = su * d_ref[...]
    row = lax.broadcasted_iota(jnp.int32, (SUBLANES, S5_STATE), 0)
    zeros = jnp.zeros((SUBLANES, S5_STATE), f32)
    for d in range(2):
        st_ref[...] = jnp.dot(su.astype(bf16), wb_ref[0, d], preferred_element_type=f32)
        lbr = jnp.broadcast_to(lb_ref[0, d, 0:1, :], (SUBLANES, S5_STATE))
        lbi = jnp.broadcast_to(lb_ref[0, d, 1:2, :], (SUBLANES, S5_STATE))

        def rows_of(k, d=d):
            kk = k if d == 0 else n_steps - 1 - k
            return pl.ds(pl.multiple_of(kk * SUBLANES, SUBLANES), SUBLANES)

        def scan_step(k, carry, lbr=lbr, lbi=lbi, rows_of=rows_of):
            xr, xi = carry
            r = rows_of(k)
            nxr = lbr * xr - lbi * xi + st_ref[r, 0:S5_STATE]
            nxi = lbr * xi + lbi * xr + st_ref[r, S5_STATE:2 * S5_STATE]
            st_ref[r, 0:S5_STATE] = nxr
            st_ref[r, S5_STATE:2 * S5_STATE] = nxi
            return nxr, nxi

        fr, fi = lax.fori_loop(0, n_steps, scan_step, (zeros, zeros))

        cr = x0r_ref[0, d:d + 1, :]
        ci = x0i_ref[0, d:d + 1, :]
        plr = pw_ref[0, d, n_steps - 1:n_steps, 0:S5_STATE]
        pli = pw_ref[0, d, n_steps - 1:n_steps, S5_STATE:2 * S5_STATE]
        cmr, cmi = zeros, zeros
        for i in (range(SUBLANES) if d == 0 else reversed(range(SUBLANES))):
            cmr = jnp.where(row == i, cr, cmr)
            cmi = jnp.where(row == i, ci, cmi)
            cr, ci = (plr * cr - pli * ci + fr[i:i + 1], plr * ci + pli * cr + fi[i:i + 1])
        xr_ref[0, d:d + 1, :] = cr
        xi_ref[0, d:d + 1, :] = ci

        def fix_step(k, carry, d=d, cmr=cmr, cmi=cmi, rows_of=rows_of):
            pr = pw_ref[0, d, pl.ds(k, 1), 0:S5_STATE]
            pi = pw_ref[0, d, pl.ds(k, 1), S5_STATE:2 * S5_STATE]
            r = rows_of(k)
            st_ref[r, 0:S5_STATE] = st_ref[r, 0:S5_STATE] + (pr * cmr - pi * cmi)
            st_ref[r, S5_STATE:2 * S5_STATE] = st_ref[r, S5_STATE:2 * S5_STATE] + (pr * cmi + pi * cmr)
            return carry

        lax.fori_loop(0, n_steps, fix_step, 0)
        y_ref[0] += jnp.dot(st_ref[...].astype(bf16), wc_ref[0, d], preferred_element_type=f32)

    ys = _gelu_tanh(y_ref[0])
    gate = jax.nn.sigmoid(jnp.dot(ys.astype(bf16), gw_ref[0], preferred_element_type=f32) + gb_ref[...])
    y_ref[0] = ys * gate


def _s5_prepare(lam_re, lam_im, log_step, b_re, b_im, c_re, c_im, glu_w, n_steps_list):
    dt = jnp.exp(log_step)[..., None]
    mag = jnp.exp(lam_re * dt)
    ang = lam_im * dt
    lb_re, lb_im = mag * jnp.cos(ang), mag * jnp.sin(ang)
    nr, ni = lb_re - 1.0, lb_im
    den = lam_re * lam_re + lam_im * lam_im
    f_re = (nr * lam_re + ni * lam_im) / den
    f_im = (ni * lam_re - nr * lam_im) / den
    bb_re = f_re[..., None] * b_re[:, None] - f_im[..., None] * b_im[:, None]
    bb_im = f_re[..., None] * b_im[:, None] + f_im[..., None] * b_re[:, None]
    eye = jnp.eye(G_S5, dtype=f32)[:, None, :, None]

    def block_diag(a):
        return (a[:, :, :, :, None, :] * eye).reshape(a.shape[:2] + (G_S5 * a.shape[3], G_S5 * a.shape[4]))

    wb = jnp.concatenate([block_diag(jnp.swapaxes(bb_re, 3, 4)), block_diag(jnp.swapaxes(bb_im, 3, 4))],
                         axis=-1).astype(bf16)
    wc = jnp.concatenate([block_diag(jnp.swapaxes(c_re, 3, 4)), -block_diag(jnp.swapaxes(c_im, 3, 4))],
                         axis=2).astype(bf16)
    lead = lb_re.shape[:2]
    lb = jnp.stack([lb_re.reshape(lead + (S5_STATE,)), lb_im.reshape(lead + (S5_STATE,))], axis=2)
    pr, pi = lb[:, :, None, 0], lb[:, :, None, 1]
    tables = {}
    for j in range(int(math.log2(max(n_steps_list))) + 1):
        if 2 ** j in n_steps_list:
            tables[2 ** j] = jnp.concatenate([pr, pi], axis=-1)
        tr, ti = pr[:, :, -1:], pi[:, :, -1:]
        pr, pi = (jnp.concatenate([pr, pr * tr - pi * ti], axis=2), jnp.concatenate([pi, pr * ti + pi * tr], axis=2))
    return wb, wc, lb, tables, glu_w.astype(bf16)


def s5_mixer(su, x0r, x0i, s5p, s5_d, glu_b, layer):
    B, L, _ = su.shape
    n_steps = L // SUBLANES
    wb, wc, lb, tables, glu_w = s5p
    pw = tables[n_steps]
    su_p = su.reshape(B, SUBLANES, n_steps, W_GROUP).transpose(0, 2, 1, 3).reshape(B, L, W_GROUP)
    full = lambda *shape: pl.BlockSpec(shape, lambda b: (0,) * len(shape))
    per_b = lambda *shape: pl.BlockSpec((1,) + shape, lambda b: (b,) + (0,) * len(shape))
    per_layer = lambda *shape: pl.BlockSpec((1,) + shape, lambda b: (layer,) + (0,) * len(shape))
    y_p, xr, xi = pl.pallas_call(
        functools.partial(_s5_kernel, seq_len=L),
        grid=(B,),
        in_specs=[per_b(L, W_GROUP), per_b(2, S5_STATE), per_b(2, S5_STATE),
                  per_layer(2, W_GROUP, 2 * S5_STATE), per_layer(2, 2 * S5_STATE, W_GROUP),
                  per_layer(2, 2, S5_STATE), per_layer(2, n_steps, 2 * S5_STATE),
                  full(1, W_GROUP), per_layer(W_GROUP, W_GROUP), full(1, W_GROUP)],
        out_specs=[per_b(L, W_GROUP), per_b(2, S5_STATE), per_b(2, S5_STATE)],
        out_shape=[jax.ShapeDtypeStruct((B, L, W_GROUP), f32),
                   jax.ShapeDtypeStruct((B, 2, S5_STATE), f32),
                   jax.ShapeDtypeStruct((B, 2, S5_STATE), f32)],
        scratch_shapes=[pltpu.VMEM((L, 2 * S5_STATE), f32)],
        compiler_params=pltpu.CompilerParams(dimension_semantics=("arbitrary",),
                                             vmem_limit_bytes=VMEM_LIMIT_BYTES),
        name="s5_mixer",
    )(su_p, x0r.reshape(B, 2, S5_STATE), x0i.reshape(B, 2, S5_STATE), wb, wc, lb, pw,
      s5_d.reshape(1, W_GROUP), glu_w, glu_b.reshape(1, W_GROUP))
    y = y_p.reshape(B, n_steps, SUBLANES, W_GROUP).transpose(0, 2, 1, 3).reshape(B, L, W_GROUP)
    return y, xr.reshape(B, 2, G_S5, P_S5), xi.reshape(B, 2, G_S5, P_S5)


Q_TILE = 256
PROJ_BLOCKS = dict(mq=0, mk=1, mv=2, mo=3, su=4, rq=5, rk=6, rv=7, rg=8, dq=9, dk=10, dv=11)
GATE_LANES = 128
N_PROJ = 12 * W_GROUP + GATE_LANES


def _log_sigmoid(x):
    return jnp.minimum(x, 0.0) - jnp.log1p(jnp.exp(-jnp.abs(x)))


def _group_ones(width, group):
    shift = int(math.log2(group))
    r = lax.broadcasted_iota(jnp.int32, (width, width), 0) >> shift
    c = lax.broadcasted_iota(jnp.int32, (width, width), 1) >> shift
    return (r == c).astype(bf16)


def _split3(x):
    hi = x.astype(bf16)
    r = x - hi.astype(f32)
    mid = r.astype(bf16)
    return hi, mid, (r - mid.astype(f32)).astype(bf16)


def _group_mean(x, ones, group):
    return sum(jnp.dot(p, ones, preferred_element_type=f32) for p in _split3(x)) * (1.0 / group)


def _dot_nt(a, b):
    return lax.dot_general(a.astype(bf16), b.astype(bf16), (((1,), (1,)), ((), ())), preferred_element_type=f32)


def _dot(a, b):
    return jnp.dot(a.astype(bf16), b.astype(bf16), preferred_element_type=f32)


def _proj_block(name, rows):
    j = PROJ_BLOCKS[name]
    return pl.BlockSpec((1, rows, W_GROUP), lambda b, qi, j=j, rows=rows: (b, qi if rows == Q_TILE else 0, j))


def _mlstm_kernel(q_ref, k_ref, v_ref, o_ref, g_ref, gt_ref, gb_ref, gbt_ref, ng_ref, c0_ref, n0_ref, m0_ref,
                  h_ref, c_ref, n_ref, m_ref, gl_ref, gu_ref, rc_ref, vt_ref, ht_ref, *, seq_len, q_tile):
    L, TQ = seq_len, q_tile
    nq = L // TQ
    qi = pl.program_id(1)
    grow = gt_ref[0] + gbt_ref[...]

    @pl.when(qi == 0)
    def _():
        ss = lax.broadcasted_iota(jnp.int32, (L, L), 0)
        tt = lax.broadcasted_iota(jnp.int32, (L, L), 1)
        tri_le = (ss <= tt).astype(bf16)
        tri_ge = (ss >= tt).astype(bf16)
        rows = _split3(_log_sigmoid(grow))
        gl = sum(jnp.dot(p, tri_le, preferred_element_type=f32) for p in rows)
        gu = sum(jnp.dot(p, tri_ge, preferred_element_type=f32) for p in rows)
        for j in range(nq):
            gl_ref[j] = gl[:, j * TQ:(j + 1) * TQ]
            gu_ref[j] = gu[:, j * TQ:(j + 1) * TQ]
        gcol = g_ref[0] + gb_ref[...]
        cols = _split3(_log_sigmoid(gcol))
        glc = sum(jnp.dot(tri_ge, p, preferred_element_type=f32) for p in cols)
        guc = sum(jnp.dot(tri_le, p, preferred_element_type=f32) for p in cols)
        lane = lax.broadcasted_iota(jnp.int32, (L, GATE_LANES), 1)
        rc_ref[...] = pltpu.roll(gcol, 4, 1) - jnp.where(lane < 8, glc, guc)

        vt = v_ref[0].T
        kk = k_ref[0] * (DH_M ** -0.5)
        one_row = (lax.broadcasted_iota(jnp.int32, (DH_M, L), 0) == 0).astype(bf16)
        for h in range(H_M):
            hs = slice(h * DH_M, (h + 1) * DH_M)
            vt_ref[2 * h * DH_M:(2 * h + 1) * DH_M, :] = vt[hs, :].astype(bf16)
            vt_ref[(2 * h + 1) * DH_M:(2 * h + 2) * DH_M, :] = one_row
        for d in range(2):
            g_all = gl if d == 0 else gu
            for h in range(H_M):
                hs = slice(h * DH_M, (h + 1) * DH_M)
                ii, fi = 8 * d + h, 8 * d + 4 + h
                g_row = g_all[fi:fi + 1, :]
                g_tot = g_row[:, L - 1:L] if d == 0 else g_row[:, 0:1]
                wlog = g_tot - g_row + grow[ii:ii + 1, :]
                m0 = m0_ref[0, d:d + 1, h:h + 1]
                m_new = jnp.maximum(g_tot + m0, jnp.max(wlog, axis=1, keepdims=True))
                decay = jnp.exp(g_tot + m0 - m_new)
                w = jnp.exp(wlog - m_new)
                kh = kk[:, hs]
                c_ref[0, d, h] = decay * c0_ref[0, d, h] + _dot(vt[hs, :] * w, kh)
                n_upd = jnp.dot(jnp.broadcast_to(w, (SUBLANES, L)), kh, precision=HIGHEST,
                                preferred_element_type=f32)[0:1, :]
                n_ref[0, d, h:h + 1, :] = decay * n0_ref[0, d, h:h + 1, :] + n_upd
                m_ref[0, d:d + 1, h:h + 1] = m_new

    gl_t = gl_ref[qi]
    gu_t = gu_ref[qi]
    s_idx = lax.broadcasted_iota(jnp.int32, (L, TQ), 0)
    t_idx = qi * TQ + lax.broadcasted_iota(jnp.int32, (L, TQ), 1)
    low = s_idx <= t_idx
    upp = s_idx >= t_idx
    qt = q_ref[0].T.astype(bf16)
    k = (k_ref[0] * (DH_M ** -0.5)).astype(bf16)
    rc = rc_ref[...]
    row0 = lax.broadcasted_iota(jnp.int32, (DH_M, DH_M), 0) == 0
    for h in range(H_M):
        hs = slice(h * DH_M, (h + 1) * DH_M)
        qth = qt[hs, :]
        vta = vt_ref[2 * h * DH_M:(2 * h + 2) * DH_M, :]
        s0 = jnp.dot(k[:, hs], qth, preferred_element_type=f32)
        h_sum = None
        for d in range(2):
            fi = 8 * d + 4 + h
            g_t = (gl_t if d == 0 else gu_t)[fi:fi + 1, :]
            dlog = jnp.where(low if d == 0 else upp, rc[:, fi:fi + 1] + g_t, NEG_INF)
            inter = g_t + m0_ref[0, d:d + 1, h:h + 1]
            m_t = jnp.maximum(inter, jnp.max(dlog, axis=0, keepdims=True))
            p = s0 * jnp.exp(dlog - m_t)
            a = jnp.exp(inter - m_t)
            c0n0 = jnp.concatenate([c0_ref[0, d, h], jnp.where(row0, n0_ref[0, d, h:h + 1, :], 0.0)], axis=0)
            numden = (jnp.dot(vta, p.astype(bf16), preferred_element_type=f32)
                      + a * jnp.dot(c0n0.astype(bf16), qth, preferred_element_type=f32))
            scale = 1.0 / jnp.maximum(jnp.abs(numden[DH_M:DH_M + 1, :]), jnp.exp(-m_t))
            hd = numden[0:DH_M, :] * scale
            h_sum = hd if h_sum is None else h_sum + hd
        ht_ref[hs, :] = h_sum * lax.rsqrt(jnp.mean(h_sum * h_sum, axis=0, keepdims=True) + EPS)
    h_ref[0] = jax.nn.sigmoid(o_ref[0]) * (ht_ref[...].T * ng_ref[...])


def mlstm_mixer(proj, gates_t, c0, n0, m0, lp):
    B, L, _ = proj.shape
    TQ = min(L, Q_TILE)
    gb = lp['mlstm_gate_b'].reshape(1, 4 * H_M)
    const = lambda *shape: pl.BlockSpec(shape, lambda b, qi: (0,) * len(shape))
    per_b = lambda *shape: pl.BlockSpec((1,) + shape, lambda b, qi: (b,) + (0,) * len(shape))
    return pl.pallas_call(
        functools.partial(_mlstm_kernel, seq_len=L, q_tile=TQ),
        grid=(B, L // TQ),
        in_specs=[_proj_block('mq', TQ), _proj_block('mk', L), _proj_block('mv', L), _proj_block('mo', TQ),
                  pl.BlockSpec((1, L, GATE_LANES), lambda b, qi: (b, 0, 12 * W_GROUP // GATE_LANES)),
                  per_b(4 * H_M, L), const(1, GATE_LANES), const(4 * H_M, 1), const(1, W_GROUP),
                  per_b(2, H_M, DH_M, DH_M), per_b(2, H_M, DH_M), per_b(2, H_M)],
        out_specs=[pl.BlockSpec((1, TQ, W_GROUP), lambda b, qi: (b, qi, 0)),
                   per_b(2, H_M, DH_M, DH_M), per_b(2, H_M, DH_M), per_b(2, H_M)],
        out_shape=[jax.ShapeDtypeStruct((B, L, W_GROUP), f32),
                   jax.ShapeDtypeStruct((B, 2, H_M, DH_M, DH_M), f32),
                   jax.ShapeDtypeStruct((B, 2, H_M, DH_M), f32),
                   jax.ShapeDtypeStruct((B, 2, H_M), f32)],
        scratch_shapes=[pltpu.VMEM((L // TQ, 4 * H_M, TQ), f32), pltpu.VMEM((L // TQ, 4 * H_M, TQ), f32),
                        pltpu.VMEM((L, GATE_LANES), f32), pltpu.VMEM((2 * W_GROUP, L), bf16),
                        pltpu.VMEM((W_GROUP, TQ), f32)],
        compiler_params=pltpu.CompilerParams(dimension_semantics=("arbitrary", "arbitrary"),
                                             vmem_limit_bytes=VMEM_LIMIT_BYTES),
        name="mlstm_mixer",
    )(proj, proj, proj, proj, proj, gates_t, jnp.pad(gb, ((0, 0), (0, GATE_LANES - 4 * H_M))),
      gb.reshape(4 * H_M, 1), lp['mlstm_norm_g'].reshape(1, W_GROUP), c0, n0, m0)


def _retention_kernel(lg_ref, q_ref, k_ref, v_ref, g_ref, gn_ref, r0_ref, h_ref, r_ref, *, seq_len, q_tile):
    L, TQ = seq_len, q_tile
    qi = pl.program_id(1)
    t_col = qi * TQ + lax.broadcasted_iota(jnp.int32, (TQ, 1), 0)
    rel = (qi * TQ + lax.broadcasted_iota(jnp.int32, (TQ, L), 0)
           - lax.broadcasted_iota(jnp.int32, (TQ, L), 1)).astype(f32)
    q = q_ref[0]
    k = k_ref[0] * (DH_R ** -0.5)
    v = v_ref[0]
    ones = _group_ones(W_GROUP, DH_R)
    for h in range(H_R):
        hs = slice(h * DH_R, (h + 1) * DH_R)
        lgf, lgb = lg_ref[0, h], lg_ref[1, h]
        qh, kh, vh = q[:, hs], k[:, hs], v[:, hs]
        decay = jnp.where(rel > 0.0, jnp.exp(lgf * jnp.maximum(rel, 0.0)),
                          jnp.where(rel < 0.0, jnp.exp(lgb * jnp.maximum(-rel, 0.0)), 2.0))
        o = _dot(_dot_nt(qh, kh) * decay, vh)
        xi_f = jnp.exp(lgf * (t_col + 1).astype(f32))
        xi_b = jnp.exp(lgb * (L - t_col).astype(f32))
        o = o + xi_f * _dot(qh, r0_ref[0, 0, h]) + xi_b * _dot(qh, r0_ref[0, 1, h])
        h_ref[0, :, hs] = o
    o = h_ref[0]
    oc = o - _group_mean(o, ones, DH_R)
    y = oc * lax.rsqrt(_group_mean(oc * oc, ones, DH_R) + EPS) * gn_ref[...]
    h_ref[0] = y * jax.nn.silu(g_ref[0])

    @pl.when(qi == 0)
    def _():
        kt = k.T
        s_row = lax.broadcasted_iota(jnp.int32, (1, L), 1).astype(f32)
        for d in range(2):
            for h in range(H_R):
                hs = slice(h * DH_R, (h + 1) * DH_R)
                lg = lg_ref[d, h]
                zeta = jnp.exp(lg * ((L - 1.0) - s_row)) if d == 0 else jnp.exp(lg * s_row)
                r_ref[0, d, h] = jnp.exp(lg * float(L)) * r0_ref[0, d, h] + _dot(kt[hs, :] * zeta, v[:, hs])


def retention_mixer(proj, r0, lp):
    B, L, _ = proj.shape
    TQ = min(L, Q_TILE)
    log_gamma = -jnp.exp(lp['ret_decay'])
    per_b = lambda *shape: pl.BlockSpec((1,) + shape, lambda b, qi: (b,) + (0,) * len(shape))
    return pl.pallas_call(
        functools.partial(_retention_kernel, seq_len=L, q_tile=TQ),
        grid=(B, L // TQ),
        in_specs=[pl.BlockSpec(memory_space=pltpu.SMEM),
                  _proj_block('rq', TQ), _proj_block('rk', L), _proj_block('rv', L), _proj_block('rg', TQ),
                  pl.BlockSpec((1, W_GROUP), lambda b, qi: (0, 0)), per_b(2, H_R, DH_R, DH_R)],
        out_specs=[pl.BlockSpec((1, TQ, W_GROUP), lambda b, qi: (b, qi, 0)), per_b(2, H_R, DH_R, DH_R)],
        out_shape=[jax.ShapeDtypeStruct((B, L, W_GROUP), f32),
                   jax.ShapeDtypeStruct((B, 2, H_R, DH_R, DH_R), f32)],
        compiler_params=pltpu.CompilerParams(dimension_semantics=("arbitrary", "arbitrary"),
                                             vmem_limit_bytes=VMEM_LIMIT_BYTES),
        name="retention_mixer",
    )(log_gamma, proj, proj, proj, proj, lp['ret_gn_g'].reshape(1, W_GROUP), r0)


def _rope_tables(L):
    half = DH_D // 2
    freqs = ROPE_BASE ** (-np.arange(0, half, 2, dtype=np.float64) / half)
    pos = np.arange(L)
    row, col = (pos // GRID_W).astype(np.float64), (pos % GRID_W).astype(np.float64)
    ang = np.concatenate([np.tile(row[:, None] * freqs, (1, 2)), np.tile(col[:, None] * freqs, (1, 2))], axis=1)
    sign = np.tile(np.concatenate([-np.ones(half // 2), np.ones(half // 2)]), 2)
    cos = np.tile(np.cos(ang), (1, 2 * H_D))
    sin = np.tile(np.sin(ang) * sign, (1, 2 * H_D))
    return jnp.asarray(cos, f32), jnp.asarray(sin, f32)


def _swap_pairs(x):
    parts = []
    for j in range(x.shape[1] // 128):
        xs = x[:, j * 128:(j + 1) * 128]
        lane = lax.broadcasted_iota(jnp.int32, xs.shape, 1)
        parts.append(jnp.where((lane & 15) < 8, pltpu.roll(xs, 120, 1), pltpu.roll(xs, 8, 1)))
    return jnp.concatenate(parts, axis=1)


def _qk_norm(x, gain, ones):
    return x * lax.rsqrt(_group_mean(x * x, ones, DH_D) + EPS) * gain


def _diff_attn_kernel(lam_ref, q_ref, k_ref, v_ref, qkg_ref, sg_ref, *rest, seq_len, q_tile, past_len, out_scale):
    L, TQ, P = seq_len, q_tile, past_len
    if P:
        kc_ref, vc_ref, cos_ref, sin_ref, h_ref, ka_ref, vt_ref, ot_ref = rest
    else:
        h_ref, kn_ref, ka_ref, vt_ref, ot_ref = rest
    qi = pl.program_id(1)
    ones = _group_ones(W_GROUP, DH_D)

    @pl.when(qi == 0)
    def _():
        kn = _qk_norm(k_ref[0], qkg_ref[1:2, :], ones)
        if P:
            kn = kn * cos_ref[...] + _swap_pairs(kn) * sin_ref[...]
            ka_ref[0:P, :] = kc_ref[0, 0].astype(bf16)
            vt_ref[:, 0:P] = vc_ref[0, 0].T.astype(bf16)
        else:
            kn_ref[0] = kn
        ka_ref[P:P + L, :] = kn.astype(bf16)
        vt_ref[:, P:P + L] = v_ref[0].T.astype(bf16)

    qn = _qk_norm(q_ref[0], qkg_ref[0:1, :], ones)
    if P:
        rows = pl.ds(pl.multiple_of(qi * TQ, TQ), TQ)
        qn = qn * cos_ref[rows, :] + _swap_pairs(qn) * sin_ref[rows, :]
    qt = (qn * (DH_D ** -0.5)).T.astype(bf16)
    lam = lam_ref[0, 0]
    ka = ka_ref[...]
    for h in range(H_D):
        probs = []
        for j in range(2):
            cs = slice((2 * h + j) * DH_D, (2 * h + j + 1) * DH_D)
            s = jnp.dot(ka[:, cs], qt[cs, :], preferred_element_type=f32)
            e = jnp.exp(s - jnp.max(s, axis=0, keepdims=True))
            probs.append(e * (1.0 / jnp.sum(e, axis=0, keepdims=True)))
        vs = slice(h * 2 * DH_D, (h + 1) * 2 * DH_D)
        ot_ref[vs, :] = jnp.dot(vt_ref[vs, :], (probs[0] - lam * probs[1]).astype(bf16), preferred_element_type=f32)
    o = ot_ref[...].T
    ones_v = _group_ones(W_GROUP, 2 * DH_D)
    h_ref[0] = o * lax.rsqrt(_group_mean(o * o, ones_v, 2 * DH_D) + EPS) * (sg_ref[...] * out_scale)


def diff_attn_mixer(proj, cache, lp, lam_init, layer):
    B, L, _ = proj.shape
    TQ = min(L, Q_TILE)
    lv = lp['diff_lambda']
    lam = (jnp.exp(jnp.sum(lv[0] * lv[1])) - jnp.exp(jnp.sum(lv[2] * lv[3])) + lam_init).reshape(1, 1)
    qkg = jnp.tile(lp['diff_qk_norm'], (1, 2 * H_D))
    sg = jnp.tile(lp['diff_subln_g'], (H_D,)).reshape(1, W_GROUP)
    const = lambda *shape: pl.BlockSpec(shape, lambda b, qi: (0,) * len(shape))
    in_specs = [pl.BlockSpec(memory_space=pltpu.SMEM),
                _proj_block('dq', TQ), _proj_block('dk', L), _proj_block('dv', L), const(2, W_GROUP), const(1, W_GROUP)]
    args = [lam, proj, proj, proj, qkg, sg]
    out_specs = [pl.BlockSpec((1, TQ, W_GROUP), lambda b, qi: (b, qi, 0))]
    out_shape = [jax.ShapeDtypeStruct((B, L, W_GROUP), f32)]
    P = 0
    if cache is not None:
        ck, cv = cache
        P = ck.shape[2]
        cspec = pl.BlockSpec((1, 1, P, W_GROUP), lambda b, qi, layer=layer: (b, layer, 0, 0))
        cos, sin = _rope_tables(L)
        in_specs += [cspec, cspec, const(L, W_GROUP), const(L, W_GROUP)]
        args += [ck, cv, cos, sin]
    else:
        out_specs.append(pl.BlockSpec((1, L, W_GROUP), lambda b, qi: (b, 0, 0)))
        out_shape.append(jax.ShapeDtypeStruct((B, L, W_GROUP), f32))
    return pl.pallas_call(
        functools.partial(_diff_attn_kernel, seq_len=L, q_tile=TQ, past_len=P, out_scale=1.0 - lam_init),
        grid=(B, L // TQ),
        in_specs=in_specs, out_specs=out_specs, out_shape=out_shape,
        scratch_shapes=[pltpu.VMEM((P + L, W_GROUP), bf16), pltpu.VMEM((W_GROUP, P + L), bf16),
                        pltpu.VMEM((W_GROUP, TQ), f32)],
        compiler_params=pltpu.CompilerParams(dimension_semantics=("arbitrary", "arbitrary"),
                                             vmem_limit_bytes=VMEM_LIMIT_BYTES),
        name="diff_attention",
    )(*args)


PROJ_ROW_TILE = 1024
PROJ_COL_TILE = 640
OUT_ROW_TILE = 512
FF_TILE = 512
D_FF = 2 * D_MODEL


def _in_proj_kernel(x_ref, g_ref, sc_ref, sh_ref, w_ref, o_ref, h_ref):
    @pl.when(pl.program_id(1) == 0)
    def _():
        x = x_ref[...]
        y = x * lax.rsqrt(jnp.mean(x * x, axis=1, keepdims=True) + EPS) * g_ref[...]
        h_ref[...] = (y * (1.0 + sc_ref[0]) + sh_ref[0]).astype(bf16)

    o_ref[...] = jnp.dot(h_ref[...], w_ref[0], preferred_element_type=f32)


def in_projection(x, gain, scale, shift, w_p, layer, rows_per_mod):
    n, D = x.shape
    TM = min(n, PROJ_ROW_TILE)
    mod = pl.BlockSpec((1, 1, D), lambda i, j: (i * TM // rows_per_mod, 0, 0))
    return pl.pallas_call(
        _in_proj_kernel,
        grid=(n // TM, N_PROJ // PROJ_COL_TILE),
        in_specs=[pl.BlockSpec((TM, D), lambda i, j: (i, 0)), pl.BlockSpec((1, D), lambda i, j: (0, 0)), mod, mod,
                  pl.BlockSpec((1, D, PROJ_COL_TILE), lambda i, j: (layer, 0, j))],
        out_specs=pl.BlockSpec((TM, PROJ_COL_TILE), lambda i, j: (i, j)),
        out_shape=jax.ShapeDtypeStruct((n, N_PROJ), f32),
        scratch_shapes=[pltpu.VMEM((TM, D), bf16)],
        compiler_params=pltpu.CompilerParams(dimension_semantics=("arbitrary", "arbitrary"),
                                             vmem_limit_bytes=VMEM_LIMIT_BYTES),
        name="in_projection",
    )(x, gain.reshape(1, D), scale, shift, w_p)


def _out_proj_kernel(x_ref, m0_ref, m1_ref, m2_ref, m3_ref, w_ref, g1_ref, ng_ref, sc_ref, sh_ref, rw_ref,
                     xo_ref, h_ref, aff_ref, wb_ref):
    @pl.when(pl.program_id(0) == 0)
    def _():
        wb_ref[...] = w_ref[0].astype(bf16)

    out = None
    for j, m_ref in enumerate((m0_ref, m1_ref, m2_ref, m3_ref)):
        part = jnp.dot(m_ref[...].astype(bf16), wb_ref[j * W_GROUP:(j + 1) * W_GROUP, :], preferred_element_type=f32)
        out = part if out is None else out + part
    x = x_ref[...] + g1_ref[0] * out
    xo_ref[...] = x
    h = x * lax.rsqrt(jnp.mean(x * x, axis=1, keepdims=True) + EPS) * ng_ref[...]
    h = h * (1.0 + sc_ref[0]) + sh_ref[0]
    h_ref[...] = h
    logits = jnp.dot(h, rw_ref[0], precision=HIGHEST, preferred_element_type=f32)
    e = jnp.exp(logits - jnp.max(logits, axis=1, keepdims=True))
    aff_ref[...] = e / jnp.sum(e, axis=1, keepdims=True)


def out_projection(x, mixed, w_out, gate1, gain2, scale2, shift2, router_w, layer, rows_per_mod):
    n, D = x.shape
    TM = OUT_ROW_TILE
    row = lambda width: pl.BlockSpec((TM, width), lambda i: (i, 0))
    const = lambda *shape: pl.BlockSpec(shape, lambda i: (0,) * len(shape))
    mod = pl.BlockSpec((1, 1, D), lambda i: (i * TM // rows_per_mod, 0, 0))
    return pl.pallas_call(
        _out_proj_kernel,
        grid=(n // TM,),
        in_specs=[row(D), row(W_GROUP), row(W_GROUP), row(W_GROUP), row(W_GROUP),
                  pl.BlockSpec((1, D, D), lambda i: (layer, 0, 0)), mod, const(1, D), mod, mod,
                  pl.BlockSpec((1, D, N_EXPERTS), lambda i: (layer, 0, 0))],
        out_specs=[row(D), row(D), row(N_EXPERTS)],
        out_shape=[jax.ShapeDtypeStruct((n, D), f32), jax.ShapeDtypeStruct((n, D), f32),
                   jax.ShapeDtypeStruct((n, N_EXPERTS), f32)],
        scratch_shapes=[pltpu.VMEM((D, D), bf16)],
        compiler_params=pltpu.CompilerParams(dimension_semantics=("arbitrary",), vmem_limit_bytes=VMEM_LIMIT_BYTES),
        name="out_projection",
    )(x, *mixed, w_out, gate1, gain2.reshape(1, D), scale2, shift2, router_w)


def _experts_kernel(xc_ref, xl_ref, gc_ref, gl_ref, g2_ref, wg_ref, wu_ref, wd_ref, yc_ref, yl_ref, ac_ref, al_ref):
    f = pl.program_id(1)
    wg = wg_ref[0, 0].astype(bf16)
    wu = wu_ref[0, 0].astype(bf16)
    wd = wd_ref[0, 0].astype(bf16)
    for x_ref, acc_ref in ((xc_ref, ac_ref), (xl_ref, al_ref)):
        x = x_ref[0].astype(bf16)
        hidden = jax.nn.silu(jnp.dot(x, wg, preferred_element_type=f32)) * jnp.dot(x, wu, preferred_element_type=f32)
        part = jnp.dot(hidden.astype(bf16), wd, preferred_element_type=f32)

        @pl.when(f == 0)
        def _(acc_ref=acc_ref, part=part):
            acc_ref[...] = part

        @pl.when(f > 0)
        def _(acc_ref=acc_ref, part=part):
            acc_ref[...] += part

    @pl.when(f == pl.num_programs(1) - 1)
    def _():
        yc_ref[0] = ac_ref[...] * gc_ref[0] * g2_ref[0]
        yl_ref[0] = al_ref[...] * gl_ref[0]


def expert_ffn(xe_c, xe_l, g_c, g_l, gate2_c, w_gate, w_up, w_down, layer):
    E, Cc, D = xe_c.shape
    Cl = xe_l.shape[1]
    tok = lambda C, width: pl.BlockSpec((1, C, width), lambda e, f: (e, 0, 0))
    return pl.pallas_call(
        _experts_kernel,
        grid=(E, D_FF // FF_TILE),
        in_specs=[tok(Cc, D), tok(Cl, D), tok(Cc, 1), tok(Cl, 1), pl.BlockSpec((1, 1, D), lambda e, f: (0, 0, 0)),
                  pl.BlockSpec((1, 1, D, FF_TILE), lambda e, f: (layer, e, 0, f)),
                  pl.BlockSpec((1, 1, D, FF_TILE), lambda e, f: (layer, e, 0, f)),
                  pl.BlockSpec((1, 1, FF_TILE, D), lambda e, f: (layer, e, f, 0))],
        out_specs=[tok(Cc, D), tok(Cl, D)],
        out_shape=[jax.ShapeDtypeStruct((E, Cc, D), f32), jax.ShapeDtypeStruct((E, Cl, D), f32)],
        scratch_shapes=[pltpu.VMEM((Cc, D), f32), pltpu.VMEM((Cl, D), f32)],
        compiler_params=pltpu.CompilerParams(dimension_semantics=("arbitrary", "arbitrary"),
                                             vmem_limit_bytes=VMEM_LIMIT_BYTES),
        name="expert_ffn",
    )(xe_c, xe_l, g_c, g_l, gate2_c, w_gate, w_up, w_down)


def _permute_w_in(w):
    gate0 = 4 * W_GROUP
    pad = jnp.zeros(w.shape[:-1] + (GATE_LANES - 4 * H_M,), w.dtype)
    return jnp.concatenate([w[..., :gate0], w[..., gate0 + 4 * H_M:], w[..., gate0:gate0 + 4 * H_M], pad],
                           axis=-1).astype(bf16)


def token_mixers(proj, lp, s5p, lam_init, states, cache, layer):
    B, L, _ = proj.shape
    mC0, mn0, mm0, s5r0, s5i0, R0 = states
    gates_t = jnp.swapaxes(proj[:, :, 12 * W_GROUP:12 * W_GROUP + 4 * H_M], 1, 2)
    hm, mC, mn, mm = mlstm_mixer(proj, gates_t, mC0, mn0, mm0, lp)
    su = proj[:, :, PROJ_BLOCKS['su'] * W_GROUP:(PROJ_BLOCKS['su'] + 1) * W_GROUP]
    ys, s5r, s5i = s5_mixer(su, s5r0, s5i0, s5p, lp['s5_d'], lp['s5_glu_b'], layer)
    hr, R = retention_mixer(proj, R0, lp)
    attn = diff_attn_mixer(proj, cache, lp, lam_init, layer)
    mixed = [a.reshape(B * L, W_GROUP) for a in (hm, ys, hr, attn[0])]
    new_ctx = None
    if cache is None:
        v = proj[:, :, PROJ_BLOCKS['dv'] * W_GROUP:(PROJ_BLOCKS['dv'] + 1) * W_GROUP]
        new_ctx = (mC, mn, mm, s5r, s5i, R, attn[1].reshape(B, L, 2 * H_D, DH_D), v.reshape(B, L, H_D, 2 * DH_D))
    return mixed, new_ctx


def _route(aff, h2):
    n = aff.shape[0]
    gates, idx = lax.top_k(aff.T, CAPACITY_FACTOR * n // N_EXPERTS)
    return gates[..., None], idx, h2[idx]


def _combine(x, ye, idx, gate2, B):
    n, D = x.shape
    if gate2 is None:
        return x.at[idx.reshape(-1)].add(ye.reshape(-1, D))
    y = jnp.zeros_like(x).at[idx.reshape(-1)].add(ye.reshape(-1, D))
    return (x.reshape(B, n // B, D) + gate2 * y.reshape(B, n // B, D)).reshape(n, D)


PER_LAYER = ('norm1_g', 'norm2_g', 'mlstm_gate_b', 'mlstm_norm_g', 's5_d', 's5_glu_b', 'ret_decay',
             'ret_gn_g', 'diff_qk_norm', 'diff_lambda', 'diff_subln_g')


def kernel(x_prompt, x_sample, state_mlstm_c, state_mlstm_n, state_mlstm_m, state_s5_re, state_s5_im, state_ret, cache_diff_k, cache_diff_v, c, c_ctx, norm1_g, norm2_g, ada_w, ada_b, w_in, w_out, mlstm_gate_b, mlstm_norm_g, s5_lambda_re, s5_lambda_im, s5_log_step, s5_b_re, s5_b_im, s5_c_re, s5_c_im, s5_d, s5_glu_w, s5_glu_b, ret_decay, ret_gn_g, diff_qk_norm, diff_lambda, diff_subln_g, router_w, exp_w_gate, exp_w_up, exp_w_down):
    weights = dict(norm1_g=norm1_g, norm2_g=norm2_g, mlstm_gate_b=mlstm_gate_b,
                   mlstm_norm_g=mlstm_norm_g, s5_d=s5_d, s5_glu_b=s5_glu_b, ret_decay=ret_decay, ret_gn_g=ret_gn_g,
                   diff_qk_norm=diff_qk_norm, diff_lambda=diff_lambda, diff_subln_g=diff_subln_g)
    w_in_p = _permute_w_in(w_in)
    s5p = _s5_prepare(s5_lambda_re, s5_lambda_im, s5_log_step, s5_b_re, s5_b_im, s5_c_re, s5_c_im, s5_glu_w,
                      (x_prompt.shape[1] // SUBLANES, x_sample.shape[1] // SUBLANES))
    Bc, Lc, D = x_prompt.shape
    Bl, Ll, _ = x_sample.shape
    xc = x_prompt.reshape(Bc * Lc, D)
    xl = x_sample.reshape(Bl * Ll, D)
    zero_states = (jnp.zeros((Bc, 2, H_M, DH_M, DH_M), f32), jnp.zeros((Bc, 2, H_M, DH_M), f32),
                   jnp.zeros((Bc, 2, H_M), f32), jnp.zeros((Bc, 2, G_S5, P_S5), f32),
                   jnp.zeros((Bc, 2, G_S5, P_S5), f32), jnp.zeros((Bc, 2, H_R, DH_R, DH_R), f32))
    cache = (cache_diff_k.reshape(cache_diff_k.shape[:3] + (W_GROUP,)),
             cache_diff_v.reshape(cache_diff_v.shape[:3] + (W_GROUP,)))
    cvec = jnp.concatenate([c_ctx[None, :], c], axis=0)
    outs = [[] for _ in range(8)]
    for l in range(DEPTH):
        lp = {name: weights[name][l] for name in PER_LAYER}
        lam_init = 0.8 - 0.6 * math.exp(-0.3 * l)
        mods = jnp.split((jax.nn.silu(cvec) @ ada_w[l] + ada_b[l])[:, None, :], 6, axis=-1)
        lat_states = (state_mlstm_c[:, l], state_mlstm_n[:, l], state_mlstm_m[:, l], state_s5_re[:, l],
                      state_s5_im[:, l], state_ret[:, l])
        routed = []
        for x, B, L, sel, states, kv in ((xc, Bc, Lc, slice(0, 1), zero_states, None),
                                        (xl, Bl, Ll, slice(1, 1 + Bl), lat_states, cache)):
            sh1, sc1, g1, sh2, sc2, g2 = (m[sel] for m in mods)
            rows_per_mod = x.shape[0] // sh1.shape[0]
            proj = in_projection(x, lp['norm1_g'], sc1, sh1, w_in_p, l, rows_per_mod).reshape(B, L, N_PROJ)
            mixed, new_ctx = token_mixers(proj, lp, s5p, lam_init, states, kv, l)
            if new_ctx is not None:
                for acc, t in zip(outs, new_ctx):
                    acc.append(t)
            x1, h2, aff = out_projection(x, mixed, w_out, g1, lp['norm2_g'], sc2, sh2, router_w, l, rows_per_mod)
            routed.append((x1, g2, sh1.shape[0]) + _route(aff, h2))
        (x1c, g2c, nbc, gc, idxc, xec), (x1l, g2l, nbl, gl, idxl, xel) = routed
        yec, yel = expert_ffn(xec, xel, gc, gl, g2c, exp_w_gate, exp_w_up, exp_w_down, l)
        xc = _combine(x1c, yec, idxc, None, nbc)
        xl = _combine(x1l, yel, idxl, g2l, nbl)
    return (xc.reshape(Bc, Lc, D), xl.reshape(Bl, Ll, D)) + tuple(jnp.stack(o, axis=1) for o in outs)
```

```python
import functools
import math

import jax
import jax.numpy as jnp
import numpy as np
from jax import lax
from jax.experimental import pallas as pl
from jax.experimental.pallas import tpu as pltpu

D_MODEL = 1024
DEPTH = 4
GRID_W = 64
W_GROUP = 256
H_M = 4
DH_M = 64
S5_CH = 16
G_S5 = 16
P_S5 = 64
S5_STATE = G_S5 * P_S5
H_R = 4
DH_R = 64
H_D = 4
DH_D = 32
N_EXPERTS = 16
CAPACITY_FACTOR = 2
ROPE_BASE = 10000.0
EPS = 1e-6
SUBLANES = 8
VMEM_LIMIT_BYTES = 56 * 1024 * 1024

f32 = jnp.float32
bf16 = jnp.bfloat16
HIGHEST = lax.Precision.HIGHEST
NEG_INF = float("-inf")


def _gelu_tanh(x):
    return 0.5 * x * (1.0 + jnp.tanh(math.sqrt(2.0 / math.pi) * (x + 0.044715 * (x * x * x))))


def _s5_kernel(su_ref, x0r_ref, x0i_ref, wb_ref, wc_ref, lb_ref, pw_ref, d_ref, gw_ref, gb_ref,
               y_ref, xr_ref, xi_ref, st_ref, *, seq_len):
    n_steps = seq_len // SUBLANES
    su = su_ref[0]
    y_ref[0] = su * d_ref[...]
    row = lax.broadcasted_iota(jnp.int32, (SUBLANES, S5_STATE), 0)
    zeros = jnp.zeros((SUBLANES, S5_STATE), f32)
    for d in range(2):
        st_ref[...] = jnp.dot(su.astype(bf16), wb_ref[0, d], preferred_element_type=f32)
        lbr = jnp.broadcast_to(lb_ref[0, d, 0:1, :], (SUBLANES, S5_STATE))
        lbi = jnp.broadcast_to(lb_ref[0, d, 1:2, :], (SUBLANES, S5_STATE))

        def rows_of(k, d=d):
            kk = k if d == 0 else n_steps - 1 - k
            return pl.ds(pl.multiple_of(kk * SUBLANES, SUBLANES), SUBLANES)

        def scan_step(k, carry, lbr=lbr, lbi=lbi, rows_of=rows_of):
            xr, xi = carry
            r = rows_of(k)
            nxr = lbr * xr - lbi * xi + st_ref[r, 0:S5_STATE]
            nxi = lbr * xi + lbi * xr + st_ref[r, S5_STATE:2 * S5_STATE]
            st_ref[r, 0:S5_STATE] = nxr
            st_ref[r, S5_STATE:2 * S5_STATE] = nxi
            return nxr, nxi

        fr, fi = lax.fori_loop(0, n_steps, scan_step, (zeros, zeros))

        cr = x0r_ref[0, d:d + 1, :]
        ci = x0i_ref[0, d:d + 1, :]
        plr = pw_ref[0, d, 0:1, :]
        pli = pw_ref[0, d, 1:2, :]
        cmr, cmi = zeros, zeros
        for i in (range(SUBLANES) if d == 0 else reversed(range(SUBLANES))):
            cmr = jnp.where(row == i, cr, cmr)
            cmi = jnp.where(row == i, ci, cmi)
            cr, ci = (plr * cr - pli * ci + fr[i:i + 1], plr * ci + pli * cr + fi[i:i + 1])
        xr_ref[0, d:d + 1, :] = cr
        xi_ref[0, d:d + 1, :] = ci

        def fix_step(k, carry, lbr=lbr, lbi=lbi, cmr=cmr, cmi=cmi, rows_of=rows_of):
            pr, pi = carry
            r = rows_of(k)
            st_ref[r, 0:S5_STATE] = st_ref[r, 0:S5_STATE] + (pr * cmr - pi * cmi)
            st_ref[r, S5_STATE:2 * S5_STATE] = st_ref[r, S5_STATE:2 * S5_STATE] + (pr * cmi + pi * cmr)
            return pr * lbr - pi * lbi, pr * lbi + pi * lbr

        lax.fori_loop(0, n_steps, fix_step, (lbr, lbi))
        y_ref[0] += jnp.dot(st_ref[...].astype(bf16), wc_ref[0, d], preferred_element_type=f32)

    ys = _gelu_tanh(y_ref[0])
    gate = jax.nn.sigmoid(jnp.dot(ys.astype(bf16), gw_ref[0], preferred_element_type=f32) + gb_ref[...])
    y_ref[0] = ys * gate


def _s5_prepare(lam_re, lam_im, log_step, b_re, b_im, c_re, c_im, glu_w, n_steps_list):
    dt = jnp.exp(log_step)[..., None]
    mag = jnp.exp(lam_re * dt)
    ang = lam_im * dt
    lb_re, lb_im = mag * jnp.cos(ang), mag * jnp.sin(ang)
    nr, ni = lb_re - 1.0, lb_im
    den = lam_re * lam_re + lam_im * lam_im
    f_re = (nr * lam_re + ni * lam_im) / den
    f_im = (ni * lam_re - nr * lam_im) / den
    bb_re = f_re[..., None] * b_re[:, None] - f_im[..., None] * b_im[:, None]
    bb_im = f_re[..., None] * b_im[:, None] + f_im[..., None] * b_re[:, None]
    eye = jnp.eye(G_S5, dtype=f32)[:, None, :, None]

    def block_diag(a):
        return (a[:, :, :, :, None, :] * eye).reshape(a.shape[:2] + (G_S5 * a.shape[3], G_S5 * a.shape[4]))

    wb = jnp.concatenate([block_diag(jnp.swapaxes(bb_re, 3, 4)), block_diag(jnp.swapaxes(bb_im, 3, 4))],
                         axis=-1).astype(bf16)
    wc = jnp.concatenate([block_diag(jnp.swapaxes(c_re, 3, 4)), -block_diag(jnp.swapaxes(c_im, 3, 4))],
                         axis=2).astype(bf16)
    lead = lb_re.shape[:2]
    lb = jnp.stack([lb_re.reshape(lead + (S5_STATE,)), lb_im.reshape(lead + (S5_STATE,))], axis=2)
    pr, pi = lb[:, :, 0], lb[:, :, 1]
    tables = {}
    for j in range(int(math.log2(max(n_steps_list))) + 1):
        if 2 ** j in n_steps_list:
            tables[2 ** j] = jnp.stack([pr, pi], axis=2)
        pr, pi = pr * pr - pi * pi, 2.0 * pr * pi
    return wb, wc, lb, tables, glu_w.astype(bf16)


def s5_mixer(su, x0r, x0i, s5p, s5_d, glu_b, layer):
    B, L, _ = su.shape
    n_steps = L // SUBLANES
    wb, wc, lb, tables, glu_w = s5p
    pw = tables[n_steps]
    su_p = su.reshape(B, SUBLANES, n_steps, W_GROUP).transpose(0, 2, 1, 3).reshape(B, L, W_GROUP)
    full = lambda *shape: pl.BlockSpec(shape, lambda b: (0,) * len(shape))
    per_b = lambda *shape: pl.BlockSpec((1,) + shape, lambda b: (b,) + (0,) * len(shape))
    per_layer = lambda *shape: pl.BlockSpec((1,) + shape, lambda b: (layer,) + (0,) * len(shape))
    y_p, xr, xi = pl.pallas_call(
        functools.partial(_s5_kernel, seq_len=L),
        grid=(B,),
        in_specs=[per_b(L, W_GROUP), per_b(2, S5_STATE), per_b(2, S5_STATE),
                  per_layer(2, W_GROUP, 2 * S5_STATE), per_layer(2, 2 * S5_STATE, W_GROUP),
                  per_layer(2, 2, S5_STATE), per_layer(2, 2, S5_STATE),
                  full(1, W_GROUP), per_layer(W_GROUP, W_GROUP), full(1, W_GROUP)],
        out_specs=[per_b(L, W_GROUP), per_b(2, S5_STATE), per_b(2, S5_STATE)],
        out_shape=[jax.ShapeDtypeStruct((B, L, W_GROUP), f32),
                   jax.ShapeDtypeStruct((B, 2, S5_STATE), f32),
                   jax.ShapeDtypeStruct((B, 2, S5_STATE), f32)],
        scratch_shapes=[pltpu.VMEM((L, 2 * S5_STATE), f32)],
        compiler_params=pltpu.CompilerParams(dimension_semantics=("arbitrary",),
                                             vmem_limit_bytes=VMEM_LIMIT_BYTES),
        name="s5_mixer",
    )(su_p, x0r.reshape(B, 2, S5_STATE), x0i.reshape(B, 2, S5_STATE), wb, wc, lb, pw,
      s5_d.reshape(1, W_GROUP), glu_w, glu_b.reshape(1, W_GROUP))
    y = y_p.reshape(B, n_steps, SUBLANES, W_GROUP).transpose(0, 2, 1, 3).reshape(B, L, W_GROUP)
    return y, xr.reshape(B, 2, G_S5, P_S5), xi.reshape(B, 2, G_S5, P_S5)


Q_TILE = 256
PROJ_BLOCKS = dict(mq=0, mk=1, mv=2, mo=3, su=4, rq=5, rk=6, rv=7, rg=8, dq=9, dk=10, dv=11)
GATE_LANES = 128
N_PROJ = 12 * W_GROUP + GATE_LANES


def _log_sigmoid(x):
    return jnp.minimum(x, 0.0) - jnp.log1p(jnp.exp(-jnp.abs(x)))


def _group_ones(width, group):
    shift = int(math.log2(group))
    r = lax.broadcasted_iota(jnp.int32, (width, width), 0) >> shift
    c = lax.broadcasted_iota(jnp.int32, (width, width), 1) >> shift
    return (r == c).astype(bf16)


def _split3(x):
    hi = x.astype(bf16)
    r = x - hi.astype(f32)
    mid = r.astype(bf16)
    return hi, mid, (r - mid.astype(f32)).astype(bf16)


def _group_mean(x, ones, group):
    return sum(jnp.dot(p, ones, preferred_element_type=f32) for p in _split3(x)) * (1.0 / group)


def _dot_nt(a, b):
    return lax.dot_general(a.astype(bf16), b.astype(bf16), (((1,), (1,)), ((), ())), preferred_element_type=f32)


def _dot(a, b):
    return jnp.dot(a.astype(bf16), b.astype(bf16), preferred_element_type=f32)


def _proj_block(name, rows):
    j = PROJ_BLOCKS[name]
    return pl.BlockSpec((1, rows, W_GROUP), lambda b, qi, j=j, rows=rows: (b, qi if rows == Q_TILE else 0, j))


def _mlstm_kernel(q_ref, k_ref, v_ref, o_ref, g_ref, gt_ref, gb_ref, gbt_ref, ng_ref, c0_ref, n0_ref, m0_ref,
                  h_ref, c_ref, n_ref, m_ref, gl_ref, gu_ref, rc_ref, vt_ref, ht_ref, *, seq_len, q_tile):
    L, TQ = seq_len, q_tile
    nq = L // TQ
    qi = pl.program_id(1)
    grow = gt_ref[0] + gbt_ref[...]

    @pl.when(qi == 0)
    def _():
        ss = lax.broadcasted_iota(jnp.int32, (L, L), 0)
        tt = lax.broadcasted_iota(jnp.int32, (L, L), 1)
        tri_le = (ss <= tt).astype(bf16)
        tri_ge = (ss >= tt).astype(bf16)
        rows = _split3(_log_sigmoid(grow))
        gl = sum(jnp.dot(p, tri_le, preferred_element_type=f32) for p in rows)
        gu = sum(jnp.dot(p, tri_ge, preferred_element_type=f32) for p in rows)
        for j in range(nq):
            gl_ref[j] = gl[:, j * TQ:(j + 1) * TQ]
            gu_ref[j] = gu[:, j * TQ:(j + 1) * TQ]
        gcol = g_ref[0] + gb_ref[...]
        cols = _split3(_log_sigmoid(gcol))
        glc = sum(jnp.dot(tri_ge, p, preferred_element_type=f32) for p in cols)
        guc = sum(jnp.dot(tri_le, p, preferred_element_type=f32) for p in cols)
        lane = lax.broadcasted_iota(jnp.int32, (L, GATE_LANES), 1)
        rc_ref[...] = pltpu.roll(gcol, 4, 1) - jnp.where(lane < 8, glc, guc)

        vt = v_ref[0].T
        kk = k_ref[0] * (DH_M ** -0.5)
        one_row = (lax.broadcasted_iota(jnp.int32, (DH_M, L), 0) == 0).astype(bf16)
        for h in range(H_M):
            hs = slice(h * DH_M, (h + 1) * DH_M)
            vt_ref[2 * h * DH_M:(2 * h + 1) * DH_M, :] = vt[hs, :].astype(bf16)
            vt_ref[(2 * h + 1) * DH_M:(2 * h + 2) * DH_M, :] = one_row
        for d in range(2):
            g_all = gl if d == 0 else gu
            for h in range(H_M):
                hs = slice(h * DH_M, (h + 1) * DH_M)
                ii, fi = 8 * d + h, 8 * d + 4 + h
                g_row = g_all[fi:fi + 1, :]
                g_tot = g_row[:, L - 1:L] if d == 0 else g_row[:, 0:1]
                wlog = g_tot - g_row + grow[ii:ii + 1, :]
                m0 = m0_ref[0, d:d + 1, h:h + 1]
                m_new = jnp.maximum(g_tot + m0, jnp.max(wlog, axis=1, keepdims=True))
                decay = jnp.exp(g_tot + m0 - m_new)
                w = jnp.exp(wlog - m_new)
                kh = kk[:, hs]
                c_ref[0, d, h] = decay * c0_ref[0, d, h] + _dot(vt[hs, :] * w, kh)
                n_upd = jnp.dot(jnp.broadcast_to(w, (SUBLANES, L)), kh, precision=HIGHEST,
                                preferred_element_type=f32)[0:1, :]
                n_ref[0, d, h:h + 1, :] = decay * n0_ref[0, d, h:h + 1, :] + n_upd
                m_ref[0, d:d + 1, h:h + 1] = m_new

    gl_t = gl_ref[qi]
    gu_t = gu_ref[qi]
    s_idx = lax.broadcasted_iota(jnp.int32, (L, TQ), 0)
    t_idx = qi * TQ + lax.broadcasted_iota(jnp.int32, (L, TQ), 1)
    low = s_idx <= t_idx
    upp = s_idx >= t_idx
    qt = q_ref[0].T.astype(bf16)
    k = (k_ref[0] * (DH_M ** -0.5)).astype(bf16)
    rc = rc_ref[...]
    row0 = lax.broadcasted_iota(jnp.int32, (DH_M, DH_M), 0) == 0
    for h in range(H_M):
        hs = slice(h * DH_M, (h + 1) * DH_M)
        qth = qt[hs, :]
        vta = vt_ref[2 * h * DH_M:(2 * h + 2) * DH_M, :]
        s0 = jnp.dot(k[:, hs], qth, preferred_element_type=f32)
        h_sum = None
        for d in range(2):
            fi = 8 * d + 4 + h
            g_t = (gl_t if d == 0 else gu_t)[fi:fi + 1, :]
            dlog = jnp.where(low if d == 0 else upp, rc[:, fi:fi + 1] + g_t, NEG_INF)
            inter = g_t + m0_ref[0, d:d + 1, h:h + 1]
            m_t = jnp.maximum(inter, jnp.max(dlog, axis=0, keepdims=True))
            p = s0 * jnp.exp(dlog - m_t)
            a = jnp.exp(inter - m_t)
            c0n0 = jnp.concatenate([c0_ref[0, d, h], jnp.where(row0, n0_ref[0, d, h:h + 1, :], 0.0)], axis=0)
            numden = (jnp.dot(vta, p.astype(bf16), preferred_element_type=f32)
                      + a * jnp.dot(c0n0.astype(bf16), qth, preferred_element_type=f32))
            scale = 1.0 / jnp.maximum(jnp.abs(numden[DH_M:DH_M + 1, :]), jnp.exp(-m_t))
            hd = numden[0:DH_M, :] * scale
            h_sum = hd if h_sum is None else h_sum + hd
        ht_ref[hs, :] = h_sum * lax.rsqrt(jnp.mean(h_sum * h_sum, axis=0, keepdims=True) + EPS)
    h_ref[0] = jax.nn.sigmoid(o_ref[0]) * (ht_ref[...].T * ng_ref[...])


def mlstm_mixer(proj, gates_t, c0, n0, m0, lp):
    B, L, _ = proj.shape
    TQ = min(L, Q_TILE)
    gb = lp['mlstm_gate_b'].reshape(1, 4 * H_M)
    const = lambda *shape: pl.BlockSpec(shape, lambda b, qi: (0,) * len(shape))
    per_b = lambda *shape: pl.BlockSpec((1,) + shape, lambda b, qi: (b,) + (0,) * len(shape))
    return pl.pallas_call(
        functools.partial(_mlstm_kernel, seq_len=L, q_tile=TQ),
        grid=(B, L // TQ),
        in_specs=[_proj_block('mq', TQ), _proj_block('mk', L), _proj_block('mv', L), _proj_block('mo', TQ),
                  pl.BlockSpec((1, L, GATE_LANES), lambda b, qi: (b, 0, 12 * W_GROUP // GATE_LANES)),
                  per_b(4 * H_M, L), const(1, GATE_LANES), const(4 * H_M, 1), const(1, W_GROUP),
                  per_b(2, H_M, DH_M, DH_M), per_b(2, H_M, DH_M), per_b(2, H_M)],
        out_specs=[pl.BlockSpec((1, TQ, W_GROUP), lambda b, qi: (b, qi, 0)),
                   per_b(2, H_M, DH_M, DH_M), per_b(2, H_M, DH_M), per_b(2, H_M)],
        out_shape=[jax.ShapeDtypeStruct((B, L, W_GROUP), f32),
                   jax.ShapeDtypeStruct((B, 2, H_M, DH_M, DH_M), f32),
                   jax.ShapeDtypeStruct((B, 2, H_M, DH_M), f32),
                   jax.ShapeDtypeStruct((B, 2, H_M), f32)],
        scratch_shapes=[pltpu.VMEM((L // TQ, 4 * H_M, TQ), f32), pltpu.VMEM((L // TQ, 4 * H_M, TQ), f32),
                        pltpu.VMEM((L, GATE_LANES), f32), pltpu.VMEM((2 * W_GROUP, L), bf16),
                        pltpu.VMEM((W_GROUP, TQ), f32)],
        compiler_params=pltpu.CompilerParams(dimension_semantics=("arbitrary", "arbitrary"),
                                             vmem_limit_bytes=VMEM_LIMIT_BYTES),
        name="mlstm_mixer",
    )(proj, proj, proj, proj, proj, gates_t, jnp.pad(gb, ((0, 0), (0, GATE_LANES - 4 * H_M))),
      gb.reshape(4 * H_M, 1), lp['mlstm_norm_g'].reshape(1, W_GROUP), c0, n0, m0)


def _retention_kernel(lg_ref, q_ref, k_ref, v_ref, g_ref, gn_ref, r0_ref, h_ref, r_ref, *, seq_len, q_tile):
    L, TQ = seq_len, q_tile
    qi = pl.program_id(1)
    t_col = qi * TQ + lax.broadcasted_iota(jnp.int32, (TQ, 1), 0)
    rel = (qi * TQ + lax.broadcasted_iota(jnp.int32, (TQ, L), 0)
           - lax.broadcasted_iota(jnp.int32, (TQ, L), 1)).astype(f32)
    q = q_ref[0]
    k = k_ref[0] * (DH_R ** -0.5)
    v = v_ref[0]
    ones = _group_ones(W_GROUP, DH_R)
    for h in range(H_R):
        hs = slice(h * DH_R, (h + 1) * DH_R)
        lgf, lgb = lg_ref[0, h], lg_ref[1, h]
        qh, kh, vh = q[:, hs], k[:, hs], v[:, hs]
        decay = jnp.where(rel > 0.0, jnp.exp(lgf * jnp.maximum(rel, 0.0)),
                          jnp.where(rel < 0.0, jnp.exp(lgb * jnp.maximum(-rel, 0.0)), 2.0))
        o = _dot(_dot_nt(qh, kh) * decay, vh)
        xi_f = jnp.exp(lgf * (t_col + 1).astype(f32))
        xi_b = jnp.exp(lgb * (L - t_col).astype(f32))
        o = o + xi_f * _dot(qh, r0_ref[0, 0, h]) + xi_b * _dot(qh, r0_ref[0, 1, h])
        h_ref[0, :, hs] = o
    o = h_ref[0]
    oc = o - _group_mean(o, ones, DH_R)
    y = oc * lax.rsqrt(_group_mean(oc * oc, ones, DH_R) + EPS) * gn_ref[...]
    h_ref[0] = y * jax.nn.silu(g_ref[0])

    @pl.when(qi == 0)
    def _():
        kt = k.T
        s_row = lax.broadcasted_iota(jnp.int32, (1, L), 1).astype(f32)
        for d in range(2):
            for h in range(H_R):
                hs = slice(h * DH_R, (h + 1) * DH_R)
                lg = lg_ref[d, h]
                zeta = jnp.exp(lg * ((L - 1.0) - s_row)) if d == 0 else jnp.exp(lg * s_row)
                r_ref[0, d, h] = jnp.exp(lg * float(L)) * r0_ref[0, d, h] + _dot(kt[hs, :] * zeta, v[:, hs])


def retention_mixer(proj, r0, lp):
    B, L, _ = proj.shape
    TQ = min(L, Q_TILE)
    log_gamma = -jnp.exp(lp['ret_decay'])
    per_b = lambda *shape: pl.BlockSpec((1,) + shape, lambda b, qi: (b,) + (0,) * len(shape))
    return pl.pallas_call(
        functools.partial(_retention_kernel, seq_len=L, q_tile=TQ),
        grid=(B, L // TQ),
        in_specs=[pl.BlockSpec(memory_space=pltpu.SMEM),
                  _proj_block('rq', TQ), _proj_block('rk', L), _proj_block('rv', L), _proj_block('rg', TQ),
                  pl.BlockSpec((1, W_GROUP), lambda b, qi: (0, 0)), per_b(2, H_R, DH_R, DH_R)],
        out_specs=[pl.BlockSpec((1, TQ, W_GROUP), lambda b, qi: (b, qi, 0)), per_b(2, H_R, DH_R, DH_R)],
        out_shape=[jax.ShapeDtypeStruct((B, L, W_GROUP), f32),
                   jax.ShapeDtypeStruct((B, 2, H_R, DH_R, DH_R), f32)],
        compiler_params=pltpu.CompilerParams(dimension_semantics=("arbitrary", "arbitrary"),
                                             vmem_limit_bytes=VMEM_LIMIT_BYTES),
        name="retention_mixer",
    )(log_gamma, proj, proj, proj, proj, lp['ret_gn_g'].reshape(1, W_GROUP), r0)


def _rope_tables(L):
    half = DH_D // 2
    freqs = ROPE_BASE ** (-np.arange(0, half, 2, dtype=np.float64) / half)
    pos = np.arange(L)
    row, col = (pos // GRID_W).astype(np.float64), (pos % GRID_W).astype(np.float64)
    ang = np.concatenate([np.tile(row[:, None] * freqs, (1, 2)), np.tile(col[:, None] * freqs, (1, 2))], axis=1)
    sign = np.tile(np.concatenate([-np.ones(half // 2), np.ones(half // 2)]), 2)
    cos = np.tile(np.cos(ang), (1, 2 * H_D))
    sin = np.tile(np.sin(ang) * sign, (1, 2 * H_D))
    return jnp.asarray(cos, f32), jnp.asarray(sin, f32)


def _swap_pairs(x):
    parts = []
    for j in range(x.shape[1] // 128):
        xs = x[:, j * 128:(j + 1) * 128]
        lane = lax.broadcasted_iota(jnp.int32, xs.shape, 1)
        parts.append(jnp.where((lane & 15) < 8, pltpu.roll(xs, 120, 1), pltpu.roll(xs, 8, 1)))
    return jnp.concatenate(parts, axis=1)


def _qk_norm(x, gain, ones):
    return x * lax.rsqrt(_group_mean(x * x, ones, DH_D) + EPS) * gain


def _diff_attn_kernel(lam_ref, q_ref, k_ref, v_ref, qkg_ref, sg_ref, *rest, seq_len, q_tile, past_len, out_scale):
    L, TQ, P = seq_len, q_tile, past_len
    if P:
        kc_ref, vc_ref, cos_ref, sin_ref, h_ref, ka_ref, vt_ref, ot_ref = rest
    else:
        h_ref, kn_ref, ka_ref, vt_ref, ot_ref = rest
    qi = pl.program_id(1)
    ones = _group_ones(W_GROUP, DH_D)

    @pl.when(qi == 0)
    def _():
        kn = _qk_norm(k_ref[0], qkg_ref[1:2, :], ones)
        if P:
            kn = kn * cos_ref[...] + _swap_pairs(kn) * sin_ref[...]
            ka_ref[0:P, :] = kc_ref[0, 0].astype(bf16)
            vt_ref[:, 0:P] = vc_ref[0, 0].T.astype(bf16)
        else:
            kn_ref[0] = kn
        ka_ref[P:P + L, :] = kn.astype(bf16)
        vt_ref[:, P:P + L] = v_ref[0].T.astype(bf16)

    qn = _qk_norm(q_ref[0], qkg_ref[0:1, :], ones)
    if P:
        rows = pl.ds(pl.multiple_of(qi * TQ, TQ), TQ)
        qn = qn * cos_ref[rows, :] + _swap_pairs(qn) * sin_ref[rows, :]
    qt = (qn * (DH_D ** -0.5)).T.astype(bf16)
    lam = lam_ref[0, 0]
    ka = ka_ref[...]
    for h in range(H_D):
        probs = []
        for j in range(2):
            cs = slice((2 * h + j) * DH_D, (2 * h + j + 1) * DH_D)
            s = jnp.dot(ka[:, cs], qt[cs, :], preferred_element_type=f32)
            e = jnp.exp(s - jnp.max(s, axis=0, keepdims=True))
            probs.append(e * (1.0 / jnp.sum(e, axis=0, keepdims=True)))
        vs = slice(h * 2 * DH_D, (h + 1) * 2 * DH_D)
        ot_ref[vs, :] = jnp.dot(vt_ref[vs, :], (probs[0] - lam * probs[1]).astype(bf16), preferred_element_type=f32)
    o = ot_ref[...].T
    ones_v = _group_ones(W_GROUP, 2 * DH_D)
    h_ref[0] = o * lax.rsqrt(_group_mean(o * o, ones_v, 2 * DH_D) + EPS) * (sg_ref[...] * out_scale)


def diff_attn_mixer(proj, cache, lp, lam_init, layer):
    B, L, _ = proj.shape
    TQ = min(L, Q_TILE)
    lv = lp['diff_lambda']
    lam = (jnp.exp(jnp.sum(lv[0] * lv[1])) - jnp.exp(jnp.sum(lv[2] * lv[3])) + lam_init).reshape(1, 1)
    qkg = jnp.tile(lp['diff_qk_norm'], (1, 2 * H_D))
    sg = jnp.tile(lp['diff_subln_g'], (H_D,)).reshape(1, W_GROUP)
    const = lambda *shape: pl.BlockSpec(shape, lambda b, qi: (0,) * len(shape))
    in_specs = [pl.BlockSpec(memory_space=pltpu.SMEM),
                _proj_block('dq', TQ), _proj_block('dk', L), _proj_block('dv', L), const(2, W_GROUP), const(1, W_GROUP)]
    args = [lam, proj, proj, proj, qkg, sg]
    out_specs = [pl.BlockSpec((1, TQ, W_GROUP), lambda b, qi: (b, qi, 0))]
    out_shape = [jax.ShapeDtypeStruct((B, L, W_GROUP), f32)]
    P = 0
    if cache is not None:
        ck, cv = cache
        P = ck.shape[2]
        cspec = pl.BlockSpec((1, 1, P, W_GROUP), lambda b, qi, layer=layer: (b, layer, 0, 0))
        cos, sin = _rope_tables(L)
        in_specs += [cspec, cspec, const(L, W_GROUP), const(L, W_GROUP)]
        args += [ck, cv, cos, sin]
    else:
        out_specs.append(pl.BlockSpec((1, L, W_GROUP), lambda b, qi: (b, 0, 0)))
        out_shape.append(jax.ShapeDtypeStruct((B, L, W_GROUP), f32))
    return pl.pallas_call(
        functools.partial(_diff_attn_kernel, seq_len=L, q_tile=TQ, past_len=P, out_scale=1.0 - lam_init),
        grid=(B, L // TQ),
        in_specs=in_specs, out_specs=out_specs, out_shape=out_shape,
        scratch_shapes=[pltpu.VMEM((P + L, W_GROUP), bf16), pltpu.VMEM((W_GROUP, P + L), bf16),
                        pltpu.VMEM((W_GROUP, TQ), f32)],
        compiler_params=pltpu.CompilerParams(dimension_semantics=("arbitrary", "arbitrary"),
                                             vmem_limit_bytes=VMEM_LIMIT_BYTES),
        name="diff_attention",
    )(*args)


PROJ_ROW_TILE = 512
OUT_ROW_TILE = 512
FF_TILE = 512
D_FF = 2 * D_MODEL


def _in_proj_kernel(x_ref, g_ref, sc_ref, sh_ref, w_ref, o_ref):
    x = x_ref[...]
    y = x * lax.rsqrt(jnp.mean(x * x, axis=1, keepdims=True) + EPS) * g_ref[...]
    h = (y * (1.0 + sc_ref[0]) + sh_ref[0]).astype(bf16)
    o_ref[...] = jnp.dot(h, w_ref[0], preferred_element_type=f32)


def in_projection(x, gain, scale, shift, w_p, layer, rows_per_mod):
    n, D = x.shape
    TM = min(n, PROJ_ROW_TILE)
    mod = pl.BlockSpec((1, 1, D), lambda i: (i * TM // rows_per_mod, 0, 0))
    return pl.pallas_call(
        _in_proj_kernel,
        grid=(n // TM,),
        in_specs=[pl.BlockSpec((TM, D), lambda i: (i, 0)), pl.BlockSpec((1, D), lambda i: (0, 0)), mod, mod,
                  pl.BlockSpec((1, D, N_PROJ), lambda i: (layer, 0, 0))],
        out_specs=pl.BlockSpec((TM, N_PROJ), lambda i: (i, 0)),
        out_shape=jax.ShapeDtypeStruct((n, N_PROJ), f32),
        compiler_params=pltpu.CompilerParams(dimension_semantics=("arbitrary",),
                                             vmem_limit_bytes=VMEM_LIMIT_BYTES),
        name="in_projection",
    )(x, gain.reshape(1, D), scale, shift, w_p)


def _out_proj_kernel(x_ref, m0_ref, m1_ref, m2_ref, m3_ref, w_ref, g1_ref, ng_ref, sc_ref, sh_ref, rw_ref,
                     xo_ref, h_ref, aff_ref, wb_ref):
    @pl.when(pl.program_id(0) == 0)
    def _():
        wb_ref[...] = w_ref[0].astype(bf16)

    out = None
    for j, m_ref in enumerate((m0_ref, m1_ref, m2_ref, m3_ref)):
        part = jnp.dot(m_ref[...].astype(bf16), wb_ref[j * W_GROUP:(j + 1) * W_GROUP, :], preferred_element_type=f32)
        out = part if out is None else out + part
    x = x_ref[...] + g1_ref[0] * out
    xo_ref[...] = x
    h = x * lax.rsqrt(jnp.mean(x * x, axis=1, keepdims=True) + EPS) * ng_ref[...]
    h = h * (1.0 + sc_ref[0]) + sh_ref[0]
    h_ref[...] = h
    h_hi = h.astype(bf16)
    h_lo = (h - h_hi.astype(f32)).astype(bf16)
    both = jnp.dot(h_hi, rw_ref[0], preferred_element_type=f32)
    logits = (both[:, 0:N_EXPERTS] + both[:, N_EXPERTS:2 * N_EXPERTS]
              + jnp.dot(h_lo, rw_ref[0], preferred_element_type=f32)[:, 0:N_EXPERTS])
    e = jnp.exp(logits - jnp.max(logits, axis=1, keepdims=True))
    aff_ref[...] = e / jnp.sum(e, axis=1, keepdims=True)


def out_projection(x, mixed, w_out, gate1, gain2, scale2, shift2, router_w2, layer, rows_per_mod):
    n, D = x.shape
    TM = OUT_ROW_TILE
    row = lambda width: pl.BlockSpec((TM, width), lambda i: (i, 0))
    const = lambda *shape: pl.BlockSpec(shape, lambda i: (0,) * len(shape))
    mod = pl.BlockSpec((1, 1, D), lambda i: (i * TM // rows_per_mod, 0, 0))
    return pl.pallas_call(
        _out_proj_kernel,
        grid=(n // TM,),
        in_specs=[row(D), row(W_GROUP), row(W_GROUP), row(W_GROUP), row(W_GROUP),
                  pl.BlockSpec((1, D, D), lambda i: (layer, 0, 0)), mod, const(1, D), mod, mod,
                  pl.BlockSpec((1, D, 2 * N_EXPERTS), lambda i: (layer, 0, 0))],
        out_specs=[row(D), row(D), row(N_EXPERTS)],
        out_shape=[jax.ShapeDtypeStruct((n, D), f32), jax.ShapeDtypeStruct((n, D), f32),
                   jax.ShapeDtypeStruct((n, N_EXPERTS), f32)],
        scratch_shapes=[pltpu.VMEM((D, D), bf16)],
        compiler_params=pltpu.CompilerParams(dimension_semantics=("arbitrary",), vmem_limit_bytes=VMEM_LIMIT_BYTES),
        name="out_projection",
    )(x, *mixed, w_out, gate1, gain2.reshape(1, D), scale2, shift2, router_w2)


def _experts_kernel(xc_ref, xl_ref, gc_ref, gl_ref, g2_ref, wg_ref, wu_ref, wd_ref, yc_ref, yl_ref, ac_ref, al_ref):
    f = pl.program_id(1)
    wg = wg_ref[0, 0].astype(bf16)
    wu = wu_ref[0, 0].astype(bf16)
    wd = wd_ref[0, 0].astype(bf16)
    for x_ref, acc_ref in ((xc_ref, ac_ref), (xl_ref, al_ref)):
        x = x_ref[0].astype(bf16)
        hidden = jax.nn.silu(jnp.dot(x, wg, preferred_element_type=f32)) * jnp.dot(x, wu, preferred_element_type=f32)
        part = jnp.dot(hidden.astype(bf16), wd, preferred_element_type=f32)

        @pl.when(f == 0)
        def _(acc_ref=acc_ref, part=part):
            acc_ref[...] = part

        @pl.when(f > 0)
        def _(acc_ref=acc_ref, part=part):
            acc_ref[...] += part

    @pl.when(f == pl.num_programs(1) - 1)
    def _():
        yc_ref[0] = ac_ref[...] * gc_ref[0] * g2_ref[0]
        yl_ref[0] = al_ref[...] * gl_ref[0]


def expert_ffn(xe_c, xe_l, g_c, g_l, gate2_c, w_gate, w_up, w_down, layer):
    E, Cc, D = xe_c.shape
    Cl = xe_l.shape[1]
    tok = lambda C, width: pl.BlockSpec((1, C, width), lambda e, f: (e, 0, 0))
    return pl.pallas_call(
        _experts_kernel,
        grid=(E, D_FF // FF_TILE),
        in_specs=[tok(Cc, D), tok(Cl, D), tok(Cc, 1), tok(Cl, 1), pl.BlockSpec((1, 1, D), lambda e, f: (0, 0, 0)),
                  pl.BlockSpec((1, 1, D, FF_TILE), lambda e, f: (layer, e, 0, f)),
                  pl.BlockSpec((1, 1, D, FF_TILE), lambda e, f: (layer, e, 0, f)),
                  pl.BlockSpec((1, 1, FF_TILE, D), lambda e, f: (layer, e, f, 0))],
        out_specs=[tok(Cc, D), tok(Cl, D)],
        out_shape=[jax.ShapeDtypeStruct((E, Cc, D), f32), jax.ShapeDtypeStruct((E, Cl, D), f32)],
        scratch_shapes=[pltpu.VMEM((Cc, D), f32), pltpu.VMEM((Cl, D), f32)],
        compiler_params=pltpu.CompilerParams(dimension_semantics=("arbitrary", "arbitrary"),
                                             vmem_limit_bytes=VMEM_LIMIT_BYTES),
        name="expert_ffn",
    )(xe_c, xe_l, g_c, g_l, gate2_c, w_gate, w_up, w_down)


def _permute_w_in(w):
    gate0 = 4 * W_GROUP
    pad = jnp.zeros(w.shape[:-1] + (GATE_LANES - 4 * H_M,), w.dtype)
    return jnp.concatenate([w[..., :gate0], w[..., gate0 + 4 * H_M:], w[..., gate0:gate0 + 4 * H_M], pad],
                           axis=-1).astype(bf16)


def token_mixers(proj, lp, s5p, lam_init, states, cache, layer):
    B, L, _ = proj.shape
    mC0, mn0, mm0, s5r0, s5i0, R0 = states
    gates_t = jnp.swapaxes(proj[:, :, 12 * W_GROUP:12 * W_GROUP + 4 * H_M], 1, 2)
    hm, mC, mn, mm = mlstm_mixer(proj, gates_t, mC0, mn0, mm0, lp)
    su = proj[:, :, PROJ_BLOCKS['su'] * W_GROUP:(PROJ_BLOCKS['su'] + 1) * W_GROUP]
    ys, s5r, s5i = s5_mixer(su, s5r0, s5i0, s5p, lp['s5_d'], lp['s5_glu_b'], layer)
    hr, R = retention_mixer(proj, R0, lp)
    attn = diff_attn_mixer(proj, cache, lp, lam_init, layer)
    mixed = [a.reshape(B * L, W_GROUP) for a in (hm, ys, hr, attn[0])]
    new_ctx = None
    if cache is None:
        v = proj[:, :, PROJ_BLOCKS['dv'] * W_GROUP:(PROJ_BLOCKS['dv'] + 1) * W_GROUP]
        new_ctx = (mC, mn, mm, s5r, s5i, R, attn[1].reshape(B, L, 2 * H_D, DH_D), v.reshape(B, L, H_D, 2 * DH_D))
    return mixed, new_ctx


def _route(aff, h2):
    n = aff.shape[0]
    gates, idx = lax.top_k(aff.T, CAPACITY_FACTOR * n // N_EXPERTS)
    return gates[..., None], idx, h2[idx]


def _combine(x, ye, idx, gate2, B):
    n, D = x.shape
    if gate2 is None:
        return x.at[idx.reshape(-1)].add(ye.reshape(-1, D))
    y = jnp.zeros_like(x).at[idx.reshape(-1)].add(ye.reshape(-1, D))
    return (x.reshape(B, n // B, D) + gate2 * y.reshape(B, n // B, D)).reshape(n, D)


PER_LAYER = ('norm1_g', 'norm2_g', 'mlstm_gate_b', 'mlstm_norm_g', 's5_d', 's5_glu_b', 'ret_decay',
             'ret_gn_g', 'diff_qk_norm', 'diff_lambda', 'diff_subln_g')


def kernel(x_prompt, x_sample, state_mlstm_c, state_mlstm_n, state_mlstm_m, state_s5_re, state_s5_im, state_ret, cache_diff_k, cache_diff_v, c, c_ctx, norm1_g, norm2_g, ada_w, ada_b, w_in, w_out, mlstm_gate_b, mlstm_norm_g, s5_lambda_re, s5_lambda_im, s5_log_step, s5_b_re, s5_b_im, s5_c_re, s5_c_im, s5_d, s5_glu_w, s5_glu_b, ret_decay, ret_gn_g, diff_qk_norm, diff_lambda, diff_subln_g, router_w, exp_w_gate, exp_w_up, exp_w_down):
    weights = dict(norm1_g=norm1_g, norm2_g=norm2_g, mlstm_gate_b=mlstm_gate_b,
                   mlstm_norm_g=mlstm_norm_g, s5_d=s5_d, s5_glu_b=s5_glu_b, ret_decay=ret_decay, ret_gn_g=ret_gn_g,
                   diff_qk_norm=diff_qk_norm, diff_lambda=diff_lambda, diff_subln_g=diff_subln_g)
    w_in_p = _permute_w_in(w_in)
    rw_hi = router_w.astype(bf16)
    router_w2 = jnp.concatenate([rw_hi, (router_w - rw_hi.astype(f32)).astype(bf16)], axis=-1)
    s5p = _s5_prepare(s5_lambda_re, s5_lambda_im, s5_log_step, s5_b_re, s5_b_im, s5_c_re, s5_c_im, s5_glu_w,
                      (x_prompt.shape[1] // SUBLANES, x_sample.shape[1] // SUBLANES))
    Bc, Lc, D = x_prompt.shape
    Bl, Ll, _ = x_sample.shape
    xc = x_prompt.reshape(Bc * Lc, D)
    xl = x_sample.reshape(Bl * Ll, D)
    zero_states = (jnp.zeros((Bc, 2, H_M, DH_M, DH_M), f32), jnp.zeros((Bc, 2, H_M, DH_M), f32),
                   jnp.zeros((Bc, 2, H_M), f32), jnp.zeros((Bc, 2, G_S5, P_S5), f32),
                   jnp.zeros((Bc, 2, G_S5, P_S5), f32), jnp.zeros((Bc, 2, H_R, DH_R, DH_R), f32))
    cache = (cache_diff_k.reshape(cache_diff_k.shape[:3] + (W_GROUP,)),
             cache_diff_v.reshape(cache_diff_v.shape[:3] + (W_GROUP,)))
    cvec = jnp.concatenate([c_ctx[None, :], c], axis=0)
    outs = [[] for _ in range(8)]
    for l in range(DEPTH):
        lp = {name: weights[name][l] for name in PER_LAYER}
        lam_init = 0.8 - 0.6 * math.exp(-0.3 * l)
        mods = jnp.split((jax.nn.silu(cvec) @ ada_w[l] + ada_b[l])[:, None, :], 6, axis=-1)
        lat_states = (state_mlstm_c[:, l], state_mlstm_n[:, l], state_mlstm_m[:, l], state_s5_re[:, l],
                      state_s5_im[:, l], state_ret[:, l])
        routed = []
        for x, B, L, sel, states, kv in ((xc, Bc, Lc, slice(0, 1), zero_states, None),
                                        (xl, Bl, Ll, slice(1, 1 + Bl), lat_states, cache)):
            sh1, sc1, g1, sh2, sc2, g2 = (m[sel] for m in mods)
            rows_per_mod = x.shape[0] // sh1.shape[0]
            proj = in_projection(x, lp['norm1_g'], sc1, sh1, w_in_p, l, rows_per_mod).reshape(B, L, N_PROJ)
            mixed, new_ctx = token_mixers(proj, lp, s5p, lam_init, states, kv, l)
            if new_ctx is not None:
                for acc, t in zip(outs, new_ctx):
                    acc.append(t)
            x1, h2, aff = out_projection(x, mixed, w_out, g1, lp['norm2_g'], sc2, sh2, router_w2, l, rows_per_mod)
            routed.append((x1, g2, sh1.shape[0]) + _route(aff, h2))
        (x1c, g2c, nbc, gc, idxc, xec), (x1l, g2l, nbl, gl, idxl, xel) = routed
        yec, yel = expert_ffn(xec, xel, gc, gl, g2c, exp_w_gate, exp_w_up, exp_w_down, l)
        xc = _combine(x1c, yec, idxc, None, nbc)
        xl = _combine(x1l, yel, idxl, g2l, nbl)
    return (xc.reshape(Bc, Lc, D), xl.reshape(Bl, Ll, D)) + tuple(jnp.stack(o, axis=1) for o in outs)
```

```python
import functools
import math

import jax
import jax.numpy as jnp
import numpy as np
from jax import lax
from jax.experimental import pallas as pl
from jax.experimental.pallas import tpu as pltpu

D_MODEL = 1024
DEPTH = 4
GRID_W = 64
W_GROUP = 256
H_M = 4
DH_M = 64
S5_CH = 16
G_S5 = 16
P_S5 = 64
S5_STATE = G_S5 * P_S5
H_R = 4
DH_R = 64
H_D = 4
DH_D = 32
N_EXPERTS = 16
CAPACITY_FACTOR = 2
ROPE_BASE = 10000.0
EPS = 1e-6
SUBLANES = 8
VMEM_LIMIT_BYTES = 56 * 1024 * 1024

f32 = jnp.float32
bf16 = jnp.bfloat16
HIGHEST = lax.Precision.HIGHEST
NEG_INF = float("-inf")


def _gelu_tanh(x):
    return 0.5 * x * (1.0 + jnp.tanh(math.sqrt(2.0 / math.pi) * (x + 0.044715 * (x * x * x))))


def _s5_kernel(su_ref, x0r_ref, x0i_ref, wb_ref, wc_ref, lb_ref, pw_ref, d_ref, gw_ref, gb_ref,
               y_ref, xr_ref, xi_ref, st_ref, *, chained):
    n_steps = st_ref.shape[0] // SUBLANES
    su = su_ref[0]
    y_ref[0] = su * d_ref[...]
    row = lax.broadcasted_iota(jnp.int32, (SUBLANES, S5_STATE), 0)
    zeros = jnp.zeros((SUBLANES, S5_STATE), f32)
    for d in range(2):
        st_ref[...] = jnp.dot(su.astype(bf16), wb_ref[0, d], preferred_element_type=f32)
        lbr = jnp.broadcast_to(lb_ref[0, d, 0:1, :], (SUBLANES, S5_STATE))
        lbi = jnp.broadcast_to(lb_ref[0, d, 1:2, :], (SUBLANES, S5_STATE))

        def rows_of(k, d=d):
            kk = k if d == 0 else n_steps - 1 - k
            return pl.ds(pl.multiple_of(kk * SUBLANES, SUBLANES), SUBLANES)

        def scan_step(k, carry, lbr=lbr, lbi=lbi, rows_of=rows_of):
            xr, xi = carry
            r = rows_of(k)
            nxr = lbr * xr - lbi * xi + st_ref[r, 0:S5_STATE]
            nxi = lbr * xi + lbi * xr + st_ref[r, S5_STATE:2 * S5_STATE]
            st_ref[r, 0:S5_STATE] = nxr
            st_ref[r, S5_STATE:2 * S5_STATE] = nxi
            return nxr, nxi

        if not chained:
            xr_ref[0, d], xi_ref[0, d] = lax.fori_loop(0, n_steps, scan_step, (x0r_ref[0, d], x0i_ref[0, d]))
            y_ref[0] += jnp.dot(st_ref[...].astype(bf16), wc_ref[0, d], preferred_element_type=f32)
            continue
        fr, fi = lax.fori_loop(0, n_steps, scan_step, (zeros, zeros))

        cr = x0r_ref[0, d]
        ci = x0i_ref[0, d]
        plr = pw_ref[0, d, 0:1, :]
        pli = pw_ref[0, d, 1:2, :]
        cmr, cmi = zeros, zeros
        for i in (range(SUBLANES) if d == 0 else reversed(range(SUBLANES))):
            cmr = jnp.where(row == i, cr, cmr)
            cmi = jnp.where(row == i, ci, cmi)
            cr, ci = (plr * cr - pli * ci + fr[i:i + 1], plr * ci + pli * cr + fi[i:i + 1])
        xr_ref[0, d] = cr
        xi_ref[0, d] = ci

        def fix_step(k, carry, lbr=lbr, lbi=lbi, cmr=cmr, cmi=cmi, rows_of=rows_of):
            pr, pi = carry
            r = rows_of(k)
            st_ref[r, 0:S5_STATE] = st_ref[r, 0:S5_STATE] + (pr * cmr - pi * cmi)
            st_ref[r, S5_STATE:2 * S5_STATE] = st_ref[r, S5_STATE:2 * S5_STATE] + (pr * cmi + pi * cmr)
            return pr * lbr - pi * lbi, pr * lbi + pi * lbr

        lax.fori_loop(0, n_steps, fix_step, (lbr, lbi))
        y_ref[0] += jnp.dot(st_ref[...].astype(bf16), wc_ref[0, d], preferred_element_type=f32)

    ys = _gelu_tanh(y_ref[0])
    gate = jax.nn.sigmoid(jnp.dot(ys.astype(bf16), gw_ref[0], preferred_element_type=f32) + gb_ref[...])
    y_ref[0] = ys * gate


def _s5_prepare(lam_re, lam_im, log_step, b_re, b_im, c_re, c_im, glu_w, n_steps_list):
    dt = jnp.exp(log_step)[..., None]
    mag = jnp.exp(lam_re * dt)
    ang = lam_im * dt
    lb_re, lb_im = mag * jnp.cos(ang), mag * jnp.sin(ang)
    nr, ni = lb_re - 1.0, lb_im
    den = lam_re * lam_re + lam_im * lam_im
    f_re = (nr * lam_re + ni * lam_im) / den
    f_im = (ni * lam_re - nr * lam_im) / den
    bb_re = f_re[..., None] * b_re[:, None] - f_im[..., None] * b_im[:, None]
    bb_im = f_re[..., None] * b_im[:, None] + f_im[..., None] * b_re[:, None]
    eye = jnp.eye(G_S5, dtype=f32)[:, None, :, None]

    def block_diag(a):
        return (a[:, :, :, :, None, :] * eye).reshape(a.shape[:2] + (G_S5 * a.shape[3], G_S5 * a.shape[4]))

    wb = jnp.concatenate([block_diag(jnp.swapaxes(bb_re, 3, 4)), block_diag(jnp.swapaxes(bb_im, 3, 4))],
                         axis=-1).astype(bf16)
    wc = jnp.concatenate([block_diag(jnp.swapaxes(c_re, 3, 4)), -block_diag(jnp.swapaxes(c_im, 3, 4))],
                         axis=2).astype(bf16)
    lead = lb_re.shape[:2]
    lb = jnp.stack([lb_re.reshape(lead + (S5_STATE,)), lb_im.reshape(lead + (S5_STATE,))], axis=2)
    pr, pi = lb[:, :, 0], lb[:, :, 1]
    tables = {}
    for j in range(int(math.log2(max(n_steps_list))) + 1):
        if 2 ** j in n_steps_list:
            tables[2 ** j] = jnp.stack([pr, pi], axis=2)
        pr, pi = pr * pr - pi * pi, 2.0 * pr * pi
    return wb, wc, lb, tables, glu_w.astype(bf16)


def s5_mixer(su, x0r, x0i, s5p, s5_d, glu_b, layer):
    B, L, _ = su.shape
    wb, wc, lb, tables, glu_w = s5p
    chained = B % SUBLANES != 0
    if chained:
        n_groups, lanes, n_steps = B, 1, L // SUBLANES
        to_rows = lambda a: a.reshape(B, SUBLANES, n_steps, -1).transpose(0, 2, 1, 3).reshape(B, L, -1)
        from_rows = lambda a: a.reshape(B, n_steps, SUBLANES, -1).transpose(0, 2, 1, 3).reshape(B, L, -1)
        pw = tables[n_steps]
    else:
        n_groups, lanes, n_steps = B // SUBLANES, SUBLANES, L
        to_rows = lambda a: a.reshape(n_groups, SUBLANES, L, -1).transpose(0, 2, 1, 3).reshape(n_groups, -1, a.shape[-1])
        from_rows = lambda a: a.reshape(n_groups, L, SUBLANES, -1).transpose(0, 2, 1, 3).reshape(B, L, -1)
        pw = lb
    rows = n_steps * SUBLANES
    state_in = lambda a: a.reshape(n_groups, lanes, 2, S5_STATE).transpose(0, 2, 1, 3)
    state_out = lambda a: a.transpose(0, 2, 1, 3).reshape(B, 2, G_S5, P_S5)
    full = lambda *shape: pl.BlockSpec(shape, lambda b: (0,) * len(shape))
    per_b = lambda *shape: pl.BlockSpec((1,) + shape, lambda b: (b,) + (0,) * len(shape))
    per_layer = lambda *shape: pl.BlockSpec((1,) + shape, lambda b: (layer,) + (0,) * len(shape))
    y_p, xr, xi = pl.pallas_call(
        functools.partial(_s5_kernel, chained=chained),
        grid=(n_groups,),
        in_specs=[per_b(rows, W_GROUP), per_b(2, lanes, S5_STATE), per_b(2, lanes, S5_STATE),
                  per_layer(2, W_GROUP, 2 * S5_STATE), per_layer(2, 2 * S5_STATE, W_GROUP),
                  per_layer(2, 2, S5_STATE), per_layer(2, 2, S5_STATE),
                  full(1, W_GROUP), per_layer(W_GROUP, W_GROUP), full(1, W_GROUP)],
        out_specs=[per_b(rows, W_GROUP), per_b(2, lanes, S5_STATE), per_b(2, lanes, S5_STATE)],
        out_shape=[jax.ShapeDtypeStruct((n_groups, rows, W_GROUP), f32),
                   jax.ShapeDtypeStruct((n_groups, 2, lanes, S5_STATE), f32),
                   jax.ShapeDtypeStruct((n_groups, 2, lanes, S5_STATE), f32)],
        scratch_shapes=[pltpu.VMEM((rows, 2 * S5_STATE), f32)],
        compiler_params=pltpu.CompilerParams(dimension_semantics=("arbitrary",),
                                             vmem_limit_bytes=VMEM_LIMIT_BYTES),
        name="s5_mixer",
    )(to_rows(su), state_in(x0r), state_in(x0i), wb, wc, lb, pw,
      s5_d.reshape(1, W_GROUP), glu_w, glu_b.reshape(1, W_GROUP))
    return from_rows(y_p), state_out(xr), state_out(xi)


Q_TILE = 256
PROJ_BLOCKS = dict(mq=0, mk=1, mv=2, mo=3, su=4, rq=5, rk=6, rv=7, rg=8, dq=9, dk=10, dv=11)
GATE_LANES = 128
N_PROJ = 12 * W_GROUP + GATE_LANES


def _log_sigmoid(x):
    return jnp.minimum(x, 0.0) - jnp.log1p(jnp.exp(-jnp.abs(x)))


def _group_ones(width, group):
    shift = int(math.log2(group))
    r = lax.broadcasted_iota(jnp.int32, (width, width), 0) >> shift
    c = lax.broadcasted_iota(jnp.int32, (width, width), 1) >> shift
    return (r == c).astype(bf16)


def _split3(x):
    hi = x.astype(bf16)
    r = x - hi.astype(f32)
    mid = r.astype(bf16)
    return hi, mid, (r - mid.astype(f32)).astype(bf16)


def _group_mean(x, ones, group):
    return sum(jnp.dot(p, ones, preferred_element_type=f32) for p in _split3(x)) * (1.0 / group)


def _dot_nt(a, b):
    return lax.dot_general(a.astype(bf16), b.astype(bf16), (((1,), (1,)), ((), ())), preferred_element_type=f32)


def _dot(a, b):
    return jnp.dot(a.astype(bf16), b.astype(bf16), preferred_element_type=f32)


def _proj_block(name, rows):
    j = PROJ_BLOCKS[name]
    return pl.BlockSpec((1, rows, W_GROUP), lambda b, qi, j=j, rows=rows: (b, qi if rows == Q_TILE else 0, j))


def _mlstm_kernel(q_ref, k_ref, v_ref, o_ref, g_ref, gt_ref, gb_ref, gbt_ref, ng_ref, c0_ref, n0_ref, m0_ref,
                  h_ref, c_ref, n_ref, m_ref, gl_ref, gu_ref, rc_ref, vt_ref, ht_ref, *, seq_len, q_tile):
    L, TQ = seq_len, q_tile
    nq = L // TQ
    qi = pl.program_id(1)
    grow = gt_ref[0] + gbt_ref[...]

    @pl.when(qi == 0)
    def _():
        ss = lax.broadcasted_iota(jnp.int32, (L, L), 0)
        tt = lax.broadcasted_iota(jnp.int32, (L, L), 1)
        tri_le = (ss <= tt).astype(bf16)
        tri_ge = (ss >= tt).astype(bf16)
        rows = _split3(_log_sigmoid(grow))
        gl = sum(jnp.dot(p, tri_le, preferred_element_type=f32) for p in rows)
        gu = sum(jnp.dot(p, tri_ge, preferred_element_type=f32) for p in rows)
        for j in range(nq):
            gl_ref[j] = gl[:, j * TQ:(j + 1) * TQ]
            gu_ref[j] = gu[:, j * TQ:(j + 1) * TQ]
        gcol = g_ref[0] + gb_ref[...]
        cols = _split3(_log_sigmoid(gcol))
        glc = sum(jnp.dot(tri_ge, p, preferred_element_type=f32) for p in cols)
        guc = sum(jnp.dot(tri_le, p, preferred_element_type=f32) for p in cols)
        lane = lax.broadcasted_iota(jnp.int32, (L, GATE_LANES), 1)
        rc_ref[...] = pltpu.roll(gcol, 4, 1) - jnp.where(lane < 8, glc, guc)

        vt = v_ref[0].T
        kk = k_ref[0] * (DH_M ** -0.5)
        one_row = (lax.broadcasted_iota(jnp.int32, (DH_M, L), 0) == 0).astype(bf16)
        for h in range(H_M):
            hs = slice(h * DH_M, (h + 1) * DH_M)
            vt_ref[2 * h * DH_M:(2 * h + 1) * DH_M, :] = vt[hs, :].astype(bf16)
            vt_ref[(2 * h + 1) * DH_M:(2 * h + 2) * DH_M, :] = one_row
        for d in range(2):
            g_all = gl if d == 0 else gu
            for h in range(H_M):
                hs = slice(h * DH_M, (h + 1) * DH_M)
                ii, fi = 8 * d + h, 8 * d + 4 + h
                g_row = g_all[fi:fi + 1, :]
                g_tot = g_row[:, L - 1:L] if d == 0 else g_row[:, 0:1]
                wlog = g_tot - g_row + grow[ii:ii + 1, :]
                m0 = m0_ref[0, d:d + 1, h:h + 1]
                m_new = jnp.maximum(g_tot + m0, jnp.max(wlog, axis=1, keepdims=True))
                decay = jnp.exp(g_tot + m0 - m_new)
                w = jnp.exp(wlog - m_new)
                kh = kk[:, hs]
                c_ref[0, d, h] = decay * c0_ref[0, d, h] + _dot(vt[hs, :] * w, kh)
                n_upd = jnp.dot(jnp.broadcast_to(w, (SUBLANES, L)), kh, precision=HIGHEST,
                                preferred_element_type=f32)[0:1, :]
                n_ref[0, d, h:h + 1, :] = decay * n0_ref[0, d, h:h + 1, :] + n_upd
                m_ref[0, d:d + 1, h:h + 1] = m_new

    gl_t = gl_ref[qi]
    gu_t = gu_ref[qi]
    s_idx = lax.broadcasted_iota(jnp.int32, (L, TQ), 0)
    t_idx = qi * TQ + lax.broadcasted_iota(jnp.int32, (L, TQ), 1)
    low = s_idx <= t_idx
    upp = s_idx >= t_idx
    qt = q_ref[0].T.astype(bf16)
    k = (k_ref[0] * (DH_M ** -0.5)).astype(bf16)
    rc = rc_ref[...]
    row0 = lax.broadcasted_iota(jnp.int32, (DH_M, DH_M), 0) == 0
    for h in range(H_M):
        hs = slice(h * DH_M, (h + 1) * DH_M)
        qth = qt[hs, :]
        vta = vt_ref[2 * h * DH_M:(2 * h + 2) * DH_M, :]
        s0 = jnp.dot(k[:, hs], qth, preferred_element_type=f32)
        h_sum = None
        for d in range(2):
            fi = 8 * d + 4 + h
            g_t = (gl_t if d == 0 else gu_t)[fi:fi + 1, :]
            dlog = jnp.where(low if d == 0 else upp, rc[:, fi:fi + 1] + g_t, NEG_INF)
            inter = g_t + m0_ref[0, d:d + 1, h:h + 1]
            m_t = jnp.maximum(inter, jnp.max(dlog, axis=0, keepdims=True))
            p = s0 * jnp.exp(dlog - m_t)
            a = jnp.exp(inter - m_t)
            c0n0 = jnp.concatenate([c0_ref[0, d, h], jnp.where(row0, n0_ref[0, d, h:h + 1, :], 0.0)], axis=0)
            numden = (jnp.dot(vta, p.astype(bf16), preferred_element_type=f32)
                      + a * jnp.dot(c0n0.astype(bf16), qth, preferred_element_type=f32))
            scale = 1.0 / jnp.maximum(jnp.abs(numden[DH_M:DH_M + 1, :]), jnp.exp(-m_t))
            hd = numden[0:DH_M, :] * scale
            h_sum = hd if h_sum is None else h_sum + hd
        ht_ref[hs, :] = h_sum * lax.rsqrt(jnp.mean(h_sum * h_sum, axis=0, keepdims=True) + EPS)
    h_ref[0] = jax.nn.sigmoid(o_ref[0]) * (ht_ref[...].T * ng_ref[...])


def mlstm_mixer(proj, gates_t, c0, n0, m0, lp):
    B, L, _ = proj.shape
    TQ = min(L, Q_TILE)
    gb = lp['mlstm_gate_b'].reshape(1, 4 * H_M)
    const = lambda *shape: pl.BlockSpec(shape, lambda b, qi: (0,) * len(shape))
    per_b = lambda *shape: pl.BlockSpec((1,) + shape, lambda b, qi: (b,) + (0,) * len(shape))
    return pl.pallas_call(
        functools.partial(_mlstm_kernel, seq_len=L, q_tile=TQ),
        grid=(B, L // TQ),
        in_specs=[_proj_block('mq', TQ), _proj_block('mk', L), _proj_block('mv', L), _proj_block('mo', TQ),
                  pl.BlockSpec((1, L, GATE_LANES), lambda b, qi: (b, 0, 12 * W_GROUP // GATE_LANES)),
                  per_b(4 * H_M, L), const(1, GATE_LANES), const(4 * H_M, 1), const(1, W_GROUP),
                  per_b(2, H_M, DH_M, DH_M), per_b(2, H_M, DH_M), per_b(2, H_M)],
        out_specs=[pl.BlockSpec((1, TQ, W_GROUP), lambda b, qi: (b, qi, 0)),
                   per_b(2, H_M, DH_M, DH_M), per_b(2, H_M, DH_M), per_b(2, H_M)],
        out_shape=[jax.ShapeDtypeStruct((B, L, W_GROUP), f32),
                   jax.ShapeDtypeStruct((B, 2, H_M, DH_M, DH_M), f32),
                   jax.ShapeDtypeStruct((B, 2, H_M, DH_M), f32),
                   jax.ShapeDtypeStruct((B, 2, H_M), f32)],
        scratch_shapes=[pltpu.VMEM((L // TQ, 4 * H_M, TQ), f32), pltpu.VMEM((L // TQ, 4 * H_M, TQ), f32),
                        pltpu.VMEM((L, GATE_LANES), f32), pltpu.VMEM((2 * W_GROUP, L), bf16),
                        pltpu.VMEM((W_GROUP, TQ), f32)],
        compiler_params=pltpu.CompilerParams(dimension_semantics=("arbitrary", "arbitrary"),
                                             vmem_limit_bytes=VMEM_LIMIT_BYTES),
        name="mlstm_mixer",
    )(proj, proj, proj, proj, proj, gates_t, jnp.pad(gb, ((0, 0), (0, GATE_LANES - 4 * H_M))),
      gb.reshape(4 * H_M, 1), lp['mlstm_norm_g'].reshape(1, W_GROUP), c0, n0, m0)


def _retention_kernel(lg_ref, q_ref, k_ref, v_ref, g_ref, gn_ref, r0_ref, h_ref, r_ref, *, seq_len, q_tile):
    L, TQ = seq_len, q_tile
    qi = pl.program_id(1)
    t_col = qi * TQ + lax.broadcasted_iota(jnp.int32, (TQ, 1), 0)
    rel = (qi * TQ + lax.broadcasted_iota(jnp.int32, (TQ, L), 0)
           - lax.broadcasted_iota(jnp.int32, (TQ, L), 1)).astype(f32)
    q = q_ref[0]
    k = k_ref[0] * (DH_R ** -0.5)
    v = v_ref[0]
    ones = _group_ones(W_GROUP, DH_R)
    for h in range(H_R):
        hs = slice(h * DH_R, (h + 1) * DH_R)
        lgf, lgb = lg_ref[0, h], lg_ref[1, h]
        qh, kh, vh = q[:, hs], k[:, hs], v[:, hs]
        decay = jnp.where(rel > 0.0, jnp.exp(lgf * jnp.maximum(rel, 0.0)),
                          jnp.where(rel < 0.0, jnp.exp(lgb * jnp.maximum(-rel, 0.0)), 2.0))
        o = _dot(_dot_nt(qh, kh) * decay, vh)
        xi_f = jnp.exp(lgf * (t_col + 1).astype(f32))
        xi_b = jnp.exp(lgb * (L - t_col).astype(f32))
        o = o + xi_f * _dot(qh, r0_ref[0, 0, h]) + xi_b * _dot(qh, r0_ref[0, 1, h])
        h_ref[0, :, hs] = o
    o = h_ref[0]
    oc = o - _group_mean(o, ones, DH_R)
    y = oc * lax.rsqrt(_group_mean(oc * oc, ones, DH_R) + EPS) * gn_ref[...]
    h_ref[0] = y * jax.nn.silu(g_ref[0])

    @pl.when(qi == 0)
    def _():
        kt = k.T
        s_row = lax.broadcasted_iota(jnp.int32, (1, L), 1).astype(f32)
        for d in range(2):
            for h in range(H_R):
                hs = slice(h * DH_R, (h + 1) * DH_R)
                lg = lg_ref[d, h]
                zeta = jnp.exp(lg * ((L - 1.0) - s_row)) if d == 0 else jnp.exp(lg * s_row)
                r_ref[0, d, h] = jnp.exp(lg * float(L)) * r0_ref[0, d, h] + _dot(kt[hs, :] * zeta, v[:, hs])


def retention_mixer(proj, r0, lp):
    B, L, _ = proj.shape
    TQ = min(L, Q_TILE)
    log_gamma = -jnp.exp(lp['ret_decay'])
    per_b = lambda *shape: pl.BlockSpec((1,) + shape, lambda b, qi: (b,) + (0,) * len(shape))
    return pl.pallas_call(
        functools.partial(_retention_kernel, seq_len=L, q_tile=TQ),
        grid=(B, L // TQ),
        in_specs=[pl.BlockSpec(memory_space=pltpu.SMEM),
                  _proj_block('rq', TQ), _proj_block('rk', L), _proj_block('rv', L), _proj_block('rg', TQ),
                  pl.BlockSpec((1, W_GROUP), lambda b, qi: (0, 0)), per_b(2, H_R, DH_R, DH_R)],
        out_specs=[pl.BlockSpec((1, TQ, W_GROUP), lambda b, qi: (b, qi, 0)), per_b(2, H_R, DH_R, DH_R)],
        out_shape=[jax.ShapeDtypeStruct((B, L, W_GROUP), f32),
                   jax.ShapeDtypeStruct((B, 2, H_R, DH_R, DH_R), f32)],
        compiler_params=pltpu.CompilerParams(dimension_semantics=("arbitrary", "arbitrary"),
                                             vmem_limit_bytes=VMEM_LIMIT_BYTES),
        name="retention_mixer",
    )(log_gamma, proj, proj, proj, proj, lp['ret_gn_g'].reshape(1, W_GROUP), r0)


def _rope_tables(L):
    half = DH_D // 2
    freqs = ROPE_BASE ** (-np.arange(0, half, 2, dtype=np.float64) / half)
    pos = np.arange(L)
    row, col = (pos // GRID_W).astype(np.float64), (pos % GRID_W).astype(np.float64)
    ang = np.concatenate([np.tile(row[:, None] * freqs, (1, 2)), np.tile(col[:, None] * freqs, (1, 2))], axis=1)
    sign = np.tile(np.concatenate([-np.ones(half // 2), np.ones(half // 2)]), 2)
    cos = np.tile(np.cos(ang), (1, 2 * H_D))
    sin = np.tile(np.sin(ang) * sign, (1, 2 * H_D))
    return jnp.asarray(cos, f32), jnp.asarray(sin, f32)


def _swap_pairs(x):
    parts = []
    for j in range(x.shape[1] // 128):
        xs = x[:, j * 128:(j + 1) * 128]
        lane = lax.broadcasted_iota(jnp.int32, xs.shape, 1)
        parts.append(jnp.where((lane & 15) < 8, pltpu.roll(xs, 120, 1), pltpu.roll(xs, 8, 1)))
    return jnp.concatenate(parts, axis=1)


def _qk_norm(x, gain, ones):
    return x * lax.rsqrt(_group_mean(x * x, ones, DH_D) + EPS) * gain


def _diff_attn_kernel(lam_ref, q_ref, k_ref, v_ref, qkg_ref, sg_ref, *rest, seq_len, q_tile, past_len, out_scale):
    L, TQ, P = seq_len, q_tile, past_len
    if P:
        kc_ref, vc_ref, cos_ref, sin_ref, h_ref, ka_ref, vt_ref, ot_ref = rest
    else:
        h_ref, kn_ref, ka_ref, vt_ref, ot_ref = rest
    qi = pl.program_id(1)
    ones = _group_ones(W_GROUP, DH_D)

    @pl.when(qi == 0)
    def _():
        kn = _qk_norm(k_ref[0], qkg_ref[1:2, :], ones)
        if P:
            kn = kn * cos_ref[...] + _swap_pairs(kn) * sin_ref[...]
            ka_ref[0:P, :] = kc_ref[0, 0].astype(bf16)
            vt_ref[:, 0:P] = vc_ref[0, 0].T.astype(bf16)
        else:
            kn_ref[0] = kn
        ka_ref[P:P + L, :] = kn.astype(bf16)
        vt_ref[:, P:P + L] = v_ref[0].T.astype(bf16)

    qn = _qk_norm(q_ref[0], qkg_ref[0:1, :], ones)
    if P:
        rows = pl.ds(pl.multiple_of(qi * TQ, TQ), TQ)
        qn = qn * cos_ref[rows, :] + _swap_pairs(qn) * sin_ref[rows, :]
    qt = (qn * (DH_D ** -0.5)).T.astype(bf16)
    lam = lam_ref[0, 0]
    ka = ka_ref[...]
    for h in range(H_D):
        probs = []
        for j in range(2):
            cs = slice((2 * h + j) * DH_D, (2 * h + j + 1) * DH_D)
            s = jnp.dot(ka[:, cs], qt[cs, :], preferred_element_type=f32)
            e = jnp.exp(s - jnp.max(s, axis=0, keepdims=True))
            probs.append(e * (1.0 / jnp.sum(e, axis=0, keepdims=True)))
        vs = slice(h * 2 * DH_D, (h + 1) * 2 * DH_D)
        ot_ref[vs, :] = jnp.dot(vt_ref[vs, :], (probs[0] - lam * probs[1]).astype(bf16), preferred_element_type=f32)
    o = ot_ref[...].T
    ones_v = _group_ones(W_GROUP, 2 * DH_D)
    h_ref[0] = o * lax.rsqrt(_group_mean(o * o, ones_v, 2 * DH_D) + EPS) * (sg_ref[...] * out_scale)


def diff_attn_mixer(proj, cache, lp, lam_init, layer):
    B, L, _ = proj.shape
    TQ = min(L, Q_TILE)
    lv = lp['diff_lambda']
    lam = (jnp.exp(jnp.sum(lv[0] * lv[1])) - jnp.exp(jnp.sum(lv[2] * lv[3])) + lam_init).reshape(1, 1)
    qkg = jnp.tile(lp['diff_qk_norm'], (1, 2 * H_D))
    sg = jnp.tile(lp['diff_subln_g'], (H_D,)).reshape(1, W_GROUP)
    const = lambda *shape: pl.BlockSpec(shape, lambda b, qi: (0,) * len(shape))
    in_specs = [pl.BlockSpec(memory_space=pltpu.SMEM),
                _proj_block('dq', TQ), _proj_block('dk', L), _proj_block('dv', L), const(2, W_GROUP), const(1, W_GROUP)]
    args = [lam, proj, proj, proj, qkg, sg]
    out_specs = [pl.BlockSpec((1, TQ, W_GROUP), lambda b, qi: (b, qi, 0))]
    out_shape = [jax.ShapeDtypeStruct((B, L, W_GROUP), f32)]
    P = 0
    if cache is not None:
        ck, cv = cache
        P = ck.shape[2]
        cspec = pl.BlockSpec((1, 1, P, W_GROUP), lambda b, qi, layer=layer: (b, layer, 0, 0))
        cos, sin = _rope_tables(L)
        in_specs += [cspec, cspec, const(L, W_GROUP), const(L, W_GROUP)]
        args += [ck, cv, cos, sin]
    else:
        out_specs.append(pl.BlockSpec((1, L, W_GROUP), lambda b, qi: (b, 0, 0)))
        out_shape.append(jax.ShapeDtypeStruct((B, L, W_GROUP), f32))
    return pl.pallas_call(
        functools.partial(_diff_attn_kernel, seq_len=L, q_tile=TQ, past_len=P, out_scale=1.0 - lam_init),
        grid=(B, L // TQ),
        in_specs=in_specs, out_specs=out_specs, out_shape=out_shape,
        scratch_shapes=[pltpu.VMEM((P + L, W_GROUP), bf16), pltpu.VMEM((W_GROUP, P + L), bf16),
                        pltpu.VMEM((W_GROUP, TQ), f32)],
        compiler_params=pltpu.CompilerParams(dimension_semantics=("arbitrary", "arbitrary"),
                                             vmem_limit_bytes=VMEM_LIMIT_BYTES),
        name="diff_attention",
    )(*args)


PROJ_ROW_TILE = 512
OUT_ROW_TILE = 512
FF_TILE = 512
D_FF = 2 * D_MODEL


def _in_proj_kernel(x_ref, g_ref, sc_ref, sh_ref, w_ref, o_ref):
    x = x_ref[...]
    y = x * lax.rsqrt(jnp.mean(x * x, axis=1, keepdims=True) + EPS) * g_ref[...]
    h = (y * (1.0 + sc_ref[0]) + sh_ref[0]).astype(bf16)
    o_ref[...] = jnp.dot(h, w_ref[0], preferred_element_type=f32)


def in_projection(x, gain, scale, shift, w_p, layer, rows_per_mod):
    n, D = x.shape
    TM = min(n, PROJ_ROW_TILE)
    mod = pl.BlockSpec((1, 1, D), lambda i: (i * TM // rows_per_mod, 0, 0))
    return pl.pallas_call(
        _in_proj_kernel,
        grid=(n // TM,),
        in_specs=[pl.BlockSpec((TM, D), lambda i: (i, 0)), pl.BlockSpec((1, D), lambda i: (0, 0)), mod, mod,
                  pl.BlockSpec((1, D, N_PROJ), lambda i: (layer, 0, 0))],
        out_specs=pl.BlockSpec((TM, N_PROJ), lambda i: (i, 0)),
        out_shape=jax.ShapeDtypeStruct((n, N_PROJ), f32),
        compiler_params=pltpu.CompilerParams(dimension_semantics=("arbitrary",),
                                             vmem_limit_bytes=VMEM_LIMIT_BYTES),
        name="in_projection",
    )(x, gain.reshape(1, D), scale, shift, w_p)


def _out_proj_kernel(x_ref, m0_ref, m1_ref, m2_ref, m3_ref, w_ref, g1_ref, ng_ref, sc_ref, sh_ref, rw_ref,
                     xo_ref, h_ref, aff_ref, wb_ref):
    @pl.when(pl.program_id(0) == 0)
    def _():
        wb_ref[...] = w_ref[0].astype(bf16)

    out = None
    for j, m_ref in enumerate((m0_ref, m1_ref, m2_ref, m3_ref)):
        part = jnp.dot(m_ref[...].astype(bf16), wb_ref[j * W_GROUP:(j + 1) * W_GROUP, :], preferred_element_type=f32)
        out = part if out is None else out + part
    x = x_ref[...] + g1_ref[0] * out
    xo_ref[...] = x
    h = x * lax.rsqrt(jnp.mean(x * x, axis=1, keepdims=True) + EPS) * ng_ref[...]
    h = h * (1.0 + sc_ref[0]) + sh_ref[0]
    h_ref[...] = h
    h_hi = h.astype(bf16)
    h_lo = (h - h_hi.astype(f32)).astype(bf16)
    both = jnp.dot(h_hi, rw_ref[0], preferred_element_type=f32)
    logits = (both[:, 0:N_EXPERTS] + both[:, N_EXPERTS:2 * N_EXPERTS]
              + jnp.dot(h_lo, rw_ref[0], preferred_element_type=f32)[:, 0:N_EXPERTS])
    e = jnp.exp(logits - jnp.max(logits, axis=1, keepdims=True))
    aff_ref[...] = e / jnp.sum(e, axis=1, keepdims=True)


def out_projection(x, mixed, w_out, gate1, gain2, scale2, shift2, router_w2, layer, rows_per_mod):
    n, D = x.shape
    TM = OUT_ROW_TILE
    row = lambda width: pl.BlockSpec((TM, width), lambda i: (i, 0))
    const = lambda *shape: pl.BlockSpec(shape, lambda i: (0,) * len(shape))
    mod = pl.BlockSpec((1, 1, D), lambda i: (i * TM // rows_per_mod, 0, 0))
    return pl.pallas_call(
        _out_proj_kernel,
        grid=(n // TM,),
        in_specs=[row(D), row(W_GROUP), row(W_GROUP), row(W_GROUP), row(W_GROUP),
                  pl.BlockSpec((1, D, D), lambda i: (layer, 0, 0)), mod, const(1, D), mod, mod,
                  pl.BlockSpec((1, D, 2 * N_EXPERTS), lambda i: (layer, 0, 0))],
        out_specs=[row(D), row(D), row(N_EXPERTS)],
        out_shape=[jax.ShapeDtypeStruct((n, D), f32), jax.ShapeDtypeStruct((n, D), f32),
                   jax.ShapeDtypeStruct((n, N_EXPERTS), f32)],
        scratch_shapes=[pltpu.VMEM((D, D), bf16)],
        compiler_params=pltpu.CompilerParams(dimension_semantics=("arbitrary",), vmem_limit_bytes=VMEM_LIMIT_BYTES),
        name="out_projection",
    )(x, *mixed, w_out, gate1, gain2.reshape(1, D), scale2, shift2, router_w2)


def _experts_kernel(xc_ref, xl_ref, gc_ref, gl_ref, g2_ref, wg_ref, wu_ref, wd_ref, yc_ref, yl_ref, ac_ref, al_ref):
    f = pl.program_id(1)
    wg = wg_ref[0, 0].astype(bf16)
    wu = wu_ref[0, 0].astype(bf16)
    wd = wd_ref[0, 0].astype(bf16)
    for x_ref, acc_ref in ((xc_ref, ac_ref), (xl_ref, al_ref)):
        x = x_ref[0].astype(bf16)
        hidden = jax.nn.silu(jnp.dot(x, wg, preferred_element_type=f32)) * jnp.dot(x, wu, preferred_element_type=f32)
        part = jnp.dot(hidden.astype(bf16), wd, preferred_element_type=f32)

        @pl.when(f == 0)
        def _(acc_ref=acc_ref, part=part):
            acc_ref[...] = part

        @pl.when(f > 0)
        def _(acc_ref=acc_ref, part=part):
            acc_ref[...] += part

    @pl.when(f == pl.num_programs(1) - 1)
    def _():
        yc_ref[0] = ac_ref[...] * gc_ref[0] * g2_ref[0]
        yl_ref[0] = al_ref[...] * gl_ref[0]


def expert_ffn(xe_c, xe_l, g_c, g_l, gate2_c, w_gate, w_up, w_down, layer):
    E, Cc, D = xe_c.shape
    Cl = xe_l.shape[1]
    tok = lambda C, width: pl.BlockSpec((1, C, width), lambda e, f: (e, 0, 0))
    return pl.pallas_call(
        _experts_kernel,
        grid=(E, D_FF // FF_TILE),
        in_specs=[tok(Cc, D), tok(Cl, D), tok(Cc, 1), tok(Cl, 1), pl.BlockSpec((1, 1, D), lambda e, f: (0, 0, 0)),
                  pl.BlockSpec((1, 1, D, FF_TILE), lambda e, f: (layer, e, 0, f)),
                  pl.BlockSpec((1, 1, D, FF_TILE), lambda e, f: (layer, e, 0, f)),
                  pl.BlockSpec((1, 1, FF_TILE, D), lambda e, f: (layer, e, f, 0))],
        out_specs=[tok(Cc, D), tok(Cl, D)],
        out_shape=[jax.ShapeDtypeStruct((E, Cc, D), f32), jax.ShapeDtypeStruct((E, Cl, D), f32)],
        scratch_shapes=[pltpu.VMEM((Cc, D), f32), pltpu.VMEM((Cl, D), f32)],
        compiler_params=pltpu.CompilerParams(dimension_semantics=("arbitrary", "arbitrary"),
                                             vmem_limit_bytes=VMEM_LIMIT_BYTES),
        name="expert_ffn",
    )(xe_c, xe_l, g_c, g_l, gate2_c, w_gate, w_up, w_down)


def _permute_w_in(w):
    gate0 = 4 * W_GROUP
    pad = jnp.zeros(w.shape[:-1] + (GATE_LANES - 4 * H_M,), w.dtype)
    return jnp.concatenate([w[..., :gate0], w[..., gate0 + 4 * H_M:], w[..., gate0:gate0 + 4 * H_M], pad],
                           axis=-1).astype(bf16)


def token_mixers(proj, lp, s5p, lam_init, states, cache, layer):
    B, L, _ = proj.shape
    mC0, mn0, mm0, s5r0, s5i0, R0 = states
    gates_t = jnp.swapaxes(proj[:, :, 12 * W_GROUP:12 * W_GROUP + 4 * H_M], 1, 2)
    hm, mC, mn, mm = mlstm_mixer(proj, gates_t, mC0, mn0, mm0, lp)
    su = proj[:, :, PROJ_BLOCKS['su'] * W_GROUP:(PROJ_BLOCKS['su'] + 1) * W_GROUP]
    ys, s5r, s5i = s5_mixer(su, s5r0, s5i0, s5p, lp['s5_d'], lp['s5_glu_b'], layer)
    hr, R = retention_mixer(proj, R0, lp)
    attn = diff_attn_mixer(proj, cache, lp, lam_init, layer)
    mixed = [a.reshape(B * L, W_GROUP) for a in (hm, ys, hr, attn[0])]
    new_ctx = None
    if cache is None:
        v = proj[:, :, PROJ_BLOCKS['dv'] * W_GROUP:(PROJ_BLOCKS['dv'] + 1) * W_GROUP]
        new_ctx = (mC, mn, mm, s5r, s5i, R, attn[1].reshape(B, L, 2 * H_D, DH_D), v.reshape(B, L, H_D, 2 * DH_D))
    return mixed, new_ctx


def _route(aff, h2):
    n = aff.shape[0]
    gates, idx = lax.top_k(aff.T, CAPACITY_FACTOR * n // N_EXPERTS)
    return gates[..., None], idx, h2[idx]


def _combine(x, ye, idx, gate2, B):
    n, D = x.shape
    if gate2 is None:
        return x.at[idx.reshape(-1)].add(ye.reshape(-1, D))
    y = jnp.zeros_like(x).at[idx.reshape(-1)].add(ye.reshape(-1, D))
    return (x.reshape(B, n // B, D) + gate2 * y.reshape(B, n // B, D)).reshape(n, D)


PER_LAYER = ('norm1_g', 'norm2_g', 'mlstm_gate_b', 'mlstm_norm_g', 's5_d', 's5_glu_b', 'ret_decay',
             'ret_gn_g', 'diff_qk_norm', 'diff_lambda', 'diff_subln_g')


def kernel(x_prompt, x_sample, state_mlstm_c, state_mlstm_n, state_mlstm_m, state_s5_re, state_s5_im, state_ret, cache_diff_k, cache_diff_v, c, c_ctx, norm1_g, norm2_g, ada_w, ada_b, w_in, w_out, mlstm_gate_b, mlstm_norm_g, s5_lambda_re, s5_lambda_im, s5_log_step, s5_b_re, s5_b_im, s5_c_re, s5_c_im, s5_d, s5_glu_w, s5_glu_b, ret_decay, ret_gn_g, diff_qk_norm, diff_lambda, diff_subln_g, router_w, exp_w_gate, exp_w_up, exp_w_down):
    weights = dict(norm1_g=norm1_g, norm2_g=norm2_g, mlstm_gate_b=mlstm_gate_b,
                   mlstm_norm_g=mlstm_norm_g, s5_d=s5_d, s5_glu_b=s5_glu_b, ret_decay=ret_decay, ret_gn_g=ret_gn_g,
                   diff_qk_norm=diff_qk_norm, diff_lambda=diff_lambda, diff_subln_g=diff_subln_g)
    w_in_p = _permute_w_in(w_in)
    rw_hi = router_w.astype(bf16)
    router_w2 = jnp.concatenate([rw_hi, (router_w - rw_hi.astype(f32)).astype(bf16)], axis=-1)
    s5p = _s5_prepare(s5_lambda_re, s5_lambda_im, s5_log_step, s5_b_re, s5_b_im, s5_c_re, s5_c_im, s5_glu_w,
                      (x_prompt.shape[1] // SUBLANES, x_sample.shape[1] // SUBLANES))
    Bc, Lc, D = x_prompt.shape
    Bl, Ll, _ = x_sample.shape
    xc = x_prompt.reshape(Bc * Lc, D)
    xl = x_sample.reshape(Bl * Ll, D)
    zero_states = (jnp.zeros((Bc, 2, H_M, DH_M, DH_M), f32), jnp.zeros((Bc, 2, H_M, DH_M), f32),
                   jnp.zeros((Bc, 2, H_M), f32), jnp.zeros((Bc, 2, G_S5, P_S5), f32),
                   jnp.zeros((Bc, 2, G_S5, P_S5), f32), jnp.zeros((Bc, 2, H_R, DH_R, DH_R), f32))
    cache = (cache_diff_k.reshape(cache_diff_k.shape[:3] + (W_GROUP,)),
             cache_diff_v.reshape(cache_diff_v.shape[:3] + (W_GROUP,)))
    cvec = jnp.concatenate([c_ctx[None, :], c], axis=0)
    outs = [[] for _ in range(8)]
    for l in range(DEPTH):
        lp = {name: weights[name][l] for name in PER_LAYER}
        lam_init = 0.8 - 0.6 * math.exp(-0.3 * l)
        mods = jnp.split((jax.nn.silu(cvec) @ ada_w[l] + ada_b[l])[:, None, :], 6, axis=-1)
        lat_states = (state_mlstm_c[:, l], state_mlstm_n[:, l], state_mlstm_m[:, l], state_s5_re[:, l],
                      state_s5_im[:, l], state_ret[:, l])
        routed = []
        for x, B, L, sel, states, kv in ((xc, Bc, Lc, slice(0, 1), zero_states, None),
                                        (xl, Bl, Ll, slice(1, 1 + Bl), lat_states, cache)):
            sh1, sc1, g1, sh2, sc2, g2 = (m[sel] for m in mods)
            rows_per_mod = x.shape[0] // sh1.shape[0]
            proj = in_projection(x, lp['norm1_g'], sc1, sh1, w_in_p, l, rows_per_mod).reshape(B, L, N_PROJ)
            mixed, new_ctx = token_mixers(proj, lp, s5p, lam_init, states, kv, l)
            if new_ctx is not None:
                for acc, t in zip(outs, new_ctx):
                    acc.append(t)
            x1, h2, aff = out_projection(x, mixed, w_out, g1, lp['norm2_g'], sc2, sh2, router_w2, l, rows_per_mod)
            routed.append((x1, g2, sh1.shape[0]) + _route(aff, h2))
        (x1c, g2c, nbc, gc, idxc, xec), (x1l, g2l, nbl, gl, idxl, xel) = routed
        yec, yel = expert_ffn(xec, xel, gc, gl, g2c, exp_w_gate, exp_w_up, exp_w_down, l)
        xc = _combine(x1c, yec, idxc, None, nbc)
        xl = _combine(x1l, yel, idxl, g2l, nbl)
    return (xc.reshape(Bc, Lc, D), xl.reshape(Bl, Ll, D)) + tuple(jnp.stack(o, axis=1) for o in outs)
```

```python
import functools
import math

import jax
import jax.numpy as jnp
import numpy as np
from jax import lax
from jax.experimental import pallas as pl
from jax.experimental.pallas import tpu as pltpu
from jax.experimental.pallas import tpu_sc as plsc

D_MODEL = 1024
DEPTH = 4
GRID_W = 64
W_GROUP = 256
H_M = 4
DH_M = 64
S5_CH = 16
G_S5 = 16
P_S5 = 64
S5_STATE = G_S5 * P_S5
H_R = 4
DH_R = 64
H_D = 4
DH_D = 32
N_EXPERTS = 16
CAPACITY_FACTOR = 2
ROPE_BASE = 10000.0
EPS = 1e-6
SUBLANES = 8
VMEM_LIMIT_BYTES = 56 * 1024 * 1024

f32 = jnp.float32
bf16 = jnp.bfloat16
HIGHEST = lax.Precision.HIGHEST
NEG_INF = float("-inf")


def _gelu_tanh(x):
    return 0.5 * x * (1.0 + jnp.tanh(math.sqrt(2.0 / math.pi) * (x + 0.044715 * (x * x * x))))


def _s5_kernel(su_ref, x0r_ref, x0i_ref, wb_ref, wc_ref, lb_ref, pw_ref, d_ref, gw_ref, gb_ref,
               y_ref, xr_ref, xi_ref, st_ref, *, chained):
    n_steps = st_ref.shape[0] // SUBLANES
    su = su_ref[0]
    y_ref[0] = su * d_ref[...]
    row = lax.broadcasted_iota(jnp.int32, (SUBLANES, S5_STATE), 0)
    zeros = jnp.zeros((SUBLANES, S5_STATE), f32)
    for d in range(2):
        st_ref[...] = jnp.dot(su.astype(bf16), wb_ref[0, d], preferred_element_type=f32)
        lbr = jnp.broadcast_to(lb_ref[0, d, 0:1, :], (SUBLANES, S5_STATE))
        lbi = jnp.broadcast_to(lb_ref[0, d, 1:2, :], (SUBLANES, S5_STATE))

        def rows_of(k, d=d):
            kk = k if d == 0 else n_steps - 1 - k
            return pl.ds(pl.multiple_of(kk * SUBLANES, SUBLANES), SUBLANES)

        def scan_step(k, carry, lbr=lbr, lbi=lbi, rows_of=rows_of):
            xr, xi = carry
            r = rows_of(k)
            nxr = lbr * xr - lbi * xi + st_ref[r, 0:S5_STATE]
            nxi = lbr * xi + lbi * xr + st_ref[r, S5_STATE:2 * S5_STATE]
            st_ref[r, 0:S5_STATE] = nxr
            st_ref[r, S5_STATE:2 * S5_STATE] = nxi
            return nxr, nxi

        if not chained:
            xr_ref[0, d], xi_ref[0, d] = lax.fori_loop(0, n_steps, scan_step, (x0r_ref[0, d], x0i_ref[0, d]))
            y_ref[0] += jnp.dot(st_ref[...].astype(bf16), wc_ref[0, d], preferred_element_type=f32)
            continue
        fr, fi = lax.fori_loop(0, n_steps, scan_step, (zeros, zeros))

        cr = x0r_ref[0, d]
        ci = x0i_ref[0, d]
        plr = pw_ref[0, d, 0:1, :]
        pli = pw_ref[0, d, 1:2, :]
        cmr, cmi = zeros, zeros
        for i in (range(SUBLANES) if d == 0 else reversed(range(SUBLANES))):
            cmr = jnp.where(row == i, cr, cmr)
            cmi = jnp.where(row == i, ci, cmi)
            cr, ci = (plr * cr - pli * ci + fr[i:i + 1], plr * ci + pli * cr + fi[i:i + 1])
        xr_ref[0, d] = cr
        xi_ref[0, d] = ci

        def fix_step(k, carry, lbr=lbr, lbi=lbi, cmr=cmr, cmi=cmi, rows_of=rows_of):
            pr, pi = carry
            r = rows_of(k)
            st_ref[r, 0:S5_STATE] = st_ref[r, 0:S5_STATE] + (pr * cmr - pi * cmi)
            st_ref[r, S5_STATE:2 * S5_STATE] = st_ref[r, S5_STATE:2 * S5_STATE] + (pr * cmi + pi * cmr)
            return pr * lbr - pi * lbi, pr * lbi + pi * lbr

        lax.fori_loop(0, n_steps, fix_step, (lbr, lbi))
        y_ref[0] += jnp.dot(st_ref[...].astype(bf16), wc_ref[0, d], preferred_element_type=f32)

    ys = _gelu_tanh(y_ref[0])
    gate = jax.nn.sigmoid(jnp.dot(ys.astype(bf16), gw_ref[0], preferred_element_type=f32) + gb_ref[...])
    y_ref[0] = ys * gate


def _s5_prepare(lam_re, lam_im, log_step, b_re, b_im, c_re, c_im, glu_w, n_steps_list):
    dt = jnp.exp(log_step)[..., None]
    mag = jnp.exp(lam_re * dt)
    ang = lam_im * dt
    lb_re, lb_im = mag * jnp.cos(ang), mag * jnp.sin(ang)
    nr, ni = lb_re - 1.0, lb_im
    den = lam_re * lam_re + lam_im * lam_im
    f_re = (nr * lam_re + ni * lam_im) / den
    f_im = (ni * lam_re - nr * lam_im) / den
    bb_re = f_re[..., None] * b_re[:, None] - f_im[..., None] * b_im[:, None]
    bb_im = f_re[..., None] * b_im[:, None] + f_im[..., None] * b_re[:, None]
    eye = jnp.eye(G_S5, dtype=f32)[:, None, :, None]

    def block_diag(a):
        return (a[:, :, :, :, None, :] * eye).reshape(a.shape[:2] + (G_S5 * a.shape[3], G_S5 * a.shape[4]))

    wb = jnp.concatenate([block_diag(jnp.swapaxes(bb_re, 3, 4)), block_diag(jnp.swapaxes(bb_im, 3, 4))],
                         axis=-1).astype(bf16)
    wc = jnp.concatenate([block_diag(jnp.swapaxes(c_re, 3, 4)), -block_diag(jnp.swapaxes(c_im, 3, 4))],
                         axis=2).astype(bf16)
    lead = lb_re.shape[:2]
    lb = jnp.stack([lb_re.reshape(lead + (S5_STATE,)), lb_im.reshape(lead + (S5_STATE,))], axis=2)
    pr, pi = lb[:, :, 0], lb[:, :, 1]
    tables = {}
    for j in range(int(math.log2(max(n_steps_list))) + 1):
        if 2 ** j in n_steps_list:
            tables[2 ** j] = jnp.stack([pr, pi], axis=2)
        pr, pi = pr * pr - pi * pi, 2.0 * pr * pi
    return wb, wc, lb, tables, glu_w.astype(bf16)


def s5_mixer(su, x0r, x0i, s5p, s5_d, glu_b, layer):
    B, L, _ = su.shape
    wb, wc, lb, tables, glu_w = s5p
    chained = B % SUBLANES != 0
    if chained:
        n_groups, lanes, n_steps = B, 1, L // SUBLANES
        to_rows = lambda a: a.reshape(B, SUBLANES, n_steps, -1).transpose(0, 2, 1, 3).reshape(B, L, -1)
        from_rows = lambda a: a.reshape(B, n_steps, SUBLANES, -1).transpose(0, 2, 1, 3).reshape(B, L, -1)
        pw = tables[n_steps]
    else:
        n_groups, lanes, n_steps = B // SUBLANES, SUBLANES, L
        to_rows = lambda a: a.reshape(n_groups, SUBLANES, L, -1).transpose(0, 2, 1, 3).reshape(n_groups, -1, a.shape[-1])
        from_rows = lambda a: a.reshape(n_groups, L, SUBLANES, -1).transpose(0, 2, 1, 3).reshape(B, L, -1)
        pw = lb
    rows = n_steps * SUBLANES
    state_in = lambda a: a.reshape(n_groups, lanes, 2, S5_STATE).transpose(0, 2, 1, 3)
    state_out = lambda a: a.transpose(0, 2, 1, 3).reshape(B, 2, G_S5, P_S5)
    full = lambda *shape: pl.BlockSpec(shape, lambda b: (0,) * len(shape))
    per_b = lambda *shape: pl.BlockSpec((1,) + shape, lambda b: (b,) + (0,) * len(shape))
    per_layer = lambda *shape: pl.BlockSpec((1,) + shape, lambda b: (layer,) + (0,) * len(shape))
    y_p, xr, xi = pl.pallas_call(
        functools.partial(_s5_kernel, chained=chained),
        grid=(n_groups,),
        in_specs=[per_b(rows, W_GROUP), per_b(2, lanes, S5_STATE), per_b(2, lanes, S5_STATE),
                  per_layer(2, W_GROUP, 2 * S5_STATE), per_layer(2, 2 * S5_STATE, W_GROUP),
                  per_layer(2, 2, S5_STATE), per_layer(2, 2, S5_STATE),
                  full(1, W_GROUP), per_layer(W_GROUP, W_GROUP), full(1, W_GROUP)],
        out_specs=[per_b(rows, W_GROUP), per_b(2, lanes, S5_STATE), per_b(2, lanes, S5_STATE)],
        out_shape=[jax.ShapeDtypeStruct((n_groups, rows, W_GROUP), f32),
                   jax.ShapeDtypeStruct((n_groups, 2, lanes, S5_STATE), f32),
                   jax.ShapeDtypeStruct((n_groups, 2, lanes, S5_STATE), f32)],
        scratch_shapes=[pltpu.VMEM((rows, 2 * S5_STATE), f32)],
        compiler_params=pltpu.CompilerParams(dimension_semantics=("arbitrary",),
                                             vmem_limit_bytes=VMEM_LIMIT_BYTES),
        name="s5_mixer",
    )(to_rows(su), state_in(x0r), state_in(x0i), wb, wc, lb, pw,
      s5_d.reshape(1, W_GROUP), glu_w, glu_b.reshape(1, W_GROUP))
    return from_rows(y_p), state_out(xr), state_out(xi)


Q_TILE = 256
PROJ_BLOCKS = dict(mq=0, mk=1, mv=2, mo=3, su=4, rq=5, rk=6, rv=7, rg=8, dq=9, dk=10, dv=11)
GATE_LANES = 128
N_PROJ = 12 * W_GROUP + GATE_LANES


def _log_sigmoid(x):
    return jnp.minimum(x, 0.0) - jnp.log1p(jnp.exp(-jnp.abs(x)))


def _group_ones(width, group):
    shift = int(math.log2(group))
    r = lax.broadcasted_iota(jnp.int32, (width, width), 0) >> shift
    c = lax.broadcasted_iota(jnp.int32, (width, width), 1) >> shift
    return (r == c).astype(bf16)


def _split3(x):
    hi = x.astype(bf16)
    r = x - hi.astype(f32)
    mid = r.astype(bf16)
    return hi, mid, (r - mid.astype(f32)).astype(bf16)


def _group_mean(x, ones, group):
    return sum(jnp.dot(p, ones, preferred_element_type=f32) for p in _split3(x)) * (1.0 / group)


def _dot_nt(a, b):
    return lax.dot_general(a.astype(bf16), b.astype(bf16), (((1,), (1,)), ((), ())), preferred_element_type=f32)


def _dot(a, b):
    return jnp.dot(a.astype(bf16), b.astype(bf16), preferred_element_type=f32)


def _proj_block(name, rows):
    j = PROJ_BLOCKS[name]
    return pl.BlockSpec((1, rows, W_GROUP), lambda b, qi, j=j, rows=rows: (b, qi if rows == Q_TILE else 0, j))


def _mlstm_kernel(q_ref, k_ref, v_ref, o_ref, g_ref, gt_ref, gb_ref, gbt_ref, ng_ref, c0_ref, n0_ref, m0_ref,
                  h_ref, c_ref, n_ref, m_ref, gl_ref, gu_ref, rc_ref, vt_ref, ht_ref, *, seq_len, q_tile):
    L, TQ = seq_len, q_tile
    nq = L // TQ
    qi = pl.program_id(1)
    grow = gt_ref[0] + gbt_ref[...]

    @pl.when(qi == 0)
    def _():
        ss = lax.broadcasted_iota(jnp.int32, (L, L), 0)
        tt = lax.broadcasted_iota(jnp.int32, (L, L), 1)
        tri_le = (ss <= tt).astype(bf16)
        tri_ge = (ss >= tt).astype(bf16)
        rows = _split3(_log_sigmoid(grow))
        gl = sum(jnp.dot(p, tri_le, preferred_element_type=f32) for p in rows)
        gu = sum(jnp.dot(p, tri_ge, preferred_element_type=f32) for p in rows)
        for j in range(nq):
            gl_ref[j] = gl[:, j * TQ:(j + 1) * TQ]
            gu_ref[j] = gu[:, j * TQ:(j + 1) * TQ]
        gcol = g_ref[0] + gb_ref[...]
        cols = _split3(_log_sigmoid(gcol))
        glc = sum(jnp.dot(tri_ge, p, preferred_element_type=f32) for p in cols)
        guc = sum(jnp.dot(tri_le, p, preferred_element_type=f32) for p in cols)
        lane = lax.broadcasted_iota(jnp.int32, (L, GATE_LANES), 1)
        rc_ref[...] = pltpu.roll(gcol, 4, 1) - jnp.where(lane < 8, glc, guc)

        vt = v_ref[0].T
        kk = k_ref[0] * (DH_M ** -0.5)
        one_row = (lax.broadcasted_iota(jnp.int32, (DH_M, L), 0) == 0).astype(bf16)
        for h in range(H_M):
            hs = slice(h * DH_M, (h + 1) * DH_M)
            vt_ref[2 * h * DH_M:(2 * h + 1) * DH_M, :] = vt[hs, :].astype(bf16)
            vt_ref[(2 * h + 1) * DH_M:(2 * h + 2) * DH_M, :] = one_row
        for d in range(2):
            g_all = gl if d == 0 else gu
            for h in range(H_M):
                hs = slice(h * DH_M, (h + 1) * DH_M)
                ii, fi = 8 * d + h, 8 * d + 4 + h
                g_row = g_all[fi:fi + 1, :]
                g_tot = g_row[:, L - 1:L] if d == 0 else g_row[:, 0:1]
                wlog = g_tot - g_row + grow[ii:ii + 1, :]
                m0 = m0_ref[0, d:d + 1, h:h + 1]
                m_new = jnp.maximum(g_tot + m0, jnp.max(wlog, axis=1, keepdims=True))
                decay = jnp.exp(g_tot + m0 - m_new)
                w = jnp.exp(wlog - m_new)
                kh = kk[:, hs]
                c_ref[0, d, h] = decay * c0_ref[0, d, h] + _dot(vt[hs, :] * w, kh)
                n_upd = jnp.dot(jnp.broadcast_to(w, (SUBLANES, L)), kh, precision=HIGHEST,
                                preferred_element_type=f32)[0:1, :]
                n_ref[0, d, h:h + 1, :] = decay * n0_ref[0, d, h:h + 1, :] + n_upd
                m_ref[0, d:d + 1, h:h + 1] = m_new

    gl_t = gl_ref[qi]
    gu_t = gu_ref[qi]
    s_idx = lax.broadcasted_iota(jnp.int32, (L, TQ), 0)
    t_idx = qi * TQ + lax.broadcasted_iota(jnp.int32, (L, TQ), 1)
    low = s_idx <= t_idx
    upp = s_idx >= t_idx
    qt = q_ref[0].T.astype(bf16)
    k = (k_ref[0] * (DH_M ** -0.5)).astype(bf16)
    rc = rc_ref[...]
    row0 = lax.broadcasted_iota(jnp.int32, (DH_M, DH_M), 0) == 0
    for h in range(H_M):
        hs = slice(h * DH_M, (h + 1) * DH_M)
        qth = qt[hs, :]
        vta = vt_ref[2 * h * DH_M:(2 * h + 2) * DH_M, :]
        s0 = jnp.dot(k[:, hs], qth, preferred_element_type=f32)
        h_sum = None
        for d in range(2):
            fi = 8 * d + 4 + h
            g_t = (gl_t if d == 0 else gu_t)[fi:fi + 1, :]
            dlog = jnp.where(low if d == 0 else upp, rc[:, fi:fi + 1] + g_t, NEG_INF)
            inter = g_t + m0_ref[0, d:d + 1, h:h + 1]
            m_t = jnp.maximum(inter, jnp.max(dlog, axis=0, keepdims=True))
            p = s0 * jnp.exp(dlog - m_t)
            a = jnp.exp(inter - m_t)
            c0n0 = jnp.concatenate([c0_ref[0, d, h], jnp.where(row0, n0_ref[0, d, h:h + 1, :], 0.0)], axis=0)
            numden = (jnp.dot(vta, p.astype(bf16), preferred_element_type=f32)
                      + a * jnp.dot(c0n0.astype(bf16), qth, preferred_element_type=f32))
            scale = 1.0 / jnp.maximum(jnp.abs(numden[DH_M:DH_M + 1, :]), jnp.exp(-m_t))
            hd = numden[0:DH_M, :] * scale
            h_sum = hd if h_sum is None else h_sum + hd
        ht_ref[hs, :] = h_sum * lax.rsqrt(jnp.mean(h_sum * h_sum, axis=0, keepdims=True) + EPS)
    h_ref[0] = jax.nn.sigmoid(o_ref[0]) * (ht_ref[...].T * ng_ref[...])


def mlstm_mixer(proj, gates_t, c0, n0, m0, lp):
    B, L, _ = proj.shape
    TQ = min(L, Q_TILE)
    gb = lp['mlstm_gate_b'].reshape(1, 4 * H_M)
    const = lambda *shape: pl.BlockSpec(shape, lambda b, qi: (0,) * len(shape))
    per_b = lambda *shape: pl.BlockSpec((1,) + shape, lambda b, qi: (b,) + (0,) * len(shape))
    return pl.pallas_call(
        functools.partial(_mlstm_kernel, seq_len=L, q_tile=TQ),
        grid=(B, L // TQ),
        in_specs=[_proj_block('mq', TQ), _proj_block('mk', L), _proj_block('mv', L), _proj_block('mo', TQ),
                  pl.BlockSpec((1, L, GATE_LANES), lambda b, qi: (b, 0, 12 * W_GROUP // GATE_LANES)),
                  per_b(4 * H_M, L), const(1, GATE_LANES), const(4 * H_M, 1), const(1, W_GROUP),
                  per_b(2, H_M, DH_M, DH_M), per_b(2, H_M, DH_M), per_b(2, H_M)],
        out_specs=[pl.BlockSpec((1, TQ, W_GROUP), lambda b, qi: (b, qi, 0)),
                   per_b(2, H_M, DH_M, DH_M), per_b(2, H_M, DH_M), per_b(2, H_M)],
        out_shape=[jax.ShapeDtypeStruct((B, L, W_GROUP), f32),
                   jax.ShapeDtypeStruct((B, 2, H_M, DH_M, DH_M), f32),
                   jax.ShapeDtypeStruct((B, 2, H_M, DH_M), f32),
                   jax.ShapeDtypeStruct((B, 2, H_M), f32)],
        scratch_shapes=[pltpu.VMEM((L // TQ, 4 * H_M, TQ), f32), pltpu.VMEM((L // TQ, 4 * H_M, TQ), f32),
                        pltpu.VMEM((L, GATE_LANES), f32), pltpu.VMEM((2 * W_GROUP, L), bf16),
                        pltpu.VMEM((W_GROUP, TQ), f32)],
        compiler_params=pltpu.CompilerParams(dimension_semantics=("arbitrary", "arbitrary"),
                                             vmem_limit_bytes=VMEM_LIMIT_BYTES),
        name="mlstm_mixer",
    )(proj, proj, proj, proj, proj, gates_t, jnp.pad(gb, ((0, 0), (0, GATE_LANES - 4 * H_M))),
      gb.reshape(4 * H_M, 1), lp['mlstm_norm_g'].reshape(1, W_GROUP), c0, n0, m0)


def _retention_kernel(lg_ref, q_ref, k_ref, v_ref, g_ref, gn_ref, r0_ref, h_ref, r_ref, *, seq_len, q_tile):
    L, TQ = seq_len, q_tile
    qi = pl.program_id(1)
    t_col = qi * TQ + lax.broadcasted_iota(jnp.int32, (TQ, 1), 0)
    rel = (qi * TQ + lax.broadcasted_iota(jnp.int32, (TQ, L), 0)
           - lax.broadcasted_iota(jnp.int32, (TQ, L), 1)).astype(f32)
    q = q_ref[0]
    k = k_ref[0] * (DH_R ** -0.5)
    v = v_ref[0]
    ones = _group_ones(W_GROUP, DH_R)
    for h in range(H_R):
        hs = slice(h * DH_R, (h + 1) * DH_R)
        lgf, lgb = lg_ref[0, h], lg_ref[1, h]
        qh, kh, vh = q[:, hs], k[:, hs], v[:, hs]
        decay = jnp.where(rel > 0.0, jnp.exp(lgf * jnp.maximum(rel, 0.0)),
                          jnp.where(rel < 0.0, jnp.exp(lgb * jnp.maximum(-rel, 0.0)), 2.0))
        o = _dot(_dot_nt(qh, kh) * decay, vh)
        xi_f = jnp.exp(lgf * (t_col + 1).astype(f32))
        xi_b = jnp.exp(lgb * (L - t_col).astype(f32))
        o = o + xi_f * _dot(qh, r0_ref[0, 0, h]) + xi_b * _dot(qh, r0_ref[0, 1, h])
        h_ref[0, :, hs] = o
    o = h_ref[0]
    oc = o - _group_mean(o, ones, DH_R)
    y = oc * lax.rsqrt(_group_mean(oc * oc, ones, DH_R) + EPS) * gn_ref[...]
    h_ref[0] = y * jax.nn.silu(g_ref[0])

    @pl.when(qi == 0)
    def _():
        kt = k.T
        s_row = lax.broadcasted_iota(jnp.int32, (1, L), 1).astype(f32)
        for d in range(2):
            for h in range(H_R):
                hs = slice(h * DH_R, (h + 1) * DH_R)
                lg = lg_ref[d, h]
                zeta = jnp.exp(lg * ((L - 1.0) - s_row)) if d == 0 else jnp.exp(lg * s_row)
                r_ref[0, d, h] = jnp.exp(lg * float(L)) * r0_ref[0, d, h] + _dot(kt[hs, :] * zeta, v[:, hs])


def retention_mixer(proj, r0, lp):
    B, L, _ = proj.shape
    TQ = min(L, Q_TILE)
    log_gamma = -jnp.exp(lp['ret_decay'])
    per_b = lambda *shape: pl.BlockSpec((1,) + shape, lambda b, qi: (b,) + (0,) * len(shape))
    return pl.pallas_call(
        functools.partial(_retention_kernel, seq_len=L, q_tile=TQ),
        grid=(B, L // TQ),
        in_specs=[pl.BlockSpec(memory_space=pltpu.SMEM),
                  _proj_block('rq', TQ), _proj_block('rk', L), _proj_block('rv', L), _proj_block('rg', TQ),
                  pl.BlockSpec((1, W_GROUP), lambda b, qi: (0, 0)), per_b(2, H_R, DH_R, DH_R)],
        out_specs=[pl.BlockSpec((1, TQ, W_GROUP), lambda b, qi: (b, qi, 0)), per_b(2, H_R, DH_R, DH_R)],
        out_shape=[jax.ShapeDtypeStruct((B, L, W_GROUP), f32),
                   jax.ShapeDtypeStruct((B, 2, H_R, DH_R, DH_R), f32)],
        compiler_params=pltpu.CompilerParams(dimension_semantics=("arbitrary", "arbitrary"),
                                             vmem_limit_bytes=VMEM_LIMIT_BYTES),
        name="retention_mixer",
    )(log_gamma, proj, proj, proj, proj, lp['ret_gn_g'].reshape(1, W_GROUP), r0)


def _rope_tables(L):
    half = DH_D // 2
    freqs = ROPE_BASE ** (-np.arange(0, half, 2, dtype=np.float64) / half)
    pos = np.arange(L)
    row, col = (pos // GRID_W).astype(np.float64), (pos % GRID_W).astype(np.float64)
    ang = np.concatenate([np.tile(row[:, None] * freqs, (1, 2)), np.tile(col[:, None] * freqs, (1, 2))], axis=1)
    sign = np.tile(np.concatenate([-np.ones(half // 2), np.ones(half // 2)]), 2)
    cos = np.tile(np.cos(ang), (1, 2 * H_D))
    sin = np.tile(np.sin(ang) * sign, (1, 2 * H_D))
    return jnp.asarray(cos, f32), jnp.asarray(sin, f32)


def _swap_pairs(x):
    parts = []
    for j in range(x.shape[1] // 128):
        xs = x[:, j * 128:(j + 1) * 128]
        lane = lax.broadcasted_iota(jnp.int32, xs.shape, 1)
        parts.append(jnp.where((lane & 15) < 8, pltpu.roll(xs, 120, 1), pltpu.roll(xs, 8, 1)))
    return jnp.concatenate(parts, axis=1)


def _qk_norm(x, gain, ones):
    return x * lax.rsqrt(_group_mean(x * x, ones, DH_D) + EPS) * gain


def _diff_attn_kernel(lam_ref, q_ref, k_ref, v_ref, qkg_ref, sg_ref, *rest, seq_len, q_tile, past_len, out_scale):
    L, TQ, P = seq_len, q_tile, past_len
    if P:
        kc_ref, vc_ref, cos_ref, sin_ref, h_ref, ka_ref, vt_ref, ot_ref = rest
    else:
        h_ref, kn_ref, ka_ref, vt_ref, ot_ref = rest
    qi = pl.program_id(1)
    ones = _group_ones(W_GROUP, DH_D)

    @pl.when(qi == 0)
    def _():
        kn = _qk_norm(k_ref[0], qkg_ref[1:2, :], ones)
        if P:
            kn = kn * cos_ref[...] + _swap_pairs(kn) * sin_ref[...]
            ka_ref[0:P, :] = kc_ref[0, 0].astype(bf16)
            vt_ref[:, 0:P] = vc_ref[0, 0].T.astype(bf16)
        else:
            kn_ref[0] = kn
        ka_ref[P:P + L, :] = kn.astype(bf16)
        vt_ref[:, P:P + L] = v_ref[0].T.astype(bf16)

    qn = _qk_norm(q_ref[0], qkg_ref[0:1, :], ones)
    if P:
        rows = pl.ds(pl.multiple_of(qi * TQ, TQ), TQ)
        qn = qn * cos_ref[rows, :] + _swap_pairs(qn) * sin_ref[rows, :]
    qt = (qn * (DH_D ** -0.5)).T.astype(bf16)
    lam = lam_ref[0, 0]
    ka = ka_ref[...]
    for h in range(H_D):
        probs = []
        for j in range(2):
            cs = slice((2 * h + j) * DH_D, (2 * h + j + 1) * DH_D)
            s = jnp.dot(ka[:, cs], qt[cs, :], preferred_element_type=f32)
            e = jnp.exp(s - jnp.max(s, axis=0, keepdims=True))
            probs.append(e * (1.0 / jnp.sum(e, axis=0, keepdims=True)))
        vs = slice(h * 2 * DH_D, (h + 1) * 2 * DH_D)
        ot_ref[vs, :] = jnp.dot(vt_ref[vs, :], (probs[0] - lam * probs[1]).astype(bf16), preferred_element_type=f32)
    o = ot_ref[...].T
    ones_v = _group_ones(W_GROUP, 2 * DH_D)
    h_ref[0] = o * lax.rsqrt(_group_mean(o * o, ones_v, 2 * DH_D) + EPS) * (sg_ref[...] * out_scale)


def diff_attn_mixer(proj, cache, lp, lam_init, layer):
    B, L, _ = proj.shape
    TQ = min(L, Q_TILE)
    lv = lp['diff_lambda']
    lam = (jnp.exp(jnp.sum(lv[0] * lv[1])) - jnp.exp(jnp.sum(lv[2] * lv[3])) + lam_init).reshape(1, 1)
    qkg = jnp.tile(lp['diff_qk_norm'], (1, 2 * H_D))
    sg = jnp.tile(lp['diff_subln_g'], (H_D,)).reshape(1, W_GROUP)
    const = lambda *shape: pl.BlockSpec(shape, lambda b, qi: (0,) * len(shape))
    in_specs = [pl.BlockSpec(memory_space=pltpu.SMEM),
                _proj_block('dq', TQ), _proj_block('dk', L), _proj_block('dv', L), const(2, W_GROUP), const(1, W_GROUP)]
    args = [lam, proj, proj, proj, qkg, sg]
    out_specs = [pl.BlockSpec((1, TQ, W_GROUP), lambda b, qi: (b, qi, 0))]
    out_shape = [jax.ShapeDtypeStruct((B, L, W_GROUP), f32)]
    P = 0
    if cache is not None:
        ck, cv = cache
        P = ck.shape[2]
        cspec = pl.BlockSpec((1, 1, P, W_GROUP), lambda b, qi, layer=layer: (b, layer, 0, 0))
        cos, sin = _rope_tables(L)
        in_specs += [cspec, cspec, const(L, W_GROUP), const(L, W_GROUP)]
        args += [ck, cv, cos, sin]
    else:
        out_specs.append(pl.BlockSpec((1, L, W_GROUP), lambda b, qi: (b, 0, 0)))
        out_shape.append(jax.ShapeDtypeStruct((B, L, W_GROUP), f32))
    return pl.pallas_call(
        functools.partial(_diff_attn_kernel, seq_len=L, q_tile=TQ, past_len=P, out_scale=1.0 - lam_init),
        grid=(B, L // TQ),
        in_specs=in_specs, out_specs=out_specs, out_shape=out_shape,
        scratch_shapes=[pltpu.VMEM((P + L, W_GROUP), bf16), pltpu.VMEM((W_GROUP, P + L), bf16),
                        pltpu.VMEM((W_GROUP, TQ), f32)],
        compiler_params=pltpu.CompilerParams(dimension_semantics=("arbitrary", "arbitrary"),
                                             vmem_limit_bytes=VMEM_LIMIT_BYTES),
        name="diff_attention",
    )(*args)


PROJ_ROW_TILE = 512
OUT_ROW_TILE = 512
FF_TILE = 512
D_FF = 2 * D_MODEL


def _in_proj_kernel(x_ref, g_ref, sc_ref, sh_ref, w_ref, o_ref):
    x = x_ref[...]
    y = x * lax.rsqrt(jnp.mean(x * x, axis=1, keepdims=True) + EPS) * g_ref[...]
    h = (y * (1.0 + sc_ref[0]) + sh_ref[0]).astype(bf16)
    o_ref[...] = jnp.dot(h, w_ref[0], preferred_element_type=f32)


def in_projection(x, gain, scale, shift, w_p, layer, rows_per_mod):
    n, D = x.shape
    TM = min(n, PROJ_ROW_TILE)
    mod = pl.BlockSpec((1, 1, D), lambda i: (i * TM // rows_per_mod, 0, 0))
    return pl.pallas_call(
        _in_proj_kernel,
        grid=(n // TM,),
        in_specs=[pl.BlockSpec((TM, D), lambda i: (i, 0)), pl.BlockSpec((1, D), lambda i: (0, 0)), mod, mod,
                  pl.BlockSpec((1, D, N_PROJ), lambda i: (layer, 0, 0))],
        out_specs=pl.BlockSpec((TM, N_PROJ), lambda i: (i, 0)),
        out_shape=jax.ShapeDtypeStruct((n, N_PROJ), f32),
        compiler_params=pltpu.CompilerParams(dimension_semantics=("arbitrary",),
                                             vmem_limit_bytes=VMEM_LIMIT_BYTES),
        name="in_projection",
    )(x, gain.reshape(1, D), scale, shift, w_p)


def _out_proj_kernel(x_ref, m0_ref, m1_ref, m2_ref, m3_ref, w_ref, g1_ref, ng_ref, sc_ref, sh_ref, rw_ref,
                     xo_ref, h_ref, aff_ref, wb_ref):
    @pl.when(pl.program_id(0) == 0)
    def _():
        wb_ref[...] = w_ref[0].astype(bf16)

    out = None
    for j, m_ref in enumerate((m0_ref, m1_ref, m2_ref, m3_ref)):
        part = jnp.dot(m_ref[...].astype(bf16), wb_ref[j * W_GROUP:(j + 1) * W_GROUP, :], preferred_element_type=f32)
        out = part if out is None else out + part
    x = x_ref[...] + g1_ref[0] * out
    xo_ref[...] = x
    h = x * lax.rsqrt(jnp.mean(x * x, axis=1, keepdims=True) + EPS) * ng_ref[...]
    h = h * (1.0 + sc_ref[0]) + sh_ref[0]
    h_ref[...] = h
    h_hi = h.astype(bf16)
    h_lo = (h - h_hi.astype(f32)).astype(bf16)
    both = jnp.dot(h_hi, rw_ref[0], preferred_element_type=f32)
    logits = (both[:, 0:N_EXPERTS] + both[:, N_EXPERTS:2 * N_EXPERTS]
              + jnp.dot(h_lo, rw_ref[0], preferred_element_type=f32)[:, 0:N_EXPERTS])
    e = jnp.exp(logits - jnp.max(logits, axis=1, keepdims=True))
    aff_ref[...] = e / jnp.sum(e, axis=1, keepdims=True)


def out_projection(x, mixed, w_out, gate1, gain2, scale2, shift2, router_w2, layer, rows_per_mod):
    n, D = x.shape
    TM = OUT_ROW_TILE
    row = lambda width: pl.BlockSpec((TM, width), lambda i: (i, 0))
    const = lambda *shape: pl.BlockSpec(shape, lambda i: (0,) * len(shape))
    mod = pl.BlockSpec((1, 1, D), lambda i: (i * TM // rows_per_mod, 0, 0))
    return pl.pallas_call(
        _out_proj_kernel,
        grid=(n // TM,),
        in_specs=[row(D), row(W_GROUP), row(W_GROUP), row(W_GROUP), row(W_GROUP),
                  pl.BlockSpec((1, D, D), lambda i: (layer, 0, 0)), mod, const(1, D), mod, mod,
                  pl.BlockSpec((1, D, 2 * N_EXPERTS), lambda i: (layer, 0, 0))],
        out_specs=[row(D), row(D), row(N_EXPERTS)],
        out_shape=[jax.ShapeDtypeStruct((n, D), f32), jax.ShapeDtypeStruct((n, D), f32),
                   jax.ShapeDtypeStruct((n, N_EXPERTS), f32)],
        scratch_shapes=[pltpu.VMEM((D, D), bf16)],
        compiler_params=pltpu.CompilerParams(dimension_semantics=("arbitrary",), vmem_limit_bytes=VMEM_LIMIT_BYTES),
        name="out_projection",
    )(x, *mixed, w_out, gate1, gain2.reshape(1, D), scale2, shift2, router_w2)


def _experts_kernel(xc_ref, xl_ref, gc_ref, gl_ref, g2_ref, wg_ref, wu_ref, wd_ref, yc_ref, yl_ref, ac_ref, al_ref):
    f = pl.program_id(1)
    wg = wg_ref[0, 0].astype(bf16)
    wu = wu_ref[0, 0].astype(bf16)
    wd = wd_ref[0, 0].astype(bf16)
    for x_ref, acc_ref in ((xc_ref, ac_ref), (xl_ref, al_ref)):
        x = x_ref[0].astype(bf16)
        hidden = jax.nn.silu(jnp.dot(x, wg, preferred_element_type=f32)) * jnp.dot(x, wu, preferred_element_type=f32)
        part = jnp.dot(hidden.astype(bf16), wd, preferred_element_type=f32)

        @pl.when(f == 0)
        def _(acc_ref=acc_ref, part=part):
            acc_ref[...] = part

        @pl.when(f > 0)
        def _(acc_ref=acc_ref, part=part):
            acc_ref[...] += part

    @pl.when(f == pl.num_programs(1) - 1)
    def _():
        yc_ref[0] = ac_ref[...] * gc_ref[0] * g2_ref[0]
        yl_ref[0] = al_ref[...] * gl_ref[0]


def expert_ffn(xe_c, xe_l, g_c, g_l, gate2_c, w_gate, w_up, w_down, layer):
    E, Cc, D = xe_c.shape
    Cl = xe_l.shape[1]
    tok = lambda C, width: pl.BlockSpec((1, C, width), lambda e, f: (e, 0, 0))
    return pl.pallas_call(
        _experts_kernel,
        grid=(E, D_FF // FF_TILE),
        in_specs=[tok(Cc, D), tok(Cl, D), tok(Cc, 1), tok(Cl, 1), pl.BlockSpec((1, 1, D), lambda e, f: (0, 0, 0)),
                  pl.BlockSpec((1, 1, D, FF_TILE), lambda e, f: (layer, e, 0, f)),
                  pl.BlockSpec((1, 1, D, FF_TILE), lambda e, f: (layer, e, 0, f)),
                  pl.BlockSpec((1, 1, FF_TILE, D), lambda e, f: (layer, e, f, 0))],
        out_specs=[tok(Cc, D), tok(Cl, D)],
        out_shape=[jax.ShapeDtypeStruct((E, Cc, D), f32), jax.ShapeDtypeStruct((E, Cl, D), f32)],
        scratch_shapes=[pltpu.VMEM((Cc, D), f32), pltpu.VMEM((Cl, D), f32)],
        compiler_params=pltpu.CompilerParams(dimension_semantics=("arbitrary", "arbitrary"),
                                             vmem_limit_bytes=VMEM_LIMIT_BYTES),
        name="expert_ffn",
    )(xe_c, xe_l, g_c, g_l, gate2_c, w_gate, w_up, w_down)


def _permute_w_in(w):
    gate0 = 4 * W_GROUP
    pad = jnp.zeros(w.shape[:-1] + (GATE_LANES - 4 * H_M,), w.dtype)
    return jnp.concatenate([w[..., :gate0], w[..., gate0 + 4 * H_M:], w[..., gate0:gate0 + 4 * H_M], pad],
                           axis=-1).astype(bf16)


def token_mixers(proj, lp, s5p, lam_init, states, cache, layer):
    B, L, _ = proj.shape
    mC0, mn0, mm0, s5r0, s5i0, R0 = states
    gates_t = jnp.swapaxes(proj[:, :, 12 * W_GROUP:12 * W_GROUP + 4 * H_M], 1, 2)
    hm, mC, mn, mm = mlstm_mixer(proj, gates_t, mC0, mn0, mm0, lp)
    su = proj[:, :, PROJ_BLOCKS['su'] * W_GROUP:(PROJ_BLOCKS['su'] + 1) * W_GROUP]
    ys, s5r, s5i = s5_mixer(su, s5r0, s5i0, s5p, lp['s5_d'], lp['s5_glu_b'], layer)
    hr, R = retention_mixer(proj, R0, lp)
    attn = diff_attn_mixer(proj, cache, lp, lam_init, layer)
    mixed = [a.reshape(B * L, W_GROUP) for a in (hm, ys, hr, attn[0])]
    new_ctx = None
    if cache is None:
        v = proj[:, :, PROJ_BLOCKS['dv'] * W_GROUP:(PROJ_BLOCKS['dv'] + 1) * W_GROUP]
        new_ctx = (mC, mn, mm, s5r, s5i, R, attn[1].reshape(B, L, 2 * H_D, DH_D), v.reshape(B, L, H_D, 2 * DH_D))
    return mixed, new_ctx


GATHER_WINDOW = 32
INDEX_LANES = 128


def gather_rows(x, idx):
    num = idx.shape[0]
    width = x.shape[1]
    mesh = plsc.VectorSubcoreMesh(core_axis_name="core", subcore_axis_name="subcore")
    per_core = num // GATHER_WINDOW // mesh.num_cores
    idx_rows = jnp.pad(idx.reshape(num // GATHER_WINDOW, GATHER_WINDOW), ((0, 0), (0, INDEX_LANES - GATHER_WINDOW)))

    @pl.kernel(out_type=jax.ShapeDtypeStruct((num, width), x.dtype), mesh=mesh)
    def gather_kernel(x_hbm, i_hbm, o_hbm):
        base = lax.axis_index("core") * per_core

        def body(i_vmem, o_vmem):
            pltpu.sync_copy(x_hbm.at[i_vmem.at[0, pl.ds(0, GATHER_WINDOW)]], o_vmem)

        pltpu.emit_pipeline(
            body,
            grid=(per_core,),
            in_specs=[pl.BlockSpec((1, INDEX_LANES), index_map=lambda i: (base + i, 0))],
            out_specs=[pl.BlockSpec((GATHER_WINDOW, width), index_map=lambda i: (base + i, 0))],
            core_axis_name="subcore",
            dimension_semantics=(pltpu.PARALLEL,),
        )(i_hbm, o_hbm)

    return gather_kernel(x, idx_rows)


def _route(aff, h2):
    n = aff.shape[0]
    gates, idx = lax.top_k(aff.T, CAPACITY_FACTOR * n // N_EXPERTS)
    return gates[..., None], idx, gather_rows(h2, idx.reshape(-1)).reshape(idx.shape + h2.shape[1:])


def _combine(x, ye, idx, gate2, B):
    n, D = x.shape
    if gate2 is None:
        return x.at[idx.reshape(-1)].add(ye.reshape(-1, D))
    y = jnp.zeros_like(x).at[idx.reshape(-1)].add(ye.reshape(-1, D))
    return (x.reshape(B, n // B, D) + gate2 * y.reshape(B, n // B, D)).reshape(n, D)


PER_LAYER = ('norm1_g', 'norm2_g', 'mlstm_gate_b', 'mlstm_norm_g', 's5_d', 's5_glu_b', 'ret_decay',
             'ret_gn_g', 'diff_qk_norm', 'diff_lambda', 'diff_subln_g')


def kernel(x_prompt, x_sample, state_mlstm_c, state_mlstm_n, state_mlstm_m, state_s5_re, state_s5_im, state_ret, cache_diff_k, cache_diff_v, c, c_ctx, norm1_g, norm2_g, ada_w, ada_b, w_in, w_out, mlstm_gate_b, mlstm_norm_g, s5_lambda_re, s5_lambda_im, s5_log_step, s5_b_re, s5_b_im, s5_c_re, s5_c_im, s5_d, s5_glu_w, s5_glu_b, ret_decay, ret_gn_g, diff_qk_norm, diff_lambda, diff_subln_g, router_w, exp_w_gate, exp_w_up, exp_w_down):
    weights = dict(norm1_g=norm1_g, norm2_g=norm2_g, mlstm_gate_b=mlstm_gate_b,
                   mlstm_norm_g=mlstm_norm_g, s5_d=s5_d, s5_glu_b=s5_glu_b, ret_decay=ret_decay, ret_gn_g=ret_gn_g,
                   diff_qk_norm=diff_qk_norm, diff_lambda=diff_lambda, diff_subln_g=diff_subln_g)
    w_in_p = _permute_w_in(w_in)
    rw_hi = router_w.astype(bf16)
    router_w2 = jnp.concatenate([rw_hi, (router_w - rw_hi.astype(f32)).astype(bf16)], axis=-1)
    s5p = _s5_prepare(s5_lambda_re, s5_lambda_im, s5_log_step, s5_b_re, s5_b_im, s5_c_re, s5_c_im, s5_glu_w,
                      (x_prompt.shape[1] // SUBLANES, x_sample.shape[1] // SUBLANES))
    Bc, Lc, D = x_prompt.shape
    Bl, Ll, _ = x_sample.shape
    xc = x_prompt.reshape(Bc * Lc, D)
    xl = x_sample.reshape(Bl * Ll, D)
    zero_states = (jnp.zeros((Bc, 2, H_M, DH_M, DH_M), f32), jnp.zeros((Bc, 2, H_M, DH_M), f32),
                   jnp.zeros((Bc, 2, H_M), f32), jnp.zeros((Bc, 2, G_S5, P_S5), f32),
                   jnp.zeros((Bc, 2, G_S5, P_S5), f32), jnp.zeros((Bc, 2, H_R, DH_R, DH_R), f32))
    cache = (cache_diff_k.reshape(cache_diff_k.shape[:3] + (W_GROUP,)),
             cache_diff_v.reshape(cache_diff_v.shape[:3] + (W_GROUP,)))
    cvec = jnp.concatenate([c_ctx[None, :], c], axis=0)
    outs = [[] for _ in range(8)]
    for l in range(DEPTH):
        lp = {name: weights[name][l] for name in PER_LAYER}
        lam_init = 0.8 - 0.6 * math.exp(-0.3 * l)
        mods = jnp.split((jax.nn.silu(cvec) @ ada_w[l] + ada_b[l])[:, None, :], 6, axis=-1)
        lat_states = (state_mlstm_c[:, l], state_mlstm_n[:, l], state_mlstm_m[:, l], state_s5_re[:, l],
                      state_s5_im[:, l], state_ret[:, l])
        routed = []
        for x, B, L, sel, states, kv in ((xc, Bc, Lc, slice(0, 1), zero_states, None),
                                        (xl, Bl, Ll, slice(1, 1 + Bl), lat_states, cache)):
            sh1, sc1, g1, sh2, sc2, g2 = (m[sel] for m in mods)
            rows_per_mod = x.shape[0] // sh1.shape[0]
            proj = in_projection(x, lp['norm1_g'], sc1, sh1, w_in_p, l, rows_per_mod).reshape(B, L, N_PROJ)
            mixed, new_ctx = token_mixers(proj, lp, s5p, lam_init, states, kv, l)
            if new_ctx is not None:
                for acc, t in zip(outs, new_ctx):
                    acc.append(t)
            x1, h2, aff = out_projection(x, mixed, w_out, g1, lp['norm2_g'], sc2, sh2, router_w2, l, rows_per_mod)
            routed.append((x1, g2, sh1.shape[0]) + _route(aff, h2))
        (x1c, g2c, nbc, gc, idxc, xec), (x1l, g2l, nbl, gl, idxl, xel) = routed
        yec, yel = expert_ffn(xec, xel, gc, gl, g2c, exp_w_gate, exp_w_up, exp_w_down, l)
        xc = _combine(x1c, yec, idxc, None, nbc)
        xl = _combine(x1l, yel, idxl, g2l, nbl)
    return (xc.reshape(Bc, Lc, D), xl.reshape(Bl, Ll, D)) + tuple(jnp.stack(o, axis=1) for o in outs)
```

```python
import functools
import math

import jax
import jax.numpy as jnp
import numpy as np
from jax import lax
from jax.experimental import pallas as pl
from jax.experimental.pallas import tpu as pltpu
from jax.experimental.pallas import tpu_sc as plsc

D_MODEL = 1024
DEPTH = 4
GRID_W = 64
W_GROUP = 256
H_M = 4
DH_M = 64
S5_CH = 16
G_S5 = 16
P_S5 = 64
S5_STATE = G_S5 * P_S5
H_R = 4
DH_R = 64
H_D = 4
DH_D = 32
N_EXPERTS = 16
CAPACITY_FACTOR = 2
ROPE_BASE = 10000.0
EPS = 1e-6
SUBLANES = 8
VMEM_LIMIT_BYTES = 56 * 1024 * 1024

f32 = jnp.float32
bf16 = jnp.bfloat16
HIGHEST = lax.Precision.HIGHEST
NEG_INF = float("-inf")


def _gelu_tanh(x):
    return 0.5 * x * (1.0 + jnp.tanh(math.sqrt(2.0 / math.pi) * (x + 0.044715 * (x * x * x))))


def _s5_kernel(su_ref, x0r_ref, x0i_ref, wb_ref, wc_ref, lb_ref, pw_ref, d_ref, gw_ref, gb_ref,
               y_ref, xr_ref, xi_ref, st_ref, *, chained):
    n_steps = st_ref.shape[0] // SUBLANES
    su = su_ref[0]
    y_ref[0] = su * d_ref[...]
    row = lax.broadcasted_iota(jnp.int32, (SUBLANES, S5_STATE), 0)
    zeros = jnp.zeros((SUBLANES, S5_STATE), f32)
    for d in range(2):
        st_ref[...] = jnp.dot(su.astype(bf16), wb_ref[0, d], preferred_element_type=f32)
        lbr = jnp.broadcast_to(lb_ref[0, d, 0:1, :], (SUBLANES, S5_STATE))
        lbi = jnp.broadcast_to(lb_ref[0, d, 1:2, :], (SUBLANES, S5_STATE))

        def rows_of(k, d=d):
            kk = k if d == 0 else n_steps - 1 - k
            return pl.ds(pl.multiple_of(kk * SUBLANES, SUBLANES), SUBLANES)

        def scan_step(k, carry, lbr=lbr, lbi=lbi, rows_of=rows_of):
            xr, xi = carry
            r = rows_of(k)
            nxr = lbr * xr - lbi * xi + st_ref[r, 0:S5_STATE]
            nxi = lbr * xi + lbi * xr + st_ref[r, S5_STATE:2 * S5_STATE]
            st_ref[r, 0:S5_STATE] = nxr
            st_ref[r, S5_STATE:2 * S5_STATE] = nxi
            return nxr, nxi

        if not chained:
            xr_ref[0, d], xi_ref[0, d] = lax.fori_loop(0, n_steps, scan_step, (x0r_ref[0, d], x0i_ref[0, d]))
            y_ref[0] += jnp.dot(st_ref[...].astype(bf16), wc_ref[0, d], preferred_element_type=f32)
            continue
        fr, fi = lax.fori_loop(0, n_steps, scan_step, (zeros, zeros))

        cr = x0r_ref[0, d]
        ci = x0i_ref[0, d]
        plr = pw_ref[0, d, 0:1, :]
        pli = pw_ref[0, d, 1:2, :]
        cmr, cmi = zeros, zeros
        for i in (range(SUBLANES) if d == 0 else reversed(range(SUBLANES))):
            cmr = jnp.where(row == i, cr, cmr)
            cmi = jnp.where(row == i, ci, cmi)
            cr, ci = (plr * cr - pli * ci + fr[i:i + 1], plr * ci + pli * cr + fi[i:i + 1])
        xr_ref[0, d] = cr
        xi_ref[0, d] = ci

        def fix_step(k, carry, lbr=lbr, lbi=lbi, cmr=cmr, cmi=cmi, rows_of=rows_of):
            pr, pi = carry
            r = rows_of(k)
            st_ref[r, 0:S5_STATE] = st_ref[r, 0:S5_STATE] + (pr * cmr - pi * cmi)
            st_ref[r, S5_STATE:2 * S5_STATE] = st_ref[r, S5_STATE:2 * S5_STATE] + (pr * cmi + pi * cmr)
            return pr * lbr - pi * lbi, pr * lbi + pi * lbr

        lax.fori_loop(0, n_steps, fix_step, (lbr, lbi))
        y_ref[0] += jnp.dot(st_ref[...].astype(bf16), wc_ref[0, d], preferred_element_type=f32)

    ys = _gelu_tanh(y_ref[0])
    gate = jax.nn.sigmoid(jnp.dot(ys.astype(bf16), gw_ref[0], preferred_element_type=f32) + gb_ref[...])
    y_ref[0] = ys * gate


def _s5_prepare(lam_re, lam_im, log_step, b_re, b_im, c_re, c_im, glu_w, n_steps_list):
    dt = jnp.exp(log_step)[..., None]
    mag = jnp.exp(lam_re * dt)
    ang = lam_im * dt
    lb_re, lb_im = mag * jnp.cos(ang), mag * jnp.sin(ang)
    nr, ni = lb_re - 1.0, lb_im
    den = lam_re * lam_re + lam_im * lam_im
    f_re = (nr * lam_re + ni * lam_im) / den
    f_im = (ni * lam_re - nr * lam_im) / den
    bb_re = f_re[..., None] * b_re[:, None] - f_im[..., None] * b_im[:, None]
    bb_im = f_re[..., None] * b_im[:, None] + f_im[..., None] * b_re[:, None]
    eye = jnp.eye(G_S5, dtype=f32)[:, None, :, None]

    def block_diag(a):
        return (a[:, :, :, :, None, :] * eye).reshape(a.shape[:2] + (G_S5 * a.shape[3], G_S5 * a.shape[4]))

    wb = jnp.concatenate([block_diag(jnp.swapaxes(bb_re, 3, 4)), block_diag(jnp.swapaxes(bb_im, 3, 4))],
                         axis=-1).astype(bf16)
    wc = jnp.concatenate([block_diag(jnp.swapaxes(c_re, 3, 4)), -block_diag(jnp.swapaxes(c_im, 3, 4))],
                         axis=2).astype(bf16)
    lead = lb_re.shape[:2]
    lb = jnp.stack([lb_re.reshape(lead + (S5_STATE,)), lb_im.reshape(lead + (S5_STATE,))], axis=2)
    pr, pi = lb[:, :, 0], lb[:, :, 1]
    tables = {}
    for j in range(int(math.log2(max(n_steps_list))) + 1):
        if 2 ** j in n_steps_list:
            tables[2 ** j] = jnp.stack([pr, pi], axis=2)
        pr, pi = pr * pr - pi * pi, 2.0 * pr * pi
    return wb, wc, lb, tables, glu_w.astype(bf16)


def s5_mixer(su, x0r, x0i, s5p, s5_d, glu_b, layer):
    B, L, _ = su.shape
    wb, wc, lb, tables, glu_w = s5p
    chained = B % SUBLANES != 0
    if chained:
        n_groups, lanes, n_steps = B, 1, L // SUBLANES
        to_rows = lambda a: a.reshape(B, SUBLANES, n_steps, -1).transpose(0, 2, 1, 3).reshape(B, L, -1)
        from_rows = lambda a: a.reshape(B, n_steps, SUBLANES, -1).transpose(0, 2, 1, 3).reshape(B, L, -1)
        pw = tables[n_steps]
    else:
        n_groups, lanes, n_steps = B // SUBLANES, SUBLANES, L
        to_rows = lambda a: a.reshape(n_groups, SUBLANES, L, -1).transpose(0, 2, 1, 3).reshape(n_groups, -1, a.shape[-1])
        from_rows = lambda a: a.reshape(n_groups, L, SUBLANES, -1).transpose(0, 2, 1, 3).reshape(B, L, -1)
        pw = lb
    rows = n_steps * SUBLANES
    state_in = lambda a: a.reshape(n_groups, lanes, 2, S5_STATE).transpose(0, 2, 1, 3)
    state_out = lambda a: a.transpose(0, 2, 1, 3).reshape(B, 2, G_S5, P_S5)
    full = lambda *shape: pl.BlockSpec(shape, lambda b: (0,) * len(shape))
    per_b = lambda *shape: pl.BlockSpec((1,) + shape, lambda b: (b,) + (0,) * len(shape))
    per_layer = lambda *shape: pl.BlockSpec((1,) + shape, lambda b: (layer,) + (0,) * len(shape))
    y_p, xr, xi = pl.pallas_call(
        functools.partial(_s5_kernel, chained=chained),
        grid=(n_groups,),
        in_specs=[per_b(rows, W_GROUP), per_b(2, lanes, S5_STATE), per_b(2, lanes, S5_STATE),
                  per_layer(2, W_GROUP, 2 * S5_STATE), per_layer(2, 2 * S5_STATE, W_GROUP),
                  per_layer(2, 2, S5_STATE), per_layer(2, 2, S5_STATE),
                  full(1, W_GROUP), per_layer(W_GROUP, W_GROUP), full(1, W_GROUP)],
        out_specs=[per_b(rows, W_GROUP), per_b(2, lanes, S5_STATE), per_b(2, lanes, S5_STATE)],
        out_shape=[jax.ShapeDtypeStruct((n_groups, rows, W_GROUP), f32),
                   jax.ShapeDtypeStruct((n_groups, 2, lanes, S5_STATE), f32),
                   jax.ShapeDtypeStruct((n_groups, 2, lanes, S5_STATE), f32)],
        scratch_shapes=[pltpu.VMEM((rows, 2 * S5_STATE), f32)],
        compiler_params=pltpu.CompilerParams(dimension_semantics=("arbitrary",),
                                             vmem_limit_bytes=VMEM_LIMIT_BYTES),
        name="s5_mixer",
    )(to_rows(su), state_in(x0r), state_in(x0i), wb, wc, lb, pw,
      s5_d.reshape(1, W_GROUP), glu_w, glu_b.reshape(1, W_GROUP))
    return from_rows(y_p), state_out(xr), state_out(xi)


Q_TILE = 256
PROJ_BLOCKS = dict(mq=0, mk=1, mv=2, mo=3, su=4, rq=5, rk=6, rv=7, rg=8, dq=9, dk=10, dv=11)
GATE_LANES = 128
N_PROJ = 12 * W_GROUP + GATE_LANES


def _log_sigmoid(x):
    return jnp.minimum(x, 0.0) - jnp.log1p(jnp.exp(-jnp.abs(x)))


def _group_ones(width, group):
    shift = int(math.log2(group))
    r = lax.broadcasted_iota(jnp.int32, (width, width), 0) >> shift
    c = lax.broadcasted_iota(jnp.int32, (width, width), 1) >> shift
    return (r == c).astype(bf16)


def _split3(x):
    hi = x.astype(bf16)
    r = x - hi.astype(f32)
    mid = r.astype(bf16)
    return hi, mid, (r - mid.astype(f32)).astype(bf16)


def _group_mean(x, ones, group):
    return sum(jnp.dot(p, ones, preferred_element_type=f32) for p in _split3(x)) * (1.0 / group)


def _dot_nt(a, b):
    return lax.dot_general(a.astype(bf16), b.astype(bf16), (((1,), (1,)), ((), ())), preferred_element_type=f32)


def _dot(a, b):
    return jnp.dot(a.astype(bf16), b.astype(bf16), preferred_element_type=f32)


def _proj_block(name, rows):
    j = PROJ_BLOCKS[name]
    return pl.BlockSpec((1, rows, W_GROUP), lambda b, qi, j=j, rows=rows: (b, qi if rows == Q_TILE else 0, j))


def _mlstm_kernel(q_ref, k_ref, v_ref, o_ref, g_ref, gt_ref, gb_ref, gbt_ref, ng_ref, c0_ref, n0_ref, m0_ref,
                  h_ref, c_ref, n_ref, m_ref, gl_ref, gu_ref, rc_ref, vt_ref, ht_ref, *, seq_len, q_tile):
    L, TQ = seq_len, q_tile
    nq = L // TQ
    qi = pl.program_id(1)
    grow = gt_ref[0] + gbt_ref[...]

    @pl.when(qi == 0)
    def _():
        ss = lax.broadcasted_iota(jnp.int32, (L, L), 0)
        tt = lax.broadcasted_iota(jnp.int32, (L, L), 1)
        tri_le = (ss <= tt).astype(bf16)
        tri_ge = (ss >= tt).astype(bf16)
        rows = _split3(_log_sigmoid(grow))
        gl = sum(jnp.dot(p, tri_le, preferred_element_type=f32) for p in rows)
        gu = sum(jnp.dot(p, tri_ge, preferred_element_type=f32) for p in rows)
        for j in range(nq):
            gl_ref[j] = gl[:, j * TQ:(j + 1) * TQ]
            gu_ref[j] = gu[:, j * TQ:(j + 1) * TQ]
        gcol = g_ref[0] + gb_ref[...]
        cols = _split3(_log_sigmoid(gcol))
        glc = sum(jnp.dot(tri_ge, p, preferred_element_type=f32) for p in cols)
        guc = sum(jnp.dot(tri_le, p, preferred_element_type=f32) for p in cols)
        lane = lax.broadcasted_iota(jnp.int32, (L, GATE_LANES), 1)
        rc_ref[...] = pltpu.roll(gcol, 4, 1) - jnp.where(lane < 8, glc, guc)

        vt = v_ref[0].T
        kk = k_ref[0] * (DH_M ** -0.5)
        one_row = (lax.broadcasted_iota(jnp.int32, (DH_M, L), 0) == 0).astype(bf16)
        for h in range(H_M):
            hs = slice(h * DH_M, (h + 1) * DH_M)
            vt_ref[2 * h * DH_M:(2 * h + 1) * DH_M, :] = vt[hs, :].astype(bf16)
            vt_ref[(2 * h + 1) * DH_M:(2 * h + 2) * DH_M, :] = one_row
        for d in range(2):
            g_all = gl if d == 0 else gu
            for h in range(H_M):
                hs = slice(h * DH_M, (h + 1) * DH_M)
                ii, fi = 8 * d + h, 8 * d + 4 + h
                g_row = g_all[fi:fi + 1, :]
                g_tot = g_row[:, L - 1:L] if d == 0 else g_row[:, 0:1]
                wlog = g_tot - g_row + grow[ii:ii + 1, :]
                m0 = m0_ref[0, d:d + 1, h:h + 1]
                m_new = jnp.maximum(g_tot + m0, jnp.max(wlog, axis=1, keepdims=True))
                decay = jnp.exp(g_tot + m0 - m_new)
                w = jnp.exp(wlog - m_new)
                kh = kk[:, hs]
                c_ref[0, d, h] = decay * c0_ref[0, d, h] + _dot(vt[hs, :] * w, kh)
                n_upd = jnp.dot(jnp.broadcast_to(w, (SUBLANES, L)), kh, precision=HIGHEST,
                                preferred_element_type=f32)[0:1, :]
                n_ref[0, d, h:h + 1, :] = decay * n0_ref[0, d, h:h + 1, :] + n_upd
                m_ref[0, d:d + 1, h:h + 1] = m_new

    gl_t = gl_ref[qi]
    gu_t = gu_ref[qi]
    s_idx = lax.broadcasted_iota(jnp.int32, (L, TQ), 0)
    t_idx = qi * TQ + lax.broadcasted_iota(jnp.int32, (L, TQ), 1)
    low = s_idx <= t_idx
    upp = s_idx >= t_idx
    qt = q_ref[0].T.astype(bf16)
    k = (k_ref[0] * (DH_M ** -0.5)).astype(bf16)
    rc = rc_ref[...]
    row0 = lax.broadcasted_iota(jnp.int32, (DH_M, DH_M), 0) == 0
    for h in range(H_M):
        hs = slice(h * DH_M, (h + 1) * DH_M)
        qth = qt[hs, :]
        vta = vt_ref[2 * h * DH_M:(2 * h + 2) * DH_M, :]
        s0 = jnp.dot(k[:, hs], qth, preferred_element_type=f32)
        h_sum = None
        for d in range(2):
            fi = 8 * d + 4 + h
            g_t = (gl_t if d == 0 else gu_t)[fi:fi + 1, :]
            dlog = jnp.where(low if d == 0 else upp, rc[:, fi:fi + 1] + g_t, NEG_INF)
            inter = g_t + m0_ref[0, d:d + 1, h:h + 1]
            m_t = jnp.maximum(inter, jnp.max(dlog, axis=0, keepdims=True))
            p = s0 * jnp.exp(dlog - m_t)
            a = jnp.exp(inter - m_t)
            c0n0 = jnp.concatenate([c0_ref[0, d, h], jnp.where(row0, n0_ref[0, d, h:h + 1, :], 0.0)], axis=0)
            numden = (jnp.dot(vta, p.astype(bf16), preferred_element_type=f32)
                      + a * jnp.dot(c0n0.astype(bf16), qth, preferred_element_type=f32))
            scale = 1.0 / jnp.maximum(jnp.abs(numden[DH_M:DH_M + 1, :]), jnp.exp(-m_t))
            hd = numden[0:DH_M, :] * scale
            h_sum = hd if h_sum is None else h_sum + hd
        ht_ref[hs, :] = h_sum * lax.rsqrt(jnp.mean(h_sum * h_sum, axis=0, keepdims=True) + EPS)
    h_ref[0] = jax.nn.sigmoid(o_ref[0]) * (ht_ref[...].T * ng_ref[...])


def mlstm_mixer(proj, gates_t, c0, n0, m0, lp):
    B, L, _ = proj.shape
    TQ = min(L, Q_TILE)
    gb = lp['mlstm_gate_b'].reshape(1, 4 * H_M)
    const = lambda *shape: pl.BlockSpec(shape, lambda b, qi: (0,) * len(shape))
    per_b = lambda *shape: pl.BlockSpec((1,) + shape, lambda b, qi: (b,) + (0,) * len(shape))
    return pl.pallas_call(
        functools.partial(_mlstm_kernel, seq_len=L, q_tile=TQ),
        grid=(B, L // TQ),
        in_specs=[_proj_block('mq', TQ), _proj_block('mk', L), _proj_block('mv', L), _proj_block('mo', TQ),
                  pl.BlockSpec((1, L, GATE_LANES), lambda b, qi: (b, 0, 12 * W_GROUP // GATE_LANES)),
                  per_b(4 * H_M, L), const(1, GATE_LANES), const(4 * H_M, 1), const(1, W_GROUP),
                  per_b(2, H_M, DH_M, DH_M), per_b(2, H_M, DH_M), per_b(2, H_M)],
        out_specs=[pl.BlockSpec((1, TQ, W_GROUP), lambda b, qi: (b, qi, 0)),
                   per_b(2, H_M, DH_M, DH_M), per_b(2, H_M, DH_M), per_b(2, H_M)],
        out_shape=[jax.ShapeDtypeStruct((B, L, W_GROUP), f32),
                   jax.ShapeDtypeStruct((B, 2, H_M, DH_M, DH_M), f32),
                   jax.ShapeDtypeStruct((B, 2, H_M, DH_M), f32),
                   jax.ShapeDtypeStruct((B, 2, H_M), f32)],
        scratch_shapes=[pltpu.VMEM((L // TQ, 4 * H_M, TQ), f32), pltpu.VMEM((L // TQ, 4 * H_M, TQ), f32),
                        pltpu.VMEM((L, GATE_LANES), f32), pltpu.VMEM((2 * W_GROUP, L), bf16),
                        pltpu.VMEM((W_GROUP, TQ), f32)],
        compiler_params=pltpu.CompilerParams(dimension_semantics=("arbitrary", "arbitrary"),
                                             vmem_limit_bytes=VMEM_LIMIT_BYTES),
        name="mlstm_mixer",
    )(proj, proj, proj, proj, proj, gates_t, jnp.pad(gb, ((0, 0), (0, GATE_LANES - 4 * H_M))),
      gb.reshape(4 * H_M, 1), lp['mlstm_norm_g'].reshape(1, W_GROUP), c0, n0, m0)


def _retention_kernel(lg_ref, q_ref, k_ref, v_ref, g_ref, gn_ref, r0_ref, h_ref, r_ref, *, seq_len, q_tile):
    L, TQ = seq_len, q_tile
    qi = pl.program_id(1)
    t_col = qi * TQ + lax.broadcasted_iota(jnp.int32, (TQ, 1), 0)
    rel = (qi * TQ + lax.broadcasted_iota(jnp.int32, (TQ, L), 0)
           - lax.broadcasted_iota(jnp.int32, (TQ, L), 1)).astype(f32)
    q = q_ref[0]
    k = k_ref[0] * (DH_R ** -0.5)
    v = v_ref[0]
    ones = _group_ones(W_GROUP, DH_R)
    for h in range(H_R):
        hs = slice(h * DH_R, (h + 1) * DH_R)
        lgf, lgb = lg_ref[0, h], lg_ref[1, h]
        qh, kh, vh = q[:, hs], k[:, hs], v[:, hs]
        decay = jnp.where(rel > 0.0, jnp.exp(lgf * jnp.maximum(rel, 0.0)),
                          jnp.where(rel < 0.0, jnp.exp(lgb * jnp.maximum(-rel, 0.0)), 2.0))
        o = _dot(_dot_nt(qh, kh) * decay, vh)
        xi_f = jnp.exp(lgf * (t_col + 1).astype(f32))
        xi_b = jnp.exp(lgb * (L - t_col).astype(f32))
        o = o + xi_f * _dot(qh, r0_ref[0, 0, h]) + xi_b * _dot(qh, r0_ref[0, 1, h])
        h_ref[0, :, hs] = o
    o = h_ref[0]
    oc = o - _group_mean(o, ones, DH_R)
    y = oc * lax.rsqrt(_group_mean(oc * oc, ones, DH_R) + EPS) * gn_ref[...]
    h_ref[0] = y * jax.nn.silu(g_ref[0])

    @pl.when(qi == 0)
    def _():
        kt = k.T
        s_row = lax.broadcasted_iota(jnp.int32, (1, L), 1).astype(f32)
        for d in range(2):
            for h in range(H_R):
                hs = slice(h * DH_R, (h + 1) * DH_R)
                lg = lg_ref[d, h]
                zeta = jnp.exp(lg * ((L - 1.0) - s_row)) if d == 0 else jnp.exp(lg * s_row)
                r_ref[0, d, h] = jnp.exp(lg * float(L)) * r0_ref[0, d, h] + _dot(kt[hs, :] * zeta, v[:, hs])


def retention_mixer(proj, r0, lp):
    B, L, _ = proj.shape
    TQ = min(L, Q_TILE)
    log_gamma = -jnp.exp(lp['ret_decay'])
    per_b = lambda *shape: pl.BlockSpec((1,) + shape, lambda b, qi: (b,) + (0,) * len(shape))
    return pl.pallas_call(
        functools.partial(_retention_kernel, seq_len=L, q_tile=TQ),
        grid=(B, L // TQ),
        in_specs=[pl.BlockSpec(memory_space=pltpu.SMEM),
                  _proj_block('rq', TQ), _proj_block('rk', L), _proj_block('rv', L), _proj_block('rg', TQ),
                  pl.BlockSpec((1, W_GROUP), lambda b, qi: (0, 0)), per_b(2, H_R, DH_R, DH_R)],
        out_specs=[pl.BlockSpec((1, TQ, W_GROUP), lambda b, qi: (b, qi, 0)), per_b(2, H_R, DH_R, DH_R)],
        out_shape=[jax.ShapeDtypeStruct((B, L, W_GROUP), f32),
                   jax.ShapeDtypeStruct((B, 2, H_R, DH_R, DH_R), f32)],
        compiler_params=pltpu.CompilerParams(dimension_semantics=("arbitrary", "arbitrary"),
                                             vmem_limit_bytes=VMEM_LIMIT_BYTES),
        name="retention_mixer",
    )(log_gamma, proj, proj, proj, proj, lp['ret_gn_g'].reshape(1, W_GROUP), r0)


def _rope_tables(L):
    half = DH_D // 2
    freqs = ROPE_BASE ** (-np.arange(0, half, 2, dtype=np.float64) / half)
    pos = np.arange(L)
    row, col = (pos // GRID_W).astype(np.float64), (pos % GRID_W).astype(np.float64)
    ang = np.concatenate([np.tile(row[:, None] * freqs, (1, 2)), np.tile(col[:, None] * freqs, (1, 2))], axis=1)
    sign = np.tile(np.concatenate([-np.ones(half // 2), np.ones(half // 2)]), 2)
    cos = np.tile(np.cos(ang), (1, 2 * H_D))
    sin = np.tile(np.sin(ang) * sign, (1, 2 * H_D))
    return jnp.asarray(cos, f32), jnp.asarray(sin, f32)


def _swap_pairs(x):
    parts = []
    for j in range(x.shape[1] // 128):
        xs = x[:, j * 128:(j + 1) * 128]
        lane = lax.broadcasted_iota(jnp.int32, xs.shape, 1)
        parts.append(jnp.where((lane & 15) < 8, pltpu.roll(xs, 120, 1), pltpu.roll(xs, 8, 1)))
    return jnp.concatenate(parts, axis=1)


def _qk_norm(x, gain, ones):
    return x * lax.rsqrt(_group_mean(x * x, ones, DH_D) + EPS) * gain


def _diff_attn_kernel(lam_ref, q_ref, k_ref, v_ref, qkg_ref, sg_ref, *rest, seq_len, q_tile, past_len, out_scale):
    L, TQ, P = seq_len, q_tile, past_len
    if P:
        kc_ref, vc_ref, cos_ref, sin_ref, h_ref, ka_ref, vt_ref, ot_ref = rest
    else:
        h_ref, kn_ref, ka_ref, vt_ref, ot_ref = rest
    qi = pl.program_id(1)
    ones = _group_ones(W_GROUP, DH_D)

    @pl.when(qi == 0)
    def _():
        kn = _qk_norm(k_ref[0], qkg_ref[1:2, :], ones)
        if P:
            kn = kn * cos_ref[...] + _swap_pairs(kn) * sin_ref[...]
            ka_ref[0:P, :] = kc_ref[0, 0].astype(bf16)
            vt_ref[:, 0:P] = vc_ref[0, 0].T.astype(bf16)
        else:
            kn_ref[0] = kn
        ka_ref[P:P + L, :] = kn.astype(bf16)
        vt_ref[:, P:P + L] = v_ref[0].T.astype(bf16)

    qn = _qk_norm(q_ref[0], qkg_ref[0:1, :], ones)
    if P:
        rows = pl.ds(pl.multiple_of(qi * TQ, TQ), TQ)
        qn = qn * cos_ref[rows, :] + _swap_pairs(qn) * sin_ref[rows, :]
    qt = (qn * (DH_D ** -0.5)).T.astype(bf16)
    lam = lam_ref[0, 0]
    ka = ka_ref[...]
    for h in range(H_D):
        probs = []
        for j in range(2):
            cs = slice((2 * h + j) * DH_D, (2 * h + j + 1) * DH_D)
            s = jnp.dot(ka[:, cs], qt[cs, :], preferred_element_type=f32)
            e = jnp.exp(s - jnp.max(s, axis=0, keepdims=True))
            probs.append(e * (1.0 / jnp.sum(e, axis=0, keepdims=True)))
        vs = slice(h * 2 * DH_D, (h + 1) * 2 * DH_D)
        ot_ref[vs, :] = jnp.dot(vt_ref[vs, :], (probs[0] - lam * probs[1]).astype(bf16), preferred_element_type=f32)
    o = ot_ref[...].T
    ones_v = _group_ones(W_GROUP, 2 * DH_D)
    h_ref[0] = o * lax.rsqrt(_group_mean(o * o, ones_v, 2 * DH_D) + EPS) * (sg_ref[...] * out_scale)


def diff_attn_mixer(proj, cache, lp, lam_init, layer):
    B, L, _ = proj.shape
    TQ = min(L, Q_TILE)
    lv = lp['diff_lambda']
    lam = (jnp.exp(jnp.sum(lv[0] * lv[1])) - jnp.exp(jnp.sum(lv[2] * lv[3])) + lam_init).reshape(1, 1)
    qkg = jnp.tile(lp['diff_qk_norm'], (1, 2 * H_D))
    sg = jnp.tile(lp['diff_subln_g'], (H_D,)).reshape(1, W_GROUP)
    const = lambda *shape: pl.BlockSpec(shape, lambda b, qi: (0,) * len(shape))
    in_specs = [pl.BlockSpec(memory_space=pltpu.SMEM),
                _proj_block('dq', TQ), _proj_block('dk', L), _proj_block('dv', L), const(2, W_GROUP), const(1, W_GROUP)]
    args = [lam, proj, proj, proj, qkg, sg]
    out_specs = [pl.BlockSpec((1, TQ, W_GROUP), lambda b, qi: (b, qi, 0))]
    out_shape = [jax.ShapeDtypeStruct((B, L, W_GROUP), f32)]
    P = 0
    if cache is not None:
        ck, cv = cache
        P = ck.shape[2]
        cspec = pl.BlockSpec((1, 1, P, W_GROUP), lambda b, qi, layer=layer: (b, layer, 0, 0))
        cos, sin = _rope_tables(L)
        in_specs += [cspec, cspec, const(L, W_GROUP), const(L, W_GROUP)]
        args += [ck, cv, cos, sin]
    else:
        out_specs.append(pl.BlockSpec((1, L, W_GROUP), lambda b, qi: (b, 0, 0)))
        out_shape.append(jax.ShapeDtypeStruct((B, L, W_GROUP), f32))
    return pl.pallas_call(
        functools.partial(_diff_attn_kernel, seq_len=L, q_tile=TQ, past_len=P, out_scale=1.0 - lam_init),
        grid=(B, L // TQ),
        in_specs=in_specs, out_specs=out_specs, out_shape=out_shape,
        scratch_shapes=[pltpu.VMEM((P + L, W_GROUP), bf16), pltpu.VMEM((W_GROUP, P + L), bf16),
                        pltpu.VMEM((W_GROUP, TQ), f32)],
        compiler_params=pltpu.CompilerParams(dimension_semantics=("arbitrary", "arbitrary"),
                                             vmem_limit_bytes=VMEM_LIMIT_BYTES),
        name="diff_attention",
    )(*args)


PROJ_ROW_TILE = 512
OUT_ROW_TILE = 512
FF_TILE = 512
D_FF = 2 * D_MODEL


def _in_proj_kernel(x_ref, g_ref, sc_ref, sh_ref, w_ref, o_ref):
    x = x_ref[...]
    y = x * lax.rsqrt(jnp.mean(x * x, axis=1, keepdims=True) + EPS) * g_ref[...]
    h = (y * (1.0 + sc_ref[0]) + sh_ref[0]).astype(bf16)
    o_ref[...] = jnp.dot(h, w_ref[0], preferred_element_type=f32)


def in_projection(x, gain, scale, shift, w_p, layer, rows_per_mod):
    n, D = x.shape
    TM = min(n, PROJ_ROW_TILE)
    mod = pl.BlockSpec((1, 1, D), lambda i: (i * TM // rows_per_mod, 0, 0))
    return pl.pallas_call(
        _in_proj_kernel,
        grid=(n // TM,),
        in_specs=[pl.BlockSpec((TM, D), lambda i: (i, 0)), pl.BlockSpec((1, D), lambda i: (0, 0)), mod, mod,
                  pl.BlockSpec((1, D, N_PROJ), lambda i: (layer, 0, 0))],
        out_specs=pl.BlockSpec((TM, N_PROJ), lambda i: (i, 0)),
        out_shape=jax.ShapeDtypeStruct((n, N_PROJ), f32),
        compiler_params=pltpu.CompilerParams(dimension_semantics=("arbitrary",),
                                             vmem_limit_bytes=VMEM_LIMIT_BYTES),
        name="in_projection",
    )(x, gain.reshape(1, D), scale, shift, w_p)


def _out_proj_kernel(x_ref, m0_ref, m1_ref, m2_ref, m3_ref, w_ref, g1_ref, ng_ref, sc_ref, sh_ref, rw_ref,
                     xo_ref, h_ref, aff_ref, wb_ref):
    @pl.when(pl.program_id(0) == 0)
    def _():
        wb_ref[...] = w_ref[0].astype(bf16)

    out = None
    for j, m_ref in enumerate((m0_ref, m1_ref, m2_ref, m3_ref)):
        part = jnp.dot(m_ref[...].astype(bf16), wb_ref[j * W_GROUP:(j + 1) * W_GROUP, :], preferred_element_type=f32)
        out = part if out is None else out + part
    x = x_ref[...] + g1_ref[0] * out
    xo_ref[...] = x
    h = x * lax.rsqrt(jnp.mean(x * x, axis=1, keepdims=True) + EPS) * ng_ref[...]
    h = h * (1.0 + sc_ref[0]) + sh_ref[0]
    h_ref[...] = h
    h_hi = h.astype(bf16)
    h_lo = (h - h_hi.astype(f32)).astype(bf16)
    both = jnp.dot(h_hi, rw_ref[0], preferred_element_type=f32)
    logits = (both[:, 0:N_EXPERTS] + both[:, N_EXPERTS:2 * N_EXPERTS]
              + jnp.dot(h_lo, rw_ref[0], preferred_element_type=f32)[:, 0:N_EXPERTS])
    e = jnp.exp(logits - jnp.max(logits, axis=1, keepdims=True))
    aff_ref[...] = e / jnp.sum(e, axis=1, keepdims=True)


def out_projection(x, mixed, w_out, gate1, gain2, scale2, shift2, router_w2, layer, rows_per_mod):
    n, D = x.shape
    TM = OUT_ROW_TILE
    row = lambda width: pl.BlockSpec((TM, width), lambda i: (i, 0))
    const = lambda *shape: pl.BlockSpec(shape, lambda i: (0,) * len(shape))
    mod = pl.BlockSpec((1, 1, D), lambda i: (i * TM // rows_per_mod, 0, 0))
    return pl.pallas_call(
        _out_proj_kernel,
        grid=(n // TM,),
        in_specs=[row(D), row(W_GROUP), row(W_GROUP), row(W_GROUP), row(W_GROUP),
                  pl.BlockSpec((1, D, D), lambda i: (layer, 0, 0)), mod, const(1, D), mod, mod,
                  pl.BlockSpec((1, D, 2 * N_EXPERTS), lambda i: (layer, 0, 0))],
        out_specs=[row(D), row(D), row(N_EXPERTS)],
        out_shape=[jax.ShapeDtypeStruct((n, D), f32), jax.ShapeDtypeStruct((n, D), f32),
                   jax.ShapeDtypeStruct((n, N_EXPERTS), f32)],
        scratch_shapes=[pltpu.VMEM((D, D), bf16)],
        compiler_params=pltpu.CompilerParams(dimension_semantics=("arbitrary",), vmem_limit_bytes=VMEM_LIMIT_BYTES),
        name="out_projection",
    )(x, *mixed, w_out, gate1, gain2.reshape(1, D), scale2, shift2, router_w2)


def _experts_kernel(xc_ref, xl_ref, gc_ref, gl_ref, g2_ref, wg_ref, wu_ref, wd_ref, yc_ref, yl_ref, ac_ref, al_ref):
    f = pl.program_id(1)
    wg = wg_ref[0, 0].astype(bf16)
    wu = wu_ref[0, 0].astype(bf16)
    wd = wd_ref[0, 0].astype(bf16)
    for x_ref, acc_ref in ((xc_ref, ac_ref), (xl_ref, al_ref)):
        x = x_ref[0].astype(bf16)
        hidden = jax.nn.silu(jnp.dot(x, wg, preferred_element_type=f32)) * jnp.dot(x, wu, preferred_element_type=f32)
        part = jnp.dot(hidden.astype(bf16), wd, preferred_element_type=f32)

        @pl.when(f == 0)
        def _(acc_ref=acc_ref, part=part):
            acc_ref[...] = part

        @pl.when(f > 0)
        def _(acc_ref=acc_ref, part=part):
            acc_ref[...] += part

    @pl.when(f == pl.num_programs(1) - 1)
    def _():
        yc_ref[0] = ac_ref[...] * gc_ref[0] * g2_ref[0]
        yl_ref[0] = al_ref[...] * gl_ref[0]


def expert_ffn(xe_c, xe_l, g_c, g_l, gate2_c, w_gate, w_up, w_down, layer):
    E, Cc, D = xe_c.shape
    Cl = xe_l.shape[1]
    tok = lambda C, width: pl.BlockSpec((1, C, width), lambda e, f: (e, 0, 0))
    return pl.pallas_call(
        _experts_kernel,
        grid=(E, D_FF // FF_TILE),
        in_specs=[tok(Cc, D), tok(Cl, D), tok(Cc, 1), tok(Cl, 1), pl.BlockSpec((1, 1, D), lambda e, f: (0, 0, 0)),
                  pl.BlockSpec((1, 1, D, FF_TILE), lambda e, f: (layer, e, 0, f)),
                  pl.BlockSpec((1, 1, D, FF_TILE), lambda e, f: (layer, e, 0, f)),
                  pl.BlockSpec((1, 1, FF_TILE, D), lambda e, f: (layer, e, f, 0))],
        out_specs=[tok(Cc, D), tok(Cl, D)],
        out_shape=[jax.ShapeDtypeStruct((E, Cc, D), f32), jax.ShapeDtypeStruct((E, Cl, D), f32)],
        scratch_shapes=[pltpu.VMEM((Cc, D), f32), pltpu.VMEM((Cl, D), f32)],
        compiler_params=pltpu.CompilerParams(dimension_semantics=("arbitrary", "arbitrary"),
                                             vmem_limit_bytes=VMEM_LIMIT_BYTES),
        name="expert_ffn",
    )(xe_c, xe_l, g_c, g_l, gate2_c, w_gate, w_up, w_down)


def _permute_w_in(w):
    gate0 = 4 * W_GROUP
    pad = jnp.zeros(w.shape[:-1] + (GATE_LANES - 4 * H_M,), w.dtype)
    return jnp.concatenate([w[..., :gate0], w[..., gate0 + 4 * H_M:], w[..., gate0:gate0 + 4 * H_M], pad],
                           axis=-1).astype(bf16)


def token_mixers(proj, lp, s5p, lam_init, states, cache, layer):
    B, L, _ = proj.shape
    mC0, mn0, mm0, s5r0, s5i0, R0 = states
    gates_t = jnp.swapaxes(proj[:, :, 12 * W_GROUP:12 * W_GROUP + 4 * H_M], 1, 2)
    hm, mC, mn, mm = mlstm_mixer(proj, gates_t, mC0, mn0, mm0, lp)
    su = proj[:, :, PROJ_BLOCKS['su'] * W_GROUP:(PROJ_BLOCKS['su'] + 1) * W_GROUP]
    ys, s5r, s5i = s5_mixer(su, s5r0, s5i0, s5p, lp['s5_d'], lp['s5_glu_b'], layer)
    hr, R = retention_mixer(proj, R0, lp)
    attn = diff_attn_mixer(proj, cache, lp, lam_init, layer)
    mixed = [a.reshape(B * L, W_GROUP) for a in (hm, ys, hr, attn[0])]
    new_ctx = None
    if cache is None:
        v = proj[:, :, PROJ_BLOCKS['dv'] * W_GROUP:(PROJ_BLOCKS['dv'] + 1) * W_GROUP]
        new_ctx = (mC, mn, mm, s5r, s5i, R, attn[1].reshape(B, L, 2 * H_D, DH_D), v.reshape(B, L, H_D, 2 * DH_D))
    return mixed, new_ctx


GATHER_WINDOW = 32
INDEX_LANES = 128


def gather_rows(x, idx):
    num = idx.shape[0]
    width = x.shape[1]
    mesh = plsc.VectorSubcoreMesh(core_axis_name="core", subcore_axis_name="subcore")
    per_core = num // GATHER_WINDOW // mesh.num_cores
    idx_rows = jnp.pad(idx.reshape(num // GATHER_WINDOW, GATHER_WINDOW), ((0, 0), (0, INDEX_LANES - GATHER_WINDOW)))

    @pl.kernel(out_type=jax.ShapeDtypeStruct((num, width), x.dtype), mesh=mesh)
    def gather_kernel(x_hbm, i_hbm, o_hbm):
        base = lax.axis_index("core") * per_core

        def body(i_vmem, o_vmem):
            pltpu.sync_copy(x_hbm.at[i_vmem.at[0, pl.ds(0, GATHER_WINDOW)]], o_vmem)

        pltpu.emit_pipeline(
            body,
            grid=(per_core,),
            in_specs=[pl.BlockSpec((1, INDEX_LANES), index_map=lambda i: (base + i, 0))],
            out_specs=[pl.BlockSpec((GATHER_WINDOW, width), index_map=lambda i: (base + i, 0))],
            core_axis_name="subcore",
            dimension_semantics=(pltpu.PARALLEL,),
        )(i_hbm, o_hbm)

    return gather_kernel(x, idx_rows)


def _route(aff, h2):
    n = aff.shape[0]
    gates, idx = lax.top_k(aff.T, CAPACITY_FACTOR * n // N_EXPERTS)
    return gates[..., None], idx, gather_rows(h2, idx.reshape(-1)).reshape(idx.shape + h2.shape[1:])


SCATTER_COLS = 128
SCATTER_WINDOW = 128


def scatter_add_rows(x, ye, idx):
    n, D = x.shape
    R = ye.shape[0]
    mesh = plsc.VectorSubcoreMesh(core_axis_name="core", subcore_axis_name="subcore")
    rows = n // mesh.num_subcores
    wins = R // SCATTER_WINDOW // mesh.num_subcores

    @pl.kernel(out_type=jax.ShapeDtypeStruct((n, D), f32), mesh=mesh,
               scratch_types=[pltpu.VMEM_SHARED((n, SCATTER_COLS), f32), pltpu.VMEM((SCATTER_WINDOW, SCATTER_COLS), f32),
                              pltpu.VMEM((SCATTER_WINDOW,), jnp.int32)])
    def scatter_kernel(x_hbm, ye_hbm, i_hbm, o_hbm, shared, buf, ibuf):
        core = lax.axis_index("core")
        sid = lax.axis_index("subcore")
        r0 = pl.multiple_of(sid * rows, SUBLANES)
        for slab in range(D // SCATTER_COLS):
            cols = pl.ds(slab * SCATTER_COLS, SCATTER_COLS)

            @pl.when(core == slab % mesh.num_cores)
            def _(cols=cols):
                pltpu.sync_copy(x_hbm.at[pl.ds(r0, rows), cols], shared.at[pl.ds(r0, rows)])
                plsc.subcore_barrier()

                @pl.loop(0, wins)
                def _(w):
                    win = w * mesh.num_subcores + sid
                    pltpu.sync_copy(i_hbm.at[win], ibuf)
                    pltpu.sync_copy(ye_hbm.at[pl.ds(pl.multiple_of(win * SCATTER_WINDOW, SCATTER_WINDOW),
                                                    SCATTER_WINDOW), cols], buf)
                    pltpu.sync_copy(buf, shared.at[ibuf], add=True)

                plsc.subcore_barrier()
                pltpu.sync_copy(shared.at[pl.ds(r0, rows)], o_hbm.at[pl.ds(r0, rows), cols])
                plsc.subcore_barrier()

    return scatter_kernel(x, ye, idx.reshape(R // SCATTER_WINDOW, SCATTER_WINDOW))


def _combine(x, ye, idx, gate2, B):
    n, D = x.shape
    if gate2 is None:
        return scatter_add_rows(x, ye.reshape(-1, D), idx.reshape(-1))
    y = jnp.zeros_like(x).at[idx.reshape(-1)].add(ye.reshape(-1, D))
    return (x.reshape(B, n // B, D) + gate2 * y.reshape(B, n // B, D)).reshape(n, D)


PER_LAYER = ('norm1_g', 'norm2_g', 'mlstm_gate_b', 'mlstm_norm_g', 's5_d', 's5_glu_b', 'ret_decay',
             'ret_gn_g', 'diff_qk_norm', 'diff_lambda', 'diff_subln_g')


def kernel(x_prompt, x_sample, state_mlstm_c, state_mlstm_n, state_mlstm_m, state_s5_re, state_s5_im, state_ret, cache_diff_k, cache_diff_v, c, c_ctx, norm1_g, norm2_g, ada_w, ada_b, w_in, w_out, mlstm_gate_b, mlstm_norm_g, s5_lambda_re, s5_lambda_im, s5_log_step, s5_b_re, s5_b_im, s5_c_re, s5_c_im, s5_d, s5_glu_w, s5_glu_b, ret_decay, ret_gn_g, diff_qk_norm, diff_lambda, diff_subln_g, router_w, exp_w_gate, exp_w_up, exp_w_down):
    weights = dict(norm1_g=norm1_g, norm2_g=norm2_g, mlstm_gate_b=mlstm_gate_b,
                   mlstm_norm_g=mlstm_norm_g, s5_d=s5_d, s5_glu_b=s5_glu_b, ret_decay=ret_decay, ret_gn_g=ret_gn_g,
                   diff_qk_norm=diff_qk_norm, diff_lambda=diff_lambda, diff_subln_g=diff_subln_g)
    w_in_p = _permute_w_in(w_in)
    rw_hi = router_w.astype(bf16)
    router_w2 = jnp.concatenate([rw_hi, (router_w - rw_hi.astype(f32)).astype(bf16)], axis=-1)
    s5p = _s5_prepare(s5_lambda_re, s5_lambda_im, s5_log_step, s5_b_re, s5_b_im, s5_c_re, s5_c_im, s5_glu_w,
                      (x_prompt.shape[1] // SUBLANES, x_sample.shape[1] // SUBLANES))
    Bc, Lc, D = x_prompt.shape
    Bl, Ll, _ = x_sample.shape
    xc = x_prompt.reshape(Bc * Lc, D)
    xl = x_sample.reshape(Bl * Ll, D)
    zero_states = (jnp.zeros((Bc, 2, H_M, DH_M, DH_M), f32), jnp.zeros((Bc, 2, H_M, DH_M), f32),
                   jnp.zeros((Bc, 2, H_M), f32), jnp.zeros((Bc, 2, G_S5, P_S5), f32),
                   jnp.zeros((Bc, 2, G_S5, P_S5), f32), jnp.zeros((Bc, 2, H_R, DH_R, DH_R), f32))
    cache = (cache_diff_k.reshape(cache_diff_k.shape[:3] + (W_GROUP,)),
             cache_diff_v.reshape(cache_diff_v.shape[:3] + (W_GROUP,)))
    cvec = jnp.concatenate([c_ctx[None, :], c], axis=0)
    outs = [[] for _ in range(8)]
    for l in range(DEPTH):
        lp = {name: weights[name][l] for name in PER_LAYER}
        lam_init = 0.8 - 0.6 * math.exp(-0.3 * l)
        mods = jnp.split((jax.nn.silu(cvec) @ ada_w[l] + ada_b[l])[:, None, :], 6, axis=-1)
        lat_states = (state_mlstm_c[:, l], state_mlstm_n[:, l], state_mlstm_m[:, l], state_s5_re[:, l],
                      state_s5_im[:, l], state_ret[:, l])
        routed = []
        for x, B, L, sel, states, kv in ((xc, Bc, Lc, slice(0, 1), zero_states, None),
                                        (xl, Bl, Ll, slice(1, 1 + Bl), lat_states, cache)):
            sh1, sc1, g1, sh2, sc2, g2 = (m[sel] for m in mods)
            rows_per_mod = x.shape[0] // sh1.shape[0]
            proj = in_projection(x, lp['norm1_g'], sc1, sh1, w_in_p, l, rows_per_mod).reshape(B, L, N_PROJ)
            mixed, new_ctx = token_mixers(proj, lp, s5p, lam_init, states, kv, l)
            if new_ctx is not None:
                for acc, t in zip(outs, new_ctx):
                    acc.append(t)
            x1, h2, aff = out_projection(x, mixed, w_out, g1, lp['norm2_g'], sc2, sh2, router_w2, l, rows_per_mod)
            routed.append((x1, g2, sh1.shape[0]) + _route(aff, h2))
        (x1c, g2c, nbc, gc, idxc, xec), (x1l, g2l, nbl, gl, idxl, xel) = routed
        yec, yel = expert_ffn(xec, xel, gc, gl, g2c, exp_w_gate, exp_w_up, exp_w_down, l)
        xc = _combine(x1c, yec, idxc, None, nbc)
        xl = _combine(x1l, yel, idxl, g2l, nbl)
    return (xc.reshape(Bc, Lc, D), xl.reshape(Bl, Ll, D)) + tuple(jnp.stack(o, axis=1) for o in outs)
```

```python
import functools
import math

import jax
import jax.numpy as jnp
import numpy as np
from jax import lax
from jax.experimental import pallas as pl
from jax.experimental.pallas import tpu as pltpu
from jax.experimental.pallas import tpu_sc as plsc

D_MODEL = 1024
DEPTH = 4
GRID_W = 64
W_GROUP = 256
H_M = 4
DH_M = 64
S5_CH = 16
G_S5 = 16
P_S5 = 64
S5_STATE = G_S5 * P_S5
H_R = 4
DH_R = 64
H_D = 4
DH_D = 32
N_EXPERTS = 16
CAPACITY_FACTOR = 2
ROPE_BASE = 10000.0
EPS = 1e-6
SUBLANES = 8
VMEM_LIMIT_BYTES = 56 * 1024 * 1024

f32 = jnp.float32
bf16 = jnp.bfloat16
HIGHEST = lax.Precision.HIGHEST
NEG_INF = float("-inf")


def _gelu_tanh(x):
    return 0.5 * x * (1.0 + jnp.tanh(math.sqrt(2.0 / math.pi) * (x + 0.044715 * (x * x * x))))


def _s5_kernel(su_ref, x0r_ref, x0i_ref, wb_ref, wc_ref, lb_ref, pw_ref, d_ref, gw_ref, gb_ref,
               y_ref, xr_ref, xi_ref, st_ref, *, chained):
    n_steps = st_ref.shape[0] // SUBLANES
    su = su_ref[0]
    y_ref[0] = su * d_ref[...]
    row = lax.broadcasted_iota(jnp.int32, (SUBLANES, S5_STATE), 0)
    zeros = jnp.zeros((SUBLANES, S5_STATE), f32)
    for d in range(2):
        st_ref[...] = jnp.dot(su.astype(bf16), wb_ref[0, d], preferred_element_type=f32)
        lbr = jnp.broadcast_to(lb_ref[0, d, 0:1, :], (SUBLANES, S5_STATE))
        lbi = jnp.broadcast_to(lb_ref[0, d, 1:2, :], (SUBLANES, S5_STATE))

        def rows_of(k, d=d):
            kk = k if d == 0 else n_steps - 1 - k
            return pl.ds(pl.multiple_of(kk * SUBLANES, SUBLANES), SUBLANES)

        def scan_step(k, carry, lbr=lbr, lbi=lbi, rows_of=rows_of):
            xr, xi = carry
            r = rows_of(k)
            nxr = lbr * xr - lbi * xi + st_ref[r, 0:S5_STATE]
            nxi = lbr * xi + lbi * xr + st_ref[r, S5_STATE:2 * S5_STATE]
            st_ref[r, 0:S5_STATE] = nxr
            st_ref[r, S5_STATE:2 * S5_STATE] = nxi
            return nxr, nxi

        if not chained:
            xr_ref[0, d], xi_ref[0, d] = lax.fori_loop(0, n_steps, scan_step, (x0r_ref[0, d], x0i_ref[0, d]))
            y_ref[0] += jnp.dot(st_ref[...].astype(bf16), wc_ref[0, d], preferred_element_type=f32)
            continue
        fr, fi = lax.fori_loop(0, n_steps, scan_step, (zeros, zeros))

        cr = x0r_ref[0, d]
        ci = x0i_ref[0, d]
        plr = pw_ref[0, d, 0:1, :]
        pli = pw_ref[0, d, 1:2, :]
        cmr, cmi = zeros, zeros
        for i in (range(SUBLANES) if d == 0 else reversed(range(SUBLANES))):
            cmr = jnp.where(row == i, cr, cmr)
            cmi = jnp.where(row == i, ci, cmi)
            cr, ci = (plr * cr - pli * ci + fr[i:i + 1], plr * ci + pli * cr + fi[i:i + 1])
        xr_ref[0, d] = cr
        xi_ref[0, d] = ci

        def fix_step(k, carry, lbr=lbr, lbi=lbi, cmr=cmr, cmi=cmi, rows_of=rows_of):
            pr, pi = carry
            r = rows_of(k)
            st_ref[r, 0:S5_STATE] = st_ref[r, 0:S5_STATE] + (pr * cmr - pi * cmi)
            st_ref[r, S5_STATE:2 * S5_STATE] = st_ref[r, S5_STATE:2 * S5_STATE] + (pr * cmi + pi * cmr)
            return pr * lbr - pi * lbi, pr * lbi + pi * lbr

        lax.fori_loop(0, n_steps, fix_step, (lbr, lbi))
        y_ref[0] += jnp.dot(st_ref[...].astype(bf16), wc_ref[0, d], preferred_element_type=f32)

    ys = _gelu_tanh(y_ref[0])
    gate = jax.nn.sigmoid(jnp.dot(ys.astype(bf16), gw_ref[0], preferred_element_type=f32) + gb_ref[...])
    y_ref[0] = ys * gate


def _s5_prepare(lam_re, lam_im, log_step, b_re, b_im, c_re, c_im, glu_w, n_steps_list):
    dt = jnp.exp(log_step)[..., None]
    mag = jnp.exp(lam_re * dt)
    ang = lam_im * dt
    lb_re, lb_im = mag * jnp.cos(ang), mag * jnp.sin(ang)
    nr, ni = lb_re - 1.0, lb_im
    den = lam_re * lam_re + lam_im * lam_im
    f_re = (nr * lam_re + ni * lam_im) / den
    f_im = (ni * lam_re - nr * lam_im) / den
    bb_re = f_re[..., None] * b_re[:, None] - f_im[..., None] * b_im[:, None]
    bb_im = f_re[..., None] * b_im[:, None] + f_im[..., None] * b_re[:, None]
    eye = jnp.eye(G_S5, dtype=f32)[:, None, :, None]

    def block_diag(a):
        return (a[:, :, :, :, None, :] * eye).reshape(a.shape[:2] + (G_S5 * a.shape[3], G_S5 * a.shape[4]))

    wb = jnp.concatenate([block_diag(jnp.swapaxes(bb_re, 3, 4)), block_diag(jnp.swapaxes(bb_im, 3, 4))],
                         axis=-1).astype(bf16)
    wc = jnp.concatenate([block_diag(jnp.swapaxes(c_re, 3, 4)), -block_diag(jnp.swapaxes(c_im, 3, 4))],
                         axis=2).astype(bf16)
    lead = lb_re.shape[:2]
    lb = jnp.stack([lb_re.reshape(lead + (S5_STATE,)), lb_im.reshape(lead + (S5_STATE,))], axis=2)
    pr, pi = lb[:, :, 0], lb[:, :, 1]
    tables = {}
    for j in range(int(math.log2(max(n_steps_list))) + 1):
        if 2 ** j in n_steps_list:
            tables[2 ** j] = jnp.stack([pr, pi], axis=2)
        pr, pi = pr * pr - pi * pi, 2.0 * pr * pi
    return wb, wc, lb, tables, glu_w.astype(bf16)


def s5_mixer(su, x0r, x0i, s5p, s5_d, glu_b, layer):
    B, L, _ = su.shape
    wb, wc, lb, tables, glu_w = s5p
    chained = B % SUBLANES != 0
    if chained:
        n_groups, lanes, n_steps = B, 1, L // SUBLANES
        to_rows = lambda a: a.reshape(B, SUBLANES, n_steps, -1).transpose(0, 2, 1, 3).reshape(B, L, -1)
        from_rows = lambda a: a.reshape(B, n_steps, SUBLANES, -1).transpose(0, 2, 1, 3).reshape(B, L, -1)
        pw = tables[n_steps]
    else:
        n_groups, lanes, n_steps = B // SUBLANES, SUBLANES, L
        to_rows = lambda a: a.reshape(n_groups, SUBLANES, L, -1).transpose(0, 2, 1, 3).reshape(n_groups, -1, a.shape[-1])
        from_rows = lambda a: a.reshape(n_groups, L, SUBLANES, -1).transpose(0, 2, 1, 3).reshape(B, L, -1)
        pw = lb
    rows = n_steps * SUBLANES
    state_in = lambda a: a.reshape(n_groups, lanes, 2, S5_STATE).transpose(0, 2, 1, 3)
    state_out = lambda a: a.transpose(0, 2, 1, 3).reshape(B, 2, G_S5, P_S5)
    full = lambda *shape: pl.BlockSpec(shape, lambda b: (0,) * len(shape))
    per_b = lambda *shape: pl.BlockSpec((1,) + shape, lambda b: (b,) + (0,) * len(shape))
    per_layer = lambda *shape: pl.BlockSpec((1,) + shape, lambda b: (layer,) + (0,) * len(shape))
    y_p, xr, xi = pl.pallas_call(
        functools.partial(_s5_kernel, chained=chained),
        grid=(n_groups,),
        in_specs=[per_b(rows, W_GROUP), per_b(2, lanes, S5_STATE), per_b(2, lanes, S5_STATE),
                  per_layer(2, W_GROUP, 2 * S5_STATE), per_layer(2, 2 * S5_STATE, W_GROUP),
                  per_layer(2, 2, S5_STATE), per_layer(2, 2, S5_STATE),
                  full(1, W_GROUP), per_layer(W_GROUP, W_GROUP), full(1, W_GROUP)],
        out_specs=[per_b(rows, W_GROUP), per_b(2, lanes, S5_STATE), per_b(2, lanes, S5_STATE)],
        out_shape=[jax.ShapeDtypeStruct((n_groups, rows, W_GROUP), f32),
                   jax.ShapeDtypeStruct((n_groups, 2, lanes, S5_STATE), f32),
                   jax.ShapeDtypeStruct((n_groups, 2, lanes, S5_STATE), f32)],
        scratch_shapes=[pltpu.VMEM((rows, 2 * S5_STATE), f32)],
        compiler_params=pltpu.CompilerParams(dimension_semantics=("arbitrary",),
                                             vmem_limit_bytes=VMEM_LIMIT_BYTES),
        name="s5_mixer",
    )(to_rows(su), state_in(x0r), state_in(x0i), wb, wc, lb, pw,
      s5_d.reshape(1, W_GROUP), glu_w, glu_b.reshape(1, W_GROUP))
    return from_rows(y_p), state_out(xr), state_out(xi)


Q_TILE = 256
PROJ_BLOCKS = dict(mq=0, mk=1, mv=2, mo=3, su=4, rq=5, rk=6, rv=7, rg=8, dq=9, dk=10, dv=11)
GATE_LANES = 128
N_PROJ = 12 * W_GROUP + GATE_LANES


def _log_sigmoid(x):
    return jnp.minimum(x, 0.0) - jnp.log1p(jnp.exp(-jnp.abs(x)))


def _group_ones(width, group):
    shift = int(math.log2(group))
    r = lax.broadcasted_iota(jnp.int32, (width, width), 0) >> shift
    c = lax.broadcasted_iota(jnp.int32, (width, width), 1) >> shift
    return (r == c).astype(bf16)


def _split3(x):
    hi = x.astype(bf16)
    r = x - hi.astype(f32)
    mid = r.astype(bf16)
    return hi, mid, (r - mid.astype(f32)).astype(bf16)


def _group_mean(x, ones, group):
    return sum(jnp.dot(p, ones, preferred_element_type=f32) for p in _split3(x)) * (1.0 / group)


def _dot_nt(a, b):
    return lax.dot_general(a.astype(bf16), b.astype(bf16), (((1,), (1,)), ((), ())), preferred_element_type=f32)


def _dot(a, b):
    return jnp.dot(a.astype(bf16), b.astype(bf16), preferred_element_type=f32)


def _proj_block(name, rows):
    j = PROJ_BLOCKS[name]
    return pl.BlockSpec((1, rows, W_GROUP), lambda b, qi, j=j, rows=rows: (b, qi if rows == Q_TILE else 0, j))


def _mlstm_kernel(q_ref, k_ref, v_ref, o_ref, g_ref, gt_ref, gb_ref, gbt_ref, ng_ref, c0_ref, n0_ref, m0_ref,
                  h_ref, c_ref, n_ref, m_ref, gl_ref, gu_ref, rc_ref, vt_ref, ht_ref, *, seq_len, q_tile):
    L, TQ = seq_len, q_tile
    nq = L // TQ
    qi = pl.program_id(1)
    grow = gt_ref[0] + gbt_ref[...]

    @pl.when(qi == 0)
    def _():
        ss = lax.broadcasted_iota(jnp.int32, (L, L), 0)
        tt = lax.broadcasted_iota(jnp.int32, (L, L), 1)
        tri_le = (ss <= tt).astype(bf16)
        tri_ge = (ss >= tt).astype(bf16)
        rows = _split3(_log_sigmoid(grow))
        gl = sum(jnp.dot(p, tri_le, preferred_element_type=f32) for p in rows)
        gu = sum(jnp.dot(p, tri_ge, preferred_element_type=f32) for p in rows)
        for j in range(nq):
            gl_ref[j] = gl[:, j * TQ:(j + 1) * TQ]
            gu_ref[j] = gu[:, j * TQ:(j + 1) * TQ]
        gcol = g_ref[0] + gb_ref[...]
        cols = _split3(_log_sigmoid(gcol))
        glc = sum(jnp.dot(tri_ge, p, preferred_element_type=f32) for p in cols)
        guc = sum(jnp.dot(tri_le, p, preferred_element_type=f32) for p in cols)
        lane = lax.broadcasted_iota(jnp.int32, (L, GATE_LANES), 1)
        rc_ref[...] = pltpu.roll(gcol, 4, 1) - jnp.where(lane < 8, glc, guc)

        vt = v_ref[0].T
        kk = k_ref[0] * (DH_M ** -0.5)
        one_row = (lax.broadcasted_iota(jnp.int32, (DH_M, L), 0) == 0).astype(bf16)
        for h in range(H_M):
            hs = slice(h * DH_M, (h + 1) * DH_M)
            vt_ref[2 * h * DH_M:(2 * h + 1) * DH_M, :] = vt[hs, :].astype(bf16)
            vt_ref[(2 * h + 1) * DH_M:(2 * h + 2) * DH_M, :] = one_row
        for d in range(2):
            g_all = gl if d == 0 else gu
            for h in range(H_M):
                hs = slice(h * DH_M, (h + 1) * DH_M)
                ii, fi = 8 * d + h, 8 * d + 4 + h
                g_row = g_all[fi:fi + 1, :]
                g_tot = g_row[:, L - 1:L] if d == 0 else g_row[:, 0:1]
                wlog = g_tot - g_row + grow[ii:ii + 1, :]
                m0 = m0_ref[0, d:d + 1, h:h + 1]
                m_new = jnp.maximum(g_tot + m0, jnp.max(wlog, axis=1, keepdims=True))
                decay = jnp.exp(g_tot + m0 - m_new)
                w = jnp.exp(wlog - m_new)
                kh = kk[:, hs]
                c_ref[0, d, h] = decay * c0_ref[0, d, h] + _dot(vt[hs, :] * w, kh)
                n_upd = jnp.dot(jnp.broadcast_to(w, (SUBLANES, L)), kh, precision=HIGHEST,
                                preferred_element_type=f32)[0:1, :]
                n_ref[0, d, h:h + 1, :] = decay * n0_ref[0, d, h:h + 1, :] + n_upd
                m_ref[0, d:d + 1, h:h + 1] = m_new

    gl_t = gl_ref[qi]
    gu_t = gu_ref[qi]
    s_idx = lax.broadcasted_iota(jnp.int32, (L, TQ), 0)
    t_idx = qi * TQ + lax.broadcasted_iota(jnp.int32, (L, TQ), 1)
    low = s_idx <= t_idx
    upp = s_idx >= t_idx
    qt = q_ref[0].T.astype(bf16)
    k = (k_ref[0] * (DH_M ** -0.5)).astype(bf16)
    rc = rc_ref[...]
    row0 = lax.broadcasted_iota(jnp.int32, (DH_M, DH_M), 0) == 0
    for h in range(H_M):
        hs = slice(h * DH_M, (h + 1) * DH_M)
        qth = qt[hs, :]
        vta = vt_ref[2 * h * DH_M:(2 * h + 2) * DH_M, :]
        s0 = jnp.dot(k[:, hs], qth, preferred_element_type=f32)
        h_sum = None
        for d in range(2):
            fi = 8 * d + 4 + h
            g_t = (gl_t if d == 0 else gu_t)[fi:fi + 1, :]
            dlog = jnp.where(low if d == 0 else upp, rc[:, fi:fi + 1] + g_t, NEG_INF)
            inter = g_t + m0_ref[0, d:d + 1, h:h + 1]
            m_t = jnp.maximum(inter, jnp.max(dlog, axis=0, keepdims=True))
            p = s0 * jnp.exp(dlog - m_t)
            a = jnp.exp(inter - m_t)
            c0n0 = jnp.concatenate([c0_ref[0, d, h], jnp.where(row0, n0_ref[0, d, h:h + 1, :], 0.0)], axis=0)
            numden = (jnp.dot(vta, p.astype(bf16), preferred_element_type=f32)
                      + a * jnp.dot(c0n0.astype(bf16), qth, preferred_element_type=f32))
            scale = 1.0 / jnp.maximum(jnp.abs(numden[DH_M:DH_M + 1, :]), jnp.exp(-m_t))
            hd = numden[0:DH_M, :] * scale
            h_sum = hd if h_sum is None else h_sum + hd
        ht_ref[hs, :] = h_sum * lax.rsqrt(jnp.mean(h_sum * h_sum, axis=0, keepdims=True) + EPS)
    h_ref[0] = jax.nn.sigmoid(o_ref[0]) * (ht_ref[...].T * ng_ref[...])


def mlstm_mixer(proj, gates_t, c0, n0, m0, lp):
    B, L, _ = proj.shape
    TQ = min(L, Q_TILE)
    gb = lp['mlstm_gate_b'].reshape(1, 4 * H_M)
    const = lambda *shape: pl.BlockSpec(shape, lambda b, qi: (0,) * len(shape))
    per_b = lambda *shape: pl.BlockSpec((1,) + shape, lambda b, qi: (b,) + (0,) * len(shape))
    return pl.pallas_call(
        functools.partial(_mlstm_kernel, seq_len=L, q_tile=TQ),
        grid=(B, L // TQ),
        in_specs=[_proj_block('mq', TQ), _proj_block('mk', L), _proj_block('mv', L), _proj_block('mo', TQ),
                  pl.BlockSpec((1, L, GATE_LANES), lambda b, qi: (b, 0, 12 * W_GROUP // GATE_LANES)),
                  per_b(4 * H_M, L), const(1, GATE_LANES), const(4 * H_M, 1), const(1, W_GROUP),
                  per_b(2, H_M, DH_M, DH_M), per_b(2, H_M, DH_M), per_b(2, H_M)],
        out_specs=[pl.BlockSpec((1, TQ, W_GROUP), lambda b, qi: (b, qi, 0)),
                   per_b(2, H_M, DH_M, DH_M), per_b(2, H_M, DH_M), per_b(2, H_M)],
        out_shape=[jax.ShapeDtypeStruct((B, L, W_GROUP), f32),
                   jax.ShapeDtypeStruct((B, 2, H_M, DH_M, DH_M), f32),
                   jax.ShapeDtypeStruct((B, 2, H_M, DH_M), f32),
                   jax.ShapeDtypeStruct((B, 2, H_M), f32)],
        scratch_shapes=[pltpu.VMEM((L // TQ, 4 * H_M, TQ), f32), pltpu.VMEM((L // TQ, 4 * H_M, TQ), f32),
                        pltpu.VMEM((L, GATE_LANES), f32), pltpu.VMEM((2 * W_GROUP, L), bf16),
                        pltpu.VMEM((W_GROUP, TQ), f32)],
        compiler_params=pltpu.CompilerParams(dimension_semantics=("arbitrary", "arbitrary"),
                                             vmem_limit_bytes=VMEM_LIMIT_BYTES),
        name="mlstm_mixer",
    )(proj, proj, proj, proj, proj, gates_t, jnp.pad(gb, ((0, 0), (0, GATE_LANES - 4 * H_M))),
      gb.reshape(4 * H_M, 1), lp['mlstm_norm_g'].reshape(1, W_GROUP), c0, n0, m0)


def _retention_kernel(lg_ref, q_ref, k_ref, v_ref, g_ref, gn_ref, r0_ref, h_ref, r_ref, *, seq_len, q_tile):
    L, TQ = seq_len, q_tile
    qi = pl.program_id(1)
    t_col = qi * TQ + lax.broadcasted_iota(jnp.int32, (TQ, 1), 0)
    rel = (qi * TQ + lax.broadcasted_iota(jnp.int32, (TQ, L), 0)
           - lax.broadcasted_iota(jnp.int32, (TQ, L), 1)).astype(f32)
    q = q_ref[0]
    k = k_ref[0] * (DH_R ** -0.5)
    v = v_ref[0]
    ones = _group_ones(W_GROUP, DH_R)
    for h in range(H_R):
        hs = slice(h * DH_R, (h + 1) * DH_R)
        lgf, lgb = lg_ref[0, h], lg_ref[1, h]
        qh, kh, vh = q[:, hs], k[:, hs], v[:, hs]
        decay = jnp.where(rel > 0.0, jnp.exp(lgf * jnp.maximum(rel, 0.0)),
                          jnp.where(rel < 0.0, jnp.exp(lgb * jnp.maximum(-rel, 0.0)), 2.0))
        o = _dot(_dot_nt(qh, kh) * decay, vh)
        xi_f = jnp.exp(lgf * (t_col + 1).astype(f32))
        xi_b = jnp.exp(lgb * (L - t_col).astype(f32))
        o = o + xi_f * _dot(qh, r0_ref[0, 0, h]) + xi_b * _dot(qh, r0_ref[0, 1, h])
        h_ref[0, :, hs] = o
    o = h_ref[0]
    oc = o - _group_mean(o, ones, DH_R)
    y = oc * lax.rsqrt(_group_mean(oc * oc, ones, DH_R) + EPS) * gn_ref[...]
    h_ref[0] = y * jax.nn.silu(g_ref[0])

    @pl.when(qi == 0)
    def _():
        kt = k.T
        s_row = lax.broadcasted_iota(jnp.int32, (1, L), 1).astype(f32)
        for d in range(2):
            for h in range(H_R):
                hs = slice(h * DH_R, (h + 1) * DH_R)
                lg = lg_ref[d, h]
                zeta = jnp.exp(lg * ((L - 1.0) - s_row)) if d == 0 else jnp.exp(lg * s_row)
                r_ref[0, d, h] = jnp.exp(lg * float(L)) * r0_ref[0, d, h] + _dot(kt[hs, :] * zeta, v[:, hs])


def retention_mixer(proj, r0, lp):
    B, L, _ = proj.shape
    TQ = min(L, Q_TILE)
    log_gamma = -jnp.exp(lp['ret_decay'])
    per_b = lambda *shape: pl.BlockSpec((1,) + shape, lambda b, qi: (b,) + (0,) * len(shape))
    return pl.pallas_call(
        functools.partial(_retention_kernel, seq_len=L, q_tile=TQ),
        grid=(B, L // TQ),
        in_specs=[pl.BlockSpec(memory_space=pltpu.SMEM),
                  _proj_block('rq', TQ), _proj_block('rk', L), _proj_block('rv', L), _proj_block('rg', TQ),
                  pl.BlockSpec((1, W_GROUP), lambda b, qi: (0, 0)), per_b(2, H_R, DH_R, DH_R)],
        out_specs=[pl.BlockSpec((1, TQ, W_GROUP), lambda b, qi: (b, qi, 0)), per_b(2, H_R, DH_R, DH_R)],
        out_shape=[jax.ShapeDtypeStruct((B, L, W_GROUP), f32),
                   jax.ShapeDtypeStruct((B, 2, H_R, DH_R, DH_R), f32)],
        compiler_params=pltpu.CompilerParams(dimension_semantics=("arbitrary", "arbitrary"),
                                             vmem_limit_bytes=VMEM_LIMIT_BYTES),
        name="retention_mixer",
    )(log_gamma, proj, proj, proj, proj, lp['ret_gn_g'].reshape(1, W_GROUP), r0)


def _rope_tables(L):
    half = DH_D // 2
    freqs = ROPE_BASE ** (-np.arange(0, half, 2, dtype=np.float64) / half)
    pos = np.arange(L)
    row, col = (pos // GRID_W).astype(np.float64), (pos % GRID_W).astype(np.float64)
    ang = np.concatenate([np.tile(row[:, None] * freqs, (1, 2)), np.tile(col[:, None] * freqs, (1, 2))], axis=1)
    sign = np.tile(np.concatenate([-np.ones(half // 2), np.ones(half // 2)]), 2)
    cos = np.tile(np.cos(ang), (1, 2 * H_D))
    sin = np.tile(np.sin(ang) * sign, (1, 2 * H_D))
    return jnp.asarray(cos, f32), jnp.asarray(sin, f32)


def _swap_pairs(x):
    parts = []
    for j in range(x.shape[1] // 128):
        xs = x[:, j * 128:(j + 1) * 128]
        lane = lax.broadcasted_iota(jnp.int32, xs.shape, 1)
        parts.append(jnp.where((lane & 15) < 8, pltpu.roll(xs, 120, 1), pltpu.roll(xs, 8, 1)))
    return jnp.concatenate(parts, axis=1)


def _qk_norm(x, gain, ones):
    return x * lax.rsqrt(_group_mean(x * x, ones, DH_D) + EPS) * gain


def _diff_attn_kernel(lam_ref, q_ref, k_ref, v_ref, qkg_ref, sg_ref, *rest, seq_len, q_tile, past_len, out_scale):
    L, TQ, P = seq_len, q_tile, past_len
    if P:
        kc_ref, vc_ref, cos_ref, sin_ref, h_ref, ka_ref, vt_ref, ot_ref = rest
    else:
        h_ref, kn_ref, ka_ref, vt_ref, ot_ref = rest
    qi = pl.program_id(1)
    ones = _group_ones(W_GROUP, DH_D)

    @pl.when(qi == 0)
    def _():
        kn = _qk_norm(k_ref[0], qkg_ref[1:2, :], ones)
        if P:
            kn = kn * cos_ref[...] + _swap_pairs(kn) * sin_ref[...]
            ka_ref[0:P, :] = kc_ref[0, 0].astype(bf16)
            vt_ref[:, 0:P] = vc_ref[0, 0].T.astype(bf16)
        else:
            kn_ref[0] = kn
        ka_ref[P:P + L, :] = kn.astype(bf16)
        vt_ref[:, P:P + L] = v_ref[0].T.astype(bf16)

    qn = _qk_norm(q_ref[0], qkg_ref[0:1, :], ones)
    if P:
        rows = pl.ds(pl.multiple_of(qi * TQ, TQ), TQ)
        qn = qn * cos_ref[rows, :] + _swap_pairs(qn) * sin_ref[rows, :]
    qt = (qn * (DH_D ** -0.5)).T.astype(bf16)
    lam = lam_ref[0, 0]
    ka = ka_ref[...]
    for h in range(H_D):
        probs = []
        for j in range(2):
            cs = slice((2 * h + j) * DH_D, (2 * h + j + 1) * DH_D)
            s = jnp.dot(ka[:, cs], qt[cs, :], preferred_element_type=f32)
            e = jnp.exp(s - jnp.max(s, axis=0, keepdims=True))
            probs.append(e * (1.0 / jnp.sum(e, axis=0, keepdims=True)))
        vs = slice(h * 2 * DH_D, (h + 1) * 2 * DH_D)
        ot_ref[vs, :] = jnp.dot(vt_ref[vs, :], (probs[0] - lam * probs[1]).astype(bf16), preferred_element_type=f32)
    o = ot_ref[...].T
    ones_v = _group_ones(W_GROUP, 2 * DH_D)
    h_ref[0] = o * lax.rsqrt(_group_mean(o * o, ones_v, 2 * DH_D) + EPS) * (sg_ref[...] * out_scale)


def diff_attn_mixer(proj, cache, lp, lam_init, layer):
    B, L, _ = proj.shape
    TQ = min(L, Q_TILE)
    lv = lp['diff_lambda']
    lam = (jnp.exp(jnp.sum(lv[0] * lv[1])) - jnp.exp(jnp.sum(lv[2] * lv[3])) + lam_init).reshape(1, 1)
    qkg = jnp.tile(lp['diff_qk_norm'], (1, 2 * H_D))
    sg = jnp.tile(lp['diff_subln_g'], (H_D,)).reshape(1, W_GROUP)
    const = lambda *shape: pl.BlockSpec(shape, lambda b, qi: (0,) * len(shape))
    in_specs = [pl.BlockSpec(memory_space=pltpu.SMEM),
                _proj_block('dq', TQ), _proj_block('dk', L), _proj_block('dv', L), const(2, W_GROUP), const(1, W_GROUP)]
    args = [lam, proj, proj, proj, qkg, sg]
    out_specs = [pl.BlockSpec((1, TQ, W_GROUP), lambda b, qi: (b, qi, 0))]
    out_shape = [jax.ShapeDtypeStruct((B, L, W_GROUP), f32)]
    P = 0
    if cache is not None:
        ck, cv = cache
        P = ck.shape[2]
        cspec = pl.BlockSpec((1, 1, P, W_GROUP), lambda b, qi, layer=layer: (b, layer, 0, 0))
        cos, sin = _rope_tables(L)
        in_specs += [cspec, cspec, const(L, W_GROUP), const(L, W_GROUP)]
        args += [ck, cv, cos, sin]
    else:
        out_specs.append(pl.BlockSpec((1, L, W_GROUP), lambda b, qi: (b, 0, 0)))
        out_shape.append(jax.ShapeDtypeStruct((B, L, W_GROUP), f32))
    return pl.pallas_call(
        functools.partial(_diff_attn_kernel, seq_len=L, q_tile=TQ, past_len=P, out_scale=1.0 - lam_init),
        grid=(B, L // TQ),
        in_specs=in_specs, out_specs=out_specs, out_shape=out_shape,
        scratch_shapes=[pltpu.VMEM((P + L, W_GROUP), bf16), pltpu.VMEM((W_GROUP, P + L), bf16),
                        pltpu.VMEM((W_GROUP, TQ), f32)],
        compiler_params=pltpu.CompilerParams(dimension_semantics=("arbitrary", "arbitrary"),
                                             vmem_limit_bytes=VMEM_LIMIT_BYTES),
        name="diff_attention",
    )(*args)


PROJ_ROW_TILE = 512
OUT_ROW_TILE = 512
FF_TILE = 512
D_FF = 2 * D_MODEL


def _in_proj_kernel(x_ref, g_ref, sc_ref, sh_ref, w_ref, o_ref):
    x = x_ref[...]
    y = x * lax.rsqrt(jnp.mean(x * x, axis=1, keepdims=True) + EPS) * g_ref[...]
    h = (y * (1.0 + sc_ref[0]) + sh_ref[0]).astype(bf16)
    o_ref[...] = jnp.dot(h, w_ref[0], preferred_element_type=f32)


def in_projection(x, gain, scale, shift, w_p, layer, rows_per_mod):
    n, D = x.shape
    TM = min(n, PROJ_ROW_TILE)
    mod = pl.BlockSpec((1, 1, D), lambda i: (i * TM // rows_per_mod, 0, 0))
    return pl.pallas_call(
        _in_proj_kernel,
        grid=(n // TM,),
        in_specs=[pl.BlockSpec((TM, D), lambda i: (i, 0)), pl.BlockSpec((1, D), lambda i: (0, 0)), mod, mod,
                  pl.BlockSpec((1, D, N_PROJ), lambda i: (layer, 0, 0))],
        out_specs=pl.BlockSpec((TM, N_PROJ), lambda i: (i, 0)),
        out_shape=jax.ShapeDtypeStruct((n, N_PROJ), f32),
        compiler_params=pltpu.CompilerParams(dimension_semantics=("arbitrary",),
                                             vmem_limit_bytes=VMEM_LIMIT_BYTES),
        name="in_projection",
    )(x, gain.reshape(1, D), scale, shift, w_p)


def _out_proj_kernel(x_ref, m0_ref, m1_ref, m2_ref, m3_ref, w_ref, g1_ref, ng_ref, sc_ref, sh_ref, rw_ref,
                     xo_ref, h_ref, aff_ref, wb_ref):
    @pl.when(pl.program_id(0) == 0)
    def _():
        wb_ref[...] = w_ref[0].astype(bf16)

    out = None
    for j, m_ref in enumerate((m0_ref, m1_ref, m2_ref, m3_ref)):
        part = jnp.dot(m_ref[...].astype(bf16), wb_ref[j * W_GROUP:(j + 1) * W_GROUP, :], preferred_element_type=f32)
        out = part if out is None else out + part
    x = x_ref[...] + g1_ref[0] * out
    xo_ref[...] = x
    h = x * lax.rsqrt(jnp.mean(x * x, axis=1, keepdims=True) + EPS) * ng_ref[...]
    h = h * (1.0 + sc_ref[0]) + sh_ref[0]
    h_ref[...] = h
    h_hi = h.astype(bf16)
    h_lo = (h - h_hi.astype(f32)).astype(bf16)
    both = jnp.dot(h_hi, rw_ref[0], preferred_element_type=f32)
    logits = (both[:, 0:N_EXPERTS] + both[:, N_EXPERTS:2 * N_EXPERTS]
              + jnp.dot(h_lo, rw_ref[0], preferred_element_type=f32)[:, 0:N_EXPERTS])
    e = jnp.exp(logits - jnp.max(logits, axis=1, keepdims=True))
    aff_ref[...] = e / jnp.sum(e, axis=1, keepdims=True)


def out_projection(x, mixed, w_out, gate1, gain2, scale2, shift2, router_w2, layer, rows_per_mod):
    n, D = x.shape
    TM = OUT_ROW_TILE
    row = lambda width: pl.BlockSpec((TM, width), lambda i: (i, 0))
    const = lambda *shape: pl.BlockSpec(shape, lambda i: (0,) * len(shape))
    mod = pl.BlockSpec((1, 1, D), lambda i: (i * TM // rows_per_mod, 0, 0))
    return pl.pallas_call(
        _out_proj_kernel,
        grid=(n // TM,),
        in_specs=[row(D), row(W_GROUP), row(W_GROUP), row(W_GROUP), row(W_GROUP),
                  pl.BlockSpec((1, D, D), lambda i: (layer, 0, 0)), mod, const(1, D), mod, mod,
                  pl.BlockSpec((1, D, 2 * N_EXPERTS), lambda i: (layer, 0, 0))],
        out_specs=[row(D), row(D), row(N_EXPERTS)],
        out_shape=[jax.ShapeDtypeStruct((n, D), f32), jax.ShapeDtypeStruct((n, D), f32),
                   jax.ShapeDtypeStruct((n, N_EXPERTS), f32)],
        scratch_shapes=[pltpu.VMEM((D, D), bf16)],
        compiler_params=pltpu.CompilerParams(dimension_semantics=("arbitrary",), vmem_limit_bytes=VMEM_LIMIT_BYTES),
        name="out_projection",
    )(x, *mixed, w_out, gate1, gain2.reshape(1, D), scale2, shift2, router_w2)


def _experts_kernel(xc_ref, xl_ref, gc_ref, gl_ref, g2_ref, wg_ref, wu_ref, wd_ref, yc_ref, yl_ref, ac_ref, al_ref):
    f = pl.program_id(1)
    wg = wg_ref[0, 0].astype(bf16)
    wu = wu_ref[0, 0].astype(bf16)
    wd = wd_ref[0, 0].astype(bf16)
    for x_ref, acc_ref in ((xc_ref, ac_ref), (xl_ref, al_ref)):
        x = x_ref[0].astype(bf16)
        hidden = jax.nn.silu(jnp.dot(x, wg, preferred_element_type=f32)) * jnp.dot(x, wu, preferred_element_type=f32)
        part = jnp.dot(hidden.astype(bf16), wd, preferred_element_type=f32)

        @pl.when(f == 0)
        def _(acc_ref=acc_ref, part=part):
            acc_ref[...] = part

        @pl.when(f > 0)
        def _(acc_ref=acc_ref, part=part):
            acc_ref[...] += part

    @pl.when(f == pl.num_programs(1) - 1)
    def _():
        yc_ref[0] = ac_ref[...] * gc_ref[0] * g2_ref[0]
        yl_ref[0] = al_ref[...] * gl_ref[0]


def expert_ffn(xe_c, xe_l, g_c, g_l, gate2_c, w_gate, w_up, w_down, layer):
    E, Cc, D = xe_c.shape
    Cl = xe_l.shape[1]
    tok = lambda C, width: pl.BlockSpec((1, C, width), lambda e, f: (e, 0, 0))
    return pl.pallas_call(
        _experts_kernel,
        grid=(E, D_FF // FF_TILE),
        in_specs=[tok(Cc, D), tok(Cl, D), tok(Cc, 1), tok(Cl, 1), pl.BlockSpec((1, 1, D), lambda e, f: (0, 0, 0)),
                  pl.BlockSpec((1, 1, D, FF_TILE), lambda e, f: (layer, e, 0, f)),
                  pl.BlockSpec((1, 1, D, FF_TILE), lambda e, f: (layer, e, 0, f)),
                  pl.BlockSpec((1, 1, FF_TILE, D), lambda e, f: (layer, e, f, 0))],
        out_specs=[tok(Cc, D), tok(Cl, D)],
        out_shape=[jax.ShapeDtypeStruct((E, Cc, D), f32), jax.ShapeDtypeStruct((E, Cl, D), f32)],
        scratch_shapes=[pltpu.VMEM((Cc, D), f32), pltpu.VMEM((Cl, D), f32)],
        compiler_params=pltpu.CompilerParams(dimension_semantics=("arbitrary", "arbitrary"),
                                             vmem_limit_bytes=VMEM_LIMIT_BYTES),
        name="expert_ffn",
    )(xe_c, xe_l, g_c, g_l, gate2_c, w_gate, w_up, w_down)


def _permute_w_in(w):
    gate0 = 4 * W_GROUP
    pad = jnp.zeros(w.shape[:-1] + (GATE_LANES - 4 * H_M,), w.dtype)
    return jnp.concatenate([w[..., :gate0], w[..., gate0 + 4 * H_M:], w[..., gate0:gate0 + 4 * H_M], pad],
                           axis=-1).astype(bf16)


def token_mixers(proj, lp, s5p, lam_init, states, cache, layer):
    B, L, _ = proj.shape
    mC0, mn0, mm0, s5r0, s5i0, R0 = states
    gates_t = jnp.swapaxes(proj[:, :, 12 * W_GROUP:12 * W_GROUP + 4 * H_M], 1, 2)
    hm, mC, mn, mm = mlstm_mixer(proj, gates_t, mC0, mn0, mm0, lp)
    su = proj[:, :, PROJ_BLOCKS['su'] * W_GROUP:(PROJ_BLOCKS['su'] + 1) * W_GROUP]
    ys, s5r, s5i = s5_mixer(su, s5r0, s5i0, s5p, lp['s5_d'], lp['s5_glu_b'], layer)
    hr, R = retention_mixer(proj, R0, lp)
    attn = diff_attn_mixer(proj, cache, lp, lam_init, layer)
    mixed = [a.reshape(B * L, W_GROUP) for a in (hm, ys, hr, attn[0])]
    new_ctx = None
    if cache is None:
        v = proj[:, :, PROJ_BLOCKS['dv'] * W_GROUP:(PROJ_BLOCKS['dv'] + 1) * W_GROUP]
        new_ctx = (mC, mn, mm, s5r, s5i, R, attn[1].reshape(B, L, 2 * H_D, DH_D), v.reshape(B, L, H_D, 2 * DH_D))
    return mixed, new_ctx


GATHER_WINDOW = 32
INDEX_LANES = 128


def gather_rows(x, idx):
    num = idx.shape[0]
    width = x.shape[1]
    mesh = plsc.VectorSubcoreMesh(core_axis_name="core", subcore_axis_name="subcore")
    per_core = num // GATHER_WINDOW // mesh.num_cores
    idx_rows = jnp.pad(idx.reshape(num // GATHER_WINDOW, GATHER_WINDOW), ((0, 0), (0, INDEX_LANES - GATHER_WINDOW)))

    @pl.kernel(out_type=jax.ShapeDtypeStruct((num, width), x.dtype), mesh=mesh)
    def gather_kernel(x_hbm, i_hbm, o_hbm):
        base = lax.axis_index("core") * per_core

        def body(i_vmem, o_vmem):
            pltpu.sync_copy(x_hbm.at[i_vmem.at[0, pl.ds(0, GATHER_WINDOW)]], o_vmem)

        pltpu.emit_pipeline(
            body,
            grid=(per_core,),
            in_specs=[pl.BlockSpec((1, INDEX_LANES), index_map=lambda i: (base + i, 0))],
            out_specs=[pl.BlockSpec((GATHER_WINDOW, width), index_map=lambda i: (base + i, 0))],
            core_axis_name="subcore",
            dimension_semantics=(pltpu.PARALLEL,),
        )(i_hbm, o_hbm)

    return gather_kernel(x, idx_rows)


def _route(aff, h2):
    n = aff.shape[0]
    gates, idx = lax.top_k(aff.T, CAPACITY_FACTOR * n // N_EXPERTS)
    return gates[..., None], idx, gather_rows(h2, idx.reshape(-1)).reshape(idx.shape + h2.shape[1:])


SCATTER_COLS = 128
SCATTER_WINDOW = 128


def scatter_add_rows(x, ye, idx):
    n, D = x.shape
    R = ye.shape[0]
    mesh = plsc.VectorSubcoreMesh(core_axis_name="core", subcore_axis_name="subcore")
    rows = n // mesh.num_subcores
    wins = R // SCATTER_WINDOW // mesh.num_subcores

    @pl.kernel(out_type=jax.ShapeDtypeStruct((n, D), f32), mesh=mesh,
               scratch_types=[pltpu.VMEM_SHARED((n, SCATTER_COLS), f32), pltpu.VMEM((SCATTER_WINDOW, SCATTER_COLS), f32),
                              pltpu.VMEM((SCATTER_WINDOW,), jnp.int32)])
    def scatter_kernel(x_hbm, ye_hbm, i_hbm, o_hbm, shared, buf, ibuf):
        core = lax.axis_index("core")
        sid = lax.axis_index("subcore")
        r0 = pl.multiple_of(sid * rows, SUBLANES)
        for slab in range(D // SCATTER_COLS):
            cols = pl.ds(slab * SCATTER_COLS, SCATTER_COLS)

            @pl.when(core == slab % mesh.num_cores)
            def _(cols=cols):
                pltpu.sync_copy(x_hbm.at[pl.ds(r0, rows), cols], shared.at[pl.ds(r0, rows)])
                plsc.subcore_barrier()

                @pl.loop(0, wins)
                def _(w):
                    win = w * mesh.num_subcores + sid
                    pltpu.sync_copy(i_hbm.at[win], ibuf)
                    pltpu.sync_copy(ye_hbm.at[pl.ds(pl.multiple_of(win * SCATTER_WINDOW, SCATTER_WINDOW),
                                                    SCATTER_WINDOW), cols], buf)
                    pltpu.sync_copy(buf, shared.at[ibuf], add=True)

                plsc.subcore_barrier()
                pltpu.sync_copy(shared.at[pl.ds(r0, rows)], o_hbm.at[pl.ds(r0, rows), cols])
                plsc.subcore_barrier()

    return scatter_kernel(x, ye, idx.reshape(R // SCATTER_WINDOW, SCATTER_WINDOW))


def _combine(x, ye, idx, gate2, B):
    n, D = x.shape
    if gate2 is None:
        return scatter_add_rows(x, ye.reshape(-1, D), idx.reshape(-1))
    y = jnp.zeros_like(x).at[idx.reshape(-1)].add(ye.reshape(-1, D))
    return (x.reshape(B, n // B, D) + gate2 * y.reshape(B, n // B, D)).reshape(n, D)


PER_LAYER = ('norm1_g', 'norm2_g', 'mlstm_gate_b', 'mlstm_norm_g', 's5_d', 's5_glu_b', 'ret_decay',
             'ret_gn_g', 'diff_qk_norm', 'diff_lambda', 'diff_subln_g')


def kernel(x_prompt, x_sample, state_mlstm_c, state_mlstm_n, state_mlstm_m, state_s5_re, state_s5_im, state_ret, cache_diff_k, cache_diff_v, c, c_ctx, norm1_g, norm2_g, ada_w, ada_b, w_in, w_out, mlstm_gate_b, mlstm_norm_g, s5_lambda_re, s5_lambda_im, s5_log_step, s5_b_re, s5_b_im, s5_c_re, s5_c_im, s5_d, s5_glu_w, s5_glu_b, ret_decay, ret_gn_g, diff_qk_norm, diff_lambda, diff_subln_g, router_w, exp_w_gate, exp_w_up, exp_w_down):
    weights = dict(norm1_g=norm1_g, norm2_g=norm2_g, mlstm_gate_b=mlstm_gate_b,
                   mlstm_norm_g=mlstm_norm_g, s5_d=s5_d, s5_glu_b=s5_glu_b, ret_decay=ret_decay, ret_gn_g=ret_gn_g,
                   diff_qk_norm=diff_qk_norm, diff_lambda=diff_lambda, diff_subln_g=diff_subln_g)
    w_in_p = _permute_w_in(w_in)
    rw_hi = router_w.astype(bf16)
    router_w2 = jnp.concatenate([rw_hi, (router_w - rw_hi.astype(f32)).astype(bf16)], axis=-1)
    s5p = _s5_prepare(s5_lambda_re, s5_lambda_im, s5_log_step, s5_b_re, s5_b_im, s5_c_re, s5_c_im, s5_glu_w,
                      (x_prompt.shape[1] // SUBLANES, x_sample.shape[1] // SUBLANES))
    Bc, Lc, D = x_prompt.shape
    Bl, Ll, _ = x_sample.shape
    xc = x_prompt.reshape(Bc * Lc, D)
    xl = x_sample.reshape(Bl * Ll, D)
    zero_states = (jnp.zeros((Bc, 2, H_M, DH_M, DH_M), f32), jnp.zeros((Bc, 2, H_M, DH_M), f32),
                   jnp.zeros((Bc, 2, H_M), f32), jnp.zeros((Bc, 2, G_S5, P_S5), f32),
                   jnp.zeros((Bc, 2, G_S5, P_S5), f32), jnp.zeros((Bc, 2, H_R, DH_R, DH_R), f32))
    cache = (cache_diff_k.reshape(cache_diff_k.shape[:3] + (W_GROUP,)),
             cache_diff_v.reshape(cache_diff_v.shape[:3] + (W_GROUP,)))
    cvec = jnp.concatenate([c_ctx[None, :], c], axis=0)
    outs = [[] for _ in range(8)]
    for l in range(DEPTH):
        lp = {name: weights[name][l] for name in PER_LAYER}
        lam_init = 0.8 - 0.6 * math.exp(-0.3 * l)
        mods = jnp.split((jax.nn.silu(cvec) @ ada_w[l] + ada_b[l])[:, None, :], 6, axis=-1)
        lat_states = (state_mlstm_c[:, l], state_mlstm_n[:, l], state_mlstm_m[:, l], state_s5_re[:, l],
                      state_s5_im[:, l], state_ret[:, l])
        def mix(x, B, L, sel, states, kv):
            sh1, sc1 = mods[0][sel], mods[1][sel]
            proj = in_projection(x, lp['norm1_g'], sc1, sh1, w_in_p, l, x.shape[0] // sh1.shape[0]).reshape(B, L, N_PROJ)
            return token_mixers(proj, lp, s5p, lam_init, states, kv, l)

        def project_and_route(x, mixed, sel):
            g1, sh2, sc2, g2 = (m[sel] for m in mods[2:])
            x1, h2, aff = out_projection(x, mixed, w_out, g1, lp['norm2_g'], sc2, sh2, router_w2, l,
                                         x.shape[0] // g1.shape[0])
            return (x1, g2, g1.shape[0]) + _route(aff, h2)

        sel_c, sel_l = slice(0, 1), slice(1, 1 + Bl)
        mixed_l, _ = mix(xl, Bl, Ll, sel_l, lat_states, cache)
        xc, mixed_l = lax.optimization_barrier((xc, mixed_l))
        mixed_c, new_ctx = mix(xc, Bc, Lc, sel_c, zero_states, None)
        for acc, t in zip(outs, new_ctx):
            acc.append(t)
        x1c, g2c, nbc, gc, idxc, xec = project_and_route(xc, mixed_c, sel_c)
        x1l, g2l, nbl, gl, idxl, xel = project_and_route(xl, mixed_l, sel_l)
        yec, yel = expert_ffn(xec, xel, gc, gl, g2c, exp_w_gate, exp_w_up, exp_w_down, l)
        xc = _combine(x1c, yec, idxc, None, nbc)
        xl = _combine(x1l, yel, idxl, g2l, nbl)
    return (xc.reshape(Bc, Lc, D), xl.reshape(Bl, Ll, D)) + tuple(jnp.stack(o, axis=1) for o in outs)
```

```python
import functools
import math

import jax
import jax.numpy as jnp
import numpy as np
from jax import lax
from jax.experimental import pallas as pl
from jax.experimental.pallas import tpu as pltpu
from jax.experimental.pallas import tpu_sc as plsc

D_MODEL = 1024
DEPTH = 4
GRID_W = 64
W_GROUP = 256
H_M = 4
DH_M = 64
S5_CH = 16
G_S5 = 16
P_S5 = 64
S5_STATE = G_S5 * P_S5
H_R = 4
DH_R = 64
H_D = 4
DH_D = 32
N_EXPERTS = 16
CAPACITY_FACTOR = 2
ROPE_BASE = 10000.0
EPS = 1e-6
SUBLANES = 8
VMEM_LIMIT_BYTES = 56 * 1024 * 1024

f32 = jnp.float32
bf16 = jnp.bfloat16
HIGHEST = lax.Precision.HIGHEST
NEG_INF = float("-inf")


def _gelu_tanh(x):
    return 0.5 * x * (1.0 + jnp.tanh(math.sqrt(2.0 / math.pi) * (x + 0.044715 * (x * x * x))))


def _s5_kernel(su_ref, x0r_ref, x0i_ref, wb_ref, wc_ref, lb_ref, pw_ref, d_ref, gw_ref, gb_ref,
               y_ref, xr_ref, xi_ref, st_ref, *, chained):
    n_steps = st_ref.shape[0] // SUBLANES
    su = su_ref[0]
    y_ref[0] = su * d_ref[...]
    row = lax.broadcasted_iota(jnp.int32, (SUBLANES, S5_STATE), 0)
    zeros = jnp.zeros((SUBLANES, S5_STATE), f32)
    for d in range(2):
        st_ref[...] = jnp.dot(su.astype(bf16), wb_ref[0, d], preferred_element_type=f32)
        lbr = jnp.broadcast_to(lb_ref[0, d, 0:1, :], (SUBLANES, S5_STATE))
        lbi = jnp.broadcast_to(lb_ref[0, d, 1:2, :], (SUBLANES, S5_STATE))

        def rows_of(k, d=d):
            kk = k if d == 0 else n_steps - 1 - k
            return pl.ds(pl.multiple_of(kk * SUBLANES, SUBLANES), SUBLANES)

        def scan_step(k, carry, lbr=lbr, lbi=lbi, rows_of=rows_of):
            xr, xi = carry
            r = rows_of(k)
            nxr = lbr * xr - lbi * xi + st_ref[r, 0:S5_STATE]
            nxi = lbr * xi + lbi * xr + st_ref[r, S5_STATE:2 * S5_STATE]
            st_ref[r, 0:S5_STATE] = nxr
            st_ref[r, S5_STATE:2 * S5_STATE] = nxi
            return nxr, nxi

        if not chained:
            xr_ref[0, d], xi_ref[0, d] = lax.fori_loop(0, n_steps, scan_step, (x0r_ref[0, d], x0i_ref[0, d]))
            y_ref[0] += jnp.dot(st_ref[...].astype(bf16), wc_ref[0, d], preferred_element_type=f32)
            continue
        fr, fi = lax.fori_loop(0, n_steps, scan_step, (zeros, zeros))

        cr = x0r_ref[0, d]
        ci = x0i_ref[0, d]
        plr = pw_ref[0, d, 0:1, :]
        pli = pw_ref[0, d, 1:2, :]
        cmr, cmi = zeros, zeros
        for i in (range(SUBLANES) if d == 0 else reversed(range(SUBLANES))):
            cmr = jnp.where(row == i, cr, cmr)
            cmi = jnp.where(row == i, ci, cmi)
            cr, ci = (plr * cr - pli * ci + fr[i:i + 1], plr * ci + pli * cr + fi[i:i + 1])
        xr_ref[0, d] = cr
        xi_ref[0, d] = ci

        def fix_step(k, carry, lbr=lbr, lbi=lbi, cmr=cmr, cmi=cmi, rows_of=rows_of):
            pr, pi = carry
            r = rows_of(k)
            st_ref[r, 0:S5_STATE] = st_ref[r, 0:S5_STATE] + (pr * cmr - pi * cmi)
            st_ref[r, S5_STATE:2 * S5_STATE] = st_ref[r, S5_STATE:2 * S5_STATE] + (pr * cmi + pi * cmr)
            return pr * lbr - pi * lbi, pr * lbi + pi * lbr

        lax.fori_loop(0, n_steps, fix_step, (lbr, lbi))
        y_ref[0] += jnp.dot(st_ref[...].astype(bf16), wc_ref[0, d], preferred_element_type=f32)

    ys = _gelu_tanh(y_ref[0])
    gate = jax.nn.sigmoid(jnp.dot(ys.astype(bf16), gw_ref[0], preferred_element_type=f32) + gb_ref[...])
    y_ref[0] = ys * gate


def _s5_prepare(lam_re, lam_im, log_step, b_re, b_im, c_re, c_im, glu_w, n_steps_list):
    dt = jnp.exp(log_step)[..., None]
    mag = jnp.exp(lam_re * dt)
    ang = lam_im * dt
    lb_re, lb_im = mag * jnp.cos(ang), mag * jnp.sin(ang)
    nr, ni = lb_re - 1.0, lb_im
    den = lam_re * lam_re + lam_im * lam_im
    f_re = (nr * lam_re + ni * lam_im) / den
    f_im = (ni * lam_re - nr * lam_im) / den
    bb_re = f_re[..., None] * b_re[:, None] - f_im[..., None] * b_im[:, None]
    bb_im = f_re[..., None] * b_im[:, None] + f_im[..., None] * b_re[:, None]
    eye = jnp.eye(G_S5, dtype=f32)[:, None, :, None]

    def block_diag(a):
        return (a[:, :, :, :, None, :] * eye).reshape(a.shape[:2] + (G_S5 * a.shape[3], G_S5 * a.shape[4]))

    wb = jnp.concatenate([block_diag(jnp.swapaxes(bb_re, 3, 4)), block_diag(jnp.swapaxes(bb_im, 3, 4))],
                         axis=-1).astype(bf16)
    wc = jnp.concatenate([block_diag(jnp.swapaxes(c_re, 3, 4)), -block_diag(jnp.swapaxes(c_im, 3, 4))],
                         axis=2).astype(bf16)
    lead = lb_re.shape[:2]
    lb = jnp.stack([lb_re.reshape(lead + (S5_STATE,)), lb_im.reshape(lead + (S5_STATE,))], axis=2)
    pr, pi = lb[:, :, 0], lb[:, :, 1]
    tables = {}
    for j in range(int(math.log2(max(n_steps_list))) + 1):
        if 2 ** j in n_steps_list:
            tables[2 ** j] = jnp.stack([pr, pi], axis=2)
        pr, pi = pr * pr - pi * pi, 2.0 * pr * pi
    return wb, wc, lb, tables, glu_w.astype(bf16)


def s5_mixer(su, x0r, x0i, s5p, s5_d, glu_b, layer):
    B, L, _ = su.shape
    wb, wc, lb, tables, glu_w = s5p
    chained = B % SUBLANES != 0
    if chained:
        n_groups, lanes, n_steps = B, 1, L // SUBLANES
        to_rows = lambda a: a.reshape(B, SUBLANES, n_steps, -1).transpose(0, 2, 1, 3).reshape(B, L, -1)
        from_rows = lambda a: a.reshape(B, n_steps, SUBLANES, -1).transpose(0, 2, 1, 3).reshape(B, L, -1)
        pw = tables[n_steps]
    else:
        n_groups, lanes, n_steps = B // SUBLANES, SUBLANES, L
        to_rows = lambda a: a.reshape(n_groups, SUBLANES, L, -1).transpose(0, 2, 1, 3).reshape(n_groups, -1, a.shape[-1])
        from_rows = lambda a: a.reshape(n_groups, L, SUBLANES, -1).transpose(0, 2, 1, 3).reshape(B, L, -1)
        pw = lb
    rows = n_steps * SUBLANES
    state_in = lambda a: a.reshape(n_groups, lanes, 2, S5_STATE).transpose(0, 2, 1, 3)
    state_out = lambda a: a.transpose(0, 2, 1, 3).reshape(B, 2, G_S5, P_S5)
    full = lambda *shape: pl.BlockSpec(shape, lambda b: (0,) * len(shape))
    per_b = lambda *shape: pl.BlockSpec((1,) + shape, lambda b: (b,) + (0,) * len(shape))
    per_layer = lambda *shape: pl.BlockSpec((1,) + shape, lambda b: (layer,) + (0,) * len(shape))
    y_p, xr, xi = pl.pallas_call(
        functools.partial(_s5_kernel, chained=chained),
        grid=(n_groups,),
        in_specs=[per_b(rows, W_GROUP), per_b(2, lanes, S5_STATE), per_b(2, lanes, S5_STATE),
                  per_layer(2, W_GROUP, 2 * S5_STATE), per_layer(2, 2 * S5_STATE, W_GROUP),
                  per_layer(2, 2, S5_STATE), per_layer(2, 2, S5_STATE),
                  full(1, W_GROUP), per_layer(W_GROUP, W_GROUP), full(1, W_GROUP)],
        out_specs=[per_b(rows, W_GROUP), per_b(2, lanes, S5_STATE), per_b(2, lanes, S5_STATE)],
        out_shape=[jax.ShapeDtypeStruct((n_groups, rows, W_GROUP), f32),
                   jax.ShapeDtypeStruct((n_groups, 2, lanes, S5_STATE), f32),
                   jax.ShapeDtypeStruct((n_groups, 2, lanes, S5_STATE), f32)],
        scratch_shapes=[pltpu.VMEM((rows, 2 * S5_STATE), f32)],
        compiler_params=pltpu.CompilerParams(dimension_semantics=("arbitrary",),
                                             vmem_limit_bytes=VMEM_LIMIT_BYTES),
        name="s5_mixer",
    )(to_rows(su), state_in(x0r), state_in(x0i), wb, wc, lb, pw,
      s5_d.reshape(1, W_GROUP), glu_w, glu_b.reshape(1, W_GROUP))
    return from_rows(y_p), state_out(xr), state_out(xi)


Q_TILE = 256
PROJ_BLOCKS = dict(mq=0, mk=1, mv=2, mo=3, su=4, rq=5, rk=6, rv=7, rg=8, dq=9, dk=10, dv=11)
GATE_LANES = 128
N_PROJ = 12 * W_GROUP + GATE_LANES


def _log_sigmoid(x):
    return jnp.minimum(x, 0.0) - jnp.log1p(jnp.exp(-jnp.abs(x)))


def _group_ones(width, group):
    shift = int(math.log2(group))
    r = lax.broadcasted_iota(jnp.int32, (width, width), 0) >> shift
    c = lax.broadcasted_iota(jnp.int32, (width, width), 1) >> shift
    return (r == c).astype(bf16)


def _split3(x):
    hi = x.astype(bf16)
    r = x - hi.astype(f32)
    mid = r.astype(bf16)
    return hi, mid, (r - mid.astype(f32)).astype(bf16)


def _group_mean(x, ones, group):
    return sum(jnp.dot(p, ones, preferred_element_type=f32) for p in _split3(x)) * (1.0 / group)


def _dot_nt(a, b):
    return lax.dot_general(a.astype(bf16), b.astype(bf16), (((1,), (1,)), ((), ())), preferred_element_type=f32)


def _dot(a, b):
    return jnp.dot(a.astype(bf16), b.astype(bf16), preferred_element_type=f32)


def _proj_block(name, rows):
    j = PROJ_BLOCKS[name]
    return pl.BlockSpec((1, rows, W_GROUP), lambda b, qi, j=j, rows=rows: (b, qi if rows == Q_TILE else 0, j))


def _mlstm_kernel(q_ref, k_ref, v_ref, o_ref, g_ref, gt_ref, gb_ref, gbt_ref, ng_ref, c0_ref, n0_ref, m0_ref,
                  h_ref, c_ref, n_ref, m_ref, gl_ref, gu_ref, rc_ref, vt_ref, ht_ref, *, seq_len, q_tile):
    L, TQ = seq_len, q_tile
    nq = L // TQ
    qi = pl.program_id(1)
    grow = gt_ref[0] + gbt_ref[...]

    @pl.when(qi == 0)
    def _():
        ss = lax.broadcasted_iota(jnp.int32, (L, L), 0)
        tt = lax.broadcasted_iota(jnp.int32, (L, L), 1)
        tri_le = (ss <= tt).astype(bf16)
        tri_ge = (ss >= tt).astype(bf16)
        rows = _split3(_log_sigmoid(grow))
        gl = sum(jnp.dot(p, tri_le, preferred_element_type=f32) for p in rows)
        gu = sum(jnp.dot(p, tri_ge, preferred_element_type=f32) for p in rows)
        for j in range(nq):
            gl_ref[j] = gl[:, j * TQ:(j + 1) * TQ]
            gu_ref[j] = gu[:, j * TQ:(j + 1) * TQ]
        gcol = g_ref[0] + gb_ref[...]
        cols = _split3(_log_sigmoid(gcol))
        glc = sum(jnp.dot(tri_ge, p, preferred_element_type=f32) for p in cols)
        guc = sum(jnp.dot(tri_le, p, preferred_element_type=f32) for p in cols)
        lane = lax.broadcasted_iota(jnp.int32, (L, GATE_LANES), 1)
        rc_ref[...] = pltpu.roll(gcol, 4, 1) - jnp.where(lane < 8, glc, guc)

        vt = v_ref[0].T
        kk = k_ref[0] * (DH_M ** -0.5)
        one_row = (lax.broadcasted_iota(jnp.int32, (DH_M, L), 0) == 0).astype(bf16)
        for h in range(H_M):
            hs = slice(h * DH_M, (h + 1) * DH_M)
            vt_ref[2 * h * DH_M:(2 * h + 1) * DH_M, :] = vt[hs, :].astype(bf16)
            vt_ref[(2 * h + 1) * DH_M:(2 * h + 2) * DH_M, :] = one_row
        for d in range(2):
            g_all = gl if d == 0 else gu
            for h in range(H_M):
                hs = slice(h * DH_M, (h + 1) * DH_M)
                ii, fi = 8 * d + h, 8 * d + 4 + h
                g_row = g_all[fi:fi + 1, :]
                g_tot = g_row[:, L - 1:L] if d == 0 else g_row[:, 0:1]
                wlog = g_tot - g_row + grow[ii:ii + 1, :]
                m0 = m0_ref[0, d:d + 1, h:h + 1]
                m_new = jnp.maximum(g_tot + m0, jnp.max(wlog, axis=1, keepdims=True))
                decay = jnp.exp(g_tot + m0 - m_new)
                w = jnp.exp(wlog - m_new)
                kh = kk[:, hs]
                c_ref[0, d, h] = decay * c0_ref[0, d, h] + _dot(vt[hs, :] * w, kh)
                n_upd = jnp.dot(jnp.broadcast_to(w, (SUBLANES, L)), kh, precision=HIGHEST,
                                preferred_element_type=f32)[0:1, :]
                n_ref[0, d, h:h + 1, :] = decay * n0_ref[0, d, h:h + 1, :] + n_upd
                m_ref[0, d:d + 1, h:h + 1] = m_new

    gl_t = gl_ref[qi]
    gu_t = gu_ref[qi]
    s_idx = lax.broadcasted_iota(jnp.int32, (L, TQ), 0)
    t_idx = qi * TQ + lax.broadcasted_iota(jnp.int32, (L, TQ), 1)
    low = s_idx <= t_idx
    upp = s_idx >= t_idx
    qt = q_ref[0].T.astype(bf16)
    k = (k_ref[0] * (DH_M ** -0.5)).astype(bf16)
    rc = rc_ref[...]
    row0 = lax.broadcasted_iota(jnp.int32, (DH_M, DH_M), 0) == 0
    for h in range(H_M):
        hs = slice(h * DH_M, (h + 1) * DH_M)
        qth = qt[hs, :]
        vta = vt_ref[2 * h * DH_M:(2 * h + 2) * DH_M, :]
        s0 = jnp.dot(k[:, hs], qth, preferred_element_type=f32)
        h_sum = None
        for d in range(2):
            fi = 8 * d + 4 + h
            g_t = (gl_t if d == 0 else gu_t)[fi:fi + 1, :]
            dlog = jnp.where(low if d == 0 else upp, rc[:, fi:fi + 1] + g_t, NEG_INF)
            inter = g_t + m0_ref[0, d:d + 1, h:h + 1]
            m_t = jnp.maximum(inter, jnp.max(dlog, axis=0, keepdims=True))
            p = s0 * jnp.exp(dlog - m_t)
            a = jnp.exp(inter - m_t)
            c0n0 = jnp.concatenate([c0_ref[0, d, h], jnp.where(row0, n0_ref[0, d, h:h + 1, :], 0.0)], axis=0)
            numden = (jnp.dot(vta, p.astype(bf16), preferred_element_type=f32)
                      + a * jnp.dot(c0n0.astype(bf16), qth, preferred_element_type=f32))
            scale = 1.0 / jnp.maximum(jnp.abs(numden[DH_M:DH_M + 1, :]), jnp.exp(-m_t))
            hd = numden[0:DH_M, :] * scale
            h_sum = hd if h_sum is None else h_sum + hd
        ht_ref[hs, :] = h_sum * lax.rsqrt(jnp.mean(h_sum * h_sum, axis=0, keepdims=True) + EPS)
    h_ref[0] = jax.nn.sigmoid(o_ref[0]) * (ht_ref[...].T * ng_ref[...])


def mlstm_mixer(proj, gates_t, c0, n0, m0, lp):
    B, L, _ = proj.shape
    TQ = min(L, Q_TILE)
    gb = lp['mlstm_gate_b'].reshape(1, 4 * H_M)
    const = lambda *shape: pl.BlockSpec(shape, lambda b, qi: (0,) * len(shape))
    per_b = lambda *shape: pl.BlockSpec((1,) + shape, lambda b, qi: (b,) + (0,) * len(shape))
    return pl.pallas_call(
        functools.partial(_mlstm_kernel, seq_len=L, q_tile=TQ),
        grid=(B, L // TQ),
        in_specs=[_proj_block('mq', TQ), _proj_block('mk', L), _proj_block('mv', L), _proj_block('mo', TQ),
                  pl.BlockSpec((1, L, GATE_LANES), lambda b, qi: (b, 0, 12 * W_GROUP // GATE_LANES)),
                  per_b(4 * H_M, L), const(1, GATE_LANES), const(4 * H_M, 1), const(1, W_GROUP),
                  per_b(2, H_M, DH_M, DH_M), per_b(2, H_M, DH_M), per_b(2, H_M)],
        out_specs=[pl.BlockSpec((1, TQ, W_GROUP), lambda b, qi: (b, qi, 0)),
                   per_b(2, H_M, DH_M, DH_M), per_b(2, H_M, DH_M), per_b(2, H_M)],
        out_shape=[jax.ShapeDtypeStruct((B, L, W_GROUP), f32),
                   jax.ShapeDtypeStruct((B, 2, H_M, DH_M, DH_M), f32),
                   jax.ShapeDtypeStruct((B, 2, H_M, DH_M), f32),
                   jax.ShapeDtypeStruct((B, 2, H_M), f32)],
        scratch_shapes=[pltpu.VMEM((L // TQ, 4 * H_M, TQ), f32), pltpu.VMEM((L // TQ, 4 * H_M, TQ), f32),
                        pltpu.VMEM((L, GATE_LANES), f32), pltpu.VMEM((2 * W_GROUP, L), bf16),
                        pltpu.VMEM((W_GROUP, TQ), f32)],
        compiler_params=pltpu.CompilerParams(dimension_semantics=("arbitrary", "arbitrary"),
                                             vmem_limit_bytes=VMEM_LIMIT_BYTES),
        name="mlstm_mixer",
    )(proj, proj, proj, proj, proj, gates_t, jnp.pad(gb, ((0, 0), (0, GATE_LANES - 4 * H_M))),
      gb.reshape(4 * H_M, 1), lp['mlstm_norm_g'].reshape(1, W_GROUP), c0, n0, m0)


def _retention_kernel(lg_ref, q_ref, k_ref, v_ref, g_ref, gn_ref, r0_ref, h_ref, r_ref, *, seq_len, q_tile):
    L, TQ = seq_len, q_tile
    qi = pl.program_id(1)
    t_col = qi * TQ + lax.broadcasted_iota(jnp.int32, (TQ, 1), 0)
    rel = (qi * TQ + lax.broadcasted_iota(jnp.int32, (TQ, L), 0)
           - lax.broadcasted_iota(jnp.int32, (TQ, L), 1)).astype(f32)
    q = q_ref[0]
    k = k_ref[0] * (DH_R ** -0.5)
    v = v_ref[0]
    ones = _group_ones(W_GROUP, DH_R)
    for h in range(H_R):
        hs = slice(h * DH_R, (h + 1) * DH_R)
        lgf, lgb = lg_ref[0, h], lg_ref[1, h]
        qh, kh, vh = q[:, hs], k[:, hs], v[:, hs]
        decay = jnp.where(rel > 0.0, jnp.exp(lgf * jnp.maximum(rel, 0.0)),
                          jnp.where(rel < 0.0, jnp.exp(lgb * jnp.maximum(-rel, 0.0)), 2.0))
        o = _dot(_dot_nt(qh, kh) * decay, vh)
        xi_f = jnp.exp(lgf * (t_col + 1).astype(f32))
        xi_b = jnp.exp(lgb * (L - t_col).astype(f32))
        o = o + xi_f * _dot(qh, r0_ref[0, 0, h]) + xi_b * _dot(qh, r0_ref[0, 1, h])
        h_ref[0, :, hs] = o
    o = h_ref[0]
    oc = o - _group_mean(o, ones, DH_R)
    y = oc * lax.rsqrt(_group_mean(oc * oc, ones, DH_R) + EPS) * gn_ref[...]
    h_ref[0] = y * jax.nn.silu(g_ref[0])

    @pl.when(qi == 0)
    def _():
        kt = k.T
        s_row = lax.broadcasted_iota(jnp.int32, (1, L), 1).astype(f32)
        for d in range(2):
            for h in range(H_R):
                hs = slice(h * DH_R, (h + 1) * DH_R)
                lg = lg_ref[d, h]
                zeta = jnp.exp(lg * ((L - 1.0) - s_row)) if d == 0 else jnp.exp(lg * s_row)
                r_ref[0, d, h] = jnp.exp(lg * float(L)) * r0_ref[0, d, h] + _dot(kt[hs, :] * zeta, v[:, hs])


def retention_mixer(proj, r0, lp):
    B, L, _ = proj.shape
    TQ = min(L, Q_TILE)
    log_gamma = -jnp.exp(lp['ret_decay'])
    per_b = lambda *shape: pl.BlockSpec((1,) + shape, lambda b, qi: (b,) + (0,) * len(shape))
    return pl.pallas_call(
        functools.partial(_retention_kernel, seq_len=L, q_tile=TQ),
        grid=(B, L // TQ),
        in_specs=[pl.BlockSpec(memory_space=pltpu.SMEM),
                  _proj_block('rq', TQ), _proj_block('rk', L), _proj_block('rv', L), _proj_block('rg', TQ),
                  pl.BlockSpec((1, W_GROUP), lambda b, qi: (0, 0)), per_b(2, H_R, DH_R, DH_R)],
        out_specs=[pl.BlockSpec((1, TQ, W_GROUP), lambda b, qi: (b, qi, 0)), per_b(2, H_R, DH_R, DH_R)],
        out_shape=[jax.ShapeDtypeStruct((B, L, W_GROUP), f32),
                   jax.ShapeDtypeStruct((B, 2, H_R, DH_R, DH_R), f32)],
        compiler_params=pltpu.CompilerParams(dimension_semantics=("arbitrary", "arbitrary"),
                                             vmem_limit_bytes=VMEM_LIMIT_BYTES),
        name="retention_mixer",
    )(log_gamma, proj, proj, proj, proj, lp['ret_gn_g'].reshape(1, W_GROUP), r0)


def _rope_tables(L):
    half = DH_D // 2
    freqs = ROPE_BASE ** (-np.arange(0, half, 2, dtype=np.float64) / half)
    pos = np.arange(L)
    row, col = (pos // GRID_W).astype(np.float64), (pos % GRID_W).astype(np.float64)
    ang = np.concatenate([np.tile(row[:, None] * freqs, (1, 2)), np.tile(col[:, None] * freqs, (1, 2))], axis=1)
    sign = np.tile(np.concatenate([-np.ones(half // 2), np.ones(half // 2)]), 2)
    cos = np.tile(np.cos(ang), (1, 2 * H_D))
    sin = np.tile(np.sin(ang) * sign, (1, 2 * H_D))
    return jnp.asarray(cos, f32), jnp.asarray(sin, f32)


def _swap_pairs(x):
    parts = []
    for j in range(x.shape[1] // 128):
        xs = x[:, j * 128:(j + 1) * 128]
        lane = lax.broadcasted_iota(jnp.int32, xs.shape, 1)
        parts.append(jnp.where((lane & 15) < 8, pltpu.roll(xs, 120, 1), pltpu.roll(xs, 8, 1)))
    return jnp.concatenate(parts, axis=1)


def _qk_norm(x, gain, ones):
    return x * lax.rsqrt(_group_mean(x * x, ones, DH_D) + EPS) * gain


def _diff_attn_kernel(lam_ref, q_ref, k_ref, v_ref, qkg_ref, sg_ref, *rest, seq_len, q_tile, past_len, out_scale):
    L, TQ, P = seq_len, q_tile, past_len
    if P:
        kc_ref, vc_ref, cos_ref, sin_ref, h_ref, ka_ref, vt_ref, ot_ref = rest
    else:
        h_ref, kn_ref, ka_ref, vt_ref, ot_ref = rest
    qi = pl.program_id(1)
    ones = _group_ones(W_GROUP, DH_D)

    @pl.when(qi == 0)
    def _():
        kn = _qk_norm(k_ref[0], qkg_ref[1:2, :], ones)
        if P:
            kn = kn * cos_ref[...] + _swap_pairs(kn) * sin_ref[...]
            ka_ref[0:P, :] = kc_ref[0, 0].astype(bf16)
            vt_ref[:, 0:P] = vc_ref[0, 0].T.astype(bf16)
        else:
            kn_ref[0] = kn
        ka_ref[P:P + L, :] = kn.astype(bf16)
        vt_ref[:, P:P + L] = v_ref[0].T.astype(bf16)

    qn = _qk_norm(q_ref[0], qkg_ref[0:1, :], ones)
    if P:
        rows = pl.ds(pl.multiple_of(qi * TQ, TQ), TQ)
        qn = qn * cos_ref[rows, :] + _swap_pairs(qn) * sin_ref[rows, :]
    qt = (qn * (DH_D ** -0.5)).T.astype(bf16)
    lam = lam_ref[0, 0]
    ka = ka_ref[...]
    for h in range(H_D):
        probs = []
        for j in range(2):
            cs = slice((2 * h + j) * DH_D, (2 * h + j + 1) * DH_D)
            s = jnp.dot(ka[:, cs], qt[cs, :], preferred_element_type=f32)
            e = jnp.exp(s - jnp.max(s, axis=0, keepdims=True))
            probs.append(e * (1.0 / jnp.sum(e, axis=0, keepdims=True)))
        vs = slice(h * 2 * DH_D, (h + 1) * 2 * DH_D)
        ot_ref[vs, :] = jnp.dot(vt_ref[vs, :], (probs[0] - lam * probs[1]).astype(bf16), preferred_element_type=f32)
    o = ot_ref[...].T
    ones_v = _group_ones(W_GROUP, 2 * DH_D)
    h_ref[0] = o * lax.rsqrt(_group_mean(o * o, ones_v, 2 * DH_D) + EPS) * (sg_ref[...] * out_scale)


def diff_attn_mixer(proj, cache, lp, lam_init, layer):
    B, L, _ = proj.shape
    TQ = min(L, Q_TILE)
    lv = lp['diff_lambda']
    lam = (jnp.exp(jnp.sum(lv[0] * lv[1])) - jnp.exp(jnp.sum(lv[2] * lv[3])) + lam_init).reshape(1, 1)
    qkg = jnp.tile(lp['diff_qk_norm'], (1, 2 * H_D))
    sg = jnp.tile(lp['diff_subln_g'], (H_D,)).reshape(1, W_GROUP)
    const = lambda *shape: pl.BlockSpec(shape, lambda b, qi: (0,) * len(shape))
    in_specs = [pl.BlockSpec(memory_space=pltpu.SMEM),
                _proj_block('dq', TQ), _proj_block('dk', L), _proj_block('dv', L), const(2, W_GROUP), const(1, W_GROUP)]
    args = [lam, proj, proj, proj, qkg, sg]
    out_specs = [pl.BlockSpec((1, TQ, W_GROUP), lambda b, qi: (b, qi, 0))]
    out_shape = [jax.ShapeDtypeStruct((B, L, W_GROUP), f32)]
    P = 0
    if cache is not None:
        ck, cv = cache
        P = ck.shape[2]
        cspec = pl.BlockSpec((1, 1, P, W_GROUP), lambda b, qi, layer=layer: (b, layer, 0, 0))
        cos, sin = _rope_tables(L)
        in_specs += [cspec, cspec, const(L, W_GROUP), const(L, W_GROUP)]
        args += [ck, cv, cos, sin]
    else:
        out_specs.append(pl.BlockSpec((1, L, W_GROUP), lambda b, qi: (b, 0, 0)))
        out_shape.append(jax.ShapeDtypeStruct((B, L, W_GROUP), f32))
    return pl.pallas_call(
        functools.partial(_diff_attn_kernel, seq_len=L, q_tile=TQ, past_len=P, out_scale=1.0 - lam_init),
        grid=(B, L // TQ),
        in_specs=in_specs, out_specs=out_specs, out_shape=out_shape,
        scratch_shapes=[pltpu.VMEM((P + L, W_GROUP), bf16), pltpu.VMEM((W_GROUP, P + L), bf16),
                        pltpu.VMEM((W_GROUP, TQ), f32)],
        compiler_params=pltpu.CompilerParams(dimension_semantics=("arbitrary", "arbitrary"),
                                             vmem_limit_bytes=VMEM_LIMIT_BYTES),
        name="diff_attention",
    )(*args)


PROJ_ROW_TILE = 512
OUT_ROW_TILE = 512
FF_TILE = 512
D_FF = 2 * D_MODEL


def _in_proj_kernel(x_ref, g_ref, sc_ref, sh_ref, w_ref, o_ref):
    x = x_ref[...]
    y = x * lax.rsqrt(jnp.mean(x * x, axis=1, keepdims=True) + EPS) * g_ref[...]
    h = (y * (1.0 + sc_ref[0]) + sh_ref[0]).astype(bf16)
    o_ref[...] = jnp.dot(h, w_ref[0], preferred_element_type=f32)


def in_projection(x, gain, scale, shift, w_p, layer, rows_per_mod):
    n, D = x.shape
    TM = min(n, PROJ_ROW_TILE)
    mod = pl.BlockSpec((1, 1, D), lambda i: (i * TM // rows_per_mod, 0, 0))
    return pl.pallas_call(
        _in_proj_kernel,
        grid=(n // TM,),
        in_specs=[pl.BlockSpec((TM, D), lambda i: (i, 0)), pl.BlockSpec((1, D), lambda i: (0, 0)), mod, mod,
                  pl.BlockSpec((1, D, N_PROJ), lambda i: (layer, 0, 0))],
        out_specs=pl.BlockSpec((TM, N_PROJ), lambda i: (i, 0)),
        out_shape=jax.ShapeDtypeStruct((n, N_PROJ), f32),
        compiler_params=pltpu.CompilerParams(dimension_semantics=("arbitrary",),
                                             vmem_limit_bytes=VMEM_LIMIT_BYTES),
        name="in_projection",
    )(x, gain.reshape(1, D), scale, shift, w_p)


def _out_proj_kernel(x_ref, m0_ref, m1_ref, m2_ref, m3_ref, w_ref, g1_ref, ng_ref, sc_ref, sh_ref, rw_ref,
                     xo_ref, h_ref, aff_ref, wb_ref):
    @pl.when(pl.program_id(0) == 0)
    def _():
        wb_ref[...] = w_ref[0].astype(bf16)

    out = None
    for j, m_ref in enumerate((m0_ref, m1_ref, m2_ref, m3_ref)):
        part = jnp.dot(m_ref[...].astype(bf16), wb_ref[j * W_GROUP:(j + 1) * W_GROUP, :], preferred_element_type=f32)
        out = part if out is None else out + part
    x = x_ref[...] + g1_ref[0] * out
    xo_ref[...] = x
    h = x * lax.rsqrt(jnp.mean(x * x, axis=1, keepdims=True) + EPS) * ng_ref[...]
    h = h * (1.0 + sc_ref[0]) + sh_ref[0]
    h_ref[...] = h
    h_hi = h.astype(bf16)
    h_lo = (h - h_hi.astype(f32)).astype(bf16)
    both = jnp.dot(h_hi, rw_ref[0], preferred_element_type=f32)
    logits = (both[:, 0:N_EXPERTS] + both[:, N_EXPERTS:2 * N_EXPERTS]
              + jnp.dot(h_lo, rw_ref[0], preferred_element_type=f32)[:, 0:N_EXPERTS])
    e = jnp.exp(logits - jnp.max(logits, axis=1, keepdims=True))
    aff_ref[...] = e / jnp.sum(e, axis=1, keepdims=True)


def out_projection(x, mixed, w_out, gate1, gain2, scale2, shift2, router_w2, layer, rows_per_mod):
    n, D = x.shape
    TM = OUT_ROW_TILE
    row = lambda width: pl.BlockSpec((TM, width), lambda i: (i, 0))
    const = lambda *shape: pl.BlockSpec(shape, lambda i: (0,) * len(shape))
    mod = pl.BlockSpec((1, 1, D), lambda i: (i * TM // rows_per_mod, 0, 0))
    return pl.pallas_call(
        _out_proj_kernel,
        grid=(n // TM,),
        in_specs=[row(D), row(W_GROUP), row(W_GROUP), row(W_GROUP), row(W_GROUP),
                  pl.BlockSpec((1, D, D), lambda i: (layer, 0, 0)), mod, const(1, D), mod, mod,
                  pl.BlockSpec((1, D, 2 * N_EXPERTS), lambda i: (layer, 0, 0))],
        out_specs=[row(D), row(D), row(N_EXPERTS)],
        out_shape=[jax.ShapeDtypeStruct((n, D), f32), jax.ShapeDtypeStruct((n, D), f32),
                   jax.ShapeDtypeStruct((n, N_EXPERTS), f32)],
        scratch_shapes=[pltpu.VMEM((D, D), bf16)],
        compiler_params=pltpu.CompilerParams(dimension_semantics=("arbitrary",), vmem_limit_bytes=VMEM_LIMIT_BYTES),
        name="out_projection",
    )(x, *mixed, w_out, gate1, gain2.reshape(1, D), scale2, shift2, router_w2)


def _experts_kernel(xc_ref, xl_ref, gc_ref, gl_ref, g2_ref, g2l_ref, il_ref, wg_ref, wu_ref, wd_ref, yc_ref, yl_ref,
                    ac_ref, al_ref, *, lat_len):
    f = pl.program_id(1)
    wg = wg_ref[0, 0].astype(bf16)
    wu = wu_ref[0, 0].astype(bf16)
    wd = wd_ref[0, 0].astype(bf16)
    for x_ref, acc_ref in ((xc_ref, ac_ref), (xl_ref, al_ref)):
        x = x_ref[0].astype(bf16)
        hidden = jax.nn.silu(jnp.dot(x, wg, preferred_element_type=f32)) * jnp.dot(x, wu, preferred_element_type=f32)
        part = jnp.dot(hidden.astype(bf16), wd, preferred_element_type=f32)

        @pl.when(f == 0)
        def _(acc_ref=acc_ref, part=part):
            acc_ref[...] = part

        @pl.when(f > 0)
        def _(acc_ref=acc_ref, part=part):
            acc_ref[...] += part

    @pl.when(f == pl.num_programs(1) - 1)
    def _():
        yc_ref[0] = ac_ref[...] * gc_ref[0] * g2_ref[0]
        gate = g2l_ref[0]
        for b in range(1, g2l_ref.shape[0]):
            gate = jnp.where(il_ref[0] >= b * lat_len, g2l_ref[b], gate)
        yl_ref[0] = al_ref[...] * gl_ref[0] * gate


def expert_ffn(xe_c, xe_l, g_c, g_l, gate2_c, gate2_l, idx_l, lat_len, w_gate, w_up, w_down, layer):
    E, Cc, D = xe_c.shape
    Cl = xe_l.shape[1]
    tok = lambda C, width: pl.BlockSpec((1, C, width), lambda e, f: (e, 0, 0))
    return pl.pallas_call(
        functools.partial(_experts_kernel, lat_len=lat_len),
        grid=(E, D_FF // FF_TILE),
        in_specs=[tok(Cc, D), tok(Cl, D), tok(Cc, 1), tok(Cl, 1), pl.BlockSpec((1, 1, D), lambda e, f: (0, 0, 0)),
                  pl.BlockSpec(gate2_l.shape, lambda e, f: (0, 0, 0)), tok(Cl, 1),
                  pl.BlockSpec((1, 1, D, FF_TILE), lambda e, f: (layer, e, 0, f)),
                  pl.BlockSpec((1, 1, D, FF_TILE), lambda e, f: (layer, e, 0, f)),
                  pl.BlockSpec((1, 1, FF_TILE, D), lambda e, f: (layer, e, f, 0))],
        out_specs=[tok(Cc, D), tok(Cl, D)],
        out_shape=[jax.ShapeDtypeStruct((E, Cc, D), f32), jax.ShapeDtypeStruct((E, Cl, D), f32)],
        scratch_shapes=[pltpu.VMEM((Cc, D), f32), pltpu.VMEM((Cl, D), f32)],
        compiler_params=pltpu.CompilerParams(dimension_semantics=("arbitrary", "arbitrary"),
                                             vmem_limit_bytes=VMEM_LIMIT_BYTES),
        name="expert_ffn",
    )(xe_c, xe_l, g_c, g_l, gate2_c, gate2_l, idx_l, w_gate, w_up, w_down)


def _permute_w_in(w):
    gate0 = 4 * W_GROUP
    pad = jnp.zeros(w.shape[:-1] + (GATE_LANES - 4 * H_M,), w.dtype)
    return jnp.concatenate([w[..., :gate0], w[..., gate0 + 4 * H_M:], w[..., gate0:gate0 + 4 * H_M], pad],
                           axis=-1).astype(bf16)


def token_mixers(proj, lp, s5p, lam_init, states, cache, layer):
    B, L, _ = proj.shape
    mC0, mn0, mm0, s5r0, s5i0, R0 = states
    gates_t = jnp.swapaxes(proj[:, :, 12 * W_GROUP:12 * W_GROUP + 4 * H_M], 1, 2)
    hm, mC, mn, mm = mlstm_mixer(proj, gates_t, mC0, mn0, mm0, lp)
    su = proj[:, :, PROJ_BLOCKS['su'] * W_GROUP:(PROJ_BLOCKS['su'] + 1) * W_GROUP]
    ys, s5r, s5i = s5_mixer(su, s5r0, s5i0, s5p, lp['s5_d'], lp['s5_glu_b'], layer)
    hr, R = retention_mixer(proj, R0, lp)
    attn = diff_attn_mixer(proj, cache, lp, lam_init, layer)
    mixed = [a.reshape(B * L, W_GROUP) for a in (hm, ys, hr, attn[0])]
    new_ctx = None
    if cache is None:
        v = proj[:, :, PROJ_BLOCKS['dv'] * W_GROUP:(PROJ_BLOCKS['dv'] + 1) * W_GROUP]
        new_ctx = (mC, mn, mm, s5r, s5i, R, attn[1].reshape(B, L, 2 * H_D, DH_D), v.reshape(B, L, H_D, 2 * DH_D))
    return mixed, new_ctx


GATHER_WINDOW = 32
INDEX_LANES = 128


def gather_rows(x, idx):
    num = idx.shape[0]
    width = x.shape[1]
    mesh = plsc.VectorSubcoreMesh(core_axis_name="core", subcore_axis_name="subcore")
    per_core = num // GATHER_WINDOW // mesh.num_cores
    idx_rows = jnp.pad(idx.reshape(num // GATHER_WINDOW, GATHER_WINDOW), ((0, 0), (0, INDEX_LANES - GATHER_WINDOW)))

    @pl.kernel(out_type=jax.ShapeDtypeStruct((num, width), x.dtype), mesh=mesh)
    def gather_kernel(x_hbm, i_hbm, o_hbm):
        base = lax.axis_index("core") * per_core

        def body(i_vmem, o_vmem):
            pltpu.sync_copy(x_hbm.at[i_vmem.at[0, pl.ds(0, GATHER_WINDOW)]], o_vmem)

        pltpu.emit_pipeline(
            body,
            grid=(per_core,),
            in_specs=[pl.BlockSpec((1, INDEX_LANES), index_map=lambda i: (base + i, 0))],
            out_specs=[pl.BlockSpec((GATHER_WINDOW, width), index_map=lambda i: (base + i, 0))],
            core_axis_name="subcore",
            dimension_semantics=(pltpu.PARALLEL,),
        )(i_hbm, o_hbm)

    return gather_kernel(x, idx_rows)


def _route(aff, h2):
    n = aff.shape[0]
    gates, idx = lax.top_k(aff.T, CAPACITY_FACTOR * n // N_EXPERTS)
    return gates[..., None], idx, gather_rows(h2, idx.reshape(-1)).reshape(idx.shape + h2.shape[1:])


SCATTER_COLS = 128
SCATTER_WINDOW = 128


def scatter_add_rows(x, ye, idx):
    n, D = x.shape
    R = ye.shape[0]
    mesh = plsc.VectorSubcoreMesh(core_axis_name="core", subcore_axis_name="subcore")
    rows = n // mesh.num_subcores
    wins = R // SCATTER_WINDOW // mesh.num_subcores

    @pl.kernel(out_type=jax.ShapeDtypeStruct((n, D), f32), mesh=mesh,
               scratch_types=[pltpu.VMEM_SHARED((n, SCATTER_COLS), f32), pltpu.VMEM((SCATTER_WINDOW, SCATTER_COLS), f32),
                              pltpu.VMEM((SCATTER_WINDOW,), jnp.int32)])
    def scatter_kernel(x_hbm, ye_hbm, i_hbm, o_hbm, shared, buf, ibuf):
        core = lax.axis_index("core")
        sid = lax.axis_index("subcore")
        r0 = pl.multiple_of(sid * rows, SUBLANES)
        for slab in range(D // SCATTER_COLS):
            cols = pl.ds(slab * SCATTER_COLS, SCATTER_COLS)

            @pl.when(core == slab % mesh.num_cores)
            def _(cols=cols):
                pltpu.sync_copy(x_hbm.at[pl.ds(r0, rows), cols], shared.at[pl.ds(r0, rows)])
                plsc.subcore_barrier()

                @pl.loop(0, wins)
                def _(w):
                    win = w * mesh.num_subcores + sid
                    pltpu.sync_copy(i_hbm.at[win], ibuf)
                    pltpu.sync_copy(ye_hbm.at[pl.ds(pl.multiple_of(win * SCATTER_WINDOW, SCATTER_WINDOW),
                                                    SCATTER_WINDOW), cols], buf)
                    pltpu.sync_copy(buf, shared.at[ibuf], add=True)

                plsc.subcore_barrier()
                pltpu.sync_copy(shared.at[pl.ds(r0, rows)], o_hbm.at[pl.ds(r0, rows), cols])
                plsc.subcore_barrier()

    return scatter_kernel(x, ye, idx.reshape(R // SCATTER_WINDOW, SCATTER_WINDOW))


PER_LAYER = ('norm1_g', 'norm2_g', 'mlstm_gate_b', 'mlstm_norm_g', 's5_d', 's5_glu_b', 'ret_decay',
             'ret_gn_g', 'diff_qk_norm', 'diff_lambda', 'diff_subln_g')


def kernel(x_prompt, x_sample, state_mlstm_c, state_mlstm_n, state_mlstm_m, state_s5_re, state_s5_im, state_ret, cache_diff_k, cache_diff_v, c, c_ctx, norm1_g, norm2_g, ada_w, ada_b, w_in, w_out, mlstm_gate_b, mlstm_norm_g, s5_lambda_re, s5_lambda_im, s5_log_step, s5_b_re, s5_b_im, s5_c_re, s5_c_im, s5_d, s5_glu_w, s5_glu_b, ret_decay, ret_gn_g, diff_qk_norm, diff_lambda, diff_subln_g, router_w, exp_w_gate, exp_w_up, exp_w_down):
    weights = dict(norm1_g=norm1_g, norm2_g=norm2_g, mlstm_gate_b=mlstm_gate_b,
                   mlstm_norm_g=mlstm_norm_g, s5_d=s5_d, s5_glu_b=s5_glu_b, ret_decay=ret_decay, ret_gn_g=ret_gn_g,
                   diff_qk_norm=diff_qk_norm, diff_lambda=diff_lambda, diff_subln_g=diff_subln_g)
    w_in_p = _permute_w_in(w_in)
    rw_hi = router_w.astype(bf16)
    router_w2 = jnp.concatenate([rw_hi, (router_w - rw_hi.astype(f32)).astype(bf16)], axis=-1)
    s5p = _s5_prepare(s5_lambda_re, s5_lambda_im, s5_log_step, s5_b_re, s5_b_im, s5_c_re, s5_c_im, s5_glu_w,
                      (x_prompt.shape[1] // SUBLANES, x_sample.shape[1] // SUBLANES))
    Bc, Lc, D = x_prompt.shape
    Bl, Ll, _ = x_sample.shape
    xc = x_prompt.reshape(Bc * Lc, D)
    xl = x_sample.reshape(Bl * Ll, D)
    zero_states = (jnp.zeros((Bc, 2, H_M, DH_M, DH_M), f32), jnp.zeros((Bc, 2, H_M, DH_M), f32),
                   jnp.zeros((Bc, 2, H_M), f32), jnp.zeros((Bc, 2, G_S5, P_S5), f32),
                   jnp.zeros((Bc, 2, G_S5, P_S5), f32), jnp.zeros((Bc, 2, H_R, DH_R, DH_R), f32))
    cache = (cache_diff_k.reshape(cache_diff_k.shape[:3] + (W_GROUP,)),
             cache_diff_v.reshape(cache_diff_v.shape[:3] + (W_GROUP,)))
    cvec = jnp.concatenate([c_ctx[None, :], c], axis=0)
    outs = [[] for _ in range(8)]
    for l in range(DEPTH):
        lp = {name: weights[name][l] for name in PER_LAYER}
        lam_init = 0.8 - 0.6 * math.exp(-0.3 * l)
        mods = jnp.split((jax.nn.silu(cvec) @ ada_w[l] + ada_b[l])[:, None, :], 6, axis=-1)
        lat_states = (state_mlstm_c[:, l], state_mlstm_n[:, l], state_mlstm_m[:, l], state_s5_re[:, l],
                      state_s5_im[:, l], state_ret[:, l])
        def mix(x, B, L, sel, states, kv):
            sh1, sc1 = mods[0][sel], mods[1][sel]
            proj = in_projection(x, lp['norm1_g'], sc1, sh1, w_in_p, l, x.shape[0] // sh1.shape[0]).reshape(B, L, N_PROJ)
            return token_mixers(proj, lp, s5p, lam_init, states, kv, l)

        def project_and_route(x, mixed, sel):
            g1, sh2, sc2, g2 = (m[sel] for m in mods[2:])
            x1, h2, aff = out_projection(x, mixed, w_out, g1, lp['norm2_g'], sc2, sh2, router_w2, l,
                                         x.shape[0] // g1.shape[0])
            return (x1, g2, g1.shape[0]) + _route(aff, h2)

        sel_c, sel_l = slice(0, 1), slice(1, 1 + Bl)
        mixed_l, _ = mix(xl, Bl, Ll, sel_l, lat_states, cache)
        xc, mixed_l = lax.optimization_barrier((xc, mixed_l))
        mixed_c, new_ctx = mix(xc, Bc, Lc, sel_c, zero_states, None)
        for acc, t in zip(outs, new_ctx):
            acc.append(t)
        x1c, g2c, nbc, gc, idxc, xec = project_and_route(xc, mixed_c, sel_c)
        x1l, g2l, nbl, gl, idxl, xel = project_and_route(xl, mixed_l, sel_l)
        yec, yel = expert_ffn(xec, xel, gc, gl, g2c, g2l, idxl[..., None], Ll, exp_w_gate, exp_w_up, exp_w_down, l)
        xl = scatter_add_rows(x1l, yel.reshape(-1, D), idxl.reshape(-1))
        xc = scatter_add_rows(x1c, yec.reshape(-1, D), idxc.reshape(-1))
    _, outs = lax.optimization_barrier((yec, outs))
    return (xc.reshape(Bc, Lc, D), xl.reshape(Bl, Ll, D)) + tuple(jnp.stack(o, axis=1) for o in outs)
```

```python
import dataclasses
import functools
import math

import jax
import jax.numpy as jnp
import numpy as np
from jax import lax
from jax.experimental import pallas as pl
from jax.experimental.pallas import tpu as pltpu
from jax.experimental.pallas import tpu_sc as plsc

D_MODEL = 1024
DEPTH = 4
GRID_W = 64
W_GROUP = 256
H_M = 4
DH_M = 64
S5_CH = 16
G_S5 = 16
P_S5 = 64
S5_STATE = G_S5 * P_S5
H_R = 4
DH_R = 64
H_D = 4
DH_D = 32
N_EXPERTS = 16
CAPACITY_FACTOR = 2
ROPE_BASE = 10000.0
EPS = 1e-6
SUBLANES = 8
VMEM_LIMIT_BYTES = 56 * 1024 * 1024

f32 = jnp.float32
bf16 = jnp.bfloat16
HIGHEST = lax.Precision.HIGHEST
NEG_INF = float("-inf")


def _gelu_tanh(x):
    return 0.5 * x * (1.0 + jnp.tanh(math.sqrt(2.0 / math.pi) * (x + 0.044715 * (x * x * x))))


def _s5_kernel(su_ref, x0r_ref, x0i_ref, wb_ref, wc_ref, lb_ref, pw_ref, d_ref, gw_ref, gb_ref,
               y_ref, xr_ref, xi_ref, st_ref, *, chained):
    n_steps = st_ref.shape[0] // SUBLANES
    su = su_ref[0]
    y_ref[0] = su * d_ref[...]
    row = lax.broadcasted_iota(jnp.int32, (SUBLANES, S5_STATE), 0)
    zeros = jnp.zeros((SUBLANES, S5_STATE), f32)
    for d in range(2):
        st_ref[...] = jnp.dot(su.astype(bf16), wb_ref[0, d], preferred_element_type=f32)
        lbr = jnp.broadcast_to(lb_ref[0, d, 0:1, :], (SUBLANES, S5_STATE))
        lbi = jnp.broadcast_to(lb_ref[0, d, 1:2, :], (SUBLANES, S5_STATE))

        def rows_of(k, d=d):
            kk = k if d == 0 else n_steps - 1 - k
            return pl.ds(pl.multiple_of(kk * SUBLANES, SUBLANES), SUBLANES)

        def scan_step(k, carry, lbr=lbr, lbi=lbi, rows_of=rows_of):
            xr, xi = carry
            r = rows_of(k)
            nxr = lbr * xr - lbi * xi + st_ref[r, 0:S5_STATE]
            nxi = lbr * xi + lbi * xr + st_ref[r, S5_STATE:2 * S5_STATE]
            st_ref[r, 0:S5_STATE] = nxr
            st_ref[r, S5_STATE:2 * S5_STATE] = nxi
            return nxr, nxi

        if not chained:
            xr_ref[0, d], xi_ref[0, d] = lax.fori_loop(0, n_steps, scan_step, (x0r_ref[0, d], x0i_ref[0, d]))
            y_ref[0] += jnp.dot(st_ref[...].astype(bf16), wc_ref[0, d], preferred_element_type=f32)
            continue
        fr, fi = lax.fori_loop(0, n_steps, scan_step, (zeros, zeros))

        cr = x0r_ref[0, d]
        ci = x0i_ref[0, d]
        plr = pw_ref[0, d, 0:1, :]
        pli = pw_ref[0, d, 1:2, :]
        cmr, cmi = zeros, zeros
        for i in (range(SUBLANES) if d == 0 else reversed(range(SUBLANES))):
            cmr = jnp.where(row == i, cr, cmr)
            cmi = jnp.where(row == i, ci, cmi)
            cr, ci = (plr * cr - pli * ci + fr[i:i + 1], plr * ci + pli * cr + fi[i:i + 1])
        xr_ref[0, d] = cr
        xi_ref[0, d] = ci

        def fix_step(k, carry, lbr=lbr, lbi=lbi, cmr=cmr, cmi=cmi, rows_of=rows_of):
            pr, pi = carry
            r = rows_of(k)
            st_ref[r, 0:S5_STATE] = st_ref[r, 0:S5_STATE] + (pr * cmr - pi * cmi)
            st_ref[r, S5_STATE:2 * S5_STATE] = st_ref[r, S5_STATE:2 * S5_STATE] + (pr * cmi + pi * cmr)
            return pr * lbr - pi * lbi, pr * lbi + pi * lbr

        lax.fori_loop(0, n_steps, fix_step, (lbr, lbi))
        y_ref[0] += jnp.dot(st_ref[...].astype(bf16), wc_ref[0, d], preferred_element_type=f32)

    ys = _gelu_tanh(y_ref[0])
    gate = jax.nn.sigmoid(jnp.dot(ys.astype(bf16), gw_ref[0], preferred_element_type=f32) + gb_ref[...])
    y_ref[0] = ys * gate


def _s5_prepare(lam_re, lam_im, log_step, b_re, b_im, c_re, c_im, glu_w, n_steps_list):
    dt = jnp.exp(log_step)[..., None]
    mag = jnp.exp(lam_re * dt)
    ang = lam_im * dt
    lb_re, lb_im = mag * jnp.cos(ang), mag * jnp.sin(ang)
    nr, ni = lb_re - 1.0, lb_im
    den = lam_re * lam_re + lam_im * lam_im
    f_re = (nr * lam_re + ni * lam_im) / den
    f_im = (ni * lam_re - nr * lam_im) / den
    bb_re = f_re[..., None] * b_re[:, None] - f_im[..., None] * b_im[:, None]
    bb_im = f_re[..., None] * b_im[:, None] + f_im[..., None] * b_re[:, None]
    eye = jnp.eye(G_S5, dtype=f32)[:, None, :, None]

    def block_diag(a):
        return (a[:, :, :, :, None, :] * eye).reshape(a.shape[:2] + (G_S5 * a.shape[3], G_S5 * a.shape[4]))

    wb = jnp.concatenate([block_diag(jnp.swapaxes(bb_re, 3, 4)), block_diag(jnp.swapaxes(bb_im, 3, 4))],
                         axis=-1).astype(bf16)
    wc = jnp.concatenate([block_diag(jnp.swapaxes(c_re, 3, 4)), -block_diag(jnp.swapaxes(c_im, 3, 4))],
                         axis=2).astype(bf16)
    lead = lb_re.shape[:2]
    lb = jnp.stack([lb_re.reshape(lead + (S5_STATE,)), lb_im.reshape(lead + (S5_STATE,))], axis=2)
    pr, pi = lb[:, :, 0], lb[:, :, 1]
    tables = {}
    for j in range(int(math.log2(max(n_steps_list))) + 1):
        if 2 ** j in n_steps_list:
            tables[2 ** j] = jnp.stack([pr, pi], axis=2)
        pr, pi = pr * pr - pi * pi, 2.0 * pr * pi
    return wb, wc, lb, tables, glu_w.astype(bf16)


def s5_mixer(su, x0r, x0i, s5p, s5_d, glu_b, layer):
    B, L, _ = su.shape
    wb, wc, lb, tables, glu_w = s5p
    chained = B % SUBLANES != 0
    if chained:
        n_groups, lanes, n_steps = B, 1, L // SUBLANES
        to_rows = lambda a: a.reshape(B, SUBLANES, n_steps, -1).transpose(0, 2, 1, 3).reshape(B, L, -1)
        from_rows = lambda a: a.reshape(B, n_steps, SUBLANES, -1).transpose(0, 2, 1, 3).reshape(B, L, -1)
        pw = tables[n_steps]
    else:
        n_groups, lanes, n_steps = B // SUBLANES, SUBLANES, L
        to_rows = lambda a: a.reshape(n_groups, SUBLANES, L, -1).transpose(0, 2, 1, 3).reshape(n_groups, -1, a.shape[-1])
        from_rows = lambda a: a.reshape(n_groups, L, SUBLANES, -1).transpose(0, 2, 1, 3).reshape(B, L, -1)
        pw = lb
    rows = n_steps * SUBLANES
    state_in = lambda a: a.reshape(n_groups, lanes, 2, S5_STATE).transpose(0, 2, 1, 3)
    state_out = lambda a: a.transpose(0, 2, 1, 3).reshape(B, 2, G_S5, P_S5)
    full = lambda *shape: pl.BlockSpec(shape, lambda b: (0,) * len(shape))
    per_b = lambda *shape: pl.BlockSpec((1,) + shape, lambda b: (b,) + (0,) * len(shape))
    per_layer = lambda *shape: pl.BlockSpec((1,) + shape, lambda b: (layer,) + (0,) * len(shape))
    y_p, xr, xi = pl.pallas_call(
        functools.partial(_s5_kernel, chained=chained),
        grid=(n_groups,),
        in_specs=[per_b(rows, W_GROUP), per_b(2, lanes, S5_STATE), per_b(2, lanes, S5_STATE),
                  per_layer(2, W_GROUP, 2 * S5_STATE), per_layer(2, 2 * S5_STATE, W_GROUP),
                  per_layer(2, 2, S5_STATE), per_layer(2, 2, S5_STATE),
                  full(1, W_GROUP), per_layer(W_GROUP, W_GROUP), full(1, W_GROUP)],
        out_specs=[per_b(rows, W_GROUP), per_b(2, lanes, S5_STATE), per_b(2, lanes, S5_STATE)],
        out_shape=[jax.ShapeDtypeStruct((n_groups, rows, W_GROUP), f32),
                   jax.ShapeDtypeStruct((n_groups, 2, lanes, S5_STATE), f32),
                   jax.ShapeDtypeStruct((n_groups, 2, lanes, S5_STATE), f32)],
        scratch_shapes=[pltpu.VMEM((rows, 2 * S5_STATE), f32)],
        compiler_params=pltpu.CompilerParams(dimension_semantics=("arbitrary",),
                                             vmem_limit_bytes=VMEM_LIMIT_BYTES),
        name="s5_mixer",
    )(to_rows(su), state_in(x0r), state_in(x0i), wb, wc, lb, pw,
      s5_d.reshape(1, W_GROUP), glu_w, glu_b.reshape(1, W_GROUP))
    return from_rows(y_p), state_out(xr), state_out(xi)


Q_TILE = 256
PROJ_BLOCKS = dict(mq=0, mk=1, mv=2, mo=3, su=4, rq=5, rk=6, rv=7, rg=8, dq=9, dk=10, dv=11)
GATE_LANES = 128
N_PROJ = 12 * W_GROUP + GATE_LANES


def _log_sigmoid(x):
    return jnp.minimum(x, 0.0) - jnp.log1p(jnp.exp(-jnp.abs(x)))


def _group_ones(width, group):
    shift = int(math.log2(group))
    r = lax.broadcasted_iota(jnp.int32, (width, width), 0) >> shift
    c = lax.broadcasted_iota(jnp.int32, (width, width), 1) >> shift
    return (r == c).astype(bf16)


def _split3(x):
    hi = x.astype(bf16)
    r = x - hi.astype(f32)
    mid = r.astype(bf16)
    return hi, mid, (r - mid.astype(f32)).astype(bf16)


def _group_mean(x, ones, group):
    return sum(jnp.dot(p, ones, preferred_element_type=f32) for p in _split3(x)) * (1.0 / group)


def _dot_nt(a, b):
    return lax.dot_general(a.astype(bf16), b.astype(bf16), (((1,), (1,)), ((), ())), preferred_element_type=f32)


def _dot(a, b):
    return jnp.dot(a.astype(bf16), b.astype(bf16), preferred_element_type=f32)


def _proj_block(name, rows):
    j = PROJ_BLOCKS[name]
    return pl.BlockSpec((1, rows, W_GROUP), lambda b, qi, j=j, rows=rows: (b, qi if rows == Q_TILE else 0, j))


def _mlstm_kernel(q_ref, k_ref, v_ref, o_ref, g_ref, gt_ref, gb_ref, gbt_ref, ng_ref, c0_ref, n0_ref, m0_ref,
                  h_ref, c_ref, n_ref, m_ref, gl_ref, gu_ref, rc_ref, vt_ref, ht_ref, *, seq_len, q_tile):
    L, TQ = seq_len, q_tile
    nq = L // TQ
    qi = pl.program_id(1)
    grow = gt_ref[0] + gbt_ref[...]

    @pl.when(qi == 0)
    def _():
        ss = lax.broadcasted_iota(jnp.int32, (L, L), 0)
        tt = lax.broadcasted_iota(jnp.int32, (L, L), 1)
        tri_le = (ss <= tt).astype(bf16)
        tri_ge = (ss >= tt).astype(bf16)
        rows = _split3(_log_sigmoid(grow))
        gl = sum(jnp.dot(p, tri_le, preferred_element_type=f32) for p in rows)
        gu = sum(jnp.dot(p, tri_ge, preferred_element_type=f32) for p in rows)
        for j in range(nq):
            gl_ref[j] = gl[:, j * TQ:(j + 1) * TQ]
            gu_ref[j] = gu[:, j * TQ:(j + 1) * TQ]
        gcol = g_ref[0] + gb_ref[...]
        cols = _split3(_log_sigmoid(gcol))
        glc = sum(jnp.dot(tri_ge, p, preferred_element_type=f32) for p in cols)
        guc = sum(jnp.dot(tri_le, p, preferred_element_type=f32) for p in cols)
        lane = lax.broadcasted_iota(jnp.int32, (L, GATE_LANES), 1)
        rc_ref[...] = pltpu.roll(gcol, 4, 1) - jnp.where(lane < 8, glc, guc)

        vt = v_ref[0].T
        kk = k_ref[0] * (DH_M ** -0.5)
        one_row = (lax.broadcasted_iota(jnp.int32, (DH_M, L), 0) == 0).astype(bf16)
        for h in range(H_M):
            hs = slice(h * DH_M, (h + 1) * DH_M)
            vt_ref[2 * h * DH_M:(2 * h + 1) * DH_M, :] = vt[hs, :].astype(bf16)
            vt_ref[(2 * h + 1) * DH_M:(2 * h + 2) * DH_M, :] = one_row
        for d in range(2):
            g_all = gl if d == 0 else gu
            for h in range(H_M):
                hs = slice(h * DH_M, (h + 1) * DH_M)
                ii, fi = 8 * d + h, 8 * d + 4 + h
                g_row = g_all[fi:fi + 1, :]
                g_tot = g_row[:, L - 1:L] if d == 0 else g_row[:, 0:1]
                wlog = g_tot - g_row + grow[ii:ii + 1, :]
                m0 = m0_ref[0, d:d + 1, h:h + 1]
                m_new = jnp.maximum(g_tot + m0, jnp.max(wlog, axis=1, keepdims=True))
                decay = jnp.exp(g_tot + m0 - m_new)
                w = jnp.exp(wlog - m_new)
                kh = kk[:, hs]
                c_ref[0, d, h] = decay * c0_ref[0, d, h] + _dot(vt[hs, :] * w, kh)
                n_upd = jnp.dot(jnp.broadcast_to(w, (SUBLANES, L)), kh, precision=HIGHEST,
                                preferred_element_type=f32)[0:1, :]
                n_ref[0, d, h:h + 1, :] = decay * n0_ref[0, d, h:h + 1, :] + n_upd
                m_ref[0, d:d + 1, h:h + 1] = m_new

    gl_t = gl_ref[qi]
    gu_t = gu_ref[qi]
    s_idx = lax.broadcasted_iota(jnp.int32, (L, TQ), 0)
    t_idx = qi * TQ + lax.broadcasted_iota(jnp.int32, (L, TQ), 1)
    low = s_idx <= t_idx
    upp = s_idx >= t_idx
    qt = q_ref[0].T.astype(bf16)
    k = (k_ref[0] * (DH_M ** -0.5)).astype(bf16)
    rc = rc_ref[...]
    row0 = lax.broadcasted_iota(jnp.int32, (DH_M, DH_M), 0) == 0
    for h in range(H_M):
        hs = slice(h * DH_M, (h + 1) * DH_M)
        qth = qt[hs, :]
        vta = vt_ref[2 * h * DH_M:(2 * h + 2) * DH_M, :]
        s0 = jnp.dot(k[:, hs], qth, preferred_element_type=f32)
        h_sum = None
        for d in range(2):
            fi = 8 * d + 4 + h
            g_t = (gl_t if d == 0 else gu_t)[fi:fi + 1, :]
            dlog = jnp.where(low if d == 0 else upp, rc[:, fi:fi + 1] + g_t, NEG_INF)
            inter = g_t + m0_ref[0, d:d + 1, h:h + 1]
            m_t = jnp.maximum(inter, jnp.max(dlog, axis=0, keepdims=True))
            p = s0 * jnp.exp(dlog - m_t)
            a = jnp.exp(inter - m_t)
            c0n0 = jnp.concatenate([c0_ref[0, d, h], jnp.where(row0, n0_ref[0, d, h:h + 1, :], 0.0)], axis=0)
            numden = (jnp.dot(vta, p.astype(bf16), preferred_element_type=f32)
                      + a * jnp.dot(c0n0.astype(bf16), qth, preferred_element_type=f32))
            scale = 1.0 / jnp.maximum(jnp.abs(numden[DH_M:DH_M + 1, :]), jnp.exp(-m_t))
            hd = numden[0:DH_M, :] * scale
            h_sum = hd if h_sum is None else h_sum + hd
        ht_ref[hs, :] = h_sum * lax.rsqrt(jnp.mean(h_sum * h_sum, axis=0, keepdims=True) + EPS)
    h_ref[0] = jax.nn.sigmoid(o_ref[0]) * (ht_ref[...].T * ng_ref[...])


def mlstm_mixer(proj, gates_t, c0, n0, m0, lp):
    B, L, _ = proj.shape
    TQ = min(L, Q_TILE)
    gb = lp['mlstm_gate_b'].reshape(1, 4 * H_M)
    const = lambda *shape: pl.BlockSpec(shape, lambda b, qi: (0,) * len(shape))
    per_b = lambda *shape: pl.BlockSpec((1,) + shape, lambda b, qi: (b,) + (0,) * len(shape))
    return pl.pallas_call(
        functools.partial(_mlstm_kernel, seq_len=L, q_tile=TQ),
        grid=(B, L // TQ),
        in_specs=[_proj_block('mq', TQ), _proj_block('mk', L), _proj_block('mv', L), _proj_block('mo', TQ),
                  pl.BlockSpec((1, L, GATE_LANES), lambda b, qi: (b, 0, 12 * W_GROUP // GATE_LANES)),
                  per_b(4 * H_M, L), const(1, GATE_LANES), const(4 * H_M, 1), const(1, W_GROUP),
                  per_b(2, H_M, DH_M, DH_M), per_b(2, H_M, DH_M), per_b(2, H_M)],
        out_specs=[pl.BlockSpec((1, TQ, W_GROUP), lambda b, qi: (b, qi, 0)),
                   per_b(2, H_M, DH_M, DH_M), per_b(2, H_M, DH_M), per_b(2, H_M)],
        out_shape=[jax.ShapeDtypeStruct((B, L, W_GROUP), f32),
                   jax.ShapeDtypeStruct((B, 2, H_M, DH_M, DH_M), f32),
                   jax.ShapeDtypeStruct((B, 2, H_M, DH_M), f32),
                   jax.ShapeDtypeStruct((B, 2, H_M), f32)],
        scratch_shapes=[pltpu.VMEM((L // TQ, 4 * H_M, TQ), f32), pltpu.VMEM((L // TQ, 4 * H_M, TQ), f32),
                        pltpu.VMEM((L, GATE_LANES), f32), pltpu.VMEM((2 * W_GROUP, L), bf16),
                        pltpu.VMEM((W_GROUP, TQ), f32)],
        compiler_params=pltpu.CompilerParams(dimension_semantics=("arbitrary", "arbitrary"),
                                             vmem_limit_bytes=VMEM_LIMIT_BYTES),
        name="mlstm_mixer",
    )(proj, proj, proj, proj, proj, gates_t, jnp.pad(gb, ((0, 0), (0, GATE_LANES - 4 * H_M))),
      gb.reshape(4 * H_M, 1), lp['mlstm_norm_g'].reshape(1, W_GROUP), c0, n0, m0)


def _retention_kernel(lg_ref, q_ref, k_ref, v_ref, g_ref, gn_ref, r0_ref, h_ref, r_ref, *, seq_len, q_tile):
    L, TQ = seq_len, q_tile
    qi = pl.program_id(1)
    t_col = qi * TQ + lax.broadcasted_iota(jnp.int32, (TQ, 1), 0)
    rel = (qi * TQ + lax.broadcasted_iota(jnp.int32, (TQ, L), 0)
           - lax.broadcasted_iota(jnp.int32, (TQ, L), 1)).astype(f32)
    q = q_ref[0]
    k = k_ref[0] * (DH_R ** -0.5)
    v = v_ref[0]
    ones = _group_ones(W_GROUP, DH_R)
    for h in range(H_R):
        hs = slice(h * DH_R, (h + 1) * DH_R)
        lgf, lgb = lg_ref[0, h], lg_ref[1, h]
        qh, kh, vh = q[:, hs], k[:, hs], v[:, hs]
        decay = jnp.where(rel > 0.0, jnp.exp(lgf * jnp.maximum(rel, 0.0)),
                          jnp.where(rel < 0.0, jnp.exp(lgb * jnp.maximum(-rel, 0.0)), 2.0))
        o = _dot(_dot_nt(qh, kh) * decay, vh)
        xi_f = jnp.exp(lgf * (t_col + 1).astype(f32))
        xi_b = jnp.exp(lgb * (L - t_col).astype(f32))
        o = o + xi_f * _dot(qh, r0_ref[0, 0, h]) + xi_b * _dot(qh, r0_ref[0, 1, h])
        h_ref[0, :, hs] = o
    o = h_ref[0]
    oc = o - _group_mean(o, ones, DH_R)
    y = oc * lax.rsqrt(_group_mean(oc * oc, ones, DH_R) + EPS) * gn_ref[...]
    h_ref[0] = y * jax.nn.silu(g_ref[0])

    @pl.when(qi == 0)
    def _():
        kt = k.T
        s_row = lax.broadcasted_iota(jnp.int32, (1, L), 1).astype(f32)
        for d in range(2):
            for h in range(H_R):
                hs = slice(h * DH_R, (h + 1) * DH_R)
                lg = lg_ref[d, h]
                zeta = jnp.exp(lg * ((L - 1.0) - s_row)) if d == 0 else jnp.exp(lg * s_row)
                r_ref[0, d, h] = jnp.exp(lg * float(L)) * r0_ref[0, d, h] + _dot(kt[hs, :] * zeta, v[:, hs])


def retention_mixer(proj, r0, lp):
    B, L, _ = proj.shape
    TQ = min(L, Q_TILE)
    log_gamma = -jnp.exp(lp['ret_decay'])
    per_b = lambda *shape: pl.BlockSpec((1,) + shape, lambda b, qi: (b,) + (0,) * len(shape))
    return pl.pallas_call(
        functools.partial(_retention_kernel, seq_len=L, q_tile=TQ),
        grid=(B, L // TQ),
        in_specs=[pl.BlockSpec(memory_space=pltpu.SMEM),
                  _proj_block('rq', TQ), _proj_block('rk', L), _proj_block('rv', L), _proj_block('rg', TQ),
                  pl.BlockSpec((1, W_GROUP), lambda b, qi: (0, 0)), per_b(2, H_R, DH_R, DH_R)],
        out_specs=[pl.BlockSpec((1, TQ, W_GROUP), lambda b, qi: (b, qi, 0)), per_b(2, H_R, DH_R, DH_R)],
        out_shape=[jax.ShapeDtypeStruct((B, L, W_GROUP), f32),
                   jax.ShapeDtypeStruct((B, 2, H_R, DH_R, DH_R), f32)],
        compiler_params=pltpu.CompilerParams(dimension_semantics=("arbitrary", "arbitrary"),
                                             vmem_limit_bytes=VMEM_LIMIT_BYTES),
        name="retention_mixer",
    )(log_gamma, proj, proj, proj, proj, lp['ret_gn_g'].reshape(1, W_GROUP), r0)


def _rope_tables(L):
    half = DH_D // 2
    freqs = ROPE_BASE ** (-np.arange(0, half, 2, dtype=np.float64) / half)
    pos = np.arange(L)
    row, col = (pos // GRID_W).astype(np.float64), (pos % GRID_W).astype(np.float64)
    ang = np.concatenate([np.tile(row[:, None] * freqs, (1, 2)), np.tile(col[:, None] * freqs, (1, 2))], axis=1)
    sign = np.tile(np.concatenate([-np.ones(half // 2), np.ones(half // 2)]), 2)
    cos = np.tile(np.cos(ang), (1, 2 * H_D))
    sin = np.tile(np.sin(ang) * sign, (1, 2 * H_D))
    return jnp.asarray(cos, f32), jnp.asarray(sin, f32)


def _swap_pairs(x):
    parts = []
    for j in range(x.shape[1] // 128):
        xs = x[:, j * 128:(j + 1) * 128]
        lane = lax.broadcasted_iota(jnp.int32, xs.shape, 1)
        parts.append(jnp.where((lane & 15) < 8, pltpu.roll(xs, 120, 1), pltpu.roll(xs, 8, 1)))
    return jnp.concatenate(parts, axis=1)


def _qk_norm(x, gain, ones):
    return x * lax.rsqrt(_group_mean(x * x, ones, DH_D) + EPS) * gain


def _diff_attn_kernel(lam_ref, q_ref, k_ref, v_ref, qkg_ref, sg_ref, *rest, seq_len, q_tile, past_len, out_scale):
    L, TQ, P = seq_len, q_tile, past_len
    if P:
        kc_ref, vc_ref, cos_ref, sin_ref, h_ref, ka_ref, vt_ref, ot_ref = rest
    else:
        h_ref, kn_ref, ka_ref, vt_ref, ot_ref = rest
    qi = pl.program_id(1)
    ones = _group_ones(W_GROUP, DH_D)

    @pl.when(qi == 0)
    def _():
        kn = _qk_norm(k_ref[0], qkg_ref[1:2, :], ones)
        if P:
            kn = kn * cos_ref[...] + _swap_pairs(kn) * sin_ref[...]
            ka_ref[0:P, :] = kc_ref[0, 0].astype(bf16)
            vt_ref[:, 0:P] = vc_ref[0, 0].T.astype(bf16)
        else:
            kn_ref[0] = kn
        ka_ref[P:P + L, :] = kn.astype(bf16)
        vt_ref[:, P:P + L] = v_ref[0].T.astype(bf16)

    qn = _qk_norm(q_ref[0], qkg_ref[0:1, :], ones)
    if P:
        rows = pl.ds(pl.multiple_of(qi * TQ, TQ), TQ)
        qn = qn * cos_ref[rows, :] + _swap_pairs(qn) * sin_ref[rows, :]
    qt = (qn * (DH_D ** -0.5)).T.astype(bf16)
    lam = lam_ref[0, 0]
    ka = ka_ref[...]
    for h in range(H_D):
        probs = []
        for j in range(2):
            cs = slice((2 * h + j) * DH_D, (2 * h + j + 1) * DH_D)
            s = jnp.dot(ka[:, cs], qt[cs, :], preferred_element_type=f32)
            e = jnp.exp(s - jnp.max(s, axis=0, keepdims=True))
            probs.append(e * (1.0 / jnp.sum(e, axis=0, keepdims=True)))
        vs = slice(h * 2 * DH_D, (h + 1) * 2 * DH_D)
        ot_ref[vs, :] = jnp.dot(vt_ref[vs, :], (probs[0] - lam * probs[1]).astype(bf16), preferred_element_type=f32)
    o = ot_ref[...].T
    ones_v = _group_ones(W_GROUP, 2 * DH_D)
    h_ref[0] = o * lax.rsqrt(_group_mean(o * o, ones_v, 2 * DH_D) + EPS) * (sg_ref[...] * out_scale)


def diff_attn_mixer(proj, cache, lp, lam_init, layer):
    B, L, _ = proj.shape
    TQ = min(L, Q_TILE)
    lv = lp['diff_lambda']
    lam = (jnp.exp(jnp.sum(lv[0] * lv[1])) - jnp.exp(jnp.sum(lv[2] * lv[3])) + lam_init).reshape(1, 1)
    qkg = jnp.tile(lp['diff_qk_norm'], (1, 2 * H_D))
    sg = jnp.tile(lp['diff_subln_g'], (H_D,)).reshape(1, W_GROUP)
    const = lambda *shape: pl.BlockSpec(shape, lambda b, qi: (0,) * len(shape))
    in_specs = [pl.BlockSpec(memory_space=pltpu.SMEM),
                _proj_block('dq', TQ), _proj_block('dk', L), _proj_block('dv', L), const(2, W_GROUP), const(1, W_GROUP)]
    args = [lam, proj, proj, proj, qkg, sg]
    out_specs = [pl.BlockSpec((1, TQ, W_GROUP), lambda b, qi: (b, qi, 0))]
    out_shape = [jax.ShapeDtypeStruct((B, L, W_GROUP), f32)]
    P = 0
    if cache is not None:
        ck, cv = cache
        P = ck.shape[2]
        cspec = pl.BlockSpec((1, 1, P, W_GROUP), lambda b, qi, layer=layer: (b, layer, 0, 0))
        cos, sin = _rope_tables(L)
        in_specs += [cspec, cspec, const(L, W_GROUP), const(L, W_GROUP)]
        args += [ck, cv, cos, sin]
    else:
        out_specs.append(pl.BlockSpec((1, L, W_GROUP), lambda b, qi: (b, 0, 0)))
        out_shape.append(jax.ShapeDtypeStruct((B, L, W_GROUP), f32))
    return pl.pallas_call(
        functools.partial(_diff_attn_kernel, seq_len=L, q_tile=TQ, past_len=P, out_scale=1.0 - lam_init),
        grid=(B, L // TQ),
        in_specs=in_specs, out_specs=out_specs, out_shape=out_shape,
        scratch_shapes=[pltpu.VMEM((P + L, W_GROUP), bf16), pltpu.VMEM((W_GROUP, P + L), bf16),
                        pltpu.VMEM((W_GROUP, TQ), f32)],
        compiler_params=pltpu.CompilerParams(dimension_semantics=("arbitrary", "arbitrary"),
                                             vmem_limit_bytes=VMEM_LIMIT_BYTES),
        name="diff_attention",
    )(*args)


PROJ_ROW_TILE = 512
OUT_ROW_TILE = 512
FF_TILE = 512
D_FF = 2 * D_MODEL


def _in_proj_kernel(x_ref, g_ref, sc_ref, sh_ref, w_ref, o_ref):
    x = x_ref[...]
    y = x * lax.rsqrt(jnp.mean(x * x, axis=1, keepdims=True) + EPS) * g_ref[...]
    h = (y * (1.0 + sc_ref[0]) + sh_ref[0]).astype(bf16)
    o_ref[...] = jnp.dot(h, w_ref[0], preferred_element_type=f32)


def in_projection(x, gain, scale, shift, w_p, layer, rows_per_mod):
    n, D = x.shape
    TM = min(n, PROJ_ROW_TILE)
    mod = pl.BlockSpec((1, 1, D), lambda i: (i * TM // rows_per_mod, 0, 0))
    return pl.pallas_call(
        _in_proj_kernel,
        grid=(n // TM,),
        in_specs=[pl.BlockSpec((TM, D), lambda i: (i, 0)), pl.BlockSpec((1, D), lambda i: (0, 0)), mod, mod,
                  pl.BlockSpec((1, D, N_PROJ), lambda i: (layer, 0, 0))],
        out_specs=pl.BlockSpec((TM, N_PROJ), lambda i: (i, 0)),
        out_shape=jax.ShapeDtypeStruct((n, N_PROJ), f32),
        compiler_params=pltpu.CompilerParams(dimension_semantics=("arbitrary",),
                                             vmem_limit_bytes=VMEM_LIMIT_BYTES),
        name="in_projection",
    )(x, gain.reshape(1, D), scale, shift, w_p)


def _out_proj_kernel(x_ref, m0_ref, m1_ref, m2_ref, m3_ref, w_ref, g1_ref, ng_ref, sc_ref, sh_ref, rw_ref,
                     xo_ref, h_ref, aff_ref, wb_ref):
    @pl.when(pl.program_id(0) == 0)
    def _():
        wb_ref[...] = w_ref[0].astype(bf16)

    out = None
    for j, m_ref in enumerate((m0_ref, m1_ref, m2_ref, m3_ref)):
        part = jnp.dot(m_ref[...].astype(bf16), wb_ref[j * W_GROUP:(j + 1) * W_GROUP, :], preferred_element_type=f32)
        out = part if out is None else out + part
    x = x_ref[...] + g1_ref[0] * out
    xo_ref[...] = x
    h = x * lax.rsqrt(jnp.mean(x * x, axis=1, keepdims=True) + EPS) * ng_ref[...]
    h = h * (1.0 + sc_ref[0]) + sh_ref[0]
    h_ref[...] = h
    h_hi = h.astype(bf16)
    h_lo = (h - h_hi.astype(f32)).astype(bf16)
    both = jnp.dot(h_hi, rw_ref[0], preferred_element_type=f32)
    logits = (both[:, 0:N_EXPERTS] + both[:, N_EXPERTS:2 * N_EXPERTS]
              + jnp.dot(h_lo, rw_ref[0], preferred_element_type=f32)[:, 0:N_EXPERTS])
    e = jnp.exp(logits - jnp.max(logits, axis=1, keepdims=True))
    aff_ref[...] = e / jnp.sum(e, axis=1, keepdims=True)


def out_projection(x, mixed, w_out, gate1, gain2, scale2, shift2, router_w2, layer, rows_per_mod):
    n, D = x.shape
    TM = OUT_ROW_TILE
    row = lambda width: pl.BlockSpec((TM, width), lambda i: (i, 0))
    const = lambda *shape: pl.BlockSpec(shape, lambda i: (0,) * len(shape))
    mod = pl.BlockSpec((1, 1, D), lambda i: (i * TM // rows_per_mod, 0, 0))
    return pl.pallas_call(
        _out_proj_kernel,
        grid=(n // TM,),
        in_specs=[row(D), row(W_GROUP), row(W_GROUP), row(W_GROUP), row(W_GROUP),
                  pl.BlockSpec((1, D, D), lambda i: (layer, 0, 0)), mod, const(1, D), mod, mod,
                  pl.BlockSpec((1, D, 2 * N_EXPERTS), lambda i: (layer, 0, 0))],
        out_specs=[row(D), row(D), row(N_EXPERTS)],
        out_shape=[jax.ShapeDtypeStruct((n, D), f32), jax.ShapeDtypeStruct((n, D), f32),
                   jax.ShapeDtypeStruct((n, N_EXPERTS), f32)],
        scratch_shapes=[pltpu.VMEM((D, D), bf16)],
        compiler_params=pltpu.CompilerParams(dimension_semantics=("arbitrary",), vmem_limit_bytes=VMEM_LIMIT_BYTES),
        name="out_projection",
    )(x, *mixed, w_out, gate1, gain2.reshape(1, D), scale2, shift2, router_w2)


def _experts_kernel(xc_ref, xl_ref, gc_ref, gl_ref, g2_ref, g2l_ref, il_ref, wg_ref, wu_ref, wd_ref, yc_ref, yl_ref,
                    ac_ref, al_ref, *, lat_len):
    f = pl.program_id(1)
    wg = wg_ref[0, 0].astype(bf16)
    wu = wu_ref[0, 0].astype(bf16)
    wd = wd_ref[0, 0].astype(bf16)
    for x_ref, acc_ref in ((xc_ref, ac_ref), (xl_ref, al_ref)):
        x = x_ref[0].astype(bf16)
        hidden = jax.nn.silu(jnp.dot(x, wg, preferred_element_type=f32)) * jnp.dot(x, wu, preferred_element_type=f32)
        part = jnp.dot(hidden.astype(bf16), wd, preferred_element_type=f32)

        @pl.when(f == 0)
        def _(acc_ref=acc_ref, part=part):
            acc_ref[...] = part

        @pl.when(f > 0)
        def _(acc_ref=acc_ref, part=part):
            acc_ref[...] += part

    @pl.when(f == pl.num_programs(1) - 1)
    def _():
        yc_ref[0] = ac_ref[...] * gc_ref[0] * g2_ref[0]
        gate = g2l_ref[0]
        for b in range(1, g2l_ref.shape[0]):
            gate = jnp.where(il_ref[0] >= b * lat_len, g2l_ref[b], gate)
        yl_ref[0] = al_ref[...] * gl_ref[0] * gate


def expert_ffn(xe_c, xe_l, g_c, g_l, gate2_c, gate2_l, idx_l, lat_len, w_gate, w_up, w_down, layer):
    E, Cc, D = xe_c.shape
    Cl = xe_l.shape[1]
    tok = lambda C, width: pl.BlockSpec((1, C, width), lambda e, f: (e, 0, 0))
    return pl.pallas_call(
        functools.partial(_experts_kernel, lat_len=lat_len),
        grid=(E, D_FF // FF_TILE),
        in_specs=[tok(Cc, D), tok(Cl, D), tok(Cc, 1), tok(Cl, 1), pl.BlockSpec((1, 1, D), lambda e, f: (0, 0, 0)),
                  pl.BlockSpec(gate2_l.shape, lambda e, f: (0, 0, 0)), tok(Cl, 1),
                  pl.BlockSpec((1, 1, D, FF_TILE), lambda e, f: (layer, e, 0, f)),
                  pl.BlockSpec((1, 1, D, FF_TILE), lambda e, f: (layer, e, 0, f)),
                  pl.BlockSpec((1, 1, FF_TILE, D), lambda e, f: (layer, e, f, 0))],
        out_specs=[tok(Cc, D), tok(Cl, D)],
        out_shape=[jax.ShapeDtypeStruct((E, Cc, D), f32), jax.ShapeDtypeStruct((E, Cl, D), f32)],
        scratch_shapes=[pltpu.VMEM((Cc, D), f32), pltpu.VMEM((Cl, D), f32)],
        compiler_params=pltpu.CompilerParams(dimension_semantics=("arbitrary", "arbitrary"),
                                             vmem_limit_bytes=VMEM_LIMIT_BYTES),
        name="expert_ffn",
    )(xe_c, xe_l, g_c, g_l, gate2_c, gate2_l, idx_l, w_gate, w_up, w_down)


def _permute_w_in(w):
    gate0 = 4 * W_GROUP
    pad = jnp.zeros(w.shape[:-1] + (GATE_LANES - 4 * H_M,), w.dtype)
    return jnp.concatenate([w[..., :gate0], w[..., gate0 + 4 * H_M:], w[..., gate0:gate0 + 4 * H_M], pad],
                           axis=-1).astype(bf16)


def token_mixers(proj, lp, s5p, lam_init, states, cache, layer):
    B, L, _ = proj.shape
    mC0, mn0, mm0, s5r0, s5i0, R0 = states
    gates_t = jnp.swapaxes(proj[:, :, 12 * W_GROUP:12 * W_GROUP + 4 * H_M], 1, 2)
    hm, mC, mn, mm = mlstm_mixer(proj, gates_t, mC0, mn0, mm0, lp)
    su = proj[:, :, PROJ_BLOCKS['su'] * W_GROUP:(PROJ_BLOCKS['su'] + 1) * W_GROUP]
    ys, s5r, s5i = s5_mixer(su, s5r0, s5i0, s5p, lp['s5_d'], lp['s5_glu_b'], layer)
    hr, R = retention_mixer(proj, R0, lp)
    attn = diff_attn_mixer(proj, cache, lp, lam_init, layer)
    mixed = [a.reshape(B * L, W_GROUP) for a in (hm, ys, hr, attn[0])]
    new_ctx = None
    if cache is None:
        v = proj[:, :, PROJ_BLOCKS['dv'] * W_GROUP:(PROJ_BLOCKS['dv'] + 1) * W_GROUP]
        new_ctx = (mC, mn, mm, s5r, s5i, R, attn[1].reshape(B, L, 2 * H_D, DH_D), v.reshape(B, L, H_D, 2 * DH_D))
    return mixed, new_ctx


GATHER_WINDOW = 32
INDEX_LANES = 128


def gather_rows(x, idx):
    num = idx.shape[0]
    width = x.shape[1]
    mesh = plsc.VectorSubcoreMesh(core_axis_name="core", subcore_axis_name="subcore")
    per_core = num // GATHER_WINDOW // mesh.num_cores
    idx_rows = jnp.pad(idx.reshape(num // GATHER_WINDOW, GATHER_WINDOW), ((0, 0), (0, INDEX_LANES - GATHER_WINDOW)))

    @pl.kernel(out_type=jax.ShapeDtypeStruct((num, width), x.dtype), mesh=mesh)
    def gather_kernel(x_hbm, i_hbm, o_hbm):
        base = lax.axis_index("core") * per_core

        def body(i_vmem, o_vmem):
            pltpu.sync_copy(x_hbm.at[i_vmem.at[0, pl.ds(0, GATHER_WINDOW)]], o_vmem)

        pltpu.emit_pipeline(
            body,
            grid=(per_core,),
            in_specs=[pl.BlockSpec((1, INDEX_LANES), index_map=lambda i: (base + i, 0))],
            out_specs=[pl.BlockSpec((GATHER_WINDOW, width), index_map=lambda i: (base + i, 0))],
            core_axis_name="subcore",
            dimension_semantics=(pltpu.PARALLEL,),
        )(i_hbm, o_hbm)

    return gather_kernel(x, idx_rows)


SC_LANES = 16


def _threshold_kernel(aff_ref, t_ref, r_ref, *, cap):
    bits = pltpu.bitcast(aff_ref[...], jnp.int32)
    t = jnp.zeros((N_EXPERTS, 1), jnp.int32)
    for b in range(30, -1, -1):
        cand = t | (1 << b)
        cnt = jnp.sum((bits >= cand).astype(f32), axis=1, keepdims=True)
        t = jnp.where(cnt >= cap, cand, t)
    n_gt = jnp.sum((bits > t).astype(f32), axis=1, keepdims=True)
    t_ref[...] = jnp.broadcast_to(t, (N_EXPERTS, GATE_LANES))
    r_ref[...] = jnp.broadcast_to(cap - n_gt.astype(jnp.int32), (N_EXPERTS, GATE_LANES))


def expert_choice_select(aff_t, cap):
    n = aff_t.shape[1]
    t_bits, n_ties = pl.pallas_call(
        functools.partial(_threshold_kernel, cap=cap),
        out_shape=[jax.ShapeDtypeStruct((N_EXPERTS, GATE_LANES), jnp.int32)] * 2,
        name="select_threshold",
    )(aff_t)
    mesh = plsc.VectorSubcoreMesh(core_axis_name="core", subcore_axis_name="subcore")
    params = pltpu.CompilerParams()
    if "needs_layout_passes" in pltpu.CompilerParams.__dataclass_fields__:
        params = dataclasses.replace(params, needs_layout_passes=False)

    @pl.kernel(out_type=[jax.ShapeDtypeStruct((N_EXPERTS, cap), jnp.int32), jax.ShapeDtypeStruct((N_EXPERTS, cap), f32)],
               mesh=mesh, compiler_params=params,
               scratch_types=[pltpu.VMEM((n,), f32), pltpu.VMEM((cap + SC_LANES,), jnp.int32),
                              pltpu.VMEM((cap + SC_LANES,), f32), pltpu.VMEM((SC_LANES,), jnp.int32),
                              pltpu.VMEM((SC_LANES,), jnp.int32)])
    def select_kernel(aff_hbm, t_hbm, r_hbm, idx_hbm, g_hbm, row, ibuf, gbuf, tv, rv):
        e = lax.axis_index("subcore")

        @pl.when(lax.axis_index("core") == 0)
        def _():
            pltpu.sync_copy(aff_hbm.at[e], row)
            pltpu.sync_copy(t_hbm.at[e], tv)
            pltpu.sync_copy(r_hbm.at[e], rv)
            t = tv[...]
            r = rv[...]
            lane = lax.iota(jnp.int32, SC_LANES)

            def body(j, carry):
                off, seen_eq = carry
                v = row[pl.ds(j * SC_LANES, SC_LANES)]
                b = plsc.bitcast(v, jnp.int32)
                eq = b == t
                eq_i = eq.astype(jnp.int32)
                take = (b > t) | (eq & (seen_eq + plsc.cumsum(eq_i) <= r))
                plsc.store_compressed(ibuf.at[pl.ds(off, SC_LANES)], lane + j * SC_LANES, mask=take)
                plsc.store_compressed(gbuf.at[pl.ds(off, SC_LANES)], v, mask=take)
                return off + jnp.sum(take.astype(jnp.int32)), seen_eq + jnp.sum(eq_i)

            lax.fori_loop(0, n // SC_LANES, body, (jnp.int32(0), jnp.int32(0)))
            pltpu.sync_copy(ibuf.at[pl.ds(0, cap)], idx_hbm.at[e])
            pltpu.sync_copy(gbuf.at[pl.ds(0, cap)], g_hbm.at[e])

    idx, gates = select_kernel(aff_t, t_bits[:, :SC_LANES], n_ties[:, :SC_LANES])
    return idx, gates


def _route(aff, h2):
    n = aff.shape[0]
    idx, gates = expert_choice_select(aff.T, CAPACITY_FACTOR * n // N_EXPERTS)
    return gates[..., None], idx, gather_rows(h2, idx.reshape(-1)).reshape(idx.shape + h2.shape[1:])


SCATTER_COLS = 128
SCATTER_WINDOW = 128


def scatter_add_rows(x, ye, idx):
    n, D = x.shape
    R = ye.shape[0]
    mesh = plsc.VectorSubcoreMesh(core_axis_name="core", subcore_axis_name="subcore")
    rows = n // mesh.num_subcores
    wins = R // SCATTER_WINDOW // mesh.num_subcores

    @pl.kernel(out_type=jax.ShapeDtypeStruct((n, D), f32), mesh=mesh,
               scratch_types=[pltpu.VMEM_SHARED((n, SCATTER_COLS), f32), pltpu.VMEM((SCATTER_WINDOW, SCATTER_COLS), f32),
                              pltpu.VMEM((SCATTER_WINDOW,), jnp.int32)])
    def scatter_kernel(x_hbm, ye_hbm, i_hbm, o_hbm, shared, buf, ibuf):
        core = lax.axis_index("core")
        sid = lax.axis_index("subcore")
        r0 = pl.multiple_of(sid * rows, SUBLANES)
        for slab in range(D // SCATTER_COLS):
            cols = pl.ds(slab * SCATTER_COLS, SCATTER_COLS)

            @pl.when(core == slab % mesh.num_cores)
            def _(cols=cols):
                pltpu.sync_copy(x_hbm.at[pl.ds(r0, rows), cols], shared.at[pl.ds(r0, rows)])
                plsc.subcore_barrier()

                @pl.loop(0, wins)
                def _(w):
                    win = w * mesh.num_subcores + sid
                    pltpu.sync_copy(i_hbm.at[win], ibuf)
                    pltpu.sync_copy(ye_hbm.at[pl.ds(pl.multiple_of(win * SCATTER_WINDOW, SCATTER_WINDOW),
                                                    SCATTER_WINDOW), cols], buf)
                    pltpu.sync_copy(buf, shared.at[ibuf], add=True)

                plsc.subcore_barrier()
                pltpu.sync_copy(shared.at[pl.ds(r0, rows)], o_hbm.at[pl.ds(r0, rows), cols])
                plsc.subcore_barrier()

    return scatter_kernel(x, ye, idx.reshape(R // SCATTER_WINDOW, SCATTER_WINDOW))


PER_LAYER = ('norm1_g', 'norm2_g', 'mlstm_gate_b', 'mlstm_norm_g', 's5_d', 's5_glu_b', 'ret_decay',
             'ret_gn_g', 'diff_qk_norm', 'diff_lambda', 'diff_subln_g')


def kernel(x_prompt, x_sample, state_mlstm_c, state_mlstm_n, state_mlstm_m, state_s5_re, state_s5_im, state_ret, cache_diff_k, cache_diff_v, c, c_ctx, norm1_g, norm2_g, ada_w, ada_b, w_in, w_out, mlstm_gate_b, mlstm_norm_g, s5_lambda_re, s5_lambda_im, s5_log_step, s5_b_re, s5_b_im, s5_c_re, s5_c_im, s5_d, s5_glu_w, s5_glu_b, ret_decay, ret_gn_g, diff_qk_norm, diff_lambda, diff_subln_g, router_w, exp_w_gate, exp_w_up, exp_w_down):
    weights = dict(norm1_g=norm1_g, norm2_g=norm2_g, mlstm_gate_b=mlstm_gate_b,
                   mlstm_norm_g=mlstm_norm_g, s5_d=s5_d, s5_glu_b=s5_glu_b, ret_decay=ret_decay, ret_gn_g=ret_gn_g,
                   diff_qk_norm=diff_qk_norm, diff_lambda=diff_lambda, diff_subln_g=diff_subln_g)
    w_in_p = _permute_w_in(w_in)
    rw_hi = router_w.astype(bf16)
    router_w2 = jnp.concatenate([rw_hi, (router_w - rw_hi.astype(f32)).astype(bf16)], axis=-1)
    s5p = _s5_prepare(s5_lambda_re, s5_lambda_im, s5_log_step, s5_b_re, s5_b_im, s5_c_re, s5_c_im, s5_glu_w,
                      (x_prompt.shape[1] // SUBLANES, x_sample.shape[1] // SUBLANES))
    Bc, Lc, D = x_prompt.shape
    Bl, Ll, _ = x_sample.shape
    xc = x_prompt.reshape(Bc * Lc, D)
    xl = x_sample.reshape(Bl * Ll, D)
    zero_states = (jnp.zeros((Bc, 2, H_M, DH_M, DH_M), f32), jnp.zeros((Bc, 2, H_M, DH_M), f32),
                   jnp.zeros((Bc, 2, H_M), f32), jnp.zeros((Bc, 2, G_S5, P_S5), f32),
                   jnp.zeros((Bc, 2, G_S5, P_S5), f32), jnp.zeros((Bc, 2, H_R, DH_R, DH_R), f32))
    cache = (cache_diff_k.reshape(cache_diff_k.shape[:3] + (W_GROUP,)),
             cache_diff_v.reshape(cache_diff_v.shape[:3] + (W_GROUP,)))
    cvec = jnp.concatenate([c_ctx[None, :], c], axis=0)
    outs = [[] for _ in range(8)]
    for l in range(DEPTH):
        lp = {name: weights[name][l] for name in PER_LAYER}
        lam_init = 0.8 - 0.6 * math.exp(-0.3 * l)
        mods = jnp.split((jax.nn.silu(cvec) @ ada_w[l] + ada_b[l])[:, None, :], 6, axis=-1)
        lat_states = (state_mlstm_c[:, l], state_mlstm_n[:, l], state_mlstm_m[:, l], state_s5_re[:, l],
                      state_s5_im[:, l], state_ret[:, l])
        def mix(x, B, L, sel, states, kv):
            sh1, sc1 = mods[0][sel], mods[1][sel]
            proj = in_projection(x, lp['norm1_g'], sc1, sh1, w_in_p, l, x.shape[0] // sh1.shape[0]).reshape(B, L, N_PROJ)
            return token_mixers(proj, lp, s5p, lam_init, states, kv, l)

        def project_and_route(x, mixed, sel):
            g1, sh2, sc2, g2 = (m[sel] for m in mods[2:])
            x1, h2, aff = out_projection(x, mixed, w_out, g1, lp['norm2_g'], sc2, sh2, router_w2, l,
                                         x.shape[0] // g1.shape[0])
            return (x1, g2, g1.shape[0]) + _route(aff, h2)

        sel_c, sel_l = slice(0, 1), slice(1, 1 + Bl)
        mixed_l, _ = mix(xl, Bl, Ll, sel_l, lat_states, cache)
        xc, mixed_l = lax.optimization_barrier((xc, mixed_l))
        mixed_c, new_ctx = mix(xc, Bc, Lc, sel_c, zero_states, None)
        for acc, t in zip(outs, new_ctx):
            acc.append(t)
        x1c, g2c, nbc, gc, idxc, xec = project_and_route(xc, mixed_c, sel_c)
        x1l, g2l, nbl, gl, idxl, xel = project_and_route(xl, mixed_l, sel_l)
        yec, yel = expert_ffn(xec, xel, gc, gl, g2c, g2l, idxl[..., None], Ll, exp_w_gate, exp_w_up, exp_w_down, l)
        xl = scatter_add_rows(x1l, yel.reshape(-1, D), idxl.reshape(-1))
        xc = scatter_add_rows(x1c, yec.reshape(-1, D), idxc.reshape(-1))
    _, outs = lax.optimization_barrier((yec, outs))
    return (xc.reshape(Bc, Lc, D), xl.reshape(Bl, Ll, D)) + tuple(jnp.stack(o, axis=1) for o in outs)
```

```python
import dataclasses
import functools
import math

import jax
import jax.numpy as jnp
import numpy as np
from jax import lax
from jax.experimental import pallas as pl
from jax.experimental.pallas import tpu as pltpu
from jax.experimental.pallas import tpu_sc as plsc

D_MODEL = 1024
DEPTH = 4
GRID_W = 64
W_GROUP = 256
H_M = 4
DH_M = 64
S5_CH = 16
G_S5 = 16
P_S5 = 64
S5_STATE = G_S5 * P_S5
H_R = 4
DH_R = 64
H_D = 4
DH_D = 32
N_EXPERTS = 16
CAPACITY_FACTOR = 2
ROPE_BASE = 10000.0
EPS = 1e-6
SUBLANES = 8
VMEM_LIMIT_BYTES = 56 * 1024 * 1024

f32 = jnp.float32
bf16 = jnp.bfloat16
HIGHEST = lax.Precision.HIGHEST
NEG_INF = float("-inf")


def _gelu_tanh(x):
    return 0.5 * x * (1.0 + jnp.tanh(math.sqrt(2.0 / math.pi) * (x + 0.044715 * (x * x * x))))


def _s5_kernel(su_ref, x0r_ref, x0i_ref, wb_ref, wc_ref, lb_ref, pw_ref, d_ref, gw_ref, gb_ref,
               y_ref, xr_ref, xi_ref, st_ref, *, chained):
    n_steps = st_ref.shape[0] // SUBLANES
    su = su_ref[0]
    y_ref[0] = su * d_ref[...]
    row = lax.broadcasted_iota(jnp.int32, (SUBLANES, S5_STATE), 0)
    zeros = jnp.zeros((SUBLANES, S5_STATE), f32)
    for d in range(2):
        st_ref[...] = jnp.dot(su.astype(bf16), wb_ref[0, d], preferred_element_type=f32)
        lbr = jnp.broadcast_to(lb_ref[0, d, 0:1, :], (SUBLANES, S5_STATE))
        lbi = jnp.broadcast_to(lb_ref[0, d, 1:2, :], (SUBLANES, S5_STATE))

        def rows_of(k, d=d):
            kk = k if d == 0 else n_steps - 1 - k
            return pl.ds(pl.multiple_of(kk * SUBLANES, SUBLANES), SUBLANES)

        def scan_step(k, carry, lbr=lbr, lbi=lbi, rows_of=rows_of):
            xr, xi = carry
            r = rows_of(k)
            nxr = lbr * xr - lbi * xi + st_ref[r, 0:S5_STATE]
            nxi = lbr * xi + lbi * xr + st_ref[r, S5_STATE:2 * S5_STATE]
            st_ref[r, 0:S5_STATE] = nxr
            st_ref[r, S5_STATE:2 * S5_STATE] = nxi
            return nxr, nxi

        if not chained:
            xr_ref[0, d], xi_ref[0, d] = lax.fori_loop(0, n_steps, scan_step, (x0r_ref[0, d], x0i_ref[0, d]))
            y_ref[0] += jnp.dot(st_ref[...].astype(bf16), wc_ref[0, d], preferred_element_type=f32)
            continue
        fr, fi = lax.fori_loop(0, n_steps, scan_step, (zeros, zeros))

        cr = x0r_ref[0, d]
        ci = x0i_ref[0, d]
        plr = pw_ref[0, d, 0:1, :]
        pli = pw_ref[0, d, 1:2, :]
        cmr, cmi = zeros, zeros
        for i in (range(SUBLANES) if d == 0 else reversed(range(SUBLANES))):
            cmr = jnp.where(row == i, cr, cmr)
            cmi = jnp.where(row == i, ci, cmi)
            cr, ci = (plr * cr - pli * ci + fr[i:i + 1], plr * ci + pli * cr + fi[i:i + 1])
        xr_ref[0, d] = cr
        xi_ref[0, d] = ci

        def fix_step(k, carry, lbr=lbr, lbi=lbi, cmr=cmr, cmi=cmi, rows_of=rows_of):
            pr, pi = carry
            r = rows_of(k)
            st_ref[r, 0:S5_STATE] = st_ref[r, 0:S5_STATE] + (pr * cmr - pi * cmi)
            st_ref[r, S5_STATE:2 * S5_STATE] = st_ref[r, S5_STATE:2 * S5_STATE] + (pr * cmi + pi * cmr)
            return pr * lbr - pi * lbi, pr * lbi + pi * lbr

        lax.fori_loop(0, n_steps, fix_step, (lbr, lbi))
        y_ref[0] += jnp.dot(st_ref[...].astype(bf16), wc_ref[0, d], preferred_element_type=f32)

    ys = _gelu_tanh(y_ref[0])
    gate = jax.nn.sigmoid(jnp.dot(ys.astype(bf16), gw_ref[0], preferred_element_type=f32) + gb_ref[...])
    y_ref[0] = ys * gate


def _s5_prepare(lam_re, lam_im, log_step, b_re, b_im, c_re, c_im, glu_w, n_steps_list):
    dt = jnp.exp(log_step)[..., None]
    mag = jnp.exp(lam_re * dt)
    ang = lam_im * dt
    lb_re, lb_im = mag * jnp.cos(ang), mag * jnp.sin(ang)
    nr, ni = lb_re - 1.0, lb_im
    den = lam_re * lam_re + lam_im * lam_im
    f_re = (nr * lam_re + ni * lam_im) / den
    f_im = (ni * lam_re - nr * lam_im) / den
    bb_re = f_re[..., None] * b_re[:, None] - f_im[..., None] * b_im[:, None]
    bb_im = f_re[..., None] * b_im[:, None] + f_im[..., None] * b_re[:, None]
    def block_diag(a):
        rows, cols = G_S5 * a.shape[3], G_S5 * a.shape[4]
        tiled = jnp.tile(a.reshape(a.shape[:2] + (rows, a.shape[4])), (1, 1, 1, G_S5))
        same_group = (np.arange(rows)[:, None] // a.shape[3]) == (np.arange(cols)[None, :] // a.shape[4])
        return jnp.where(same_group, tiled, 0.0).astype(bf16)

    wb = jnp.zeros(bb_re.shape[:2] + (W_GROUP, 2 * S5_STATE), bf16)
    wb = wb.at[..., :S5_STATE].set(block_diag(jnp.swapaxes(bb_re, 3, 4)))
    wb = wb.at[..., S5_STATE:].set(block_diag(jnp.swapaxes(bb_im, 3, 4)))
    wc = jnp.zeros(c_re.shape[:2] + (2 * S5_STATE, W_GROUP), bf16)
    wc = wc.at[:, :, :S5_STATE].set(block_diag(jnp.swapaxes(c_re, 3, 4)))
    wc = wc.at[:, :, S5_STATE:].set(block_diag(jnp.swapaxes(-c_im, 3, 4)))
    lead = lb_re.shape[:2]
    lb = jnp.stack([lb_re.reshape(lead + (S5_STATE,)), lb_im.reshape(lead + (S5_STATE,))], axis=2)
    pr, pi = lb[:, :, 0], lb[:, :, 1]
    tables = {}
    for j in range(int(math.log2(max(n_steps_list))) + 1):
        if 2 ** j in n_steps_list:
            tables[2 ** j] = jnp.stack([pr, pi], axis=2)
        pr, pi = pr * pr - pi * pi, 2.0 * pr * pi
    return wb, wc, lb, tables, glu_w.astype(bf16)


def s5_mixer(su, x0r, x0i, s5p, s5_d, glu_b, layer):
    B, L, _ = su.shape
    wb, wc, lb, tables, glu_w = s5p
    chained = B % SUBLANES != 0
    if chained:
        n_groups, lanes, n_steps = B, 1, L // SUBLANES
        to_rows = lambda a: a.reshape(B, SUBLANES, n_steps, -1).transpose(0, 2, 1, 3).reshape(B, L, -1)
        from_rows = lambda a: a.reshape(B, n_steps, SUBLANES, -1).transpose(0, 2, 1, 3).reshape(B, L, -1)
        pw = tables[n_steps]
    else:
        n_groups, lanes, n_steps = B // SUBLANES, SUBLANES, L
        to_rows = lambda a: a.reshape(n_groups, SUBLANES, L, -1).transpose(0, 2, 1, 3).reshape(n_groups, -1, a.shape[-1])
        from_rows = lambda a: a.reshape(n_groups, L, SUBLANES, -1).transpose(0, 2, 1, 3).reshape(B, L, -1)
        pw = lb
    rows = n_steps * SUBLANES
    state_in = lambda a: a.reshape(n_groups, lanes, 2, S5_STATE).transpose(0, 2, 1, 3)
    state_out = lambda a: a.transpose(0, 2, 1, 3).reshape(B, 2, G_S5, P_S5)
    full = lambda *shape: pl.BlockSpec(shape, lambda b: (0,) * len(shape))
    per_b = lambda *shape: pl.BlockSpec((1,) + shape, lambda b: (b,) + (0,) * len(shape))
    per_layer = lambda *shape: pl.BlockSpec((1,) + shape, lambda b: (layer,) + (0,) * len(shape))
    y_p, xr, xi = pl.pallas_call(
        functools.partial(_s5_kernel, chained=chained),
        grid=(n_groups,),
        in_specs=[per_b(rows, W_GROUP), per_b(2, lanes, S5_STATE), per_b(2, lanes, S5_STATE),
                  per_layer(2, W_GROUP, 2 * S5_STATE), per_layer(2, 2 * S5_STATE, W_GROUP),
                  per_layer(2, 2, S5_STATE), per_layer(2, 2, S5_STATE),
                  full(1, W_GROUP), per_layer(W_GROUP, W_GROUP), full(1, W_GROUP)],
        out_specs=[per_b(rows, W_GROUP), per_b(2, lanes, S5_STATE), per_b(2, lanes, S5_STATE)],
        out_shape=[jax.ShapeDtypeStruct((n_groups, rows, W_GROUP), f32),
                   jax.ShapeDtypeStruct((n_groups, 2, lanes, S5_STATE), f32),
                   jax.ShapeDtypeStruct((n_groups, 2, lanes, S5_STATE), f32)],
        scratch_shapes=[pltpu.VMEM((rows, 2 * S5_STATE), f32)],
        compiler_params=pltpu.CompilerParams(dimension_semantics=("arbitrary",),
                                             vmem_limit_bytes=VMEM_LIMIT_BYTES),
        name="s5_mixer",
    )(to_rows(su), state_in(x0r), state_in(x0i), wb, wc, lb, pw,
      s5_d.reshape(1, W_GROUP), glu_w, glu_b.reshape(1, W_GROUP))
    return from_rows(y_p), state_out(xr), state_out(xi)


Q_TILE = 256
PROJ_BLOCKS = dict(mq=0, mk=1, mv=2, mo=3, su=4, rq=5, rk=6, rv=7, rg=8, dq=9, dk=10, dv=11)
GATE_LANES = 128
N_PROJ = 12 * W_GROUP + GATE_LANES


def _log_sigmoid(x):
    return jnp.minimum(x, 0.0) - jnp.log1p(jnp.exp(-jnp.abs(x)))


def _group_ones(width, group):
    shift = int(math.log2(group))
    r = lax.broadcasted_iota(jnp.int32, (width, width), 0) >> shift
    c = lax.broadcasted_iota(jnp.int32, (width, width), 1) >> shift
    return (r == c).astype(bf16)


def _split3(x):
    hi = x.astype(bf16)
    r = x - hi.astype(f32)
    mid = r.astype(bf16)
    return hi, mid, (r - mid.astype(f32)).astype(bf16)


def _group_mean(x, ones, group):
    return sum(jnp.dot(p, ones, preferred_element_type=f32) for p in _split3(x)) * (1.0 / group)


def _dot_nt(a, b):
    return lax.dot_general(a.astype(bf16), b.astype(bf16), (((1,), (1,)), ((), ())), preferred_element_type=f32)


def _dot(a, b):
    return jnp.dot(a.astype(bf16), b.astype(bf16), preferred_element_type=f32)


def _proj_block(name, rows):
    j = PROJ_BLOCKS[name]
    return pl.BlockSpec((1, rows, W_GROUP), lambda b, qi, j=j, rows=rows: (b, qi if rows == Q_TILE else 0, j))


def _mlstm_kernel(q_ref, k_ref, v_ref, o_ref, g_ref, gt_ref, gb_ref, gbt_ref, ng_ref, c0_ref, n0_ref, m0_ref,
                  h_ref, c_ref, n_ref, m_ref, gl_ref, gu_ref, rc_ref, vt_ref, ht_ref, *, seq_len, q_tile):
    L, TQ = seq_len, q_tile
    nq = L // TQ
    qi = pl.program_id(1)
    grow = gt_ref[0] + gbt_ref[...]

    @pl.when(qi == 0)
    def _():
        ss = lax.broadcasted_iota(jnp.int32, (L, L), 0)
        tt = lax.broadcasted_iota(jnp.int32, (L, L), 1)
        tri_le = (ss <= tt).astype(bf16)
        tri_ge = (ss >= tt).astype(bf16)
        rows = _split3(_log_sigmoid(grow))
        gl = sum(jnp.dot(p, tri_le, preferred_element_type=f32) for p in rows)
        gu = sum(jnp.dot(p, tri_ge, preferred_element_type=f32) for p in rows)
        for j in range(nq):
            gl_ref[j] = gl[:, j * TQ:(j + 1) * TQ]
            gu_ref[j] = gu[:, j * TQ:(j + 1) * TQ]
        gcol = g_ref[0] + gb_ref[...]
        cols = _split3(_log_sigmoid(gcol))
        glc = sum(jnp.dot(tri_ge, p, preferred_element_type=f32) for p in cols)
        guc = sum(jnp.dot(tri_le, p, preferred_element_type=f32) for p in cols)
        lane = lax.broadcasted_iota(jnp.int32, (L, GATE_LANES), 1)
        rc_ref[...] = pltpu.roll(gcol, 4, 1) - jnp.where(lane < 8, glc, guc)

        vt = v_ref[0].T
        kk = k_ref[0] * (DH_M ** -0.5)
        one_row = (lax.broadcasted_iota(jnp.int32, (DH_M, L), 0) == 0).astype(bf16)
        for h in range(H_M):
            hs = slice(h * DH_M, (h + 1) * DH_M)
            vt_ref[2 * h * DH_M:(2 * h + 1) * DH_M, :] = vt[hs, :].astype(bf16)
            vt_ref[(2 * h + 1) * DH_M:(2 * h + 2) * DH_M, :] = one_row
        for d in range(2):
            g_all = gl if d == 0 else gu
            for h in range(H_M):
                hs = slice(h * DH_M, (h + 1) * DH_M)
                ii, fi = 8 * d + h, 8 * d + 4 + h
                g_row = g_all[fi:fi + 1, :]
                g_tot = g_row[:, L - 1:L] if d == 0 else g_row[:, 0:1]
                wlog = g_tot - g_row + grow[ii:ii + 1, :]
                m0 = m0_ref[0, d:d + 1, h:h + 1]
                m_new = jnp.maximum(g_tot + m0, jnp.max(wlog, axis=1, keepdims=True))
                decay = jnp.exp(g_tot + m0 - m_new)
                w = jnp.exp(wlog - m_new)
                kh = kk[:, hs]
                c_ref[0, d, h] = decay * c0_ref[0, d, h] + _dot(vt[hs, :] * w, kh)
                n_upd = jnp.dot(jnp.broadcast_to(w, (SUBLANES, L)), kh, precision=HIGHEST,
                                preferred_element_type=f32)[0:1, :]
                n_ref[0, d, h:h + 1, :] = decay * n0_ref[0, d, h:h + 1, :] + n_upd
                m_ref[0, d:d + 1, h:h + 1] = m_new

    gl_t = gl_ref[qi]
    gu_t = gu_ref[qi]
    s_idx = lax.broadcasted_iota(jnp.int32, (L, TQ), 0)
    t_idx = qi * TQ + lax.broadcasted_iota(jnp.int32, (L, TQ), 1)
    low = s_idx <= t_idx
    upp = s_idx >= t_idx
    qt = q_ref[0].T.astype(bf16)
    k = (k_ref[0] * (DH_M ** -0.5)).astype(bf16)
    rc = rc_ref[...]
    row0 = lax.broadcasted_iota(jnp.int32, (DH_M, DH_M), 0) == 0
    for h in range(H_M):
        hs = slice(h * DH_M, (h + 1) * DH_M)
        qth = qt[hs, :]
        vta = vt_ref[2 * h * DH_M:(2 * h + 2) * DH_M, :]
        s0 = jnp.dot(k[:, hs], qth, preferred_element_type=f32)
        h_sum = None
        for d in range(2):
            fi = 8 * d + 4 + h
            g_t = (gl_t if d == 0 else gu_t)[fi:fi + 1, :]
            dlog = jnp.where(low if d == 0 else upp, rc[:, fi:fi + 1] + g_t, NEG_INF)
            inter = g_t + m0_ref[0, d:d + 1, h:h + 1]
            m_t = jnp.maximum(inter, jnp.max(dlog, axis=0, keepdims=True))
            p = s0 * jnp.exp(dlog - m_t)
            a = jnp.exp(inter - m_t)
            c0n0 = jnp.concatenate([c0_ref[0, d, h], jnp.where(row0, n0_ref[0, d, h:h + 1, :], 0.0)], axis=0)
            numden = (jnp.dot(vta, p.astype(bf16), preferred_element_type=f32)
                      + a * jnp.dot(c0n0.astype(bf16), qth, preferred_element_type=f32))
            scale = 1.0 / jnp.maximum(jnp.abs(numden[DH_M:DH_M + 1, :]), jnp.exp(-m_t))
            hd = numden[0:DH_M, :] * scale
            h_sum = hd if h_sum is None else h_sum + hd
        ht_ref[hs, :] = h_sum * lax.rsqrt(jnp.mean(h_sum * h_sum, axis=0, keepdims=True) + EPS)
    h_ref[0] = jax.nn.sigmoid(o_ref[0]) * (ht_ref[...].T * ng_ref[...])


def mlstm_mixer(proj, gates_t, c0, n0, m0, lp):
    B, L, _ = proj.shape
    TQ = min(L, Q_TILE)
    gb = lp['mlstm_gate_b'].reshape(1, 4 * H_M)
    const = lambda *shape: pl.BlockSpec(shape, lambda b, qi: (0,) * len(shape))
    per_b = lambda *shape: pl.BlockSpec((1,) + shape, lambda b, qi: (b,) + (0,) * len(shape))
    return pl.pallas_call(
        functools.partial(_mlstm_kernel, seq_len=L, q_tile=TQ),
        grid=(B, L // TQ),
        in_specs=[_proj_block('mq', TQ), _proj_block('mk', L), _proj_block('mv', L), _proj_block('mo', TQ),
                  pl.BlockSpec((1, L, GATE_LANES), lambda b, qi: (b, 0, 12 * W_GROUP // GATE_LANES)),
                  per_b(4 * H_M, L), const(1, GATE_LANES), const(4 * H_M, 1), const(1, W_GROUP),
                  per_b(2, H_M, DH_M, DH_M), per_b(2, H_M, DH_M), per_b(2, H_M)],
        out_specs=[pl.BlockSpec((1, TQ, W_GROUP), lambda b, qi: (b, qi, 0)),
                   per_b(2, H_M, DH_M, DH_M), per_b(2, H_M, DH_M), per_b(2, H_M)],
        out_shape=[jax.ShapeDtypeStruct((B, L, W_GROUP), f32),
                   jax.ShapeDtypeStruct((B, 2, H_M, DH_M, DH_M), f32),
                   jax.ShapeDtypeStruct((B, 2, H_M, DH_M), f32),
                   jax.ShapeDtypeStruct((B, 2, H_M), f32)],
        scratch_shapes=[pltpu.VMEM((L // TQ, 4 * H_M, TQ), f32), pltpu.VMEM((L // TQ, 4 * H_M, TQ), f32),
                        pltpu.VMEM((L, GATE_LANES), f32), pltpu.VMEM((2 * W_GROUP, L), bf16),
                        pltpu.VMEM((W_GROUP, TQ), f32)],
        compiler_params=pltpu.CompilerParams(dimension_semantics=("arbitrary", "arbitrary"),
                                             vmem_limit_bytes=VMEM_LIMIT_BYTES),
        name="mlstm_mixer",
    )(proj, proj, proj, proj, proj, gates_t, jnp.pad(gb, ((0, 0), (0, GATE_LANES - 4 * H_M))),
      gb.reshape(4 * H_M, 1), lp['mlstm_norm_g'].reshape(1, W_GROUP), c0, n0, m0)


def _retention_kernel(lg_ref, q_ref, k_ref, v_ref, g_ref, gn_ref, r0_ref, h_ref, r_ref, *, seq_len, q_tile):
    L, TQ = seq_len, q_tile
    qi = pl.program_id(1)
    t_col = qi * TQ + lax.broadcasted_iota(jnp.int32, (TQ, 1), 0)
    rel = (qi * TQ + lax.broadcasted_iota(jnp.int32, (TQ, L), 0)
           - lax.broadcasted_iota(jnp.int32, (TQ, L), 1)).astype(f32)
    q = q_ref[0]
    k = k_ref[0] * (DH_R ** -0.5)
    v = v_ref[0]
    ones = _group_ones(W_GROUP, DH_R)
    for h in range(H_R):
        hs = slice(h * DH_R, (h + 1) * DH_R)
        lgf, lgb = lg_ref[0, h], lg_ref[1, h]
        qh, kh, vh = q[:, hs], k[:, hs], v[:, hs]
        decay = jnp.where(rel > 0.0, jnp.exp(lgf * jnp.maximum(rel, 0.0)),
                          jnp.where(rel < 0.0, jnp.exp(lgb * jnp.maximum(-rel, 0.0)), 2.0))
        o = _dot(_dot_nt(qh, kh) * decay, vh)
        xi_f = jnp.exp(lgf * (t_col + 1).astype(f32))
        xi_b = jnp.exp(lgb * (L - t_col).astype(f32))
        o = o + xi_f * _dot(qh, r0_ref[0, 0, h]) + xi_b * _dot(qh, r0_ref[0, 1, h])
        h_ref[0, :, hs] = o
    o = h_ref[0]
    oc = o - _group_mean(o, ones, DH_R)
    y = oc * lax.rsqrt(_group_mean(oc * oc, ones, DH_R) + EPS) * gn_ref[...]
    h_ref[0] = y * jax.nn.silu(g_ref[0])

    @pl.when(qi == 0)
    def _():
        kt = k.T
        s_row = lax.broadcasted_iota(jnp.int32, (1, L), 1).astype(f32)
        for d in range(2):
            for h in range(H_R):
                hs = slice(h * DH_R, (h + 1) * DH_R)
                lg = lg_ref[d, h]
                zeta = jnp.exp(lg * ((L - 1.0) - s_row)) if d == 0 else jnp.exp(lg * s_row)
                r_ref[0, d, h] = jnp.exp(lg * float(L)) * r0_ref[0, d, h] + _dot(kt[hs, :] * zeta, v[:, hs])


def retention_mixer(proj, r0, lp):
    B, L, _ = proj.shape
    TQ = min(L, Q_TILE)
    log_gamma = -jnp.exp(lp['ret_decay'])
    per_b = lambda *shape: pl.BlockSpec((1,) + shape, lambda b, qi: (b,) + (0,) * len(shape))
    return pl.pallas_call(
        functools.partial(_retention_kernel, seq_len=L, q_tile=TQ),
        grid=(B, L // TQ),
        in_specs=[pl.BlockSpec(memory_space=pltpu.SMEM),
                  _proj_block('rq', TQ), _proj_block('rk', L), _proj_block('rv', L), _proj_block('rg', TQ),
                  pl.BlockSpec((1, W_GROUP), lambda b, qi: (0, 0)), per_b(2, H_R, DH_R, DH_R)],
        out_specs=[pl.BlockSpec((1, TQ, W_GROUP), lambda b, qi: (b, qi, 0)), per_b(2, H_R, DH_R, DH_R)],
        out_shape=[jax.ShapeDtypeStruct((B, L, W_GROUP), f32),
                   jax.ShapeDtypeStruct((B, 2, H_R, DH_R, DH_R), f32)],
        compiler_params=pltpu.CompilerParams(dimension_semantics=("arbitrary", "arbitrary"),
                                             vmem_limit_bytes=VMEM_LIMIT_BYTES),
        name="retention_mixer",
    )(log_gamma, proj, proj, proj, proj, lp['ret_gn_g'].reshape(1, W_GROUP), r0)


def _rope_tables(L):
    half = DH_D // 2
    freqs = ROPE_BASE ** (-np.arange(0, half, 2, dtype=np.float64) / half)
    pos = np.arange(L)
    row, col = (pos // GRID_W).astype(np.float64), (pos % GRID_W).astype(np.float64)
    ang = np.concatenate([np.tile(row[:, None] * freqs, (1, 2)), np.tile(col[:, None] * freqs, (1, 2))], axis=1)
    sign = np.tile(np.concatenate([-np.ones(half // 2), np.ones(half // 2)]), 2)
    cos = np.tile(np.cos(ang), (1, 2 * H_D))
    sin = np.tile(np.sin(ang) * sign, (1, 2 * H_D))
    return jnp.asarray(cos, f32), jnp.asarray(sin, f32)


def _swap_pairs(x):
    parts = []
    for j in range(x.shape[1] // 128):
        xs = x[:, j * 128:(j + 1) * 128]
        lane = lax.broadcasted_iota(jnp.int32, xs.shape, 1)
        parts.append(jnp.where((lane & 15) < 8, pltpu.roll(xs, 120, 1), pltpu.roll(xs, 8, 1)))
    return jnp.concatenate(parts, axis=1)


def _qk_norm(x, gain, ones):
    return x * lax.rsqrt(_group_mean(x * x, ones, DH_D) + EPS) * gain


def _diff_attn_kernel(lam_ref, q_ref, k_ref, v_ref, qkg_ref, sg_ref, *rest, seq_len, q_tile, past_len, out_scale):
    L, TQ, P = seq_len, q_tile, past_len
    if P:
        kc_ref, vc_ref, cos_ref, sin_ref, h_ref, ka_ref, vt_ref, ot_ref = rest
    else:
        h_ref, kn_ref, ka_ref, vt_ref, ot_ref = rest
    qi = pl.program_id(1)
    ones = _group_ones(W_GROUP, DH_D)

    @pl.when(qi == 0)
    def _():
        kn = _qk_norm(k_ref[0], qkg_ref[1:2, :], ones)
        if P:
            kn = kn * cos_ref[...] + _swap_pairs(kn) * sin_ref[...]
            ka_ref[0:P, :] = kc_ref[0, 0].astype(bf16)
            vt_ref[:, 0:P] = vc_ref[0, 0].T.astype(bf16)
        else:
            kn_ref[0] = kn
        ka_ref[P:P + L, :] = kn.astype(bf16)
        vt_ref[:, P:P + L] = v_ref[0].T.astype(bf16)

    qn = _qk_norm(q_ref[0], qkg_ref[0:1, :], ones)
    if P:
        rows = pl.ds(pl.multiple_of(qi * TQ, TQ), TQ)
        qn = qn * cos_ref[rows, :] + _swap_pairs(qn) * sin_ref[rows, :]
    qt = (qn * (DH_D ** -0.5)).T.astype(bf16)
    lam = lam_ref[0, 0]
    ka = ka_ref[...]
    for h in range(H_D):
        probs = []
        for j in range(2):
            cs = slice((2 * h + j) * DH_D, (2 * h + j + 1) * DH_D)
            s = jnp.dot(ka[:, cs], qt[cs, :], preferred_element_type=f32)
            e = jnp.exp(s - jnp.max(s, axis=0, keepdims=True))
            probs.append(e * (1.0 / jnp.sum(e, axis=0, keepdims=True)))
        vs = slice(h * 2 * DH_D, (h + 1) * 2 * DH_D)
        ot_ref[vs, :] = jnp.dot(vt_ref[vs, :], (probs[0] - lam * probs[1]).astype(bf16), preferred_element_type=f32)
    o = ot_ref[...].T
    ones_v = _group_ones(W_GROUP, 2 * DH_D)
    h_ref[0] = o * lax.rsqrt(_group_mean(o * o, ones_v, 2 * DH_D) + EPS) * (sg_ref[...] * out_scale)


def diff_attn_mixer(proj, cache, lp, lam_init, layer):
    B, L, _ = proj.shape
    TQ = min(L, Q_TILE)
    lv = lp['diff_lambda']
    lam = (jnp.exp(jnp.sum(lv[0] * lv[1])) - jnp.exp(jnp.sum(lv[2] * lv[3])) + lam_init).reshape(1, 1)
    qkg = jnp.tile(lp['diff_qk_norm'], (1, 2 * H_D))
    sg = jnp.tile(lp['diff_subln_g'], (H_D,)).reshape(1, W_GROUP)
    const = lambda *shape: pl.BlockSpec(shape, lambda b, qi: (0,) * len(shape))
    in_specs = [pl.BlockSpec(memory_space=pltpu.SMEM),
                _proj_block('dq', TQ), _proj_block('dk', L), _proj_block('dv', L), const(2, W_GROUP), const(1, W_GROUP)]
    args = [lam, proj, proj, proj, qkg, sg]
    out_specs = [pl.BlockSpec((1, TQ, W_GROUP), lambda b, qi: (b, qi, 0))]
    out_shape = [jax.ShapeDtypeStruct((B, L, W_GROUP), f32)]
    P = 0
    if cache is not None:
        ck, cv = cache
        P = ck.shape[2]
        cspec = pl.BlockSpec((1, 1, P, W_GROUP), lambda b, qi, layer=layer: (b, layer, 0, 0))
        cos, sin = _rope_tables(L)
        in_specs += [cspec, cspec, const(L, W_GROUP), const(L, W_GROUP)]
        args += [ck, cv, cos, sin]
    else:
        out_specs.append(pl.BlockSpec((1, L, W_GROUP), lambda b, qi: (b, 0, 0)))
        out_shape.append(jax.ShapeDtypeStruct((B, L, W_GROUP), f32))
    return pl.pallas_call(
        functools.partial(_diff_attn_kernel, seq_len=L, q_tile=TQ, past_len=P, out_scale=1.0 - lam_init),
        grid=(B, L // TQ),
        in_specs=in_specs, out_specs=out_specs, out_shape=out_shape,
        scratch_shapes=[pltpu.VMEM((P + L, W_GROUP), bf16), pltpu.VMEM((W_GROUP, P + L), bf16),
                        pltpu.VMEM((W_GROUP, TQ), f32)],
        compiler_params=pltpu.CompilerParams(dimension_semantics=("arbitrary", "arbitrary"),
                                             vmem_limit_bytes=VMEM_LIMIT_BYTES),
        name="diff_attention",
    )(*args)


PROJ_ROW_TILE = 512
OUT_ROW_TILE = 512
FF_TILE = 512
D_FF = 2 * D_MODEL


def _in_proj_kernel(x_ref, g_ref, sc_ref, sh_ref, w_ref, o_ref):
    x = x_ref[...]
    y = x * lax.rsqrt(jnp.mean(x * x, axis=1, keepdims=True) + EPS) * g_ref[...]
    h = (y * (1.0 + sc_ref[0]) + sh_ref[0]).astype(bf16)
    o_ref[...] = jnp.dot(h, w_ref[0], preferred_element_type=f32)


def in_projection(x, gain, scale, shift, w_p, layer, rows_per_mod):
    n, D = x.shape
    TM = min(n, PROJ_ROW_TILE)
    mod = pl.BlockSpec((1, 1, D), lambda i: (i * TM // rows_per_mod, 0, 0))
    return pl.pallas_call(
        _in_proj_kernel,
        grid=(n // TM,),
        in_specs=[pl.BlockSpec((TM, D), lambda i: (i, 0)), pl.BlockSpec((1, D), lambda i: (0, 0)), mod, mod,
                  pl.BlockSpec((1, D, N_PROJ), lambda i: (layer, 0, 0))],
        out_specs=pl.BlockSpec((TM, N_PROJ), lambda i: (i, 0)),
        out_shape=jax.ShapeDtypeStruct((n, N_PROJ), f32),
        compiler_params=pltpu.CompilerParams(dimension_semantics=("arbitrary",),
                                             vmem_limit_bytes=VMEM_LIMIT_BYTES),
        name="in_projection",
    )(x, gain.reshape(1, D), scale, shift, w_p)


def _out_proj_kernel(x_ref, m0_ref, m1_ref, m2_ref, m3_ref, w_ref, g1_ref, ng_ref, sc_ref, sh_ref, rw_ref,
                     xo_ref, h_ref, aff_ref, wb_ref):
    @pl.when(pl.program_id(0) == 0)
    def _():
        wb_ref[...] = w_ref[0].astype(bf16)

    out = None
    for j, m_ref in enumerate((m0_ref, m1_ref, m2_ref, m3_ref)):
        part = jnp.dot(m_ref[...].astype(bf16), wb_ref[j * W_GROUP:(j + 1) * W_GROUP, :], preferred_element_type=f32)
        out = part if out is None else out + part
    x = x_ref[...] + g1_ref[0] * out
    xo_ref[...] = x
    h = x * lax.rsqrt(jnp.mean(x * x, axis=1, keepdims=True) + EPS) * ng_ref[...]
    h = h * (1.0 + sc_ref[0]) + sh_ref[0]
    h_ref[...] = h
    h_hi = h.astype(bf16)
    h_lo = (h - h_hi.astype(f32)).astype(bf16)
    both = jnp.dot(h_hi, rw_ref[0], preferred_element_type=f32)
    logits = (both[:, 0:N_EXPERTS] + both[:, N_EXPERTS:2 * N_EXPERTS]
              + jnp.dot(h_lo, rw_ref[0], preferred_element_type=f32)[:, 0:N_EXPERTS])
    e = jnp.exp(logits - jnp.max(logits, axis=1, keepdims=True))
    aff_ref[...] = e / jnp.sum(e, axis=1, keepdims=True)


def out_projection(x, mixed, w_out, gate1, gain2, scale2, shift2, router_w2, layer, rows_per_mod):
    n, D = x.shape
    TM = OUT_ROW_TILE
    row = lambda width: pl.BlockSpec((TM, width), lambda i: (i, 0))
    const = lambda *shape: pl.BlockSpec(shape, lambda i: (0,) * len(shape))
    mod = pl.BlockSpec((1, 1, D), lambda i: (i * TM // rows_per_mod, 0, 0))
    return pl.pallas_call(
        _out_proj_kernel,
        grid=(n // TM,),
        in_specs=[row(D), row(W_GROUP), row(W_GROUP), row(W_GROUP), row(W_GROUP),
                  pl.BlockSpec((1, D, D), lambda i: (layer, 0, 0)), mod, const(1, D), mod, mod,
                  pl.BlockSpec((1, D, 2 * N_EXPERTS), lambda i: (layer, 0, 0))],
        out_specs=[row(D), row(D), row(N_EXPERTS)],
        out_shape=[jax.ShapeDtypeStruct((n, D), f32), jax.ShapeDtypeStruct((n, D), f32),
                   jax.ShapeDtypeStruct((n, N_EXPERTS), f32)],
        scratch_shapes=[pltpu.VMEM((D, D), bf16)],
        compiler_params=pltpu.CompilerParams(dimension_semantics=("arbitrary",), vmem_limit_bytes=VMEM_LIMIT_BYTES),
        name="out_projection",
    )(x, *mixed, w_out, gate1, gain2.reshape(1, D), scale2, shift2, router_w2)


def _experts_kernel(xc_ref, xl_ref, gc_ref, gl_ref, g2_ref, g2l_ref, il_ref, wg_ref, wu_ref, wd_ref, yc_ref, yl_ref,
                    ac_ref, al_ref, *, lat_len):
    f = pl.program_id(1)
    wg = wg_ref[0, 0].astype(bf16)
    wu = wu_ref[0, 0].astype(bf16)
    wd = wd_ref[0, 0].astype(bf16)
    for x_ref, acc_ref in ((xc_ref, ac_ref), (xl_ref, al_ref)):
        x = x_ref[0].astype(bf16)
        hidden = jax.nn.silu(jnp.dot(x, wg, preferred_element_type=f32)) * jnp.dot(x, wu, preferred_element_type=f32)
        part = jnp.dot(hidden.astype(bf16), wd, preferred_element_type=f32)

        @pl.when(f == 0)
        def _(acc_ref=acc_ref, part=part):
            acc_ref[...] = part

        @pl.when(f > 0)
        def _(acc_ref=acc_ref, part=part):
            acc_ref[...] += part

    @pl.when(f == pl.num_programs(1) - 1)
    def _():
        yc_ref[0] = ac_ref[...] * gc_ref[0] * g2_ref[0]
        gate = g2l_ref[0]
        for b in range(1, g2l_ref.shape[0]):
            gate = jnp.where(il_ref[0] >= b * lat_len, g2l_ref[b], gate)
        yl_ref[0] = al_ref[...] * gl_ref[0] * gate


def expert_ffn(xe_c, xe_l, g_c, g_l, gate2_c, gate2_l, idx_l, lat_len, w_gate, w_up, w_down, layer):
    E, Cc, D = xe_c.shape
    Cl = xe_l.shape[1]
    tok = lambda C, width: pl.BlockSpec((1, C, width), lambda e, f: (e, 0, 0))
    return pl.pallas_call(
        functools.partial(_experts_kernel, lat_len=lat_len),
        grid=(E, D_FF // FF_TILE),
        in_specs=[tok(Cc, D), tok(Cl, D), tok(Cc, 1), tok(Cl, 1), pl.BlockSpec((1, 1, D), lambda e, f: (0, 0, 0)),
                  pl.BlockSpec(gate2_l.shape, lambda e, f: (0, 0, 0)), tok(Cl, 1),
                  pl.BlockSpec((1, 1, D, FF_TILE), lambda e, f: (layer, e, 0, f)),
                  pl.BlockSpec((1, 1, D, FF_TILE), lambda e, f: (layer, e, 0, f)),
                  pl.BlockSpec((1, 1, FF_TILE, D), lambda e, f: (layer, e, f, 0))],
        out_specs=[tok(Cc, D), tok(Cl, D)],
        out_shape=[jax.ShapeDtypeStruct((E, Cc, D), f32), jax.ShapeDtypeStruct((E, Cl, D), f32)],
        scratch_shapes=[pltpu.VMEM((Cc, D), f32), pltpu.VMEM((Cl, D), f32)],
        compiler_params=pltpu.CompilerParams(dimension_semantics=("arbitrary", "arbitrary"),
                                             vmem_limit_bytes=VMEM_LIMIT_BYTES),
        name="expert_ffn",
    )(xe_c, xe_l, g_c, g_l, gate2_c, gate2_l, idx_l, w_gate, w_up, w_down)


def _permute_w_in(w):
    gate0, n_gate = 4 * W_GROUP, 4 * H_M
    out = jnp.zeros(w.shape[:-1] + (N_PROJ,), bf16)
    out = out.at[..., :gate0].set(w[..., :gate0].astype(bf16))
    out = out.at[..., gate0:12 * W_GROUP].set(w[..., gate0 + n_gate:].astype(bf16))
    return out.at[..., 12 * W_GROUP:12 * W_GROUP + n_gate].set(w[..., gate0:gate0 + n_gate].astype(bf16))


def token_mixers(proj, lp, s5p, lam_init, states, cache, layer):
    B, L, _ = proj.shape
    mC0, mn0, mm0, s5r0, s5i0, R0 = states
    gates_t = jnp.swapaxes(proj[:, :, 12 * W_GROUP:12 * W_GROUP + 4 * H_M], 1, 2)
    hm, mC, mn, mm = mlstm_mixer(proj, gates_t, mC0, mn0, mm0, lp)
    su = proj[:, :, PROJ_BLOCKS['su'] * W_GROUP:(PROJ_BLOCKS['su'] + 1) * W_GROUP]
    ys, s5r, s5i = s5_mixer(su, s5r0, s5i0, s5p, lp['s5_d'], lp['s5_glu_b'], layer)
    hr, R = retention_mixer(proj, R0, lp)
    attn = diff_attn_mixer(proj, cache, lp, lam_init, layer)
    mixed = [a.reshape(B * L, W_GROUP) for a in (hm, ys, hr, attn[0])]
    new_ctx = None
    if cache is None:
        v = proj[:, :, PROJ_BLOCKS['dv'] * W_GROUP:(PROJ_BLOCKS['dv'] + 1) * W_GROUP]
        new_ctx = (mC, mn, mm, s5r, s5i, R, attn[1].reshape(B, L, 2 * H_D, DH_D), v.reshape(B, L, H_D, 2 * DH_D))
    return mixed, new_ctx


GATHER_WINDOW = 32
INDEX_LANES = 128


def gather_rows(x, idx):
    num = idx.shape[0]
    width = x.shape[1]
    mesh = plsc.VectorSubcoreMesh(core_axis_name="core", subcore_axis_name="subcore")
    per_core = num // GATHER_WINDOW // mesh.num_cores
    idx_rows = jnp.pad(idx.reshape(num // GATHER_WINDOW, GATHER_WINDOW), ((0, 0), (0, INDEX_LANES - GATHER_WINDOW)))

    @pl.kernel(out_type=jax.ShapeDtypeStruct((num, width), x.dtype), mesh=mesh)
    def gather_kernel(x_hbm, i_hbm, o_hbm):
        base = lax.axis_index("core") * per_core

        def body(i_vmem, o_vmem):
            pltpu.sync_copy(x_hbm.at[i_vmem.at[0, pl.ds(0, GATHER_WINDOW)]], o_vmem)

        pltpu.emit_pipeline(
            body,
            grid=(per_core,),
            in_specs=[pl.BlockSpec((1, INDEX_LANES), index_map=lambda i: (base + i, 0))],
            out_specs=[pl.BlockSpec((GATHER_WINDOW, width), index_map=lambda i: (base + i, 0))],
            core_axis_name="subcore",
            dimension_semantics=(pltpu.PARALLEL,),
        )(i_hbm, o_hbm)

    return gather_kernel(x, idx_rows)


SC_LANES = 16


def _threshold_kernel(aff_ref, t_ref, r_ref, *, cap):
    bits = pltpu.bitcast(aff_ref[...], jnp.int32)
    t = jnp.zeros((N_EXPERTS, 1), jnp.int32)
    for b in range(30, -1, -1):
        cand = t | (1 << b)
        cnt = jnp.sum((bits >= cand).astype(f32), axis=1, keepdims=True)
        t = jnp.where(cnt >= cap, cand, t)
    n_gt = jnp.sum((bits > t).astype(f32), axis=1, keepdims=True)
    t_ref[...] = jnp.broadcast_to(t, (N_EXPERTS, GATE_LANES))
    r_ref[...] = jnp.broadcast_to(cap - n_gt.astype(jnp.int32), (N_EXPERTS, GATE_LANES))


def expert_choice_select(aff_t, cap):
    n = aff_t.shape[1]
    t_bits, n_ties = pl.pallas_call(
        functools.partial(_threshold_kernel, cap=cap),
        out_shape=[jax.ShapeDtypeStruct((N_EXPERTS, GATE_LANES), jnp.int32)] * 2,
        name="select_threshold",
    )(aff_t)
    mesh = plsc.VectorSubcoreMesh(core_axis_name="core", subcore_axis_name="subcore")
    params = pltpu.CompilerParams()
    if "needs_layout_passes" in pltpu.CompilerParams.__dataclass_fields__:
        params = dataclasses.replace(params, needs_layout_passes=False)

    @pl.kernel(out_type=[jax.ShapeDtypeStruct((N_EXPERTS, cap), jnp.int32), jax.ShapeDtypeStruct((N_EXPERTS, cap), f32)],
               mesh=mesh, compiler_params=params,
               scratch_types=[pltpu.VMEM((n,), f32), pltpu.VMEM((cap + SC_LANES,), jnp.int32),
                              pltpu.VMEM((cap + SC_LANES,), f32), pltpu.VMEM((SC_LANES,), jnp.int32),
                              pltpu.VMEM((SC_LANES,), jnp.int32)])
    def select_kernel(aff_hbm, t_hbm, r_hbm, idx_hbm, g_hbm, row, ibuf, gbuf, tv, rv):
        e = lax.axis_index("subcore")

        @pl.when(lax.axis_index("core") == 0)
        def _():
            pltpu.sync_copy(aff_hbm.at[e], row)
            pltpu.sync_copy(t_hbm.at[e], tv)
            pltpu.sync_copy(r_hbm.at[e], rv)
            t = tv[...]
            r = rv[...]
            lane = lax.iota(jnp.int32, SC_LANES)

            def body(j, carry):
                off, seen_eq = carry
                v = row[pl.ds(j * SC_LANES, SC_LANES)]
                b = plsc.bitcast(v, jnp.int32)
                eq = b == t
                eq_i = eq.astype(jnp.int32)
                take = (b > t) | (eq & (seen_eq + plsc.cumsum(eq_i) <= r))
                plsc.store_compressed(ibuf.at[pl.ds(off, SC_LANES)], lane + j * SC_LANES, mask=take)
                plsc.store_compressed(gbuf.at[pl.ds(off, SC_LANES)], v, mask=take)
                return off + jnp.sum(take.astype(jnp.int32)), seen_eq + jnp.sum(eq_i)

            lax.fori_loop(0, n // SC_LANES, body, (jnp.int32(0), jnp.int32(0)))
            pltpu.sync_copy(ibuf.at[pl.ds(0, cap)], idx_hbm.at[e])
            pltpu.sync_copy(gbuf.at[pl.ds(0, cap)], g_hbm.at[e])

    idx, gates = select_kernel(aff_t, t_bits[:, :SC_LANES], n_ties[:, :SC_LANES])
    return idx, gates


def _route(aff, h2):
    n = aff.shape[0]
    idx, gates = expert_choice_select(aff.T, CAPACITY_FACTOR * n // N_EXPERTS)
    return gates[..., None], idx, gather_rows(h2, idx.reshape(-1)).reshape(idx.shape + h2.shape[1:])


SCATTER_COLS = 128
SCATTER_WINDOW = 128


def scatter_add_rows(x, ye, idx):
    n, D = x.shape
    R = ye.shape[0]
    mesh = plsc.VectorSubcoreMesh(core_axis_name="core", subcore_axis_name="subcore")
    rows = n // mesh.num_subcores
    wins = R // SCATTER_WINDOW // mesh.num_subcores

    @pl.kernel(out_type=jax.ShapeDtypeStruct((n, D), f32), mesh=mesh,
               scratch_types=[pltpu.VMEM_SHARED((n, SCATTER_COLS), f32), pltpu.VMEM((SCATTER_WINDOW, SCATTER_COLS), f32),
                              pltpu.VMEM((SCATTER_WINDOW,), jnp.int32)])
    def scatter_kernel(x_hbm, ye_hbm, i_hbm, o_hbm, shared, buf, ibuf):
        core = lax.axis_index("core")
        sid = lax.axis_index("subcore")
        r0 = pl.multiple_of(sid * rows, SUBLANES)
        for slab in range(D // SCATTER_COLS):
            cols = pl.ds(slab * SCATTER_COLS, SCATTER_COLS)

            @pl.when(core == slab % mesh.num_cores)
            def _(cols=cols):
                pltpu.sync_copy(x_hbm.at[pl.ds(r0, rows), cols], shared.at[pl.ds(r0, rows)])
                plsc.subcore_barrier()

                @pl.loop(0, wins)
                def _(w):
                    win = w * mesh.num_subcores + sid
                    pltpu.sync_copy(i_hbm.at[win], ibuf)
                    pltpu.sync_copy(ye_hbm.at[pl.ds(pl.multiple_of(win * SCATTER_WINDOW, SCATTER_WINDOW),
                                                    SCATTER_WINDOW), cols], buf)
                    pltpu.sync_copy(buf, shared.at[ibuf], add=True)

                plsc.subcore_barrier()
                pltpu.sync_copy(shared.at[pl.ds(r0, rows)], o_hbm.at[pl.ds(r0, rows), cols])
                plsc.subcore_barrier()

    return scatter_kernel(x, ye, idx.reshape(R // SCATTER_WINDOW, SCATTER_WINDOW))


PER_LAYER = ('norm1_g', 'norm2_g', 'mlstm_gate_b', 'mlstm_norm_g', 's5_d', 's5_glu_b', 'ret_decay',
             'ret_gn_g', 'diff_qk_norm', 'diff_lambda', 'diff_subln_g')


def kernel(x_prompt, x_sample, state_mlstm_c, state_mlstm_n, state_mlstm_m, state_s5_re, state_s5_im, state_ret, cache_diff_k, cache_diff_v, c, c_ctx, norm1_g, norm2_g, ada_w, ada_b, w_in, w_out, mlstm_gate_b, mlstm_norm_g, s5_lambda_re, s5_lambda_im, s5_log_step, s5_b_re, s5_b_im, s5_c_re, s5_c_im, s5_d, s5_glu_w, s5_glu_b, ret_decay, ret_gn_g, diff_qk_norm, diff_lambda, diff_subln_g, router_w, exp_w_gate, exp_w_up, exp_w_down):
    weights = dict(norm1_g=norm1_g, norm2_g=norm2_g, mlstm_gate_b=mlstm_gate_b,
                   mlstm_norm_g=mlstm_norm_g, s5_d=s5_d, s5_glu_b=s5_glu_b, ret_decay=ret_decay, ret_gn_g=ret_gn_g,
                   diff_qk_norm=diff_qk_norm, diff_lambda=diff_lambda, diff_subln_g=diff_subln_g)
    w_in_p = _permute_w_in(w_in)
    rw_hi = router_w.astype(bf16)
    router_w2 = jnp.concatenate([rw_hi, (router_w - rw_hi.astype(f32)).astype(bf16)], axis=-1)
    s5p = _s5_prepare(s5_lambda_re, s5_lambda_im, s5_log_step, s5_b_re, s5_b_im, s5_c_re, s5_c_im, s5_glu_w,
                      (x_prompt.shape[1] // SUBLANES, x_sample.shape[1] // SUBLANES))
    Bc, Lc, D = x_prompt.shape
    Bl, Ll, _ = x_sample.shape
    xc = x_prompt.reshape(Bc * Lc, D)
    xl = x_sample.reshape(Bl * Ll, D)
    zero_states = (jnp.zeros((Bc, 2, H_M, DH_M, DH_M), f32), jnp.zeros((Bc, 2, H_M, DH_M), f32),
                   jnp.zeros((Bc, 2, H_M), f32), jnp.zeros((Bc, 2, G_S5, P_S5), f32),
                   jnp.zeros((Bc, 2, G_S5, P_S5), f32), jnp.zeros((Bc, 2, H_R, DH_R, DH_R), f32))
    cache = (cache_diff_k.reshape(cache_diff_k.shape[:3] + (W_GROUP,)),
             cache_diff_v.reshape(cache_diff_v.shape[:3] + (W_GROUP,)))
    cvec = jnp.concatenate([c_ctx[None, :], c], axis=0)
    outs = [[] for _ in range(8)]
    for l in range(DEPTH):
        lp = {name: weights[name][l] for name in PER_LAYER}
        lam_init = 0.8 - 0.6 * math.exp(-0.3 * l)
        mods = jnp.split((jax.nn.silu(cvec) @ ada_w[l] + ada_b[l])[:, None, :], 6, axis=-1)
        lat_states = (state_mlstm_c[:, l], state_mlstm_n[:, l], state_mlstm_m[:, l], state_s5_re[:, l],
                      state_s5_im[:, l], state_ret[:, l])
        def mix(x, B, L, sel, states, kv):
            sh1, sc1 = mods[0][sel], mods[1][sel]
            proj = in_projection(x, lp['norm1_g'], sc1, sh1, w_in_p, l, x.shape[0] // sh1.shape[0]).reshape(B, L, N_PROJ)
            return token_mixers(proj, lp, s5p, lam_init, states, kv, l)

        def project_and_route(x, mixed, sel):
            g1, sh2, sc2, g2 = (m[sel] for m in mods[2:])
            x1, h2, aff = out_projection(x, mixed, w_out, g1, lp['norm2_g'], sc2, sh2, router_w2, l,
                                         x.shape[0] // g1.shape[0])
            return (x1, g2, g1.shape[0]) + _route(aff, h2)

        sel_c, sel_l = slice(0, 1), slice(1, 1 + Bl)
        mixed_l, _ = mix(xl, Bl, Ll, sel_l, lat_states, cache)
        xc, mixed_l = lax.optimization_barrier((xc, mixed_l))
        mixed_c, new_ctx = mix(xc, Bc, Lc, sel_c, zero_states, None)
        for acc, t in zip(outs, new_ctx):
            acc.append(t)
        x1c, g2c, nbc, gc, idxc, xec = project_and_route(xc, mixed_c, sel_c)
        x1l, g2l, nbl, gl, idxl, xel = project_and_route(xl, mixed_l, sel_l)
        yec, yel = expert_ffn(xec, xel, gc, gl, g2c, g2l, idxl[..., None], Ll, exp_w_gate, exp_w_up, exp_w_down, l)
        xl = scatter_add_rows(x1l, yel.reshape(-1, D), idxl.reshape(-1))
        xc = scatter_add_rows(x1c, yec.reshape(-1, D), idxc.reshape(-1))
    _, outs = lax.optimization_barrier((yec, outs))
    return (xc.reshape(Bc, Lc, D), xl.reshape(Bl, Ll, D)) + tuple(jnp.stack(o, axis=1) for o in outs)
```

```python
import dataclasses
import functools
import math

import jax
import jax.numpy as jnp
import numpy as np
from jax import lax
from jax.experimental import pallas as pl
from jax.experimental.pallas import tpu as pltpu
from jax.experimental.pallas import tpu_sc as plsc

D_MODEL = 1024
DEPTH = 4
GRID_W = 64
W_GROUP = 256
H_M = 4
DH_M = 64
S5_CH = 16
G_S5 = 16
P_S5 = 64
S5_STATE = G_S5 * P_S5
H_R = 4
DH_R = 64
H_D = 4
DH_D = 32
N_EXPERTS = 16
CAPACITY_FACTOR = 2
ROPE_BASE = 10000.0
EPS = 1e-6
SUBLANES = 8
VMEM_LIMIT_BYTES = 56 * 1024 * 1024

f32 = jnp.float32
bf16 = jnp.bfloat16
HIGHEST = lax.Precision.HIGHEST
NEG_INF = float("-inf")


def _gelu_tanh(x):
    return 0.5 * x * (1.0 + jnp.tanh(math.sqrt(2.0 / math.pi) * (x + 0.044715 * (x * x * x))))


def _s5_kernel(su_ref, x0r_ref, x0i_ref, wb_ref, wc_ref, lb_ref, pw_ref, d_ref, gw_ref, gb_ref,
               y_ref, xr_ref, xi_ref, st_ref, *, chained):
    n_steps = st_ref.shape[0] // SUBLANES
    su = su_ref[0]
    y_ref[0] = su * d_ref[...]
    row = lax.broadcasted_iota(jnp.int32, (SUBLANES, S5_STATE), 0)
    zeros = jnp.zeros((SUBLANES, S5_STATE), f32)
    for d in range(2):
        st_ref[...] = jnp.dot(su.astype(bf16), wb_ref[0, d], preferred_element_type=f32)
        lbr = jnp.broadcast_to(lb_ref[0, d, 0:1, :], (SUBLANES, S5_STATE))
        lbi = jnp.broadcast_to(lb_ref[0, d, 1:2, :], (SUBLANES, S5_STATE))

        def rows_of(k, d=d):
            kk = k if d == 0 else n_steps - 1 - k
            return pl.ds(pl.multiple_of(kk * SUBLANES, SUBLANES), SUBLANES)

        def scan_step(k, carry, lbr=lbr, lbi=lbi, rows_of=rows_of):
            xr, xi = carry
            r = rows_of(k)
            nxr = lbr * xr - lbi * xi + st_ref[r, 0:S5_STATE]
            nxi = lbr * xi + lbi * xr + st_ref[r, S5_STATE:2 * S5_STATE]
            st_ref[r, 0:S5_STATE] = nxr
            st_ref[r, S5_STATE:2 * S5_STATE] = nxi
            return nxr, nxi

        if not chained:
            xr_ref[0, d], xi_ref[0, d] = lax.fori_loop(0, n_steps, scan_step, (x0r_ref[0, d], x0i_ref[0, d]))
            y_ref[0] += jnp.dot(st_ref[...].astype(bf16), wc_ref[0, d], preferred_element_type=f32)
            continue
        fr, fi = lax.fori_loop(0, n_steps, scan_step, (zeros, zeros))

        cr = x0r_ref[0, d]
        ci = x0i_ref[0, d]
        plr = pw_ref[0, d, 0:1, :]
        pli = pw_ref[0, d, 1:2, :]
        cmr, cmi = zeros, zeros
        for i in (range(SUBLANES) if d == 0 else reversed(range(SUBLANES))):
            cmr = jnp.where(row == i, cr, cmr)
            cmi = jnp.where(row == i, ci, cmi)
            cr, ci = (plr * cr - pli * ci + fr[i:i + 1], plr * ci + pli * cr + fi[i:i + 1])
        xr_ref[0, d] = cr
        xi_ref[0, d] = ci

        def fix_step(k, carry, lbr=lbr, lbi=lbi, cmr=cmr, cmi=cmi, rows_of=rows_of):
            pr, pi = carry
            r = rows_of(k)
            st_ref[r, 0:S5_STATE] = st_ref[r, 0:S5_STATE] + (pr * cmr - pi * cmi)
            st_ref[r, S5_STATE:2 * S5_STATE] = st_ref[r, S5_STATE:2 * S5_STATE] + (pr * cmi + pi * cmr)
            return pr * lbr - pi * lbi, pr * lbi + pi * lbr

        lax.fori_loop(0, n_steps, fix_step, (lbr, lbi))
        y_ref[0] += jnp.dot(st_ref[...].astype(bf16), wc_ref[0, d], preferred_element_type=f32)

    ys = _gelu_tanh(y_ref[0])
    gate = jax.nn.sigmoid(jnp.dot(ys.astype(bf16), gw_ref[0], preferred_element_type=f32) + gb_ref[...])
    y_ref[0] = ys * gate


def _s5_prepare(lam_re, lam_im, log_step, b_re, b_im, c_re, c_im, glu_w, n_steps_list):
    dt = jnp.exp(log_step)[..., None]
    mag = jnp.exp(lam_re * dt)
    ang = lam_im * dt
    lb_re, lb_im = mag * jnp.cos(ang), mag * jnp.sin(ang)
    nr, ni = lb_re - 1.0, lb_im
    den = lam_re * lam_re + lam_im * lam_im
    f_re = (nr * lam_re + ni * lam_im) / den
    f_im = (ni * lam_re - nr * lam_im) / den
    bb_re = f_re[..., None] * b_re[:, None] - f_im[..., None] * b_im[:, None]
    bb_im = f_re[..., None] * b_im[:, None] + f_im[..., None] * b_re[:, None]
    def block_diag(a):
        rows, cols = G_S5 * a.shape[3], G_S5 * a.shape[4]
        tiled = jnp.tile(a.reshape(a.shape[:2] + (rows, a.shape[4])), (1, 1, 1, G_S5))
        same_group = (np.arange(rows)[:, None] // a.shape[3]) == (np.arange(cols)[None, :] // a.shape[4])
        return jnp.where(same_group, tiled, 0.0).astype(bf16)

    wb = jnp.zeros(bb_re.shape[:2] + (W_GROUP, 2 * S5_STATE), bf16)
    wb = wb.at[..., :S5_STATE].set(block_diag(jnp.swapaxes(bb_re, 3, 4)))
    wb = wb.at[..., S5_STATE:].set(block_diag(jnp.swapaxes(bb_im, 3, 4)))
    wc = jnp.zeros(c_re.shape[:2] + (2 * S5_STATE, W_GROUP), bf16)
    wc = wc.at[:, :, :S5_STATE].set(block_diag(jnp.swapaxes(c_re, 3, 4)))
    wc = wc.at[:, :, S5_STATE:].set(block_diag(jnp.swapaxes(-c_im, 3, 4)))
    lead = lb_re.shape[:2]
    lb = jnp.stack([lb_re.reshape(lead + (S5_STATE,)), lb_im.reshape(lead + (S5_STATE,))], axis=2)
    pr, pi = lb[:, :, 0], lb[:, :, 1]
    tables = {}
    for j in range(int(math.log2(max(n_steps_list))) + 1):
        if 2 ** j in n_steps_list:
            tables[2 ** j] = jnp.stack([pr, pi], axis=2)
        pr, pi = pr * pr - pi * pi, 2.0 * pr * pi
    return wb, wc, lb, tables, glu_w.astype(bf16)


def s5_mixer(su, x0r, x0i, s5p, s5_d, glu_b, layer):
    B, L, _ = su.shape
    wb, wc, lb, tables, glu_w = s5p
    chained = B % SUBLANES != 0
    if chained:
        n_groups, lanes, n_steps = B, 1, L // SUBLANES
        to_rows = lambda a: a.reshape(B, SUBLANES, n_steps, -1).transpose(0, 2, 1, 3).reshape(B, L, -1)
        from_rows = lambda a: a.reshape(B, n_steps, SUBLANES, -1).transpose(0, 2, 1, 3).reshape(B, L, -1)
        pw = tables[n_steps]
    else:
        n_groups, lanes, n_steps = B // SUBLANES, SUBLANES, L
        to_rows = lambda a: a.reshape(n_groups, SUBLANES, L, -1).transpose(0, 2, 1, 3).reshape(n_groups, -1, a.shape[-1])
        from_rows = lambda a: a.reshape(n_groups, L, SUBLANES, -1).transpose(0, 2, 1, 3).reshape(B, L, -1)
        pw = lb
    rows = n_steps * SUBLANES
    state_in = lambda a: a.reshape(n_groups, lanes, 2, S5_STATE).transpose(0, 2, 1, 3)
    state_out = lambda a: a.transpose(0, 2, 1, 3).reshape(B, 2, G_S5, P_S5)
    full = lambda *shape: pl.BlockSpec(shape, lambda b: (0,) * len(shape))
    per_b = lambda *shape: pl.BlockSpec((1,) + shape, lambda b: (b,) + (0,) * len(shape))
    per_layer = lambda *shape: pl.BlockSpec((1,) + shape, lambda b: (layer,) + (0,) * len(shape))
    y_p, xr, xi = pl.pallas_call(
        functools.partial(_s5_kernel, chained=chained),
        grid=(n_groups,),
        in_specs=[per_b(rows, W_GROUP), per_b(2, lanes, S5_STATE), per_b(2, lanes, S5_STATE),
                  per_layer(2, W_GROUP, 2 * S5_STATE), per_layer(2, 2 * S5_STATE, W_GROUP),
                  per_layer(2, 2, S5_STATE), per_layer(2, 2, S5_STATE),
                  full(1, W_GROUP), per_layer(W_GROUP, W_GROUP), full(1, W_GROUP)],
        out_specs=[per_b(rows, W_GROUP), per_b(2, lanes, S5_STATE), per_b(2, lanes, S5_STATE)],
        out_shape=[jax.ShapeDtypeStruct((n_groups, rows, W_GROUP), f32),
                   jax.ShapeDtypeStruct((n_groups, 2, lanes, S5_STATE), f32),
                   jax.ShapeDtypeStruct((n_groups, 2, lanes, S5_STATE), f32)],
        scratch_shapes=[pltpu.VMEM((rows, 2 * S5_STATE), f32)],
        compiler_params=pltpu.CompilerParams(dimension_semantics=("arbitrary",),
                                             vmem_limit_bytes=VMEM_LIMIT_BYTES),
        name="s5_mixer",
    )(to_rows(su), state_in(x0r), state_in(x0i), wb, wc, lb, pw,
      s5_d.reshape(1, W_GROUP), glu_w, glu_b.reshape(1, W_GROUP))
    return from_rows(y_p), state_out(xr), state_out(xi)


Q_TILE = 256
PROJ_BLOCKS = dict(mq=0, mk=1, mv=2, mo=3, su=4, rq=5, rk=6, rv=7, rg=8, dq=9, dk=10, dv=11)
GATE_LANES = 128
N_PROJ = 12 * W_GROUP + GATE_LANES


def _log_sigmoid(x):
    return jnp.minimum(x, 0.0) - jnp.log1p(jnp.exp(-jnp.abs(x)))


def _group_ones(width, group):
    shift = int(math.log2(group))
    r = lax.broadcasted_iota(jnp.int32, (width, width), 0) >> shift
    c = lax.broadcasted_iota(jnp.int32, (width, width), 1) >> shift
    return (r == c).astype(bf16)


def _split3(x):
    hi = x.astype(bf16)
    r = x - hi.astype(f32)
    mid = r.astype(bf16)
    return hi, mid, (r - mid.astype(f32)).astype(bf16)


def _group_mean(x, ones, group):
    return sum(jnp.dot(p, ones, preferred_element_type=f32) for p in _split3(x)) * (1.0 / group)


def _dot_nt(a, b):
    return lax.dot_general(a.astype(bf16), b.astype(bf16), (((1,), (1,)), ((), ())), preferred_element_type=f32)


def _dot(a, b):
    return jnp.dot(a.astype(bf16), b.astype(bf16), preferred_element_type=f32)


def _proj_block(name, rows):
    j = PROJ_BLOCKS[name]
    return pl.BlockSpec((1, rows, W_GROUP), lambda b, qi, j=j, rows=rows: (b, qi if rows == Q_TILE else 0, j))


def _mlstm_kernel(q_ref, k_ref, v_ref, o_ref, g_ref, gt_ref, gb_ref, gbt_ref, ng_ref, c0_ref, n0_ref, m0_ref,
                  h_ref, c_ref, n_ref, m_ref, gl_ref, gu_ref, rc_ref, vt_ref, ht_ref, *, seq_len, q_tile):
    L, TQ = seq_len, q_tile
    nq = L // TQ
    qi = pl.program_id(1)
    grow = gt_ref[0] + gbt_ref[...]

    @pl.when(qi == 0)
    def _():
        ss = lax.broadcasted_iota(jnp.int32, (L, L), 0)
        tt = lax.broadcasted_iota(jnp.int32, (L, L), 1)
        tri_le = (ss <= tt).astype(bf16)
        tri_ge = (ss >= tt).astype(bf16)
        rows = _split3(_log_sigmoid(grow))
        gl = sum(jnp.dot(p, tri_le, preferred_element_type=f32) for p in rows)
        gu = sum(jnp.dot(p, tri_ge, preferred_element_type=f32) for p in rows)
        for j in range(nq):
            gl_ref[j] = gl[:, j * TQ:(j + 1) * TQ]
            gu_ref[j] = gu[:, j * TQ:(j + 1) * TQ]
        gcol = g_ref[0] + gb_ref[...]
        cols = _split3(_log_sigmoid(gcol))
        glc = sum(jnp.dot(tri_ge, p, preferred_element_type=f32) for p in cols)
        guc = sum(jnp.dot(tri_le, p, preferred_element_type=f32) for p in cols)
        lane = lax.broadcasted_iota(jnp.int32, (L, GATE_LANES), 1)
        rc_ref[...] = pltpu.roll(gcol, 4, 1) - jnp.where(lane < 8, glc, guc)

        vt = v_ref[0].T
        kk = k_ref[0] * (DH_M ** -0.5)
        one_row = (lax.broadcasted_iota(jnp.int32, (DH_M, L), 0) == 0).astype(bf16)
        for h in range(H_M):
            hs = slice(h * DH_M, (h + 1) * DH_M)
            vt_ref[2 * h * DH_M:(2 * h + 1) * DH_M, :] = vt[hs, :].astype(bf16)
            vt_ref[(2 * h + 1) * DH_M:(2 * h + 2) * DH_M, :] = one_row
        for d in range(2):
            g_all = gl if d == 0 else gu
            for h in range(H_M):
                hs = slice(h * DH_M, (h + 1) * DH_M)
                ii, fi = 8 * d + h, 8 * d + 4 + h
                g_row = g_all[fi:fi + 1, :]
                g_tot = g_row[:, L - 1:L] if d == 0 else g_row[:, 0:1]
                wlog = g_tot - g_row + grow[ii:ii + 1, :]
                m0 = m0_ref[0, d:d + 1, h:h + 1]
                m_new = jnp.maximum(g_tot + m0, jnp.max(wlog, axis=1, keepdims=True))
                decay = jnp.exp(g_tot + m0 - m_new)
                w = jnp.exp(wlog - m_new)
                kh = kk[:, hs]
                c_ref[0, d, h] = decay * c0_ref[0, d, h] + _dot(vt[hs, :] * w, kh)
                n_upd = jnp.dot(jnp.broadcast_to(w, (SUBLANES, L)), kh, precision=HIGHEST,
                                preferred_element_type=f32)[0:1, :]
                n_ref[0, d, h:h + 1, :] = decay * n0_ref[0, d, h:h + 1, :] + n_upd
                m_ref[0, d:d + 1, h:h + 1] = m_new

    gl_t = gl_ref[qi]
    gu_t = gu_ref[qi]
    s_idx = lax.broadcasted_iota(jnp.int32, (L, TQ), 0)
    t_idx = qi * TQ + lax.broadcasted_iota(jnp.int32, (L, TQ), 1)
    low = s_idx <= t_idx
    upp = s_idx >= t_idx
    qt = q_ref[0].T.astype(bf16)
    k = (k_ref[0] * (DH_M ** -0.5)).astype(bf16)
    rc = rc_ref[...]
    row0 = lax.broadcasted_iota(jnp.int32, (DH_M, DH_M), 0) == 0
    for h in range(H_M):
        hs = slice(h * DH_M, (h + 1) * DH_M)
        qth = qt[hs, :]
        vta = vt_ref[2 * h * DH_M:(2 * h + 2) * DH_M, :]
        s0 = jnp.dot(k[:, hs], qth, preferred_element_type=f32)
        h_sum = None
        for d in range(2):
            fi = 8 * d + 4 + h
            g_t = (gl_t if d == 0 else gu_t)[fi:fi + 1, :]
            dlog = jnp.where(low if d == 0 else upp, rc[:, fi:fi + 1] + g_t, NEG_INF)
            inter = g_t + m0_ref[0, d:d + 1, h:h + 1]
            m_t = jnp.maximum(inter, jnp.max(dlog, axis=0, keepdims=True))
            p = s0 * jnp.exp(dlog - m_t)
            a = jnp.exp(inter - m_t)
            c0n0 = jnp.concatenate([c0_ref[0, d, h], jnp.where(row0, n0_ref[0, d, h:h + 1, :], 0.0)], axis=0)
            numden = (jnp.dot(vta, p.astype(bf16), preferred_element_type=f32)
                      + a * jnp.dot(c0n0.astype(bf16), qth, preferred_element_type=f32))
            scale = 1.0 / jnp.maximum(jnp.abs(numden[DH_M:DH_M + 1, :]), jnp.exp(-m_t))
            hd = numden[0:DH_M, :] * scale
            h_sum = hd if h_sum is None else h_sum + hd
        ht_ref[hs, :] = h_sum * lax.rsqrt(jnp.mean(h_sum * h_sum, axis=0, keepdims=True) + EPS)
    h_ref[0] = jax.nn.sigmoid(o_ref[0]) * (ht_ref[...].T * ng_ref[...])


def mlstm_mixer(proj, gates_t, c0, n0, m0, lp):
    B, L, _ = proj.shape
    TQ = min(L, Q_TILE)
    gb = lp['mlstm_gate_b'].reshape(1, 4 * H_M)
    const = lambda *shape: pl.BlockSpec(shape, lambda b, qi: (0,) * len(shape))
    per_b = lambda *shape: pl.BlockSpec((1,) + shape, lambda b, qi: (b,) + (0,) * len(shape))
    return pl.pallas_call(
        functools.partial(_mlstm_kernel, seq_len=L, q_tile=TQ),
        grid=(B, L // TQ),
        in_specs=[_proj_block('mq', TQ), _proj_block('mk', L), _proj_block('mv', L), _proj_block('mo', TQ),
                  pl.BlockSpec((1, L, GATE_LANES), lambda b, qi: (b, 0, 12 * W_GROUP // GATE_LANES)),
                  per_b(4 * H_M, L), const(1, GATE_LANES), const(4 * H_M, 1), const(1, W_GROUP),
                  per_b(2, H_M, DH_M, DH_M), per_b(2, H_M, DH_M), per_b(2, H_M)],
        out_specs=[pl.BlockSpec((1, TQ, W_GROUP), lambda b, qi: (b, qi, 0)),
                   per_b(2, H_M, DH_M, DH_M), per_b(2, H_M, DH_M), per_b(2, H_M)],
        out_shape=[jax.ShapeDtypeStruct((B, L, W_GROUP), f32),
                   jax.ShapeDtypeStruct((B, 2, H_M, DH_M, DH_M), f32),
                   jax.ShapeDtypeStruct((B, 2, H_M, DH_M), f32),
                   jax.ShapeDtypeStruct((B, 2, H_M), f32)],
        scratch_shapes=[pltpu.VMEM((L // TQ, 4 * H_M, TQ), f32), pltpu.VMEM((L // TQ, 4 * H_M, TQ), f32),
                        pltpu.VMEM((L, GATE_LANES), f32), pltpu.VMEM((2 * W_GROUP, L), bf16),
                        pltpu.VMEM((W_GROUP, TQ), f32)],
        compiler_params=pltpu.CompilerParams(dimension_semantics=("arbitrary", "arbitrary"),
                                             vmem_limit_bytes=VMEM_LIMIT_BYTES),
        name="mlstm_mixer",
    )(proj, proj, proj, proj, proj, gates_t, jnp.pad(gb, ((0, 0), (0, GATE_LANES - 4 * H_M))),
      gb.reshape(4 * H_M, 1), lp['mlstm_norm_g'].reshape(1, W_GROUP), c0, n0, m0)


def _retention_kernel(lg_ref, q_ref, k_ref, v_ref, g_ref, gn_ref, r0_ref, h_ref, r_ref, *, seq_len, q_tile):
    L, TQ = seq_len, q_tile
    qi = pl.program_id(1)
    t_col = qi * TQ + lax.broadcasted_iota(jnp.int32, (TQ, 1), 0)
    rel = (qi * TQ + lax.broadcasted_iota(jnp.int32, (TQ, L), 0)
           - lax.broadcasted_iota(jnp.int32, (TQ, L), 1)).astype(f32)
    q = q_ref[0]
    k = k_ref[0] * (DH_R ** -0.5)
    v = v_ref[0]
    ones = _group_ones(W_GROUP, DH_R)
    for h in range(H_R):
        hs = slice(h * DH_R, (h + 1) * DH_R)
        lgf, lgb = lg_ref[0, h], lg_ref[1, h]
        qh, kh, vh = q[:, hs], k[:, hs], v[:, hs]
        decay = jnp.where(rel > 0.0, jnp.exp(lgf * jnp.maximum(rel, 0.0)),
                          jnp.where(rel < 0.0, jnp.exp(lgb * jnp.maximum(-rel, 0.0)), 2.0))
        o = _dot(_dot_nt(qh, kh) * decay, vh)
        xi_f = jnp.exp(lgf * (t_col + 1).astype(f32))
        xi_b = jnp.exp(lgb * (L - t_col).astype(f32))
        o = o + xi_f * _dot(qh, r0_ref[0, 0, h]) + xi_b * _dot(qh, r0_ref[0, 1, h])
        h_ref[0, :, hs] = o
    o = h_ref[0]
    oc = o - _group_mean(o, ones, DH_R)
    y = oc * lax.rsqrt(_group_mean(oc * oc, ones, DH_R) + EPS) * gn_ref[...]
    h_ref[0] = y * jax.nn.silu(g_ref[0])

    @pl.when(qi == 0)
    def _():
        kt = k.T
        s_row = lax.broadcasted_iota(jnp.int32, (1, L), 1).astype(f32)
        for d in range(2):
            for h in range(H_R):
                hs = slice(h * DH_R, (h + 1) * DH_R)
                lg = lg_ref[d, h]
                zeta = jnp.exp(lg * ((L - 1.0) - s_row)) if d == 0 else jnp.exp(lg * s_row)
                r_ref[0, d, h] = jnp.exp(lg * float(L)) * r0_ref[0, d, h] + _dot(kt[hs, :] * zeta, v[:, hs])


def retention_mixer(proj, r0, lp):
    B, L, _ = proj.shape
    TQ = min(L, Q_TILE)
    log_gamma = -jnp.exp(lp['ret_decay'])
    per_b = lambda *shape: pl.BlockSpec((1,) + shape, lambda b, qi: (b,) + (0,) * len(shape))
    return pl.pallas_call(
        functools.partial(_retention_kernel, seq_len=L, q_tile=TQ),
        grid=(B, L // TQ),
        in_specs=[pl.BlockSpec(memory_space=pltpu.SMEM),
                  _proj_block('rq', TQ), _proj_block('rk', L), _proj_block('rv', L), _proj_block('rg', TQ),
                  pl.BlockSpec((1, W_GROUP), lambda b, qi: (0, 0)), per_b(2, H_R, DH_R, DH_R)],
        out_specs=[pl.BlockSpec((1, TQ, W_GROUP), lambda b, qi: (b, qi, 0)), per_b(2, H_R, DH_R, DH_R)],
        out_shape=[jax.ShapeDtypeStruct((B, L, W_GROUP), f32),
                   jax.ShapeDtypeStruct((B, 2, H_R, DH_R, DH_R), f32)],
        compiler_params=pltpu.CompilerParams(dimension_semantics=("arbitrary", "arbitrary"),
                                             vmem_limit_bytes=VMEM_LIMIT_BYTES),
        name="retention_mixer",
    )(log_gamma, proj, proj, proj, proj, lp['ret_gn_g'].reshape(1, W_GROUP), r0)


def _rope_tables(L):
    half = DH_D // 2
    freqs = ROPE_BASE ** (-np.arange(0, half, 2, dtype=np.float64) / half)
    pos = np.arange(L)
    row, col = (pos // GRID_W).astype(np.float64), (pos % GRID_W).astype(np.float64)
    ang = np.concatenate([np.tile(row[:, None] * freqs, (1, 2)), np.tile(col[:, None] * freqs, (1, 2))], axis=1)
    sign = np.tile(np.concatenate([-np.ones(half // 2), np.ones(half // 2)]), 2)
    cos = np.tile(np.cos(ang), (1, 2 * H_D))
    sin = np.tile(np.sin(ang) * sign, (1, 2 * H_D))
    return jnp.asarray(cos, f32), jnp.asarray(sin, f32)


def _swap_pairs(x):
    parts = []
    for j in range(x.shape[1] // 128):
        xs = x[:, j * 128:(j + 1) * 128]
        lane = lax.broadcasted_iota(jnp.int32, xs.shape, 1)
        parts.append(jnp.where((lane & 15) < 8, pltpu.roll(xs, 120, 1), pltpu.roll(xs, 8, 1)))
    return jnp.concatenate(parts, axis=1)


def _qk_norm(x, gain, ones):
    return x * lax.rsqrt(_group_mean(x * x, ones, DH_D) + EPS) * gain


def _diff_attn_kernel(lam_ref, q_ref, k_ref, v_ref, qkg_ref, sg_ref, *rest, seq_len, q_tile, past_len, out_scale):
    L, TQ, P = seq_len, q_tile, past_len
    if P:
        kc_ref, vc_ref, cos_ref, sin_ref, h_ref, ka_ref, vt_ref, ot_ref = rest
    else:
        h_ref, kn_ref, ka_ref, vt_ref, ot_ref = rest
    qi = pl.program_id(1)
    ones = _group_ones(W_GROUP, DH_D)

    @pl.when(qi == 0)
    def _():
        kn = _qk_norm(k_ref[0], qkg_ref[1:2, :], ones)
        if P:
            kn = kn * cos_ref[...] + _swap_pairs(kn) * sin_ref[...]
            ka_ref[0:P, :] = kc_ref[0, 0].astype(bf16)
            vt_ref[:, 0:P] = vc_ref[0, 0].T.astype(bf16)
        else:
            kn_ref[0] = kn
        ka_ref[P:P + L, :] = kn.astype(bf16)
        vt_ref[:, P:P + L] = v_ref[0].T.astype(bf16)

    qn = _qk_norm(q_ref[0], qkg_ref[0:1, :], ones)
    if P:
        rows = pl.ds(pl.multiple_of(qi * TQ, TQ), TQ)
        qn = qn * cos_ref[rows, :] + _swap_pairs(qn) * sin_ref[rows, :]
    qt = (qn * (DH_D ** -0.5)).T.astype(bf16)
    lam = lam_ref[0, 0]
    ka = ka_ref[...]
    for h in range(H_D):
        probs = []
        for j in range(2):
            cs = slice((2 * h + j) * DH_D, (2 * h + j + 1) * DH_D)
            s = jnp.dot(ka[:, cs], qt[cs, :], preferred_element_type=f32)
            e = jnp.exp(s - jnp.max(s, axis=0, keepdims=True))
            probs.append(e * (1.0 / jnp.sum(e, axis=0, keepdims=True)))
        vs = slice(h * 2 * DH_D, (h + 1) * 2 * DH_D)
        ot_ref[vs, :] = jnp.dot(vt_ref[vs, :], (probs[0] - lam * probs[1]).astype(bf16), preferred_element_type=f32)
    o = ot_ref[...].T
    ones_v = _group_ones(W_GROUP, 2 * DH_D)
    h_ref[0] = o * lax.rsqrt(_group_mean(o * o, ones_v, 2 * DH_D) + EPS) * (sg_ref[...] * out_scale)


def diff_attn_mixer(proj, cache, lp, lam_init, layer):
    B, L, _ = proj.shape
    TQ = min(L, Q_TILE)
    lv = lp['diff_lambda']
    lam = (jnp.exp(jnp.sum(lv[0] * lv[1])) - jnp.exp(jnp.sum(lv[2] * lv[3])) + lam_init).reshape(1, 1)
    qkg = jnp.tile(lp['diff_qk_norm'], (1, 2 * H_D))
    sg = jnp.tile(lp['diff_subln_g'], (H_D,)).reshape(1, W_GROUP)
    const = lambda *shape: pl.BlockSpec(shape, lambda b, qi: (0,) * len(shape))
    in_specs = [pl.BlockSpec(memory_space=pltpu.SMEM),
                _proj_block('dq', TQ), _proj_block('dk', L), _proj_block('dv', L), const(2, W_GROUP), const(1, W_GROUP)]
    args = [lam, proj, proj, proj, qkg, sg]
    out_specs = [pl.BlockSpec((1, TQ, W_GROUP), lambda b, qi: (b, qi, 0))]
    out_shape = [jax.ShapeDtypeStruct((B, L, W_GROUP), f32)]
    P = 0
    if cache is not None:
        ck, cv = cache
        P = ck.shape[2]
        cspec = pl.BlockSpec((1, 1, P, W_GROUP), lambda b, qi, layer=layer: (b, layer, 0, 0))
        cos, sin = _rope_tables(L)
        in_specs += [cspec, cspec, const(L, W_GROUP), const(L, W_GROUP)]
        args += [ck, cv, cos, sin]
    else:
        out_specs.append(pl.BlockSpec((1, L, W_GROUP), lambda b, qi: (b, 0, 0)))
        out_shape.append(jax.ShapeDtypeStruct((B, L, W_GROUP), f32))
    return pl.pallas_call(
        functools.partial(_diff_attn_kernel, seq_len=L, q_tile=TQ, past_len=P, out_scale=1.0 - lam_init),
        grid=(B, L // TQ),
        in_specs=in_specs, out_specs=out_specs, out_shape=out_shape,
        scratch_shapes=[pltpu.VMEM((P + L, W_GROUP), bf16), pltpu.VMEM((W_GROUP, P + L), bf16),
                        pltpu.VMEM((W_GROUP, TQ), f32)],
        compiler_params=pltpu.CompilerParams(dimension_semantics=("arbitrary", "arbitrary"),
                                             vmem_limit_bytes=VMEM_LIMIT_BYTES),
        name="diff_attention",
    )(*args)


PROJ_ROW_TILE = 512
OUT_ROW_TILE = 512
FF_TILE = 512
D_FF = 2 * D_MODEL


def _in_proj_kernel(x_ref, g_ref, sc_ref, sh_ref, w_ref, o_ref):
    x = x_ref[...]
    y = x * lax.rsqrt(jnp.mean(x * x, axis=1, keepdims=True) + EPS) * g_ref[...]
    h = (y * (1.0 + sc_ref[0]) + sh_ref[0]).astype(bf16)
    o_ref[...] = jnp.dot(h, w_ref[0], preferred_element_type=f32)


def in_projection(x, gain, scale, shift, w_p, layer, rows_per_mod):
    n, D = x.shape
    TM = min(n, PROJ_ROW_TILE)
    mod = pl.BlockSpec((1, 1, D), lambda i: (i * TM // rows_per_mod, 0, 0))
    return pl.pallas_call(
        _in_proj_kernel,
        grid=(n // TM,),
        in_specs=[pl.BlockSpec((TM, D), lambda i: (i, 0)), pl.BlockSpec((1, D), lambda i: (0, 0)), mod, mod,
                  pl.BlockSpec((1, D, N_PROJ), lambda i: (layer, 0, 0))],
        out_specs=pl.BlockSpec((TM, N_PROJ), lambda i: (i, 0)),
        out_shape=jax.ShapeDtypeStruct((n, N_PROJ), f32),
        compiler_params=pltpu.CompilerParams(dimension_semantics=("arbitrary",),
                                             vmem_limit_bytes=VMEM_LIMIT_BYTES),
        name="in_projection",
    )(x, gain.reshape(1, D), scale, shift, w_p)


def _out_proj_kernel(x_ref, m0_ref, m1_ref, m2_ref, m3_ref, w_ref, g1_ref, ng_ref, sc_ref, sh_ref, rw_ref,
                     xo_ref, h_ref, aff_ref, wb_ref):
    @pl.when(pl.program_id(0) == 0)
    def _():
        wb_ref[...] = w_ref[0].astype(bf16)

    out = None
    for j, m_ref in enumerate((m0_ref, m1_ref, m2_ref, m3_ref)):
        part = jnp.dot(m_ref[...].astype(bf16), wb_ref[j * W_GROUP:(j + 1) * W_GROUP, :], preferred_element_type=f32)
        out = part if out is None else out + part
    x = x_ref[...] + g1_ref[0] * out
    xo_ref[...] = x
    h = x * lax.rsqrt(jnp.mean(x * x, axis=1, keepdims=True) + EPS) * ng_ref[...]
    h = h * (1.0 + sc_ref[0]) + sh_ref[0]
    h_hi = h.astype(bf16)
    bits = pltpu.bitcast(h_hi.astype(f32), jnp.uint32)
    half = bits.shape[1] // 2
    h_ref[...] = (bits[:, :half] >> 16) | bits[:, half:]
    h_lo = (h - h_hi.astype(f32)).astype(bf16)
    both = jnp.dot(h_hi, rw_ref[0], preferred_element_type=f32)
    logits = (both[:, 0:N_EXPERTS] + both[:, N_EXPERTS:2 * N_EXPERTS]
              + jnp.dot(h_lo, rw_ref[0], preferred_element_type=f32)[:, 0:N_EXPERTS])
    e = jnp.exp(logits - jnp.max(logits, axis=1, keepdims=True))
    aff_ref[...] = e / jnp.sum(e, axis=1, keepdims=True)


def out_projection(x, mixed, w_out, gate1, gain2, scale2, shift2, router_w2, layer, rows_per_mod):
    n, D = x.shape
    TM = OUT_ROW_TILE
    row = lambda width: pl.BlockSpec((TM, width), lambda i: (i, 0))
    const = lambda *shape: pl.BlockSpec(shape, lambda i: (0,) * len(shape))
    mod = pl.BlockSpec((1, 1, D), lambda i: (i * TM // rows_per_mod, 0, 0))
    return pl.pallas_call(
        _out_proj_kernel,
        grid=(n // TM,),
        in_specs=[row(D), row(W_GROUP), row(W_GROUP), row(W_GROUP), row(W_GROUP),
                  pl.BlockSpec((1, D, D), lambda i: (layer, 0, 0)), mod, const(1, D), mod, mod,
                  pl.BlockSpec((1, D, 2 * N_EXPERTS), lambda i: (layer, 0, 0))],
        out_specs=[row(D), row(D // 2), row(N_EXPERTS)],
        out_shape=[jax.ShapeDtypeStruct((n, D), f32), jax.ShapeDtypeStruct((n, D // 2), jnp.uint32),
                   jax.ShapeDtypeStruct((n, N_EXPERTS), f32)],
        scratch_shapes=[pltpu.VMEM((D, D), bf16)],
        compiler_params=pltpu.CompilerParams(dimension_semantics=("arbitrary",), vmem_limit_bytes=VMEM_LIMIT_BYTES),
        name="out_projection",
    )(x, *mixed, w_out, gate1, gain2.reshape(1, D), scale2, shift2, router_w2)


def _experts_kernel(xc_ref, xl_ref, gc_ref, gl_ref, g2_ref, g2l_ref, il_ref, wg_ref, wu_ref, wd_ref, yc_ref, yl_ref,
                    ac_ref, al_ref, *, lat_len):
    f = pl.program_id(1)
    wg = wg_ref[0, 0].astype(bf16)
    wu = wu_ref[0, 0].astype(bf16)
    wd = wd_ref[0, 0].astype(bf16)
    for x_ref, acc_ref in ((xc_ref, ac_ref), (xl_ref, al_ref)):
        words = x_ref[0]
        x = jnp.concatenate([pltpu.bitcast(words << 16, f32), pltpu.bitcast(words & jnp.uint32(0xFFFF0000), f32)],
                            axis=1).astype(bf16)
        hidden = jax.nn.silu(jnp.dot(x, wg, preferred_element_type=f32)) * jnp.dot(x, wu, preferred_element_type=f32)
        part = jnp.dot(hidden.astype(bf16), wd, preferred_element_type=f32)

        @pl.when(f == 0)
        def _(acc_ref=acc_ref, part=part):
            acc_ref[...] = part

        @pl.when(f > 0)
        def _(acc_ref=acc_ref, part=part):
            acc_ref[...] += part

    @pl.when(f == pl.num_programs(1) - 1)
    def _():
        yc_ref[0] = ac_ref[...] * gc_ref[0] * g2_ref[0]
        gate = g2l_ref[0]
        for b in range(1, g2l_ref.shape[0]):
            gate = jnp.where(il_ref[0] >= b * lat_len, g2l_ref[b], gate)
        yl_ref[0] = al_ref[...] * gl_ref[0] * gate


def expert_ffn(xe_c, xe_l, g_c, g_l, gate2_c, gate2_l, idx_l, lat_len, w_gate, w_up, w_down, layer):
    E, Cc, half_d = xe_c.shape
    D = 2 * half_d
    Cl = xe_l.shape[1]
    tok = lambda C, width: pl.BlockSpec((1, C, width), lambda e, f: (e, 0, 0))
    return pl.pallas_call(
        functools.partial(_experts_kernel, lat_len=lat_len),
        grid=(E, D_FF // FF_TILE),
        in_specs=[tok(Cc, half_d), tok(Cl, half_d), tok(Cc, 1), tok(Cl, 1), pl.BlockSpec((1, 1, D), lambda e, f: (0, 0, 0)),
                  pl.BlockSpec(gate2_l.shape, lambda e, f: (0, 0, 0)), tok(Cl, 1),
                  pl.BlockSpec((1, 1, D, FF_TILE), lambda e, f: (layer, e, 0, f)),
                  pl.BlockSpec((1, 1, D, FF_TILE), lambda e, f: (layer, e, 0, f)),
                  pl.BlockSpec((1, 1, FF_TILE, D), lambda e, f: (layer, e, f, 0))],
        out_specs=[tok(Cc, D), tok(Cl, D)],
        out_shape=[jax.ShapeDtypeStruct((E, Cc, D), f32), jax.ShapeDtypeStruct((E, Cl, D), f32)],
        scratch_shapes=[pltpu.VMEM((Cc, D), f32), pltpu.VMEM((Cl, D), f32)],
        compiler_params=pltpu.CompilerParams(dimension_semantics=("arbitrary", "arbitrary"),
                                             vmem_limit_bytes=VMEM_LIMIT_BYTES),
        name="expert_ffn",
    )(xe_c, xe_l, g_c, g_l, gate2_c, gate2_l, idx_l, w_gate, w_up, w_down)


def _permute_w_in(w):
    gate0, n_gate = 4 * W_GROUP, 4 * H_M
    out = jnp.zeros(w.shape[:-1] + (N_PROJ,), bf16)
    out = out.at[..., :gate0].set(w[..., :gate0].astype(bf16))
    out = out.at[..., gate0:12 * W_GROUP].set(w[..., gate0 + n_gate:].astype(bf16))
    return out.at[..., 12 * W_GROUP:12 * W_GROUP + n_gate].set(w[..., gate0:gate0 + n_gate].astype(bf16))


def token_mixers(proj, lp, s5p, lam_init, states, cache, layer):
    B, L, _ = proj.shape
    mC0, mn0, mm0, s5r0, s5i0, R0 = states
    gates_t = jnp.swapaxes(proj[:, :, 12 * W_GROUP:12 * W_GROUP + 4 * H_M], 1, 2)
    hm, mC, mn, mm = mlstm_mixer(proj, gates_t, mC0, mn0, mm0, lp)
    su = proj[:, :, PROJ_BLOCKS['su'] * W_GROUP:(PROJ_BLOCKS['su'] + 1) * W_GROUP]
    ys, s5r, s5i = s5_mixer(su, s5r0, s5i0, s5p, lp['s5_d'], lp['s5_glu_b'], layer)
    hr, R = retention_mixer(proj, R0, lp)
    attn = diff_attn_mixer(proj, cache, lp, lam_init, layer)
    mixed = [a.reshape(B * L, W_GROUP) for a in (hm, ys, hr, attn[0])]
    new_ctx = None
    if cache is None:
        v = proj[:, :, PROJ_BLOCKS['dv'] * W_GROUP:(PROJ_BLOCKS['dv'] + 1) * W_GROUP]
        new_ctx = (mC, mn, mm, s5r, s5i, R, attn[1].reshape(B, L, 2 * H_D, DH_D), v.reshape(B, L, H_D, 2 * DH_D))
    return mixed, new_ctx


GATHER_WINDOW = 32
INDEX_LANES = 128


def gather_rows(x, idx):
    num = idx.shape[0]
    width = x.shape[1]
    mesh = plsc.VectorSubcoreMesh(core_axis_name="core", subcore_axis_name="subcore")
    per_core = num // GATHER_WINDOW // mesh.num_cores
    idx_rows = jnp.pad(idx.reshape(num // GATHER_WINDOW, GATHER_WINDOW), ((0, 0), (0, INDEX_LANES - GATHER_WINDOW)))

    @pl.kernel(out_type=jax.ShapeDtypeStruct((num, width), x.dtype), mesh=mesh)
    def gather_kernel(x_hbm, i_hbm, o_hbm):
        base = lax.axis_index("core") * per_core

        def body(i_vmem, o_vmem):
            pltpu.sync_copy(x_hbm.at[i_vmem.at[0, pl.ds(0, GATHER_WINDOW)]], o_vmem)

        pltpu.emit_pipeline(
            body,
            grid=(per_core,),
            in_specs=[pl.BlockSpec((1, INDEX_LANES), index_map=lambda i: (base + i, 0))],
            out_specs=[pl.BlockSpec((GATHER_WINDOW, width), index_map=lambda i: (base + i, 0))],
            core_axis_name="subcore",
            dimension_semantics=(pltpu.PARALLEL,),
        )(i_hbm, o_hbm)

    return gather_kernel(x, idx_rows)


SC_LANES = 16


def _threshold_kernel(aff_ref, t_ref, r_ref, *, cap):
    bits = pltpu.bitcast(aff_ref[...], jnp.int32)
    t = jnp.zeros((N_EXPERTS, 1), jnp.int32)
    for b in range(30, -1, -1):
        cand = t | (1 << b)
        cnt = jnp.sum((bits >= cand).astype(f32), axis=1, keepdims=True)
        t = jnp.where(cnt >= cap, cand, t)
    n_gt = jnp.sum((bits > t).astype(f32), axis=1, keepdims=True)
    t_ref[...] = jnp.broadcast_to(t, (N_EXPERTS, GATE_LANES))
    r_ref[...] = jnp.broadcast_to(cap - n_gt.astype(jnp.int32), (N_EXPERTS, GATE_LANES))


def expert_choice_select(aff_t, cap):
    n = aff_t.shape[1]
    t_bits, n_ties = pl.pallas_call(
        functools.partial(_threshold_kernel, cap=cap),
        out_shape=[jax.ShapeDtypeStruct((N_EXPERTS, GATE_LANES), jnp.int32)] * 2,
        name="select_threshold",
    )(aff_t)
    mesh = plsc.VectorSubcoreMesh(core_axis_name="core", subcore_axis_name="subcore")
    params = pltpu.CompilerParams()
    if "needs_layout_passes" in pltpu.CompilerParams.__dataclass_fields__:
        params = dataclasses.replace(params, needs_layout_passes=False)

    @pl.kernel(out_type=[jax.ShapeDtypeStruct((N_EXPERTS, cap), jnp.int32), jax.ShapeDtypeStruct((N_EXPERTS, cap), f32)],
               mesh=mesh, compiler_params=params,
               scratch_types=[pltpu.VMEM((n,), f32), pltpu.VMEM((cap + SC_LANES,), jnp.int32),
                              pltpu.VMEM((cap + SC_LANES,), f32), pltpu.VMEM((SC_LANES,), jnp.int32),
                              pltpu.VMEM((SC_LANES,), jnp.int32)])
    def select_kernel(aff_hbm, t_hbm, r_hbm, idx_hbm, g_hbm, row, ibuf, gbuf, tv, rv):
        e = lax.axis_index("subcore")

        @pl.when(lax.axis_index("core") == 0)
        def _():
            pltpu.sync_copy(aff_hbm.at[e], row)
            pltpu.sync_copy(t_hbm.at[e], tv)
            pltpu.sync_copy(r_hbm.at[e], rv)
            t = tv[...]
            r = rv[...]
            lane = lax.iota(jnp.int32, SC_LANES)

            def body(j, carry):
                off, seen_eq = carry
                v = row[pl.ds(j * SC_LANES, SC_LANES)]
                b = plsc.bitcast(v, jnp.int32)
                eq = b == t
                eq_i = eq.astype(jnp.int32)
                take = (b > t) | (eq & (seen_eq + plsc.cumsum(eq_i) <= r))
                plsc.store_compressed(ibuf.at[pl.ds(off, SC_LANES)], lane + j * SC_LANES, mask=take)
                plsc.store_compressed(gbuf.at[pl.ds(off, SC_LANES)], v, mask=take)
                return off + jnp.sum(take.astype(jnp.int32)), seen_eq + jnp.sum(eq_i)

            lax.fori_loop(0, n // SC_LANES, body, (jnp.int32(0), jnp.int32(0)))
            pltpu.sync_copy(ibuf.at[pl.ds(0, cap)], idx_hbm.at[e])
            pltpu.sync_copy(gbuf.at[pl.ds(0, cap)], g_hbm.at[e])

    idx, gates = select_kernel(aff_t, t_bits[:, :SC_LANES], n_ties[:, :SC_LANES])
    return idx, gates


def _route(aff, h2):
    n = aff.shape[0]
    idx, gates = expert_choice_select(aff.T, CAPACITY_FACTOR * n // N_EXPERTS)
    return gates[..., None], idx, gather_rows(h2, idx.reshape(-1)).reshape(idx.shape + h2.shape[1:])


SCATTER_COLS = 128
SCATTER_WINDOW = 128


def scatter_add_rows(x, ye, idx):
    n, D = x.shape
    R = ye.shape[0]
    mesh = plsc.VectorSubcoreMesh(core_axis_name="core", subcore_axis_name="subcore")
    rows = n // mesh.num_subcores
    wins = R // SCATTER_WINDOW // mesh.num_subcores

    @pl.kernel(out_type=jax.ShapeDtypeStruct((n, D), f32), mesh=mesh,
               scratch_types=[pltpu.VMEM_SHARED((n, SCATTER_COLS), f32), pltpu.VMEM((SCATTER_WINDOW, SCATTER_COLS), f32),
                              pltpu.VMEM((SCATTER_WINDOW,), jnp.int32)])
    def scatter_kernel(x_hbm, ye_hbm, i_hbm, o_hbm, shared, buf, ibuf):
        core = lax.axis_index("core")
        sid = lax.axis_index("subcore")
        r0 = pl.multiple_of(sid * rows, SUBLANES)
        for slab in range(D // SCATTER_COLS):
            cols = pl.ds(slab * SCATTER_COLS, SCATTER_COLS)

            @pl.when(core == slab % mesh.num_cores)
            def _(cols=cols):
                pltpu.sync_copy(x_hbm.at[pl.ds(r0, rows), cols], shared.at[pl.ds(r0, rows)])
                plsc.subcore_barrier()

                @pl.loop(0, wins)
                def _(w):
                    win = w * mesh.num_subcores + sid
                    pltpu.sync_copy(i_hbm.at[win], ibuf)
                    pltpu.sync_copy(ye_hbm.at[pl.ds(pl.multiple_of(win * SCATTER_WINDOW, SCATTER_WINDOW),
                                                    SCATTER_WINDOW), cols], buf)
                    pltpu.sync_copy(buf, shared.at[ibuf], add=True)

                plsc.subcore_barrier()
                pltpu.sync_copy(shared.at[pl.ds(r0, rows)], o_hbm.at[pl.ds(r0, rows), cols])
                plsc.subcore_barrier()

    return scatter_kernel(x, ye, idx.reshape(R // SCATTER_WINDOW, SCATTER_WINDOW))


PER_LAYER = ('norm1_g', 'norm2_g', 'mlstm_gate_b', 'mlstm_norm_g', 's5_d', 's5_glu_b', 'ret_decay',
             'ret_gn_g', 'diff_qk_norm', 'diff_lambda', 'diff_subln_g')


def kernel(x_prompt, x_sample, state_mlstm_c, state_mlstm_n, state_mlstm_m, state_s5_re, state_s5_im, state_ret, cache_diff_k, cache_diff_v, c, c_ctx, norm1_g, norm2_g, ada_w, ada_b, w_in, w_out, mlstm_gate_b, mlstm_norm_g, s5_lambda_re, s5_lambda_im, s5_log_step, s5_b_re, s5_b_im, s5_c_re, s5_c_im, s5_d, s5_glu_w, s5_glu_b, ret_decay, ret_gn_g, diff_qk_norm, diff_lambda, diff_subln_g, router_w, exp_w_gate, exp_w_up, exp_w_down):
    weights = dict(norm1_g=norm1_g, norm2_g=norm2_g, mlstm_gate_b=mlstm_gate_b,
                   mlstm_norm_g=mlstm_norm_g, s5_d=s5_d, s5_glu_b=s5_glu_b, ret_decay=ret_decay, ret_gn_g=ret_gn_g,
                   diff_qk_norm=diff_qk_norm, diff_lambda=diff_lambda, diff_subln_g=diff_subln_g)
    w_in_p = _permute_w_in(w_in)
    rw_hi = router_w.astype(bf16)
    router_w2 = jnp.concatenate([rw_hi, (router_w - rw_hi.astype(f32)).astype(bf16)], axis=-1)
    s5p = _s5_prepare(s5_lambda_re, s5_lambda_im, s5_log_step, s5_b_re, s5_b_im, s5_c_re, s5_c_im, s5_glu_w,
                      (x_prompt.shape[1] // SUBLANES, x_sample.shape[1] // SUBLANES))
    Bc, Lc, D = x_prompt.shape
    Bl, Ll, _ = x_sample.shape
    xc = x_prompt.reshape(Bc * Lc, D)
    xl = x_sample.reshape(Bl * Ll, D)
    zero_states = (jnp.zeros((Bc, 2, H_M, DH_M, DH_M), f32), jnp.zeros((Bc, 2, H_M, DH_M), f32),
                   jnp.zeros((Bc, 2, H_M), f32), jnp.zeros((Bc, 2, G_S5, P_S5), f32),
                   jnp.zeros((Bc, 2, G_S5, P_S5), f32), jnp.zeros((Bc, 2, H_R, DH_R, DH_R), f32))
    cache = (cache_diff_k.reshape(cache_diff_k.shape[:3] + (W_GROUP,)),
             cache_diff_v.reshape(cache_diff_v.shape[:3] + (W_GROUP,)))
    cvec = jnp.concatenate([c_ctx[None, :], c], axis=0)
    outs = [[] for _ in range(8)]
    for l in range(DEPTH):
        lp = {name: weights[name][l] for name in PER_LAYER}
        lam_init = 0.8 - 0.6 * math.exp(-0.3 * l)
        mods = jnp.split((jax.nn.silu(cvec) @ ada_w[l] + ada_b[l])[:, None, :], 6, axis=-1)
        lat_states = (state_mlstm_c[:, l], state_mlstm_n[:, l], state_mlstm_m[:, l], state_s5_re[:, l],
                      state_s5_im[:, l], state_ret[:, l])
        def mix(x, B, L, sel, states, kv):
            sh1, sc1 = mods[0][sel], mods[1][sel]
            proj = in_projection(x, lp['norm1_g'], sc1, sh1, w_in_p, l, x.shape[0] // sh1.shape[0]).reshape(B, L, N_PROJ)
            return token_mixers(proj, lp, s5p, lam_init, states, kv, l)

        def project_and_route(x, mixed, sel):
            g1, sh2, sc2, g2 = (m[sel] for m in mods[2:])
            x1, h2, aff = out_projection(x, mixed, w_out, g1, lp['norm2_g'], sc2, sh2, router_w2, l,
                                         x.shape[0] // g1.shape[0])
            return (x1, g2, g1.shape[0]) + _route(aff, h2)

        sel_c, sel_l = slice(0, 1), slice(1, 1 + Bl)
        mixed_l, _ = mix(xl, Bl, Ll, sel_l, lat_states, cache)
        xc, mixed_l = lax.optimization_barrier((xc, mixed_l))
        mixed_c, new_ctx = mix(xc, Bc, Lc, sel_c, zero_states, None)
        for acc, t in zip(outs, new_ctx):
            acc.append(t)
        x1c, g2c, nbc, gc, idxc, xec = project_and_route(xc, mixed_c, sel_c)
        x1l, g2l, nbl, gl, idxl, xel = project_and_route(xl, mixed_l, sel_l)
        yec, yel = expert_ffn(xec, xel, gc, gl, g2c, g2l, idxl[..., None], Ll, exp_w_gate, exp_w_up, exp_w_down, l)
        xl = scatter_add_rows(x1l, yel.reshape(-1, D), idxl.reshape(-1))
        xc = scatter_add_rows(x1c, yec.reshape(-1, D), idxc.reshape(-1))
    _, outs = lax.optimization_barrier((yec, outs))
    return (xc.reshape(Bc, Lc, D), xl.reshape(Bl, Ll, D)) + tuple(jnp.stack(o, axis=1) for o in outs)
```

```python
import dataclasses
import functools
import math

import jax
import jax.numpy as jnp
import numpy as np
from jax import lax
from jax.experimental import pallas as pl
from jax.experimental.pallas import tpu as pltpu
from jax.experimental.pallas import tpu_sc as plsc

D_MODEL = 1024
DEPTH = 4
GRID_W = 64
W_GROUP = 256
H_M = 4
DH_M = 64
S5_CH = 16
G_S5 = 16
P_S5 = 64
S5_STATE = G_S5 * P_S5
H_R = 4
DH_R = 64
H_D = 4
DH_D = 32
N_EXPERTS = 16
CAPACITY_FACTOR = 2
ROPE_BASE = 10000.0
EPS = 1e-6
SUBLANES = 8
VMEM_LIMIT_BYTES = 56 * 1024 * 1024

f32 = jnp.float32
bf16 = jnp.bfloat16
HIGHEST = lax.Precision.HIGHEST
NEG_INF = float("-inf")


def _gelu_tanh(x):
    return 0.5 * x * (1.0 + jnp.tanh(math.sqrt(2.0 / math.pi) * (x + 0.044715 * (x * x * x))))


def _s5_kernel(su_ref, x0r_ref, x0i_ref, wb_ref, wc_ref, lb_ref, pw_ref, d_ref, gw_ref, gb_ref,
               y_ref, xr_ref, xi_ref, st_ref, *, chained):
    n_steps = st_ref.shape[0] // SUBLANES
    su = su_ref[0]
    y_ref[0] = su * d_ref[...]
    row = lax.broadcasted_iota(jnp.int32, (SUBLANES, S5_STATE), 0)
    zeros = jnp.zeros((SUBLANES, S5_STATE), f32)
    for d in range(2):
        st_ref[...] = jnp.dot(su.astype(bf16), wb_ref[0, d], preferred_element_type=f32)
        lbr = jnp.broadcast_to(lb_ref[0, d, 0:1, :], (SUBLANES, S5_STATE))
        lbi = jnp.broadcast_to(lb_ref[0, d, 1:2, :], (SUBLANES, S5_STATE))

        def rows_of(k, d=d):
            kk = k if d == 0 else n_steps - 1 - k
            return pl.ds(pl.multiple_of(kk * SUBLANES, SUBLANES), SUBLANES)

        def scan_step(k, carry, lbr=lbr, lbi=lbi, rows_of=rows_of):
            xr, xi = carry
            r = rows_of(k)
            nxr = lbr * xr - lbi * xi + st_ref[r, 0:S5_STATE]
            nxi = lbr * xi + lbi * xr + st_ref[r, S5_STATE:2 * S5_STATE]
            st_ref[r, 0:S5_STATE] = nxr
            st_ref[r, S5_STATE:2 * S5_STATE] = nxi
            return nxr, nxi

        if not chained:
            xr_ref[0, d], xi_ref[0, d] = lax.fori_loop(0, n_steps, scan_step, (x0r_ref[0, d], x0i_ref[0, d]))
            y_ref[0] += jnp.dot(st_ref[...].astype(bf16), wc_ref[0, d], preferred_element_type=f32)
            continue
        fr, fi = lax.fori_loop(0, n_steps, scan_step, (zeros, zeros))

        cr = x0r_ref[0, d]
        ci = x0i_ref[0, d]
        plr = pw_ref[0, d, 0:1, :]
        pli = pw_ref[0, d, 1:2, :]
        cmr, cmi = zeros, zeros
        for i in (range(SUBLANES) if d == 0 else reversed(range(SUBLANES))):
            cmr = jnp.where(row == i, cr, cmr)
            cmi = jnp.where(row == i, ci, cmi)
            cr, ci = (plr * cr - pli * ci + fr[i:i + 1], plr * ci + pli * cr + fi[i:i + 1])
        xr_ref[0, d] = cr
        xi_ref[0, d] = ci

        def fix_step(k, carry, lbr=lbr, lbi=lbi, cmr=cmr, cmi=cmi, rows_of=rows_of):
            pr, pi = carry
            r = rows_of(k)
            st_ref[r, 0:S5_STATE] = st_ref[r, 0:S5_STATE] + (pr * cmr - pi * cmi)
            st_ref[r, S5_STATE:2 * S5_STATE] = st_ref[r, S5_STATE:2 * S5_STATE] + (pr * cmi + pi * cmr)
            return pr * lbr - pi * lbi, pr * lbi + pi * lbr

        lax.fori_loop(0, n_steps, fix_step, (lbr, lbi))
        y_ref[0] += jnp.dot(st_ref[...].astype(bf16), wc_ref[0, d], preferred_element_type=f32)

    ys = _gelu_tanh(y_ref[0])
    gate = jax.nn.sigmoid(jnp.dot(ys.astype(bf16), gw_ref[0], preferred_element_type=f32) + gb_ref[...])
    y_ref[0] = ys * gate


def _s5_prepare(lam_re, lam_im, log_step, b_re, b_im, c_re, c_im, glu_w, n_steps_list):
    dt = jnp.exp(log_step)[..., None]
    mag = jnp.exp(lam_re * dt)
    ang = lam_im * dt
    lb_re, lb_im = mag * jnp.cos(ang), mag * jnp.sin(ang)
    nr, ni = lb_re - 1.0, lb_im
    den = lam_re * lam_re + lam_im * lam_im
    f_re = (nr * lam_re + ni * lam_im) / den
    f_im = (ni * lam_re - nr * lam_im) / den
    bb_re = f_re[..., None] * b_re[:, None] - f_im[..., None] * b_im[:, None]
    bb_im = f_re[..., None] * b_im[:, None] + f_im[..., None] * b_re[:, None]
    def block_diag(a):
        rows, cols = G_S5 * a.shape[3], G_S5 * a.shape[4]
        tiled = jnp.tile(a.reshape(a.shape[:2] + (rows, a.shape[4])), (1, 1, 1, G_S5))
        same_group = (np.arange(rows)[:, None] // a.shape[3]) == (np.arange(cols)[None, :] // a.shape[4])
        return jnp.where(same_group, tiled, 0.0).astype(bf16)

    wb = jnp.zeros(bb_re.shape[:2] + (W_GROUP, 2 * S5_STATE), bf16)
    wb = wb.at[..., :S5_STATE].set(block_diag(jnp.swapaxes(bb_re, 3, 4)))
    wb = wb.at[..., S5_STATE:].set(block_diag(jnp.swapaxes(bb_im, 3, 4)))
    wc = jnp.zeros(c_re.shape[:2] + (2 * S5_STATE, W_GROUP), bf16)
    wc = wc.at[:, :, :S5_STATE].set(block_diag(jnp.swapaxes(c_re, 3, 4)))
    wc = wc.at[:, :, S5_STATE:].set(block_diag(jnp.swapaxes(-c_im, 3, 4)))
    lead = lb_re.shape[:2]
    lb = jnp.stack([lb_re.reshape(lead + (S5_STATE,)), lb_im.reshape(lead + (S5_STATE,))], axis=2)
    pr, pi = lb[:, :, 0], lb[:, :, 1]
    tables = {}
    for j in range(int(math.log2(max(n_steps_list))) + 1):
        if 2 ** j in n_steps_list:
            tables[2 ** j] = jnp.stack([pr, pi], axis=2)
        pr, pi = pr * pr - pi * pi, 2.0 * pr * pi
    return wb, wc, lb, tables, glu_w.astype(bf16)


def s5_mixer(su, x0r, x0i, s5p, s5_d, glu_b, layer):
    B, L, _ = su.shape
    wb, wc, lb, tables, glu_w = s5p
    chained = B % SUBLANES != 0
    if chained:
        n_groups, lanes, n_steps = B, 1, L // SUBLANES
        to_rows = lambda a: a.reshape(B, SUBLANES, n_steps, -1).transpose(0, 2, 1, 3).reshape(B, L, -1)
        from_rows = lambda a: a.reshape(B, n_steps, SUBLANES, -1).transpose(0, 2, 1, 3).reshape(B, L, -1)
        pw = tables[n_steps]
    else:
        n_groups, lanes, n_steps = B // SUBLANES, SUBLANES, L
        to_rows = lambda a: a.reshape(n_groups, SUBLANES, L, -1).transpose(0, 2, 1, 3).reshape(n_groups, -1, a.shape[-1])
        from_rows = lambda a: a.reshape(n_groups, L, SUBLANES, -1).transpose(0, 2, 1, 3).reshape(B, L, -1)
        pw = lb
    rows = n_steps * SUBLANES
    state_in = lambda a: a.reshape(n_groups, lanes, 2, S5_STATE).transpose(0, 2, 1, 3)
    state_out = lambda a: a.transpose(0, 2, 1, 3).reshape(B, 2, G_S5, P_S5)
    full = lambda *shape: pl.BlockSpec(shape, lambda b: (0,) * len(shape))
    per_b = lambda *shape: pl.BlockSpec((1,) + shape, lambda b: (b,) + (0,) * len(shape))
    per_layer = lambda *shape: pl.BlockSpec((1,) + shape, lambda b: (layer,) + (0,) * len(shape))
    y_p, xr, xi = pl.pallas_call(
        functools.partial(_s5_kernel, chained=chained),
        grid=(n_groups,),
        in_specs=[per_b(rows, W_GROUP), per_b(2, lanes, S5_STATE), per_b(2, lanes, S5_STATE),
                  per_layer(2, W_GROUP, 2 * S5_STATE), per_layer(2, 2 * S5_STATE, W_GROUP),
                  per_layer(2, 2, S5_STATE), per_layer(2, 2, S5_STATE),
                  full(1, W_GROUP), per_layer(W_GROUP, W_GROUP), full(1, W_GROUP)],
        out_specs=[per_b(rows, W_GROUP), per_b(2, lanes, S5_STATE), per_b(2, lanes, S5_STATE)],
        out_shape=[jax.ShapeDtypeStruct((n_groups, rows, W_GROUP), f32),
                   jax.ShapeDtypeStruct((n_groups, 2, lanes, S5_STATE), f32),
                   jax.ShapeDtypeStruct((n_groups, 2, lanes, S5_STATE), f32)],
        scratch_shapes=[pltpu.VMEM((rows, 2 * S5_STATE), f32)],
        compiler_params=pltpu.CompilerParams(dimension_semantics=("arbitrary",),
                                             vmem_limit_bytes=VMEM_LIMIT_BYTES),
        name="s5_mixer",
    )(to_rows(su), state_in(x0r), state_in(x0i), wb, wc, lb, pw,
      s5_d.reshape(1, W_GROUP), glu_w, glu_b.reshape(1, W_GROUP))
    return from_rows(y_p), state_out(xr), state_out(xi)


Q_TILE = 256
PROJ_BLOCKS = dict(mq=0, mk=1, mv=2, mo=3, su=4, rq=5, rk=6, rv=7, rg=8, dq=9, dk=10, dv=11)
GATE_LANES = 128
N_PROJ = 12 * W_GROUP + GATE_LANES


def _log_sigmoid(x):
    return jnp.minimum(x, 0.0) - jnp.log1p(jnp.exp(-jnp.abs(x)))


def _group_ones(width, group):
    shift = int(math.log2(group))
    r = lax.broadcasted_iota(jnp.int32, (width, width), 0) >> shift
    c = lax.broadcasted_iota(jnp.int32, (width, width), 1) >> shift
    return (r == c).astype(bf16)


def _split3(x):
    hi = x.astype(bf16)
    r = x - hi.astype(f32)
    mid = r.astype(bf16)
    return hi, mid, (r - mid.astype(f32)).astype(bf16)


def _group_mean(x, ones, group):
    return sum(jnp.dot(p, ones, preferred_element_type=f32) for p in _split3(x)) * (1.0 / group)


def _dot_nt(a, b):
    return lax.dot_general(a.astype(bf16), b.astype(bf16), (((1,), (1,)), ((), ())), preferred_element_type=f32)


def _dot(a, b):
    return jnp.dot(a.astype(bf16), b.astype(bf16), preferred_element_type=f32)


def _proj_block(name, rows):
    j = PROJ_BLOCKS[name]
    return pl.BlockSpec((1, rows, W_GROUP), lambda b, qi, j=j, rows=rows: (b, qi if rows == Q_TILE else 0, j))


def _mlstm_kernel(q_ref, k_ref, v_ref, o_ref, g_ref, gt_ref, gb_ref, gbt_ref, ng_ref, c0_ref, n0_ref, m0_ref,
                  h_ref, c_ref, n_ref, m_ref, gl_ref, gu_ref, rc_ref, vt_ref, ht_ref, *, seq_len, q_tile):
    L, TQ = seq_len, q_tile
    nq = L // TQ
    qi = pl.program_id(1)
    grow = gt_ref[0] + gbt_ref[...]

    @pl.when(qi == 0)
    def _():
        ss = lax.broadcasted_iota(jnp.int32, (L, L), 0)
        tt = lax.broadcasted_iota(jnp.int32, (L, L), 1)
        tri_le = (ss <= tt).astype(bf16)
        tri_ge = (ss >= tt).astype(bf16)
        rows = _split3(_log_sigmoid(grow))
        gl = sum(jnp.dot(p, tri_le, preferred_element_type=f32) for p in rows)
        gu = sum(jnp.dot(p, tri_ge, preferred_element_type=f32) for p in rows)
        for j in range(nq):
            gl_ref[j] = gl[:, j * TQ:(j + 1) * TQ]
            gu_ref[j] = gu[:, j * TQ:(j + 1) * TQ]
        gcol = g_ref[0] + gb_ref[...]
        cols = _split3(_log_sigmoid(gcol))
        glc = sum(jnp.dot(tri_ge, p, preferred_element_type=f32) for p in cols)
        guc = sum(jnp.dot(tri_le, p, preferred_element_type=f32) for p in cols)
        lane = lax.broadcasted_iota(jnp.int32, (L, GATE_LANES), 1)
        rc_ref[...] = pltpu.roll(gcol, 4, 1) - jnp.where(lane < 8, glc, guc)

        vt = v_ref[0].T
        kk = k_ref[0] * (DH_M ** -0.5)
        one_row = (lax.broadcasted_iota(jnp.int32, (DH_M, L), 0) == 0).astype(bf16)
        for h in range(H_M):
            hs = slice(h * DH_M, (h + 1) * DH_M)
            vt_ref[2 * h * DH_M:(2 * h + 1) * DH_M, :] = vt[hs, :].astype(bf16)
            vt_ref[(2 * h + 1) * DH_M:(2 * h + 2) * DH_M, :] = one_row
        for d in range(2):
            g_all = gl if d == 0 else gu
            for h in range(H_M):
                hs = slice(h * DH_M, (h + 1) * DH_M)
                ii, fi = 8 * d + h, 8 * d + 4 + h
                g_row = g_all[fi:fi + 1, :]
                g_tot = g_row[:, L - 1:L] if d == 0 else g_row[:, 0:1]
                wlog = g_tot - g_row + grow[ii:ii + 1, :]
                m0 = m0_ref[0, d:d + 1, h:h + 1]
                m_new = jnp.maximum(g_tot + m0, jnp.max(wlog, axis=1, keepdims=True))
                decay = jnp.exp(g_tot + m0 - m_new)
                w = jnp.exp(wlog - m_new)
                kh = kk[:, hs]
                c_ref[0, d, h] = decay * c0_ref[0, d, h] + _dot(vt[hs, :] * w, kh)
                n_upd = jnp.dot(jnp.broadcast_to(w, (SUBLANES, L)), kh, precision=HIGHEST,
                                preferred_element_type=f32)[0:1, :]
                n_ref[0, d, h:h + 1, :] = decay * n0_ref[0, d, h:h + 1, :] + n_upd
                m_ref[0, d:d + 1, h:h + 1] = m_new

    gl_t = gl_ref[qi]
    gu_t = gu_ref[qi]
    s_idx = lax.broadcasted_iota(jnp.int32, (L, TQ), 0)
    t_idx = qi * TQ + lax.broadcasted_iota(jnp.int32, (L, TQ), 1)
    low = s_idx <= t_idx
    upp = s_idx >= t_idx
    qt = q_ref[0].T.astype(bf16)
    k = (k_ref[0] * (DH_M ** -0.5)).astype(bf16)
    rc = rc_ref[...]
    row0 = lax.broadcasted_iota(jnp.int32, (DH_M, DH_M), 0) == 0
    for h in range(H_M):
        hs = slice(h * DH_M, (h + 1) * DH_M)
        qth = qt[hs, :]
        vta = vt_ref[2 * h * DH_M:(2 * h + 2) * DH_M, :]
        s0 = jnp.dot(k[:, hs], qth, preferred_element_type=f32)
        h_sum = None
        for d in range(2):
            fi = 8 * d + 4 + h
            g_t = (gl_t if d == 0 else gu_t)[fi:fi + 1, :]
            dlog = jnp.where(low if d == 0 else upp, rc[:, fi:fi + 1] + g_t, NEG_INF)
            inter = g_t + m0_ref[0, d:d + 1, h:h + 1]
            m_t = jnp.maximum(inter, jnp.max(dlog, axis=0, keepdims=True))
            p = s0 * jnp.exp(dlog - m_t)
            a = jnp.exp(inter - m_t)
            c0n0 = jnp.concatenate([c0_ref[0, d, h], jnp.where(row0, n0_ref[0, d, h:h + 1, :], 0.0)], axis=0)
            numden = (jnp.dot(vta, p.astype(bf16), preferred_element_type=f32)
                      + a * jnp.dot(c0n0.astype(bf16), qth, preferred_element_type=f32))
            scale = 1.0 / jnp.maximum(jnp.abs(numden[DH_M:DH_M + 1, :]), jnp.exp(-m_t))
            hd = numden[0:DH_M, :] * scale
            h_sum = hd if h_sum is None else h_sum + hd
        ht_ref[hs, :] = h_sum * lax.rsqrt(jnp.mean(h_sum * h_sum, axis=0, keepdims=True) + EPS)
    h_ref[0] = jax.nn.sigmoid(o_ref[0]) * (ht_ref[...].T * ng_ref[...])


def mlstm_mixer(proj, gates_t, c0, n0, m0, lp):
    B, L, _ = proj.shape
    TQ = min(L, Q_TILE)
    gb = lp['mlstm_gate_b'].reshape(1, 4 * H_M)
    const = lambda *shape: pl.BlockSpec(shape, lambda b, qi: (0,) * len(shape))
    per_b = lambda *shape: pl.BlockSpec((1,) + shape, lambda b, qi: (b,) + (0,) * len(shape))
    return pl.pallas_call(
        functools.partial(_mlstm_kernel, seq_len=L, q_tile=TQ),
        grid=(B, L // TQ),
        in_specs=[_proj_block('mq', TQ), _proj_block('mk', L), _proj_block('mv', L), _proj_block('mo', TQ),
                  pl.BlockSpec((1, L, GATE_LANES), lambda b, qi: (b, 0, 12 * W_GROUP // GATE_LANES)),
                  per_b(4 * H_M, L), const(1, GATE_LANES), const(4 * H_M, 1), const(1, W_GROUP),
                  per_b(2, H_M, DH_M, DH_M), per_b(2, H_M, DH_M), per_b(2, H_M)],
        out_specs=[pl.BlockSpec((1, TQ, W_GROUP), lambda b, qi: (b, qi, 0)),
                   per_b(2, H_M, DH_M, DH_M), per_b(2, H_M, DH_M), per_b(2, H_M)],
        out_shape=[jax.ShapeDtypeStruct((B, L, W_GROUP), f32),
                   jax.ShapeDtypeStruct((B, 2, H_M, DH_M, DH_M), f32),
                   jax.ShapeDtypeStruct((B, 2, H_M, DH_M), f32),
                   jax.ShapeDtypeStruct((B, 2, H_M), f32)],
        scratch_shapes=[pltpu.VMEM((L // TQ, 4 * H_M, TQ), f32), pltpu.VMEM((L // TQ, 4 * H_M, TQ), f32),
                        pltpu.VMEM((L, GATE_LANES), f32), pltpu.VMEM((2 * W_GROUP, L), bf16),
                        pltpu.VMEM((W_GROUP, TQ), f32)],
        compiler_params=pltpu.CompilerParams(dimension_semantics=("arbitrary", "arbitrary"),
                                             vmem_limit_bytes=VMEM_LIMIT_BYTES),
        name="mlstm_mixer",
    )(proj, proj, proj, proj, proj, gates_t, jnp.pad(gb, ((0, 0), (0, GATE_LANES - 4 * H_M))),
      gb.reshape(4 * H_M, 1), lp['mlstm_norm_g'].reshape(1, W_GROUP), c0, n0, m0)


def _retention_kernel(lg_ref, q_ref, k_ref, v_ref, g_ref, gn_ref, r0_ref, h_ref, r_ref, *, seq_len, q_tile):
    L, TQ = seq_len, q_tile
    qi = pl.program_id(1)
    t_col = qi * TQ + lax.broadcasted_iota(jnp.int32, (TQ, 1), 0)
    rel = (qi * TQ + lax.broadcasted_iota(jnp.int32, (TQ, L), 0)
           - lax.broadcasted_iota(jnp.int32, (TQ, L), 1)).astype(f32)
    q = q_ref[0]
    k = k_ref[0] * (DH_R ** -0.5)
    v = v_ref[0]
    ones = _group_ones(W_GROUP, DH_R)
    for h in range(H_R):
        hs = slice(h * DH_R, (h + 1) * DH_R)
        lgf, lgb = lg_ref[0, h], lg_ref[1, h]
        qh, kh, vh = q[:, hs], k[:, hs], v[:, hs]
        decay = jnp.where(rel > 0.0, jnp.exp(lgf * jnp.maximum(rel, 0.0)),
                          jnp.where(rel < 0.0, jnp.exp(lgb * jnp.maximum(-rel, 0.0)), 2.0))
        o = _dot(_dot_nt(qh, kh) * decay, vh)
        xi_f = jnp.exp(lgf * (t_col + 1).astype(f32))
        xi_b = jnp.exp(lgb * (L - t_col).astype(f32))
        o = o + xi_f * _dot(qh, r0_ref[0, 0, h]) + xi_b * _dot(qh, r0_ref[0, 1, h])
        h_ref[0, :, hs] = o
    o = h_ref[0]
    oc = o - _group_mean(o, ones, DH_R)
    y = oc * lax.rsqrt(_group_mean(oc * oc, ones, DH_R) + EPS) * gn_ref[...]
    h_ref[0] = y * jax.nn.silu(g_ref[0])

    @pl.when(qi == 0)
    def _():
        kt = k.T
        s_row = lax.broadcasted_iota(jnp.int32, (1, L), 1).astype(f32)
        for d in range(2):
            for h in range(H_R):
                hs = slice(h * DH_R, (h + 1) * DH_R)
                lg = lg_ref[d, h]
                zeta = jnp.exp(lg * ((L - 1.0) - s_row)) if d == 0 else jnp.exp(lg * s_row)
                r_ref[0, d, h] = jnp.exp(lg * float(L)) * r0_ref[0, d, h] + _dot(kt[hs, :] * zeta, v[:, hs])


def retention_mixer(proj, r0, lp):
    B, L, _ = proj.shape
    TQ = min(L, Q_TILE)
    log_gamma = -jnp.exp(lp['ret_decay'])
    per_b = lambda *shape: pl.BlockSpec((1,) + shape, lambda b, qi: (b,) + (0,) * len(shape))
    return pl.pallas_call(
        functools.partial(_retention_kernel, seq_len=L, q_tile=TQ),
        grid=(B, L // TQ),
        in_specs=[pl.BlockSpec(memory_space=pltpu.SMEM),
                  _proj_block('rq', TQ), _proj_block('rk', L), _proj_block('rv', L), _proj_block('rg', TQ),
                  pl.BlockSpec((1, W_GROUP), lambda b, qi: (0, 0)), per_b(2, H_R, DH_R, DH_R)],
        out_specs=[pl.BlockSpec((1, TQ, W_GROUP), lambda b, qi: (b, qi, 0)), per_b(2, H_R, DH_R, DH_R)],
        out_shape=[jax.ShapeDtypeStruct((B, L, W_GROUP), f32),
                   jax.ShapeDtypeStruct((B, 2, H_R, DH_R, DH_R), f32)],
        compiler_params=pltpu.CompilerParams(dimension_semantics=("arbitrary", "arbitrary"),
                                             vmem_limit_bytes=VMEM_LIMIT_BYTES),
        name="retention_mixer",
    )(log_gamma, proj, proj, proj, proj, lp['ret_gn_g'].reshape(1, W_GROUP), r0)


def _rope_tables(L):
    half = DH_D // 2
    freqs = ROPE_BASE ** (-np.arange(0, half, 2, dtype=np.float64) / half)
    pos = np.arange(L)
    row, col = (pos // GRID_W).astype(np.float64), (pos % GRID_W).astype(np.float64)
    ang = np.concatenate([np.tile(row[:, None] * freqs, (1, 2)), np.tile(col[:, None] * freqs, (1, 2))], axis=1)
    sign = np.tile(np.concatenate([-np.ones(half // 2), np.ones(half // 2)]), 2)
    cos = np.tile(np.cos(ang), (1, 2 * H_D))
    sin = np.tile(np.sin(ang) * sign, (1, 2 * H_D))
    return jnp.asarray(cos, f32), jnp.asarray(sin, f32)


def _swap_pairs(x):
    parts = []
    for j in range(x.shape[1] // 128):
        xs = x[:, j * 128:(j + 1) * 128]
        lane = lax.broadcasted_iota(jnp.int32, xs.shape, 1)
        parts.append(jnp.where((lane & 15) < 8, pltpu.roll(xs, 120, 1), pltpu.roll(xs, 8, 1)))
    return jnp.concatenate(parts, axis=1)


def _qk_norm(x, gain, ones):
    return x * lax.rsqrt(_group_mean(x * x, ones, DH_D) + EPS) * gain


def _diff_attn_kernel(lam_ref, q_ref, k_ref, v_ref, qkg_ref, sg_ref, *rest, seq_len, q_tile, past_len, out_scale):
    L, TQ, P = seq_len, q_tile, past_len
    if P:
        kc_ref, vc_ref, cos_ref, sin_ref, h_ref, ka_ref, vt_ref, ot_ref = rest
    else:
        h_ref, kn_ref, ka_ref, vt_ref, ot_ref = rest
    qi = pl.program_id(1)
    ones = _group_ones(W_GROUP, DH_D)

    @pl.when(qi == 0)
    def _():
        kn = _qk_norm(k_ref[0], qkg_ref[1:2, :], ones)
        if P:
            kn = kn * cos_ref[...] + _swap_pairs(kn) * sin_ref[...]
            ka_ref[0:P, :] = kc_ref[0, 0].astype(bf16)
            vt_ref[:, 0:P] = vc_ref[0, 0].T.astype(bf16)
        else:
            kn_ref[0] = kn
        ka_ref[P:P + L, :] = kn.astype(bf16)
        vt_ref[:, P:P + L] = v_ref[0].T.astype(bf16)

    qn = _qk_norm(q_ref[0], qkg_ref[0:1, :], ones)
    if P:
        rows = pl.ds(pl.multiple_of(qi * TQ, TQ), TQ)
        qn = qn * cos_ref[rows, :] + _swap_pairs(qn) * sin_ref[rows, :]
    qt = (qn * (DH_D ** -0.5)).T.astype(bf16)
    lam = lam_ref[0, 0]
    ka = ka_ref[...]
    for h in range(H_D):
        probs = []
        for j in range(2):
            cs = slice((2 * h + j) * DH_D, (2 * h + j + 1) * DH_D)
            s = jnp.dot(ka[:, cs], qt[cs, :], preferred_element_type=f32)
            e = jnp.exp(s - jnp.max(s, axis=0, keepdims=True))
            probs.append(e * (1.0 / jnp.sum(e, axis=0, keepdims=True)))
        vs = slice(h * 2 * DH_D, (h + 1) * 2 * DH_D)
        ot_ref[vs, :] = jnp.dot(vt_ref[vs, :], (probs[0] - lam * probs[1]).astype(bf16), preferred_element_type=f32)
    o = ot_ref[...].T
    ones_v = _group_ones(W_GROUP, 2 * DH_D)
    h_ref[0] = o * lax.rsqrt(_group_mean(o * o, ones_v, 2 * DH_D) + EPS) * (sg_ref[...] * out_scale)


def diff_attn_mixer(proj, cache, lp, lam_init, layer):
    B, L, _ = proj.shape
    TQ = min(L, Q_TILE)
    lv = lp['diff_lambda']
    lam = (jnp.exp(jnp.sum(lv[0] * lv[1])) - jnp.exp(jnp.sum(lv[2] * lv[3])) + lam_init).reshape(1, 1)
    qkg = jnp.tile(lp['diff_qk_norm'], (1, 2 * H_D))
    sg = jnp.tile(lp['diff_subln_g'], (H_D,)).reshape(1, W_GROUP)
    const = lambda *shape: pl.BlockSpec(shape, lambda b, qi: (0,) * len(shape))
    in_specs = [pl.BlockSpec(memory_space=pltpu.SMEM),
                _proj_block('dq', TQ), _proj_block('dk', L), _proj_block('dv', L), const(2, W_GROUP), const(1, W_GROUP)]
    args = [lam, proj, proj, proj, qkg, sg]
    out_specs = [pl.BlockSpec((1, TQ, W_GROUP), lambda b, qi: (b, qi, 0))]
    out_shape = [jax.ShapeDtypeStruct((B, L, W_GROUP), f32)]
    P = 0
    if cache is not None:
        ck, cv = cache
        P = ck.shape[2]
        cspec = pl.BlockSpec((1, 1, P, W_GROUP), lambda b, qi, layer=layer: (b, layer, 0, 0))
        cos, sin = _rope_tables(L)
        in_specs += [cspec, cspec, const(L, W_GROUP), const(L, W_GROUP)]
        args += [ck, cv, cos, sin]
    else:
        out_specs.append(pl.BlockSpec((1, L, W_GROUP), lambda b, qi: (b, 0, 0)))
        out_shape.append(jax.ShapeDtypeStruct((B, L, W_GROUP), f32))
    return pl.pallas_call(
        functools.partial(_diff_attn_kernel, seq_len=L, q_tile=TQ, past_len=P, out_scale=1.0 - lam_init),
        grid=(B, L // TQ),
        in_specs=in_specs, out_specs=out_specs, out_shape=out_shape,
        scratch_shapes=[pltpu.VMEM((P + L, W_GROUP), bf16), pltpu.VMEM((W_GROUP, P + L), bf16),
                        pltpu.VMEM((W_GROUP, TQ), f32)],
        compiler_params=pltpu.CompilerParams(dimension_semantics=("arbitrary", "arbitrary"),
                                             vmem_limit_bytes=VMEM_LIMIT_BYTES),
        name="diff_attention",
    )(*args)


PROJ_ROW_TILE = 512
OUT_ROW_TILE = 512
FF_TILE = 512
D_FF = 2 * D_MODEL


def _in_proj_kernel(x_ref, g_ref, sc_ref, sh_ref, w_ref, o_ref):
    x = x_ref[...]
    y = x * lax.rsqrt(jnp.mean(x * x, axis=1, keepdims=True) + EPS) * g_ref[...]
    h = (y * (1.0 + sc_ref[0]) + sh_ref[0]).astype(bf16)
    o_ref[...] = jnp.dot(h, w_ref[0], preferred_element_type=f32)


def in_projection(x, gain, scale, shift, w_p, layer, rows_per_mod):
    n, D = x.shape
    TM = min(n, PROJ_ROW_TILE)
    mod = pl.BlockSpec((1, 1, D), lambda i: (i * TM // rows_per_mod, 0, 0))
    return pl.pallas_call(
        _in_proj_kernel,
        grid=(n // TM,),
        in_specs=[pl.BlockSpec((TM, D), lambda i: (i, 0)), pl.BlockSpec((1, D), lambda i: (0, 0)), mod, mod,
                  pl.BlockSpec((1, D, N_PROJ), lambda i: (layer, 0, 0))],
        out_specs=pl.BlockSpec((TM, N_PROJ), lambda i: (i, 0)),
        out_shape=jax.ShapeDtypeStruct((n, N_PROJ), f32),
        compiler_params=pltpu.CompilerParams(dimension_semantics=("arbitrary",),
                                             vmem_limit_bytes=VMEM_LIMIT_BYTES),
        name="in_projection",
    )(x, gain.reshape(1, D), scale, shift, w_p)


def _out_proj_kernel(x_ref, m0_ref, m1_ref, m2_ref, m3_ref, w_ref, g1_ref, ng_ref, sc_ref, sh_ref, rw_ref,
                     xo_ref, h_ref, aff_ref, wb_ref):
    @pl.when(pl.program_id(0) == 0)
    def _():
        wb_ref[...] = w_ref[0].astype(bf16)

    out = None
    for j, m_ref in enumerate((m0_ref, m1_ref, m2_ref, m3_ref)):
        part = jnp.dot(m_ref[...].astype(bf16), wb_ref[j * W_GROUP:(j + 1) * W_GROUP, :], preferred_element_type=f32)
        out = part if out is None else out + part
    x = x_ref[...] + g1_ref[0] * out
    xo_ref[...] = x
    h = x * lax.rsqrt(jnp.mean(x * x, axis=1, keepdims=True) + EPS) * ng_ref[...]
    h = h * (1.0 + sc_ref[0]) + sh_ref[0]
    h_hi = h.astype(bf16)
    bits = pltpu.bitcast(h_hi.astype(f32), jnp.uint32)
    half = bits.shape[1] // 2
    h_ref[...] = (bits[:, :half] >> 16) | bits[:, half:]
    h_lo = (h - h_hi.astype(f32)).astype(bf16)
    both = jnp.dot(h_hi, rw_ref[0], preferred_element_type=f32)
    logits = (both[:, 0:N_EXPERTS] + both[:, N_EXPERTS:2 * N_EXPERTS]
              + jnp.dot(h_lo, rw_ref[0], preferred_element_type=f32)[:, 0:N_EXPERTS])
    e = jnp.exp(logits - jnp.max(logits, axis=1, keepdims=True))
    aff_ref[...] = e / jnp.sum(e, axis=1, keepdims=True)


def out_projection(x, mixed, w_out, gate1, gain2, scale2, shift2, router_w2, layer, rows_per_mod):
    n, D = x.shape
    TM = OUT_ROW_TILE
    row = lambda width: pl.BlockSpec((TM, width), lambda i: (i, 0))
    const = lambda *shape: pl.BlockSpec(shape, lambda i: (0,) * len(shape))
    mod = pl.BlockSpec((1, 1, D), lambda i: (i * TM // rows_per_mod, 0, 0))
    return pl.pallas_call(
        _out_proj_kernel,
        grid=(n // TM,),
        in_specs=[row(D), row(W_GROUP), row(W_GROUP), row(W_GROUP), row(W_GROUP),
                  pl.BlockSpec((1, D, D), lambda i: (layer, 0, 0)), mod, const(1, D), mod, mod,
                  pl.BlockSpec((1, D, 2 * N_EXPERTS), lambda i: (layer, 0, 0))],
        out_specs=[row(D), row(D // 2), row(N_EXPERTS)],
        out_shape=[jax.ShapeDtypeStruct((n, D), f32), jax.ShapeDtypeStruct((n, D // 2), jnp.uint32),
                   jax.ShapeDtypeStruct((n, N_EXPERTS), f32)],
        scratch_shapes=[pltpu.VMEM((D, D), bf16)],
        compiler_params=pltpu.CompilerParams(dimension_semantics=("arbitrary",), vmem_limit_bytes=VMEM_LIMIT_BYTES),
        name="out_projection",
    )(x, *mixed, w_out, gate1, gain2.reshape(1, D), scale2, shift2, router_w2)


def _experts_kernel(xc_ref, xl_ref, gc_ref, gl_ref, g2_ref, g2l_ref, il_ref, wg_ref, wu_ref, wd_ref, yc_ref, yl_ref,
                    ac_ref, al_ref, *, lat_len):
    f = pl.program_id(1)
    wg = wg_ref[0, 0].astype(bf16)
    wu = wu_ref[0, 0].astype(bf16)
    wd = wd_ref[0, 0].astype(bf16)
    for x_ref, acc_ref in ((xc_ref, ac_ref), (xl_ref, al_ref)):
        words = x_ref[0]
        x = jnp.concatenate([pltpu.bitcast(words << 16, f32), pltpu.bitcast(words & jnp.uint32(0xFFFF0000), f32)],
                            axis=1).astype(bf16)
        hidden = jax.nn.silu(jnp.dot(x, wg, preferred_element_type=f32)) * jnp.dot(x, wu, preferred_element_type=f32)
        part = jnp.dot(hidden.astype(bf16), wd, preferred_element_type=f32)

        @pl.when(f == 0)
        def _(acc_ref=acc_ref, part=part):
            acc_ref[...] = part

        @pl.when(f > 0)
        def _(acc_ref=acc_ref, part=part):
            acc_ref[...] += part

    @pl.when(f == pl.num_programs(1) - 1)
    def _():
        yc_ref[0] = ac_ref[...] * gc_ref[0] * g2_ref[0]
        gate = g2l_ref[0]
        for b in range(1, g2l_ref.shape[0]):
            gate = jnp.where(il_ref[0] >= b * lat_len, g2l_ref[b], gate)
        yl_ref[0] = al_ref[...] * gl_ref[0] * gate


def expert_ffn(xe_c, xe_l, g_c, g_l, gate2_c, gate2_l, idx_l, lat_len, w_gate, w_up, w_down, layer):
    E, Cc, half_d = xe_c.shape
    D = 2 * half_d
    Cl = xe_l.shape[1]
    tok = lambda C, width: pl.BlockSpec((1, C, width), lambda e, f: (e, 0, 0))
    return pl.pallas_call(
        functools.partial(_experts_kernel, lat_len=lat_len),
        grid=(E, D_FF // FF_TILE),
        in_specs=[tok(Cc, half_d), tok(Cl, half_d), tok(Cc, 1), tok(Cl, 1), pl.BlockSpec((1, 1, D), lambda e, f: (0, 0, 0)),
                  pl.BlockSpec(gate2_l.shape, lambda e, f: (0, 0, 0)), tok(Cl, 1),
                  pl.BlockSpec((1, 1, D, FF_TILE), lambda e, f: (layer, e, 0, f)),
                  pl.BlockSpec((1, 1, D, FF_TILE), lambda e, f: (layer, e, 0, f)),
                  pl.BlockSpec((1, 1, FF_TILE, D), lambda e, f: (layer, e, f, 0))],
        out_specs=[tok(Cc, D), tok(Cl, D)],
        out_shape=[jax.ShapeDtypeStruct((E, Cc, D), f32), jax.ShapeDtypeStruct((E, Cl, D), f32)],
        scratch_shapes=[pltpu.VMEM((Cc, D), f32), pltpu.VMEM((Cl, D), f32)],
        compiler_params=pltpu.CompilerParams(dimension_semantics=("arbitrary", "arbitrary"),
                                             vmem_limit_bytes=VMEM_LIMIT_BYTES),
        name="expert_ffn",
    )(xe_c, xe_l, g_c, g_l, gate2_c, gate2_l, idx_l, w_gate, w_up, w_down)


def _permute_w_in(w):
    gate0, n_gate = 4 * W_GROUP, 4 * H_M
    out = jnp.zeros(w.shape[:-1] + (N_PROJ,), bf16)
    out = out.at[..., :gate0].set(w[..., :gate0].astype(bf16))
    out = out.at[..., gate0:12 * W_GROUP].set(w[..., gate0 + n_gate:].astype(bf16))
    return out.at[..., 12 * W_GROUP:12 * W_GROUP + n_gate].set(w[..., gate0:gate0 + n_gate].astype(bf16))


def token_mixers(proj, lp, s5p, lam_init, states, cache, layer):
    B, L, _ = proj.shape
    mC0, mn0, mm0, s5r0, s5i0, R0 = states
    gates_t = jnp.swapaxes(proj[:, :, 12 * W_GROUP:12 * W_GROUP + 4 * H_M], 1, 2)
    hm, mC, mn, mm = mlstm_mixer(proj, gates_t, mC0, mn0, mm0, lp)
    su = proj[:, :, PROJ_BLOCKS['su'] * W_GROUP:(PROJ_BLOCKS['su'] + 1) * W_GROUP]
    ys, s5r, s5i = s5_mixer(su, s5r0, s5i0, s5p, lp['s5_d'], lp['s5_glu_b'], layer)
    hr, R = retention_mixer(proj, R0, lp)
    attn = diff_attn_mixer(proj, cache, lp, lam_init, layer)
    mixed = [a.reshape(B * L, W_GROUP) for a in (hm, ys, hr, attn[0])]
    new_ctx = None
    if cache is None:
        v = proj[:, :, PROJ_BLOCKS['dv'] * W_GROUP:(PROJ_BLOCKS['dv'] + 1) * W_GROUP]
        new_ctx = (mC, mn, mm, s5r, s5i, R, attn[1].reshape(B, L, 2 * H_D, DH_D), v.reshape(B, L, H_D, 2 * DH_D))
    return mixed, new_ctx


GATHER_WINDOW = 64
INDEX_LANES = 128


def gather_rows(x, idx):
    num = idx.shape[0]
    width = x.shape[1]
    mesh = plsc.VectorSubcoreMesh(core_axis_name="core", subcore_axis_name="subcore")
    per_core = num // GATHER_WINDOW // mesh.num_cores
    idx_rows = jnp.pad(idx.reshape(num // GATHER_WINDOW, GATHER_WINDOW), ((0, 0), (0, INDEX_LANES - GATHER_WINDOW)))

    @pl.kernel(out_type=jax.ShapeDtypeStruct((num, width), x.dtype), mesh=mesh)
    def gather_kernel(x_hbm, i_hbm, o_hbm):
        base = lax.axis_index("core") * per_core

        def body(i_vmem, o_vmem):
            pltpu.sync_copy(x_hbm.at[i_vmem.at[0, pl.ds(0, GATHER_WINDOW)]], o_vmem)

        pltpu.emit_pipeline(
            body,
            grid=(per_core,),
            in_specs=[pl.BlockSpec((1, INDEX_LANES), index_map=lambda i: (base + i, 0))],
            out_specs=[pl.BlockSpec((GATHER_WINDOW, width), index_map=lambda i: (base + i, 0))],
            core_axis_name="subcore",
            dimension_semantics=(pltpu.PARALLEL,),
        )(i_hbm, o_hbm)

    return gather_kernel(x, idx_rows)


SC_LANES = 16


def _threshold_kernel(aff_ref, t_ref, r_ref, *, cap):
    bits = pltpu.bitcast(aff_ref[...], jnp.int32)
    t = jnp.zeros((N_EXPERTS, 1), jnp.int32)
    for b in range(30, -1, -1):
        cand = t | (1 << b)
        cnt = jnp.sum((bits >= cand).astype(f32), axis=1, keepdims=True)
        t = jnp.where(cnt >= cap, cand, t)
    n_gt = jnp.sum((bits > t).astype(f32), axis=1, keepdims=True)
    t_ref[...] = jnp.broadcast_to(t, (N_EXPERTS, GATE_LANES))
    r_ref[...] = jnp.broadcast_to(cap - n_gt.astype(jnp.int32), (N_EXPERTS, GATE_LANES))


def expert_choice_select(aff_t, cap):
    n = aff_t.shape[1]
    t_bits, n_ties = pl.pallas_call(
        functools.partial(_threshold_kernel, cap=cap),
        out_shape=[jax.ShapeDtypeStruct((N_EXPERTS, GATE_LANES), jnp.int32)] * 2,
        name="select_threshold",
    )(aff_t)
    mesh = plsc.VectorSubcoreMesh(core_axis_name="core", subcore_axis_name="subcore")
    params = pltpu.CompilerParams()
    if "needs_layout_passes" in pltpu.CompilerParams.__dataclass_fields__:
        params = dataclasses.replace(params, needs_layout_passes=False)

    @pl.kernel(out_type=[jax.ShapeDtypeStruct((N_EXPERTS, cap), jnp.int32), jax.ShapeDtypeStruct((N_EXPERTS, cap), f32)],
               mesh=mesh, compiler_params=params,
               scratch_types=[pltpu.VMEM((n,), f32), pltpu.VMEM((cap + SC_LANES,), jnp.int32),
                              pltpu.VMEM((cap + SC_LANES,), f32), pltpu.VMEM((SC_LANES,), jnp.int32),
                              pltpu.VMEM((SC_LANES,), jnp.int32)])
    def select_kernel(aff_hbm, t_hbm, r_hbm, idx_hbm, g_hbm, row, ibuf, gbuf, tv, rv):
        e = lax.axis_index("subcore")

        @pl.when(lax.axis_index("core") == 0)
        def _():
            pltpu.sync_copy(aff_hbm.at[e], row)
            pltpu.sync_copy(t_hbm.at[e], tv)
            pltpu.sync_copy(r_hbm.at[e], rv)
            t = tv[...]
            r = rv[...]
            lane = lax.iota(jnp.int32, SC_LANES)

            def body(j, carry):
                off, seen_eq = carry
                v = row[pl.ds(j * SC_LANES, SC_LANES)]
                b = plsc.bitcast(v, jnp.int32)
                eq = b == t
                eq_i = eq.astype(jnp.int32)
                take = (b > t) | (eq & (seen_eq + plsc.cumsum(eq_i) <= r))
                plsc.store_compressed(ibuf.at[pl.ds(off, SC_LANES)], lane + j * SC_LANES, mask=take)
                plsc.store_compressed(gbuf.at[pl.ds(off, SC_LANES)], v, mask=take)
                return off + jnp.sum(take.astype(jnp.int32)), seen_eq + jnp.sum(eq_i)

            lax.fori_loop(0, n // SC_LANES, body, (jnp.int32(0), jnp.int32(0)))
            pltpu.sync_copy(ibuf.at[pl.ds(0, cap)], idx_hbm.at[e])
            pltpu.sync_copy(gbuf.at[pl.ds(0, cap)], g_hbm.at[e])

    idx, gates = select_kernel(aff_t, t_bits[:, :SC_LANES], n_ties[:, :SC_LANES])
    return idx, gates


def _route(aff, h2):
    n = aff.shape[0]
    idx, gates = expert_choice_select(aff.T, CAPACITY_FACTOR * n // N_EXPERTS)
    return gates[..., None], idx, gather_rows(h2, idx.reshape(-1)).reshape(idx.shape + h2.shape[1:])


SCATTER_COLS = 128
SCATTER_WINDOW = 128


def scatter_add_rows(x, ye, idx):
    n, D = x.shape
    R = ye.shape[0]
    mesh = plsc.VectorSubcoreMesh(core_axis_name="core", subcore_axis_name="subcore")
    rows = n // mesh.num_subcores
    wins = R // SCATTER_WINDOW // mesh.num_subcores

    @pl.kernel(out_type=jax.ShapeDtypeStruct((n, D), f32), mesh=mesh,
               scratch_types=[pltpu.VMEM_SHARED((n, SCATTER_COLS), f32), pltpu.VMEM((SCATTER_WINDOW, SCATTER_COLS), f32),
                              pltpu.VMEM((SCATTER_WINDOW,), jnp.int32)])
    def scatter_kernel(x_hbm, ye_hbm, i_hbm, o_hbm, shared, buf, ibuf):
        core = lax.axis_index("core")
        sid = lax.axis_index("subcore")
        r0 = pl.multiple_of(sid * rows, SUBLANES)
        for slab in range(D // SCATTER_COLS):
            cols = pl.ds(slab * SCATTER_COLS, SCATTER_COLS)

            @pl.when(core == slab % mesh.num_cores)
            def _(cols=cols):
                pltpu.sync_copy(x_hbm.at[pl.ds(r0, rows), cols], shared.at[pl.ds(r0, rows)])
                plsc.subcore_barrier()

                @pl.loop(0, wins)
                def _(w):
                    win = w * mesh.num_subcores + sid
                    pltpu.sync_copy(i_hbm.at[win], ibuf)
                    pltpu.sync_copy(ye_hbm.at[pl.ds(pl.multiple_of(win * SCATTER_WINDOW, SCATTER_WINDOW),
                                                    SCATTER_WINDOW), cols], buf)
                    pltpu.sync_copy(buf, shared.at[ibuf], add=True)

                plsc.subcore_barrier()
                pltpu.sync_copy(shared.at[pl.ds(r0, rows)], o_hbm.at[pl.ds(r0, rows), cols])
                plsc.subcore_barrier()

    return scatter_kernel(x, ye, idx.reshape(R // SCATTER_WINDOW, SCATTER_WINDOW))


PER_LAYER = ('norm1_g', 'norm2_g', 'mlstm_gate_b', 'mlstm_norm_g', 's5_d', 's5_glu_b', 'ret_decay',
             'ret_gn_g', 'diff_qk_norm', 'diff_lambda', 'diff_subln_g')


def kernel(x_prompt, x_sample, state_mlstm_c, state_mlstm_n, state_mlstm_m, state_s5_re, state_s5_im, state_ret, cache_diff_k, cache_diff_v, c, c_ctx, norm1_g, norm2_g, ada_w, ada_b, w_in, w_out, mlstm_gate_b, mlstm_norm_g, s5_lambda_re, s5_lambda_im, s5_log_step, s5_b_re, s5_b_im, s5_c_re, s5_c_im, s5_d, s5_glu_w, s5_glu_b, ret_decay, ret_gn_g, diff_qk_norm, diff_lambda, diff_subln_g, router_w, exp_w_gate, exp_w_up, exp_w_down):
    weights = dict(norm1_g=norm1_g, norm2_g=norm2_g, mlstm_gate_b=mlstm_gate_b,
                   mlstm_norm_g=mlstm_norm_g, s5_d=s5_d, s5_glu_b=s5_glu_b, ret_decay=ret_decay, ret_gn_g=ret_gn_g,
                   diff_qk_norm=diff_qk_norm, diff_lambda=diff_lambda, diff_subln_g=diff_subln_g)
    w_in_p = _permute_w_in(w_in)
    rw_hi = router_w.astype(bf16)
    router_w2 = jnp.concatenate([rw_hi, (router_w - rw_hi.astype(f32)).astype(bf16)], axis=-1)
    s5p = _s5_prepare(s5_lambda_re, s5_lambda_im, s5_log_step, s5_b_re, s5_b_im, s5_c_re, s5_c_im, s5_glu_w,
                      (x_prompt.shape[1] // SUBLANES, x_sample.shape[1] // SUBLANES))
    Bc, Lc, D = x_prompt.shape
    Bl, Ll, _ = x_sample.shape
    xc = x_prompt.reshape(Bc * Lc, D)
    xl = x_sample.reshape(Bl * Ll, D)
    zero_states = (jnp.zeros((Bc, 2, H_M, DH_M, DH_M), f32), jnp.zeros((Bc, 2, H_M, DH_M), f32),
                   jnp.zeros((Bc, 2, H_M), f32), jnp.zeros((Bc, 2, G_S5, P_S5), f32),
                   jnp.zeros((Bc, 2, G_S5, P_S5), f32), jnp.zeros((Bc, 2, H_R, DH_R, DH_R), f32))
    cache = (cache_diff_k.reshape(cache_diff_k.shape[:3] + (W_GROUP,)),
             cache_diff_v.reshape(cache_diff_v.shape[:3] + (W_GROUP,)))
    cvec = jnp.concatenate([c_ctx[None, :], c], axis=0)
    outs = [[] for _ in range(8)]
    for l in range(DEPTH):
        lp = {name: weights[name][l] for name in PER_LAYER}
        lam_init = 0.8 - 0.6 * math.exp(-0.3 * l)
        mods = jnp.split((jax.nn.silu(cvec) @ ada_w[l] + ada_b[l])[:, None, :], 6, axis=-1)
        lat_states = (state_mlstm_c[:, l], state_mlstm_n[:, l], state_mlstm_m[:, l], state_s5_re[:, l],
                      state_s5_im[:, l], state_ret[:, l])
        def mix(x, B, L, sel, states, kv):
            sh1, sc1 = mods[0][sel], mods[1][sel]
            proj = in_projection(x, lp['norm1_g'], sc1, sh1, w_in_p, l, x.shape[0] // sh1.shape[0]).reshape(B, L, N_PROJ)
            return token_mixers(proj, lp, s5p, lam_init, states, kv, l)

        def project_and_route(x, mixed, sel):
            g1, sh2, sc2, g2 = (m[sel] for m in mods[2:])
            x1, h2, aff = out_projection(x, mixed, w_out, g1, lp['norm2_g'], sc2, sh2, router_w2, l,
                                         x.shape[0] // g1.shape[0])
            return (x1, g2, g1.shape[0]) + _route(aff, h2)

        sel_c, sel_l = slice(0, 1), slice(1, 1 + Bl)
        mixed_l, _ = mix(xl, Bl, Ll, sel_l, lat_states, cache)
        xc, mixed_l = lax.optimization_barrier((xc, mixed_l))
        mixed_c, new_ctx = mix(xc, Bc, Lc, sel_c, zero_states, None)
        for acc, t in zip(outs, new_ctx):
            acc.append(t)
        x1c, g2c, nbc, gc, idxc, xec = project_and_route(xc, mixed_c, sel_c)
        x1l, g2l, nbl, gl, idxl, xel = project_and_route(xl, mixed_l, sel_l)
        yec, yel = expert_ffn(xec, xel, gc, gl, g2c, g2l, idxl[..., None], Ll, exp_w_gate, exp_w_up, exp_w_down, l)
        xl = scatter_add_rows(x1l, yel.reshape(-1, D), idxl.reshape(-1))
        xc = scatter_add_rows(x1c, yec.reshape(-1, D), idxc.reshape(-1))
    _, outs = lax.optimization_barrier((yec, outs))
    return (xc.reshape(Bc, Lc, D), xl.reshape(Bl, Ll, D)) + tuple(jnp.stack(o, axis=1) for o in outs)
```
